```python
import math
import jax, jax.numpy as jnp
from jax import lax
import numpy as np

D_MODEL = 2048
BATCH = 16
SEQ = 256
DEPTH = 1
DEC_BATCH = 4
DEC_SEQ = 2048
PAST_LEN = 512

GRID_W = 64
MIX_WIDTH = D_MODEL
RET_WIDTH = MIX_WIDTH // 2
POOL_WIDTH = MIX_WIDTH - RET_WIDTH
RET_HEADS = 4
RET_DV = RET_WIDTH // RET_HEADS
RET_DK = RET_DV // 2
RET_QK_WIDTH = RET_HEADS * RET_DK
CHUNK = 128
N_DIRS = 2
POOL_WINDOWS = (2, 4, 8, 16)
POOL_GROUPS = len(POOL_WINDOWS)
POOL_DG = POOL_WIDTH // POOL_GROUPS
IN_WIDTH = 2 * RET_QK_WIDTH + 2 * RET_WIDTH + POOL_WIDTH
N_EXPERTS = 32
TOP_K = 4
D_FF = D_MODEL
SWIGLU_LIMIT = 7.0
SWIGLU_ALPHA = 1.702
ROPE_BASE = 10000.0
NORM_EPS = 1e-6
GN_EPS = 1e-6

kernel_name = 'hybrid_retention_pool_moe_diffusion_step'


def rms_norm(x, g):
    x32 = x.astype(jnp.float32)
    y = x32 * lax.rsqrt(jnp.mean(x32 * x32, axis=-1, keepdims=True) + NORM_EPS)
    return (y * g.astype(jnp.float32)).astype(x.dtype)


def _rope_half(x, ang):
    d2 = x.shape[-1] // 2
    x1, x2 = x[..., :d2], x[..., d2:]
    cos = jnp.cos(ang)[None, :, None, :].astype(x.dtype)
    sin = jnp.sin(ang)[None, :, None, :].astype(x.dtype)
    return jnp.concatenate([x1 * cos - x2 * sin, x1 * sin + x2 * cos], axis=-1)


def axial_rope(x):
    L = x.shape[1]
    t = jnp.arange(L)
    row = (t // GRID_W).astype(jnp.float32)
    col = (t % GRID_W).astype(jnp.float32)
    half = x.shape[-1] // 2
    n_freq = half // 2
    inv = ROPE_BASE ** (-jnp.arange(n_freq, dtype=jnp.float32) / n_freq)
    return jnp.concatenate([_rope_half(x[..., :half], row[:, None] * inv),
                            _rope_half(x[..., half:], col[:, None] * inv)], axis=-1)


def retention_scan(q, k, v, log_gamma, s0):
    B, L, H, _ = q.shape
    n = L // CHUNK

    def chunks(a):
        return a.reshape(B, n, CHUNK, H, a.shape[-1]).transpose(1, 0, 2, 3, 4)

    pos = jnp.arange(CHUNK, dtype=jnp.float32)
    diff = pos[:, None] - pos[None, :]
    causal = diff >= 0
    intra = jnp.where(causal[None], jnp.exp(log_gamma[:, None, None] * jnp.where(causal, diff, 0.0)[None]), 0.0)
    xi = jnp.exp(log_gamma[None, :] * (pos[:, None] + 1.0))
    zeta = jnp.exp(log_gamma[None, :] * (CHUNK - 1.0 - pos[:, None]))
    chunk_decay = jnp.exp(log_gamma * CHUNK)[None, :, None, None]

    def step(S, blk):
        qc, kc, vc = blk
        scores = jnp.einsum('bihd,bjhd->bhij', qc, kc) * intra[None]
        inner = jnp.einsum('bhij,bjhe->bihe', scores, vc)
        cross = jnp.einsum('bihd,bhde->bihe', qc, S) * xi[None, :, :, None]
        S_new = chunk_decay * S + jnp.einsum('bjhd,bjhe->bhde', kc * zeta[None, :, :, None], vc)
        return S_new, inner + cross

    S_fin, out = lax.scan(step, s0, (chunks(q), chunks(k), chunks(v)))
    out = out.transpose(1, 0, 2, 3, 4).reshape(B, L, H, v.shape[-1])
    return out, S_fin


def box_mean(x, w, axis):
    L = x.shape[axis]
    t = jnp.arange(L)
    lo = jnp.clip(t - w // 2, 0, L)
    hi = jnp.clip(t - w // 2 + w, 0, L)
    cs = jnp.cumsum(x, axis=axis)
    cs = jnp.concatenate([jnp.zeros_like(lax.slice_in_dim(cs, 0, 1, axis=axis)), cs], axis=axis)
    s = jnp.take(cs, hi, axis=axis) - jnp.take(cs, lo, axis=axis)
    shape = [1] * x.ndim
    shape[axis] = L
    cnt = (hi - lo).astype(jnp.float32).reshape(shape)
    return s / cnt


def multiscale_pool(p, w_pool, pool_scale, grid):
    B, L, _ = p.shape
    pg = p.astype(jnp.float32).reshape(B, L, POOL_GROUPS, POOL_DG)
    outs = []
    for gi, w in enumerate(POOL_WINDOWS):
        xg = pg[:, :, gi]
        if grid:
            rows = L // GRID_W
            xs = xg.reshape(B, rows, GRID_W, POOL_DG)
            m = box_mean(box_mean(xs, w, 1), w, 2).reshape(B, L, POOL_DG)
        else:
            m = box_mean(xg, w, 1)
        outs.append(m - xg)
    d = jnp.stack(outs, axis=2).astype(p.dtype)
    y = jnp.einsum('blgc,gcd->blgd', d, w_pool).reshape(B, L, POOL_WIDTH)
    return y * pool_scale


def mixer(h, w_in, ret_decay, w_pool, pool_scale, w_out, s0, grid):
    B, L, _ = h.shape
    proj = h @ w_in
    q, k, v, g, p = jnp.split(proj, [RET_QK_WIDTH, 2 * RET_QK_WIDTH,
                                     2 * RET_QK_WIDTH + RET_WIDTH, 2 * RET_QK_WIDTH + 2 * RET_WIDTH], axis=-1)
    q = q.reshape(B, L, RET_HEADS, RET_DK).astype(jnp.float32)
    k = k.reshape(B, L, RET_HEADS, RET_DK).astype(jnp.float32) * (RET_DK ** -0.5)
    v = v.reshape(B, L, RET_HEADS, RET_DV).astype(jnp.float32)
    if grid:
        q = axial_rope(q)
        k = axial_rope(k)
    lg = -jnp.exp(ret_decay.astype(jnp.float32))
    s0 = s0.astype(jnp.float32)
    o_f, s_f = retention_scan(q, k, v, lg[0], s0[:, 0])
    o_b, s_b = retention_scan(q[:, ::-1], k[:, ::-1], v[:, ::-1], lg[1], s0[:, 1])
    o = o_f + o_b[:, ::-1]
    mu = jnp.mean(o, axis=-1, keepdims=True)
    var = jnp.mean(jnp.square(o - mu), axis=-1, keepdims=True)
    o = (o - mu) * lax.rsqrt(var + GN_EPS)
    ret = o.reshape(B, L, RET_WIDTH).astype(h.dtype) * jax.nn.silu(g)
    pool = multiscale_pool(p, w_pool, pool_scale, grid)
    y = jnp.concatenate([ret, pool], axis=-1) @ w_out
    return y, jnp.stack([s_f, s_b], axis=1)


def moe(h, w_router, b_router, w_gu, b_gu, w_dn, b_dn):
    B, L, D = h.shape
    t = h.reshape(B * L, D)
    logits = (t @ w_router).astype(jnp.float32) + b_router.astype(jnp.float32)
    top_v, top_i = lax.top_k(logits, TOP_K)
    probs = jax.nn.softmax(top_v, axis=-1)
    combine = jnp.sum(jax.nn.one_hot(top_i, N_EXPERTS, dtype=jnp.float32) * probs[..., None], axis=1).astype(h.dtype)

    def expert(acc, prm):
        wgu, bgu, wdn, bdn, cw = prm
        gu = t @ wgu + bgu
        gate = jnp.minimum(gu[:, :D_FF], SWIGLU_LIMIT)
        up = jnp.clip(gu[:, D_FF:], -SWIGLU_LIMIT, SWIGLU_LIMIT)
        act = (up + 1.0) * gate * jax.nn.sigmoid(SWIGLU_ALPHA * gate)
        out = act @ wdn + bdn
        return acc + out * cw[:, None], None

    acc, _ = lax.scan(expert, jnp.zeros_like(t), (w_gu, b_gu, w_dn, b_dn, combine.T))
    return acc.reshape(B, L, D)


def trunk_layer(x, cond, n1, n2, w_mod, b_mod, w_in, ret_decay, w_pool, pool_scale, w_out,
                w_router, b_router, w_gu, b_gu, w_dn, b_dn, s0, grid):
    m = jax.nn.silu(cond) @ w_mod + b_mod
    sh1, sc1, g1, sh2, sc2, g2 = [u[:, None, :] for u in jnp.split(m, 6, axis=-1)]
    h = rms_norm(x, n1) * (1.0 + sc1) + sh1
    mix, st = mixer(h, w_in, ret_decay, w_pool, pool_scale, w_out, s0, grid)
    x = x + g1 * mix
    h = rms_norm(x, n2) * (1.0 + sc2) + sh2
    x = x + g2 * moe(h, w_router, b_router, w_gu, b_gu, w_dn, b_dn)
    return x, st


def setup_inputs(seed: int = 0) -> dict:
    key = jax.random.key(seed)
    ks = jax.random.split(key, 24)
    f32 = jnp.float32
    nrm = lambda k, s: jax.random.normal(k, s, f32)
    base_decay = np.log(-np.log(1.0 - 2.0 ** (-5.0 - np.arange(RET_HEADS)))).astype(np.float32)
    return {
        'x_prompt': nrm(ks[0], (BATCH, SEQ, D_MODEL)),
        'x_sample': nrm(ks[1], (DEC_BATCH, DEC_SEQ, D_MODEL)),
        'c': nrm(ks[2], (DEC_BATCH, D_MODEL)),
        'c_ctx': nrm(ks[3], (D_MODEL,)),
        'state_ret': nrm(ks[4], (DEC_BATCH, DEPTH, N_DIRS, RET_HEADS, RET_DK, RET_DV)),
        'norm1_g': 1.0 + 0.02 * nrm(ks[5], (DEPTH, D_MODEL)),
        'norm2_g': 1.0 + 0.02 * nrm(ks[6], (DEPTH, D_MODEL)),
        'w_mod': 0.5 * D_MODEL ** -0.5 * nrm(ks[7], (DEPTH, D_MODEL, 6 * D_MODEL)),
        'b_mod': 0.01 * nrm(ks[8], (DEPTH, 6 * D_MODEL)),
        'w_in': D_MODEL ** -0.5 * nrm(ks[9], (DEPTH, D_MODEL, IN_WIDTH)),
        'ret_decay': jnp.asarray(base_decay)[None, None, :] + 0.05 * nrm(ks[10], (DEPTH, N_DIRS, RET_HEADS)),
        'w_pool': POOL_DG ** -0.5 * nrm(ks[11], (DEPTH, POOL_GROUPS, POOL_DG, POOL_DG)),
        'pool_scale': 1.0 + 0.02 * nrm(ks[12], (DEPTH, POOL_WIDTH)),
        'w_out': MIX_WIDTH ** -0.5 * nrm(ks[13], (DEPTH, MIX_WIDTH, D_MODEL)),
        'w_router': D_MODEL ** -0.5 * nrm(ks[14], (DEPTH, D_MODEL, N_EXPERTS)),
        'b_router': 0.01 * nrm(ks[15], (DEPTH, N_EXPERTS)),
        'w_gate_up': D_MODEL ** -0.5 * nrm(ks[16], (DEPTH, N_EXPERTS, D_MODEL, 2 * D_FF)),
        'b_gate_up': 0.01 * nrm(ks[17], (DEPTH, N_EXPERTS, 2 * D_FF)),
        'w_down': D_FF ** -0.5 * nrm(ks[18], (DEPTH, N_EXPERTS, D_FF, D_MODEL)),
        'b_down': 0.01 * nrm(ks[19], (DEPTH, N_EXPERTS, D_MODEL)),
        'final_g': 1.0 + 0.02 * nrm(ks[20], (D_MODEL,)),
    }


def reference(x_prompt, x_sample, c, c_ctx, state_ret, norm1_g, norm2_g, w_mod, b_mod, w_in, ret_decay,
              w_pool, pool_scale, w_out, w_router, b_router, w_gate_up, b_gate_up, w_down, b_down, final_g):
    xc = x_prompt
    xl = x_sample
    zero_state = jnp.zeros((x_prompt.shape[0], N_DIRS, RET_HEADS, RET_DK, RET_DV), jnp.float32)
    ctx_states = []
    for l in range(DEPTH):
        lw = (norm1_g[l], norm2_g[l], w_mod[l], b_mod[l], w_in[l], ret_decay[l], w_pool[l], pool_scale[l],
              w_out[l], w_router[l], b_router[l], w_gate_up[l], b_gate_up[l], w_down[l], b_down[l])
        xc, st = trunk_layer(xc, c_ctx[None, :], *lw, zero_state, False)
        ctx_states.append(st)
        xl, _ = trunk_layer(xl, c, *lw, state_ret[:, l], True)
    y_prompt = rms_norm(xc, final_g)
    y_sample = rms_norm(xl, final_g)
    new_state_ret = jnp.stack(ctx_states, axis=1)
    return (y_prompt, y_sample, new_state_ret)
```

```python
import functools

import numpy as np
import jax
import jax.numpy as jnp
from jax import lax
from jax.experimental import pallas as pl
from jax.experimental.pallas import tpu as pltpu

F32 = jnp.float32
BF16 = jnp.bfloat16

D_MODEL = 2048
BATCH = 16
SEQ = 256
DEC_BATCH = 4
DEC_SEQ = 2048
GRID_W = 64
RET_HEADS = 4
RET_DK = 128
RET_DV = 256
RET_QK_WIDTH = RET_HEADS * RET_DK
RET_WIDTH = RET_HEADS * RET_DV
POOL_WINDOWS = (2, 4, 8, 16)
POOL_GROUPS = 4
POOL_DG = 256
POOL_WIDTH = POOL_GROUPS * POOL_DG
IN_WIDTH = 2 * RET_QK_WIDTH + 2 * RET_WIDTH + POOL_WIDTH
N_EXPERTS = 32
TOP_K = 4
D_FF = D_MODEL
SWIGLU_LIMIT = 7.0
SWIGLU_ALPHA = 1.702
ROPE_BASE = 10000.0
NORM_EPS = 1e-6
GN_EPS = 1e-6

T_CTX = BATCH * SEQ
T_LAT = DEC_BATCH * DEC_SEQ
T_ALL = T_CTX + T_LAT
N_COND = 1 + DEC_BATCH
COND_ROWS = 8
LANES = 128

RET_CHUNK = 128

TM_IN = 512
TN_IN = 1024
TM_OUT = 256
TM_CMB = 256
POOL_TILE = 256
POOL_PAD = (max(POOL_WINDOWS) // 2) * GRID_W

TM_EXP = 256
F_EXP = 512
J_EXP = D_FF // F_EXP
SUP_EXP = 6
NT_EXP = (T_ALL * TOP_K + N_EXPERTS * (TM_EXP - 1)) // TM_EXP
S_EXP = J_EXP * NT_EXP
U_EXP = N_EXPERTS + NT_EXP // SUP_EXP + 1

VMEM_LIMIT = 56 * 1024 * 1024


def _cparams(sem, vmem=None):
    return pltpu.CompilerParams(dimension_semantics=sem, vmem_limit_bytes=vmem)


def _cond_row(i, tm):
    nctx = T_CTX // tm
    return jnp.where(i < nctx, 0, 1 + (i - nctx) // (DEC_SEQ // tm))


def _sigmoid(x):
    return 1.0 / (1.0 + jnp.exp(-x))


def _mod_kernel(c_ref, w_ref, b_ref, o_ref):
    c = c_ref[...]
    s = (c * _sigmoid(c)).astype(BF16)
    o_ref[...] = jnp.dot(s, w_ref[...].astype(BF16), preferred_element_type=F32) + b_ref[...]


def _modulation(cond, w_mod, b_mod):
    n = w_mod.shape[1]
    tn = 1024
    return pl.pallas_call(
        _mod_kernel,
        grid=(n // tn,),
        in_specs=[pl.BlockSpec((COND_ROWS, D_MODEL), lambda j: (0, 0)),
                  pl.BlockSpec((D_MODEL, tn), lambda j: (0, j)),
                  pl.BlockSpec((1, tn), lambda j: (0, j))],
        out_specs=pl.BlockSpec((COND_ROWS, tn), lambda j: (0, j)),
        out_shape=jax.ShapeDtypeStruct((COND_ROWS, n), F32),
        compiler_params=_cparams(("arbitrary",), 40 * 1024 * 1024),
        name="modulation",
    )(cond, w_mod, b_mod.reshape(1, n))


def _inproj_kernel(x_ref, g_ref, mod_ref, w_ref, o_ref, h_ref):
    @pl.when(pl.program_id(1) == 0)
    def _():
        x = x_ref[...]
        y = x * lax.rsqrt(jnp.mean(x * x, axis=-1, keepdims=True) + NORM_EPS) * g_ref[...]
        shift = mod_ref[0, 0:1, :]
        scale = mod_ref[0, 1:2, :]
        h_ref[...] = (y * (1.0 + scale) + shift).astype(BF16)

    o_ref[...] = jnp.dot(h_ref[...], w_ref[...], preferred_element_type=F32)


def _in_projection(x_all, norm_g, mod3, w_in_bf16):
    return pl.pallas_call(
        _inproj_kernel,
        grid=(T_ALL // TM_IN, IN_WIDTH // TN_IN),
        in_specs=[pl.BlockSpec((TM_IN, D_MODEL), lambda i, j: (i, 0)),
                  pl.BlockSpec((1, D_MODEL), lambda i, j: (0, 0)),
                  pl.BlockSpec((1, 6, D_MODEL), lambda i, j: (_cond_row(i, TM_IN), 0, 0)),
                  pl.BlockSpec((D_MODEL, TN_IN), lambda i, j: (0, j))],
        out_specs=pl.BlockSpec((TM_IN, TN_IN), lambda i, j: (i, j)),
        out_shape=jax.ShapeDtypeStruct((T_ALL, IN_WIDTH), F32),
        scratch_shapes=[pltpu.VMEM((TM_IN, D_MODEL), BF16)],
        compiler_params=_cparams(("arbitrary", "arbitrary"), 40 * 1024 * 1024),
        name="in_projection",
    )(x_all, norm_g.reshape(1, D_MODEL), mod3, w_in_bf16)


def _rope_tables(seq_len):
    t = jnp.arange(seq_len)
    row = (t // GRID_W).astype(F32)
    col = (t % GRID_W).astype(F32)
    half = RET_DK // 2
    n_freq = half // 2
    inv = ROPE_BASE ** (-jnp.arange(n_freq, dtype=F32) / n_freq)
    ang_r = row[:, None] * inv
    ang_c = col[:, None] * inv
    cos = jnp.concatenate([jnp.cos(ang_r), jnp.cos(ang_r), jnp.cos(ang_c), jnp.cos(ang_c)], axis=-1)
    sin = jnp.concatenate([-jnp.sin(ang_r), jnp.sin(ang_r), -jnp.sin(ang_c), jnp.sin(ang_c)], axis=-1)
    return cos, sin


def _retention_kernel(rd_ref, *refs, seq_len, chunk, use_rope, has_state_in, has_state_out):
    refs = list(refs)
    q_ref, k_ref, v_ref, g_ref = refs[:4]
    pos = 4
    if has_state_in:
        s0_ref = refs[pos]
        pos += 1
    if use_rope:
        cos_ref, sin_ref = refs[pos], refs[pos + 1]
        pos += 2
    o_ref = refs[pos]
    pos += 1
    if has_state_out:
        st_ref = refs[pos]
        pos += 1
    qs_ref, ks_ref, acc_ref, sf_ref, sb_ref = refs[pos:pos + 5]

    C = chunk
    n_chunks = seq_len // C
    h = pl.program_id(1)

    lgf = -jnp.exp(jnp.full((C, 1), rd_ref[0, h], F32))
    lgb = -jnp.exp(jnp.full((C, 1), rd_ref[1, h], F32))
    ii = lax.broadcasted_iota(jnp.int32, (C, C), 0)
    jj = lax.broadcasted_iota(jnp.int32, (C, C), 1)
    diff = (ii - jj).astype(F32)
    decay = (jnp.where(diff >= 0, jnp.exp(lgf * jnp.maximum(diff, 0.0)), 0.0)
             + jnp.where(diff <= 0, jnp.exp(lgb * jnp.maximum(-diff, 0.0)), 0.0))
    p = lax.broadcasted_iota(jnp.int32, (C, 1), 0).astype(F32)
    xi_f = jnp.exp(lgf * (p + 1.0))
    zeta_f = jnp.exp(lgf * (C - 1.0 - p))
    xi_b = jnp.exp(lgb * (C - p))
    zeta_b = jnp.exp(lgb * p)
    cd_f = jnp.exp(lgf[0:1, :] * C)
    cd_b = jnp.exp(lgb[0:1, :] * C)

    if has_state_in:
        sf_ref[...] = s0_ref[0, 0, 0]
        sb_ref[...] = s0_ref[0, 1, 0]
    else:
        sf_ref[...] = jnp.zeros_like(sf_ref)
        sb_ref[...] = jnp.zeros_like(sb_ref)

    lane = lax.broadcasted_iota(jnp.int32, (C, RET_DK), 1)
    first_half = (lane & 32) == 0

    def rope(x, cs, sn):
        swapped = jnp.where(first_half, pltpu.roll(x, RET_DK - 32, axis=1), pltpu.roll(x, 32, axis=1))
        return x * cs + swapped * sn

    def state_update(s_ref, kz, v, cd):
        upd = lax.dot_general(kz, v, (((0,), (0,)), ((), ())), preferred_element_type=F32)
        s_ref[...] = cd * s_ref[...] + upd

    def fwd_body(c, carry):
        sl = pl.ds(pl.multiple_of(c * C, C), C)
        q = q_ref[0, sl, :]
        k = k_ref[0, sl, :] * (RET_DK ** -0.5)
        if use_rope:
            cs = cos_ref[sl, :]
            sn = sin_ref[sl, :]
            q = rope(q, cs, sn)
            k = rope(k, cs, sn)
        qb = q.astype(BF16)
        kb = k.astype(BF16)
        qs_ref[sl, :] = qb
        ks_ref[sl, :] = k
        v = v_ref[0, sl, :].astype(BF16)
        scores = lax.dot_general(qb, kb, (((1,), (1,)), ((), ())), preferred_element_type=F32)
        inner = jnp.dot((scores * decay).astype(BF16), v, preferred_element_type=F32)
        cross = jnp.dot(qb, sf_ref[...].astype(BF16), preferred_element_type=F32) * xi_f
        acc_ref[sl, :] = inner + cross
        state_update(sf_ref, (k * zeta_f).astype(BF16), v, cd_f)
        return carry

    lax.fori_loop(0, n_chunks, fwd_body, 0)

    def bwd_body(i, carry):
        c = n_chunks - 1 - i
        sl = pl.ds(pl.multiple_of(c * C, C), C)
        qb = qs_ref[sl, :]
        k = ks_ref[sl, :]
        v = v_ref[0, sl, :].astype(BF16)
        cross = jnp.dot(qb, sb_ref[...].astype(BF16), preferred_element_type=F32) * xi_b
        o = acc_ref[sl, :] + cross
        mu = jnp.mean(o, axis=-1, keepdims=True)
        oc = o - mu
        var = jnp.mean(oc * oc, axis=-1, keepdims=True)
        on = oc * lax.rsqrt(var + GN_EPS)
        g = g_ref[0, sl, :]
        o_ref[0, sl, :] = (on * (g * _sigmoid(g))).astype(BF16)
        state_update(sb_ref, (k * zeta_b).astype(BF16), v, cd_b)
        return carry

    lax.fori_loop(0, n_chunks, bwd_body, 0)

    if has_state_out:
        st_ref[0, 0, 0] = sf_ref[...]
        st_ref[0, 1, 0] = sb_ref[...]


def _retention(proj3, ret_decay, state_in, rope_tabs, want_state):
    B, L, _ = proj3.shape
    use_rope = rope_tabs is not None
    has_state_in = state_in is not None
    kq = RET_QK_WIDTH // RET_DK
    kv = 2 * RET_QK_WIDTH // RET_DV
    kg = kv + RET_HEADS
    in_specs = [pl.BlockSpec((1, L, RET_DK), lambda b, h, rd: (b, 0, h)),
                pl.BlockSpec((1, L, RET_DK), lambda b, h, rd: (b, 0, kq + h)),
                pl.BlockSpec((1, L, RET_DV), lambda b, h, rd: (b, 0, kv + h)),
                pl.BlockSpec((1, L, RET_DV), lambda b, h, rd: (b, 0, kg + h))]
    args = [proj3, proj3, proj3, proj3]
    if has_state_in:
        in_specs.append(pl.BlockSpec((1, 2, 1, RET_DK, RET_DV), lambda b, h, rd: (b, 0, h, 0, 0)))
        args.append(state_in)
    if use_rope:
        in_specs += [pl.BlockSpec((L, RET_DK), lambda b, h, rd: (0, 0))] * 2
        args += list(rope_tabs)
    out_specs = [pl.BlockSpec((1, L, RET_DV), lambda b, h, rd: (b, 0, h))]
    out_shape = [jax.ShapeDtypeStruct((B, L, RET_WIDTH), BF16)]
    if want_state:
        out_specs.append(pl.BlockSpec((1, 2, 1, RET_DK, RET_DV), lambda b, h, rd: (b, 0, h, 0, 0)))
        out_shape.append(jax.ShapeDtypeStruct((B, 2, RET_HEADS, RET_DK, RET_DV), F32))
    kern = functools.partial(_retention_kernel, seq_len=L, chunk=RET_CHUNK, use_rope=use_rope,
                             has_state_in=has_state_in, has_state_out=want_state)
    res = pl.pallas_call(
        kern,
        grid_spec=pltpu.PrefetchScalarGridSpec(
            num_scalar_prefetch=1,
            grid=(B, RET_HEADS),
            in_specs=in_specs,
            out_specs=out_specs,
            scratch_shapes=[pltpu.VMEM((L, RET_DK), BF16),
                            pltpu.VMEM((L, RET_DK), F32),
                            pltpu.VMEM((L, RET_DV), F32),
                            pltpu.VMEM((RET_DK, RET_DV), F32),
                            pltpu.VMEM((RET_DK, RET_DV), F32)]),
        out_shape=out_shape,
        compiler_params=_cparams(("arbitrary", "arbitrary"), 40 * 1024 * 1024),
        name="retention_grid" if use_rope else "retention_seq",
    )(ret_decay, *args)
    return res if want_state else (res[0], None)


def _split3(x):
    hi = x.astype(BF16)
    r1 = x - hi.astype(F32)
    mid = r1.astype(BF16)
    lo = (r1 - mid.astype(F32)).astype(BF16)
    return jnp.concatenate([hi, mid, lo], axis=-1)


def _pool_kernel(p_ref, w_ref, sc_ref, o_ref, pad_ref, *, seq_len, grid_mode):
    PT = POOL_TILE
    n_tiles = seq_len // PT
    ii = lax.broadcasted_iota(jnp.int32, (PT, PT), 0)
    jj = lax.broadcasted_iota(jnp.int32, (PT, PT), 1)
    d = jj - ii
    ti = lax.broadcasted_iota(jnp.int32, (PT, 1), 0)

    if grid_mode:
        zeros = jnp.zeros((POOL_PAD, POOL_DG), F32)
        pad_ref[0:POOL_PAD, :] = zeros
        pad_ref[POOL_PAD + seq_len:POOL_PAD + seq_len + POOL_PAD, :] = zeros

    for gi, w in enumerate(POOL_WINDOWS):
        lo_off = -(w // 2)
        hi_off = w - w // 2 - 1
        cols = slice(gi * POOL_DG, (gi + 1) * POOL_DG)
        in_window = (d >= lo_off) & (d <= hi_off)
        if grid_mode:
            in_window = in_window & ((ii // GRID_W) == (jj // GRID_W))
        band = jnp.where(in_window, 1.0, 0.0).astype(BF16)
        wg = w_ref[gi]
        scale = sc_ref[gi]

        def window_sum(t0):
            x = p_ref[0, pl.ds(t0, PT), cols]
            s3 = jnp.dot(band, _split3(x), preferred_element_type=F32)
            return s3[:, 0:POOL_DG] + s3[:, POOL_DG:2 * POOL_DG] + s3[:, 2 * POOL_DG:3 * POOL_DG]

        def finish(t0, s, cnt):
            x = p_ref[0, pl.ds(t0, PT), cols]
            diff = (s / cnt - x).astype(BF16)
            y = jnp.dot(diff, wg, preferred_element_type=F32) * scale
            o_ref[0, pl.ds(t0, PT), cols] = y.astype(BF16)

        def count(pos, n):
            lo = jnp.maximum(pos + lo_off, 0)
            hi = jnp.minimum(pos + lo_off + w, n)
            return (hi - lo).astype(F32)

        if grid_mode:
            for t in range(n_tiles):
                pad_ref[POOL_PAD + t * PT:POOL_PAD + (t + 1) * PT, :] = window_sum(t * PT)
            for t in range(n_tiles):
                base = POOL_PAD + t * PT
                s = pad_ref[base + lo_off * GRID_W:base + lo_off * GRID_W + PT, :]
                for r in range(lo_off + 1, hi_off + 1):
                    s = s + pad_ref[base + r * GRID_W:base + r * GRID_W + PT, :]
                tok = ti + t * PT
                cnt = count(tok // GRID_W, seq_len // GRID_W) * count(tok % GRID_W, GRID_W)
                finish(t * PT, s, cnt)
        else:
            for t in range(n_tiles):
                finish(t * PT, window_sum(t * PT), count(ti + t * PT, seq_len))


def _pooling(proj3, w_pool_bf16, pool_scale, grid_mode):
    B, L, _ = proj3.shape
    if not grid_mode:
        assert L == POOL_TILE
    pcol = (IN_WIDTH - POOL_WIDTH) // POOL_WIDTH
    kern = functools.partial(_pool_kernel, seq_len=L, grid_mode=grid_mode)
    return pl.pallas_call(
        kern,
        grid=(B,),
        in_specs=[pl.BlockSpec((1, L, POOL_WIDTH), lambda b: (b, 0, pcol)),
                  pl.BlockSpec((POOL_GROUPS, POOL_DG, POOL_DG), lambda b: (0, 0, 0)),
                  pl.BlockSpec((POOL_GROUPS, 1, POOL_DG), lambda b: (0, 0, 0))],
        out_specs=pl.BlockSpec((1, L, POOL_WIDTH), lambda b: (b, 0, 0)),
        out_shape=jax.ShapeDtypeStruct((B, L, POOL_WIDTH), BF16),
        scratch_shapes=[pltpu.VMEM((L + 2 * POOL_PAD, POOL_DG), F32)],
        compiler_params=_cparams(("arbitrary",), 48 * 1024 * 1024),
        name="pool_grid" if grid_mode else "pool_seq",
    )(proj3, w_pool_bf16, pool_scale.reshape(POOL_GROUPS, 1, POOL_DG))


def _outproj_kernel(ret_ref, pool_ref, x_ref, mod_ref, g_ref, wt_ref, wb_ref, wrh_ref, wrl_ref, br_ref,
                    x1_ref, h2_ref, ri_ref, rp_ref, rr_ref, cnt_ref, run_ref):
    i = pl.program_id(0)

    @pl.when(i == 0)
    def _():
        run_ref[...] = jnp.zeros_like(run_ref)

    y = (jnp.dot(ret_ref[...], wt_ref[...], preferred_element_type=F32)
         + jnp.dot(pool_ref[...], wb_ref[...], preferred_element_type=F32))
    x1 = x_ref[...] + mod_ref[0, 2:3, :] * y
    x1_ref[...] = x1
    hn = x1 * lax.rsqrt(jnp.mean(x1 * x1, axis=-1, keepdims=True) + NORM_EPS) * g_ref[...]
    h2 = hn * (1.0 + mod_ref[0, 4:5, :]) + mod_ref[0, 3:4, :]
    hi = h2.astype(BF16)
    h2_ref[...] = hi
    lo = (h2 - hi.astype(F32)).astype(BF16)
    logits = (jnp.dot(hi, wrh_ref[...], preferred_element_type=F32)
              + jnp.dot(lo, wrh_ref[...], preferred_element_type=F32)
              + jnp.dot(hi, wrl_ref[...], preferred_element_type=F32)) + br_ref[...]

    tm = logits.shape[0]
    lane = lax.broadcasted_iota(jnp.int32, (tm, LANES), 1)
    work = logits
    vals, idxs, hots = [], [], []
    for _ in range(TOP_K):
        m = jnp.max(work, axis=-1, keepdims=True)
        idx = jnp.min(jnp.where(work == m, lane, LANES), axis=-1, keepdims=True)
        hot = lane == idx
        vals.append(m)
        idxs.append(idx)
        hots.append(hot)
        work = jnp.where(hot, -jnp.inf, work)
    exps = [jnp.exp(v - vals[0]) for v in vals]
    denom = exps[0] + exps[1] + exps[2] + exps[3]

    selected = jnp.zeros((tm, LANES), F32)
    for hot in hots:
        selected = selected + jnp.where(hot, 1.0, 0.0)
    r_i = lax.broadcasted_iota(jnp.int32, (tm, tm), 0)
    c_i = lax.broadcasted_iota(jnp.int32, (tm, tm), 1)
    tri = jnp.where(c_i < r_i, 1.0, 0.0).astype(BF16)
    before = jnp.dot(tri, selected.astype(BF16), preferred_element_type=F32) + run_ref[0:1, :]

    ri = jnp.zeros((tm, LANES), jnp.int32)
    rp = jnp.zeros((tm, LANES), F32)
    rr = jnp.zeros((tm, LANES), jnp.int32)
    for k in range(TOP_K):
        rank = jnp.sum(jnp.where(hots[k], before, 0.0), axis=-1, keepdims=True).astype(jnp.int32)
        ri = jnp.where(lane == k, idxs[k], ri)
        rp = jnp.where(lane == k, exps[k] / denom, rp)
        rr = jnp.where(lane == k, rank, rr)
    ri_ref[...] = ri
    rp_ref[...] = rp
    rr_ref[...] = rr

    run = run_ref[0:1, :] + jnp.sum(selected, axis=0, keepdims=True)
    run_ref[...] = jnp.broadcast_to(run, run_ref.shape)
    cnt_ref[...] = jnp.broadcast_to(run, cnt_ref.shape).astype(jnp.int32)


def _out_projection(ret, pool, x_all, mod3, norm_g, w_out_bf16, wr_hi, wr_lo, b_router_pad):
    tm = TM_OUT
    row = lambda i: (i, 0)
    const = lambda i: (0, 0)
    return pl.pallas_call(
        _outproj_kernel,
        grid=(T_ALL // tm,),
        in_specs=[pl.BlockSpec((tm, RET_WIDTH), row),
                  pl.BlockSpec((tm, POOL_WIDTH), row),
                  pl.BlockSpec((tm, D_MODEL), row),
                  pl.BlockSpec((1, 6, D_MODEL), lambda i: (_cond_row(i, tm), 0, 0)),
                  pl.BlockSpec((1, D_MODEL), const),
                  pl.BlockSpec((RET_WIDTH, D_MODEL), const),
                  pl.BlockSpec((POOL_WIDTH, D_MODEL), lambda i: (1, 0)),
                  pl.BlockSpec((D_MODEL, LANES), const),
                  pl.BlockSpec((D_MODEL, LANES), const),
                  pl.BlockSpec((1, LANES), const)],
        out_specs=[pl.BlockSpec((tm, D_MODEL), row),
                   pl.BlockSpec((tm, D_MODEL), row),
                   pl.BlockSpec((tm, LANES), row),
                   pl.BlockSpec((tm, LANES), row),
                   pl.BlockSpec((tm, LANES), row),
                   pl.BlockSpec((8, LANES), const)],
        out_shape=[jax.ShapeDtypeStruct((T_ALL, D_MODEL), F32),
                   jax.ShapeDtypeStruct((T_ALL, D_MODEL), BF16),
                   jax.ShapeDtypeStruct((T_ALL, LANES), jnp.int32),
                   jax.ShapeDtypeStruct((T_ALL, LANES), F32),
                   jax.ShapeDtypeStruct((T_ALL, LANES), jnp.int32),
                   jax.ShapeDtypeStruct((8, LANES), jnp.int32)],
        scratch_shapes=[pltpu.VMEM((8, LANES), F32)],
        compiler_params=_cparams(("arbitrary",), 48 * 1024 * 1024),
        name="out_projection_router",
    )(ret, pool, x_all, mod3, norm_g.reshape(1, D_MODEL), w_out_bf16, w_out_bf16, wr_hi, wr_lo, b_router_pad)


def _expert_schedule(counts):
    i32 = jnp.int32
    ntile = (counts + TM_EXP - 1) // TM_EXP
    tile_off = jnp.cumsum(ntile) - ntile
    nsup = (ntile + SUP_EXP - 1) // SUP_EXP
    sup_end = jnp.cumsum(nsup)
    u = jnp.arange(U_EXP, dtype=i32)
    sup_valid = u < sup_end[-1]
    sup_e = jnp.minimum(jnp.searchsorted(sup_end, u, side="right").astype(i32), N_EXPERTS - 1)
    sup_idx = u - (sup_end - nsup)[sup_e]
    sup_tile0 = tile_off[sup_e] + sup_idx * SUP_EXP
    sup_nt = jnp.where(sup_valid, jnp.clip(ntile[sup_e] - sup_idx * SUP_EXP, 0, SUP_EXP), 0)
    sup_steps_end = jnp.cumsum(J_EXP * sup_nt)
    total_steps = sup_steps_end[-1]
    s = jnp.arange(S_EXP, dtype=i32)
    sc = jnp.minimum(s, total_steps - 1)
    us = jnp.minimum(jnp.searchsorted(sup_steps_end, sc, side="right").astype(i32), U_EXP - 1)
    r = sc - (sup_steps_end - J_EXP * sup_nt)[us]
    nt = jnp.maximum(sup_nt[us], 1)
    step_j = r // nt
    step_t = r % nt
    step_e = sup_e[us]
    step_tile = sup_tile0[us] + step_t
    step_valid = (s < total_steps).astype(i32)
    return tile_off * TM_EXP, (step_e.astype(i32), step_j.astype(i32), step_t.astype(i32),
                               step_tile.astype(i32), step_valid)


def _expert_kernel(se_ref, sj_ref, st_ref, stile_ref, sv_ref,
                   x_ref, wg_ref, wu_ref, bg_ref, bu_ref, wd_ref, bd_ref, o_ref,
                   wg_s, wu_s, wd_s, acc_ref):
    s = pl.program_id(0)
    j = sj_ref[s]
    t = st_ref[s]

    @pl.when(sv_ref[s] == 1)
    def _():
        @pl.when(t == 0)
        def _():
            wg_s[...] = wg_ref[0].astype(BF16)
            wu_s[...] = wu_ref[0].astype(BF16)
            wd_s[...] = wd_ref[0].astype(BF16)

        x = x_ref[...]
        gate = jnp.dot(x, wg_s[...], preferred_element_type=F32) + bg_ref[0]
        up = jnp.dot(x, wu_s[...], preferred_element_type=F32) + bu_ref[0]
        gate = jnp.minimum(gate, SWIGLU_LIMIT)
        up = jnp.clip(up, -SWIGLU_LIMIT, SWIGLU_LIMIT)
        act = (up + 1.0) * gate * _sigmoid(SWIGLU_ALPHA * gate)
        part = jnp.dot(act.astype(BF16), wd_s[...], preferred_element_type=F32)
        rows = pl.ds(pl.multiple_of(t * TM_EXP, TM_EXP), TM_EXP)

        @pl.when(j == 0)
        def _():
            acc_ref[rows, :] = part

        @pl.when((j > 0) & (j < J_EXP - 1))
        def _():
            acc_ref[rows, :] += part

        @pl.when(j == J_EXP - 1)
        def _():
            o_ref[...] = (acc_ref[rows, :] + part + bd_ref[0]).astype(BF16)


def _experts(xs, sched, w_gate_up, b_gate_up, w_down, b_down):
    n_rows = NT_EXP * TM_EXP
    last = J_EXP - 1

    def out_map(s, se, sj, st, stile, sv):
        return (jnp.where(sj[s] == last, stile[s], NT_EXP), 0)

    return pl.pallas_call(
        _expert_kernel,
        grid_spec=pltpu.PrefetchScalarGridSpec(
            num_scalar_prefetch=5,
            grid=(S_EXP,),
            in_specs=[
                pl.BlockSpec((TM_EXP, D_MODEL), lambda s, se, sj, st, stile, sv: (stile[s], 0)),
                pl.BlockSpec((1, D_MODEL, F_EXP), lambda s, se, sj, st, stile, sv: (se[s], 0, sj[s])),
                pl.BlockSpec((1, D_MODEL, F_EXP), lambda s, se, sj, st, stile, sv: (se[s], 0, J_EXP + sj[s])),
                pl.BlockSpec((1, 1, F_EXP), lambda s, se, sj, st, stile, sv: (se[s], 0, sj[s])),
                pl.BlockSpec((1, 1, F_EXP), lambda s, se, sj, st, stile, sv: (se[s], 0, J_EXP + sj[s])),
                pl.BlockSpec((1, F_EXP, D_MODEL), lambda s, se, sj, st, stile, sv: (se[s], sj[s], 0)),
                pl.BlockSpec((1, 1, D_MODEL), lambda s, se, sj, st, stile, sv: (se[s], 0, 0)),
            ],
            out_specs=pl.BlockSpec((TM_EXP, D_MODEL), out_map),
            scratch_shapes=[pltpu.VMEM((D_MODEL, F_EXP), BF16),
                            pltpu.VMEM((D_MODEL, F_EXP), BF16),
                            pltpu.VMEM((F_EXP, D_MODEL), BF16),
                            pltpu.VMEM((SUP_EXP * TM_EXP, D_MODEL), F32)]),
        out_shape=jax.ShapeDtypeStruct((n_rows + TM_EXP, D_MODEL), BF16),
        compiler_params=_cparams(("arbitrary",), VMEM_LIMIT),
        name="experts",
    )(*sched, xs, w_gate_up, w_gate_up,
      b_gate_up.reshape(N_EXPERTS, 1, 2 * D_FF), b_gate_up.reshape(N_EXPERTS, 1, 2 * D_FF),
      w_down, b_down.reshape(N_EXPERTS, 1, D_MODEL))


def _combine_kernel(x1_ref, y4_ref, rp_ref, mod_ref, g_ref, o_ref):
    rp = rp_ref[...]
    moe = rp[:, 0:1] * y4_ref[:, 0:D_MODEL].astype(F32)
    for k in range(1, TOP_K):
        moe = moe + rp[:, k:k + 1] * y4_ref[:, k * D_MODEL:(k + 1) * D_MODEL].astype(F32)
    x2 = x1_ref[...] + mod_ref[0, 5:6, :] * moe
    o_ref[...] = x2 * lax.rsqrt(jnp.mean(x2 * x2, axis=-1, keepdims=True) + NORM_EPS) * g_ref[...]


def _combine(x1, y4, route_p, mod3, final_g):
    tm = TM_CMB
    return pl.pallas_call(
        _combine_kernel,
        grid=(T_ALL // tm,),
        in_specs=[pl.BlockSpec((tm, D_MODEL), lambda i: (i, 0)),
                  pl.BlockSpec((tm, TOP_K * D_MODEL), lambda i: (i, 0)),
                  pl.BlockSpec((tm, LANES), lambda i: (i, 0)),
                  pl.BlockSpec((1, 6, D_MODEL), lambda i: (_cond_row(i, tm), 0, 0)),
                  pl.BlockSpec((1, D_MODEL), lambda i: (0, 0))],
        out_specs=pl.BlockSpec((tm, D_MODEL), lambda i: (i, 0)),
        out_shape=jax.ShapeDtypeStruct((T_ALL, D_MODEL), F32),
        compiler_params=_cparams(("arbitrary",), 40 * 1024 * 1024),
        name="combine_final_norm",
    )(x1, y4, route_p, mod3, final_g.reshape(1, D_MODEL))


def kernel(x_prompt, x_sample, c, c_ctx, state_ret, norm1_g, norm2_g, w_mod, b_mod, w_in, ret_decay,
           w_pool, pool_scale, w_out, w_router, b_router, w_gate_up, b_gate_up, w_down, b_down, final_g):
    assert w_mod.shape[0] == 1, "single trunk layer"
    x_all = jnp.concatenate([x_prompt.reshape(T_CTX, D_MODEL), x_sample.reshape(T_LAT, D_MODEL)], axis=0)
    cond = jnp.zeros((COND_ROWS, D_MODEL), F32).at[0].set(c_ctx).at[1:N_COND].set(c)

    mod = _modulation(cond, w_mod[0], b_mod[0])
    mod3 = mod.reshape(COND_ROWS, 6, D_MODEL)

    proj = _in_projection(x_all, norm1_g[0], mod3, w_in[0].astype(BF16))
    proj_ctx = proj[:T_CTX].reshape(BATCH, SEQ, IN_WIDTH)
    proj_lat = proj[T_CTX:].reshape(DEC_BATCH, DEC_SEQ, IN_WIDTH)

    ret_ctx, new_state = _retention(proj_ctx, ret_decay[0], None, None, True)
    ret_lat, _ = _retention(proj_lat, ret_decay[0], state_ret[:, 0], _rope_tables(DEC_SEQ), False)
    w_pool_bf16 = w_pool[0].astype(BF16)
    pool_ctx = _pooling(proj_ctx, w_pool_bf16, pool_scale[0], False)
    pool_lat = _pooling(proj_lat, w_pool_bf16, pool_scale[0], True)
    ret = jnp.concatenate([ret_ctx.reshape(T_CTX, RET_WIDTH), ret_lat.reshape(T_LAT, RET_WIDTH)], axis=0)
    pool = jnp.concatenate([pool_ctx.reshape(T_CTX, POOL_WIDTH), pool_lat.reshape(T_LAT, POOL_WIDTH)], axis=0)

    wr = jnp.zeros((D_MODEL, LANES), F32).at[:, :N_EXPERTS].set(w_router[0])
    wr_hi = wr.astype(BF16)
    wr_lo = (wr - wr_hi.astype(F32)).astype(BF16)
    br = jnp.full((1, LANES), -1e30, F32).at[0, :N_EXPERTS].set(b_router[0])
    x1, h2, route_i, route_p, route_r, counts = _out_projection(
        ret, pool, x_all, mod3, norm2_g[0], w_out[0].astype(BF16), wr_hi, wr_lo, br)

    row_off, sched = _expert_schedule(counts[0, :N_EXPERTS])
    slot = (row_off[route_i[:, :TOP_K]] + route_r[:, :TOP_K]).reshape(-1)
    token = jnp.repeat(jnp.arange(T_ALL, dtype=jnp.int32), TOP_K)
    token_of_slot = jnp.zeros((NT_EXP * TM_EXP,), jnp.int32).at[slot].set(token)
    xs = jnp.take(h2, token_of_slot, axis=0)
    ys = _experts(xs, sched, w_gate_up[0], b_gate_up[0], w_down[0], b_down[0])
    y4 = jnp.take(ys, slot, axis=0).reshape(T_ALL, TOP_K * D_MODEL)

    y_all = _combine(x1, y4, route_p, mod3, final_g)
    y_prompt = y_all[:T_CTX].reshape(BATCH, SEQ, D_MODEL)
    y_sample = y_all[T_CTX:].reshape(DEC_BATCH, DEC_SEQ, D_MODEL)
    return (y_prompt, y_sample, new_state.reshape(BATCH, 1, 2, RET_HEADS, RET_DK, RET_DV))
```

```python
import functools

import numpy as np
import jax
import jax.numpy as jnp
from jax import lax
from jax.experimental import pallas as pl
from jax.experimental.pallas import tpu as pltpu

F32 = jnp.float32
BF16 = jnp.bfloat16

D_MODEL = 2048
BATCH = 16
SEQ = 256
DEC_BATCH = 4
DEC_SEQ = 2048
GRID_W = 64
RET_HEADS = 4
RET_DK = 128
RET_DV = 256
RET_QK_WIDTH = RET_HEADS * RET_DK
RET_WIDTH = RET_HEADS * RET_DV
POOL_WINDOWS = (2, 4, 8, 16)
POOL_GROUPS = 4
POOL_DG = 256
POOL_WIDTH = POOL_GROUPS * POOL_DG
IN_WIDTH = 2 * RET_QK_WIDTH + 2 * RET_WIDTH + POOL_WIDTH
N_EXPERTS = 32
TOP_K = 4
D_FF = D_MODEL
SWIGLU_LIMIT = 7.0
SWIGLU_ALPHA = 1.702
ROPE_BASE = 10000.0
NORM_EPS = 1e-6
GN_EPS = 1e-6

T_CTX = BATCH * SEQ
T_LAT = DEC_BATCH * DEC_SEQ
T_ALL = T_CTX + T_LAT
N_COND = 1 + DEC_BATCH
COND_ROWS = 8
LANES = 128

RET_CHUNK = 128

TM_IN = 512
TN_IN = 1024
TM_OUT = 256
TM_CMB = 256
POOL_TILE = 256
POOL_PAD = (max(POOL_WINDOWS) // 2) * GRID_W

TM_EXP = 256
F_EXP = 512
J_EXP = D_FF // F_EXP
SUP_EXP = 6
NT_EXP = (T_ALL * TOP_K + N_EXPERTS * (TM_EXP - 1)) // TM_EXP
S_EXP = J_EXP * NT_EXP
U_EXP = N_EXPERTS + NT_EXP // SUP_EXP + 1

VMEM_LIMIT = 56 * 1024 * 1024


def _cparams(sem, vmem=None):
    return pltpu.CompilerParams(dimension_semantics=sem, vmem_limit_bytes=vmem)


def _cond_row(i, tm):
    nctx = T_CTX // tm
    return jnp.where(i < nctx, 0, 1 + (i - nctx) // (DEC_SEQ // tm))


def _sigmoid(x):
    return 1.0 / (1.0 + jnp.exp(-x))


def _mod_kernel(c_ref, w_ref, b_ref, o_ref):
    c = c_ref[...]
    s = (c * _sigmoid(c)).astype(BF16)
    o_ref[...] = jnp.dot(s, w_ref[...].astype(BF16), preferred_element_type=F32) + b_ref[...]


def _modulation(cond, w_mod, b_mod):
    n = w_mod.shape[1]
    tn = 1024
    return pl.pallas_call(
        _mod_kernel,
        grid=(n // tn,),
        in_specs=[pl.BlockSpec((COND_ROWS, D_MODEL), lambda j: (0, 0)),
                  pl.BlockSpec((D_MODEL, tn), lambda j: (0, j)),
                  pl.BlockSpec((1, tn), lambda j: (0, j))],
        out_specs=pl.BlockSpec((COND_ROWS, tn), lambda j: (0, j)),
        out_shape=jax.ShapeDtypeStruct((COND_ROWS, n), F32),
        compiler_params=_cparams(("arbitrary",), 40 * 1024 * 1024),
        name="modulation",
    )(cond, w_mod, b_mod.reshape(1, n))


def _ctx_tile(i, tm):
    return jnp.minimum(i, T_CTX // tm - 1)


def _lat_tile(i, tm):
    return jnp.maximum(i - T_CTX // tm, 0)


def _inproj_kernel(xc_ref, xl_ref, g_ref, mod_ref, w_ref, o_ref, h_ref):
    def modulated_norm(x_ref):
        x = x_ref[...]
        y = x * lax.rsqrt(jnp.mean(x * x, axis=-1, keepdims=True) + NORM_EPS) * g_ref[...]
        shift = mod_ref[0, 0:1, :]
        scale = mod_ref[0, 1:2, :]
        h_ref[...] = (y * (1.0 + scale) + shift).astype(BF16)

    first = pl.program_id(1) == 0
    is_ctx = pl.program_id(0) < T_CTX // TM_IN
    pl.when(first & is_ctx)(lambda: modulated_norm(xc_ref))
    pl.when(first & jnp.logical_not(is_ctx))(lambda: modulated_norm(xl_ref))

    o_ref[...] = jnp.dot(h_ref[...], w_ref[...], preferred_element_type=F32)


def _in_projection(x_ctx, x_lat, norm_g, mod3, w_in_bf16):
    return pl.pallas_call(
        _inproj_kernel,
        grid=(T_ALL // TM_IN, IN_WIDTH // TN_IN),
        in_specs=[pl.BlockSpec((TM_IN, D_MODEL), lambda i, j: (_ctx_tile(i, TM_IN), 0)),
                  pl.BlockSpec((TM_IN, D_MODEL), lambda i, j: (_lat_tile(i, TM_IN), 0)),
                  pl.BlockSpec((1, D_MODEL), lambda i, j: (0, 0)),
                  pl.BlockSpec((1, 6, D_MODEL), lambda i, j: (_cond_row(i, TM_IN), 0, 0)),
                  pl.BlockSpec((D_MODEL, TN_IN), lambda i, j: (0, j))],
        out_specs=pl.BlockSpec((TM_IN, TN_IN), lambda i, j: (i, j)),
        out_shape=jax.ShapeDtypeStruct((T_ALL, IN_WIDTH), F32),
        scratch_shapes=[pltpu.VMEM((TM_IN, D_MODEL), BF16)],
        compiler_params=_cparams(("arbitrary", "arbitrary"), 40 * 1024 * 1024),
        name="in_projection",
    )(x_ctx, x_lat, norm_g.reshape(1, D_MODEL), mod3, w_in_bf16)


def _rope_tables(seq_len):
    t = jnp.arange(seq_len)
    row = (t // GRID_W).astype(F32)
    col = (t % GRID_W).astype(F32)
    half = RET_DK // 2
    n_freq = half // 2
    inv = ROPE_BASE ** (-jnp.arange(n_freq, dtype=F32) / n_freq)
    ang_r = row[:, None] * inv
    ang_c = col[:, None] * inv
    cos = jnp.concatenate([jnp.cos(ang_r), jnp.cos(ang_r), jnp.cos(ang_c), jnp.cos(ang_c)], axis=-1)
    sin = jnp.concatenate([-jnp.sin(ang_r), jnp.sin(ang_r), -jnp.sin(ang_c), jnp.sin(ang_c)], axis=-1)
    return cos, sin


def _retention_kernel(rd_ref, *refs, seq_len, chunk, use_rope, has_state_in, has_state_out):
    refs = list(refs)
    q_ref, k_ref, v_ref, g_ref = refs[:4]
    pos = 4
    if has_state_in:
        s0_ref = refs[pos]
        pos += 1
    if use_rope:
        cos_ref, sin_ref = refs[pos], refs[pos + 1]
        pos += 2
    o_ref = refs[pos]
    pos += 1
    if has_state_out:
        st_ref = refs[pos]
        pos += 1
    qs_ref, ks_ref, acc_ref, sf_ref, sb_ref = refs[pos:pos + 5]

    C = chunk
    n_chunks = seq_len // C
    h = pl.program_id(1)

    lgf = -jnp.exp(jnp.full((C, 1), rd_ref[0, h], F32))
    lgb = -jnp.exp(jnp.full((C, 1), rd_ref[1, h], F32))
    ii = lax.broadcasted_iota(jnp.int32, (C, C), 0)
    jj = lax.broadcasted_iota(jnp.int32, (C, C), 1)
    diff = (ii - jj).astype(F32)
    decay = (jnp.where(diff >= 0, jnp.exp(lgf * jnp.maximum(diff, 0.0)), 0.0)
             + jnp.where(diff <= 0, jnp.exp(lgb * jnp.maximum(-diff, 0.0)), 0.0))
    p = lax.broadcasted_iota(jnp.int32, (C, 1), 0).astype(F32)
    xi_f = jnp.exp(lgf * (p + 1.0))
    zeta_f = jnp.exp(lgf * (C - 1.0 - p))
    xi_b = jnp.exp(lgb * (C - p))
    zeta_b = jnp.exp(lgb * p)
    cd_f = jnp.exp(lgf[0:1, :] * C)
    cd_b = jnp.exp(lgb[0:1, :] * C)

    if has_state_in:
        sf_ref[...] = s0_ref[0, 0, 0]
        sb_ref[...] = s0_ref[0, 1, 0]
    else:
        sf_ref[...] = jnp.zeros_like(sf_ref)
        sb_ref[...] = jnp.zeros_like(sb_ref)

    lane = lax.broadcasted_iota(jnp.int32, (C, RET_DK), 1)
    first_half = (lane & 32) == 0

    def rope(x, cs, sn):
        swapped = jnp.where(first_half, pltpu.roll(x, RET_DK - 32, axis=1), pltpu.roll(x, 32, axis=1))
        return x * cs + swapped * sn

    def state_update(s_ref, kz, v, cd):
        upd = lax.dot_general(kz, v, (((0,), (0,)), ((), ())), preferred_element_type=F32)
        s_ref[...] = cd * s_ref[...] + upd

    def fwd_body(c, carry):
        sl = pl.ds(pl.multiple_of(c * C, C), C)
        q = q_ref[sl, :]
        k = k_ref[sl, :] * (RET_DK ** -0.5)
        if use_rope:
            cs = cos_ref[sl, :]
            sn = sin_ref[sl, :]
            q = rope(q, cs, sn)
            k = rope(k, cs, sn)
        qb = q.astype(BF16)
        kb = k.astype(BF16)
        qs_ref[sl, :] = qb
        ks_ref[sl, :] = k
        v = v_ref[sl, :].astype(BF16)
        scores = lax.dot_general(qb, kb, (((1,), (1,)), ((), ())), preferred_element_type=F32)
        inner = jnp.dot((scores * decay).astype(BF16), v, preferred_element_type=F32)
        cross = jnp.dot(qb, sf_ref[...].astype(BF16), preferred_element_type=F32) * xi_f
        acc_ref[sl, :] = inner + cross
        state_update(sf_ref, (k * zeta_f).astype(BF16), v, cd_f)
        return carry

    lax.fori_loop(0, n_chunks, fwd_body, 0)

    def bwd_body(i, carry):
        c = n_chunks - 1 - i
        sl = pl.ds(pl.multiple_of(c * C, C), C)
        qb = qs_ref[sl, :]
        k = ks_ref[sl, :]
        v = v_ref[sl, :].astype(BF16)
        cross = jnp.dot(qb, sb_ref[...].astype(BF16), preferred_element_type=F32) * xi_b
        o = acc_ref[sl, :] + cross
        mu = jnp.mean(o, axis=-1, keepdims=True)
        oc = o - mu
        var = jnp.mean(oc * oc, axis=-1, keepdims=True)
        on = oc * lax.rsqrt(var + GN_EPS)
        g = g_ref[sl, :]
        o_ref[sl, :] = (on * (g * _sigmoid(g))).astype(BF16)
        state_update(sb_ref, (k * zeta_b).astype(BF16), v, cd_b)
        return carry

    lax.fori_loop(0, n_chunks, bwd_body, 0)

    if has_state_out:
        st_ref[0, 0, 0] = sf_ref[...]
        st_ref[0, 1, 0] = sb_ref[...]


def _retention(proj, row0, B, L, ret_decay, state_in, rope_tabs, want_state):
    rb = row0 // L
    use_rope = rope_tabs is not None
    has_state_in = state_in is not None
    kq = RET_QK_WIDTH // RET_DK
    kv = 2 * RET_QK_WIDTH // RET_DV
    kg = kv + RET_HEADS
    in_specs = [pl.BlockSpec((L, RET_DK), lambda b, h, rd: (rb + b, h)),
                pl.BlockSpec((L, RET_DK), lambda b, h, rd: (rb + b, kq + h)),
                pl.BlockSpec((L, RET_DV), lambda b, h, rd: (rb + b, kv + h)),
                pl.BlockSpec((L, RET_DV), lambda b, h, rd: (rb + b, kg + h))]
    args = [proj, proj, proj, proj]
    if has_state_in:
        in_specs.append(pl.BlockSpec((1, 2, 1, RET_DK, RET_DV), lambda b, h, rd: (b, 0, h, 0, 0)))
        args.append(state_in)
    if use_rope:
        in_specs += [pl.BlockSpec((L, RET_DK), lambda b, h, rd: (0, 0))] * 2
        args += list(rope_tabs)
    out_specs = [pl.BlockSpec((L, RET_DV), lambda b, h, rd: (b, h))]
    out_shape = [jax.ShapeDtypeStruct((B * L, RET_WIDTH), BF16)]
    if want_state:
        out_specs.append(pl.BlockSpec((1, 2, 1, RET_DK, RET_DV), lambda b, h, rd: (b, 0, h, 0, 0)))
        out_shape.append(jax.ShapeDtypeStruct((B, 2, RET_HEADS, RET_DK, RET_DV), F32))
    kern = functools.partial(_retention_kernel, seq_len=L, chunk=RET_CHUNK, use_rope=use_rope,
                             has_state_in=has_state_in, has_state_out=want_state)
    res = pl.pallas_call(
        kern,
        grid_spec=pltpu.PrefetchScalarGridSpec(
            num_scalar_prefetch=1,
            grid=(B, RET_HEADS),
            in_specs=in_specs,
            out_specs=out_specs,
            scratch_shapes=[pltpu.VMEM((L, RET_DK), BF16),
                            pltpu.VMEM((L, RET_DK), F32),
                            pltpu.VMEM((L, RET_DV), F32),
                            pltpu.VMEM((RET_DK, RET_DV), F32),
                            pltpu.VMEM((RET_DK, RET_DV), F32)]),
        out_shape=out_shape,
        compiler_params=_cparams(("arbitrary", "arbitrary"), 40 * 1024 * 1024),
        name="retention_grid" if use_rope else "retention_seq",
    )(ret_decay, *args)
    return res if want_state else (res[0], None)


def _split3(x):
    hi = x.astype(BF16)
    r1 = x - hi.astype(F32)
    mid = r1.astype(BF16)
    lo = (r1 - mid.astype(F32)).astype(BF16)
    return jnp.concatenate([hi, mid, lo], axis=-1)


def _pool_kernel(p_ref, w_ref, sc_ref, o_ref, pad_ref, *, seq_len, grid_mode):
    PT = POOL_TILE
    n_tiles = seq_len // PT
    ii = lax.broadcasted_iota(jnp.int32, (PT, PT), 0)
    jj = lax.broadcasted_iota(jnp.int32, (PT, PT), 1)
    d = jj - ii
    ti = lax.broadcasted_iota(jnp.int32, (PT, 1), 0)

    if grid_mode:
        zeros = jnp.zeros((POOL_PAD, POOL_DG), F32)
        pad_ref[0:POOL_PAD, :] = zeros
        pad_ref[POOL_PAD + seq_len:POOL_PAD + seq_len + POOL_PAD, :] = zeros

    for gi, w in enumerate(POOL_WINDOWS):
        lo_off = -(w // 2)
        hi_off = w - w // 2 - 1
        cols = slice(gi * POOL_DG, (gi + 1) * POOL_DG)
        in_window = (d >= lo_off) & (d <= hi_off)
        if grid_mode:
            in_window = in_window & ((ii // GRID_W) == (jj // GRID_W))
        band = jnp.where(in_window, 1.0, 0.0).astype(BF16)
        wg = w_ref[gi]
        scale = sc_ref[gi]

        def window_sum(t0):
            x = p_ref[pl.ds(t0, PT), cols]
            s3 = jnp.dot(band, _split3(x), preferred_element_type=F32)
            return s3[:, 0:POOL_DG] + s3[:, POOL_DG:2 * POOL_DG] + s3[:, 2 * POOL_DG:3 * POOL_DG]

        def finish(t0, s, cnt):
            x = p_ref[pl.ds(t0, PT), cols]
            diff = (s / cnt - x).astype(BF16)
            y = jnp.dot(diff, wg, preferred_element_type=F32) * scale
            o_ref[pl.ds(t0, PT), cols] = y.astype(BF16)

        def count(pos, n):
            lo = jnp.maximum(pos + lo_off, 0)
            hi = jnp.minimum(pos + lo_off + w, n)
            return (hi - lo).astype(F32)

        if grid_mode:
            for t in range(n_tiles):
                pad_ref[POOL_PAD + t * PT:POOL_PAD + (t + 1) * PT, :] = window_sum(t * PT)
            for t in range(n_tiles):
                base = POOL_PAD + t * PT
                s = pad_ref[base + lo_off * GRID_W:base + lo_off * GRID_W + PT, :]
                for r in range(lo_off + 1, hi_off + 1):
                    s = s + pad_ref[base + r * GRID_W:base + r * GRID_W + PT, :]
                tok = ti + t * PT
                cnt = count(tok // GRID_W, seq_len // GRID_W) * count(tok % GRID_W, GRID_W)
                finish(t * PT, s, cnt)
        else:
            for t in range(n_tiles):
                finish(t * PT, window_sum(t * PT), count(ti + t * PT, seq_len))


def _pooling(proj, row0, B, L, w_pool_bf16, pool_scale, grid_mode):
    if not grid_mode:
        assert L == POOL_TILE
    rb = row0 // L
    pcol = (IN_WIDTH - POOL_WIDTH) // POOL_WIDTH
    kern = functools.partial(_pool_kernel, seq_len=L, grid_mode=grid_mode)
    return pl.pallas_call(
        kern,
        grid=(B,),
        in_specs=[pl.BlockSpec((L, POOL_WIDTH), lambda b: (rb + b, pcol)),
                  pl.BlockSpec((POOL_GROUPS, POOL_DG, POOL_DG), lambda b: (0, 0, 0)),
                  pl.BlockSpec((POOL_GROUPS, 1, POOL_DG), lambda b: (0, 0, 0))],
        out_specs=pl.BlockSpec((L, POOL_WIDTH), lambda b: (b, 0)),
        out_shape=jax.ShapeDtypeStruct((B * L, POOL_WIDTH), BF16),
        scratch_shapes=[pltpu.VMEM((L + 2 * POOL_PAD, POOL_DG), F32)],
        compiler_params=_cparams(("arbitrary",), 48 * 1024 * 1024),
        name="pool_grid" if grid_mode else "pool_seq",
    )(proj, w_pool_bf16, pool_scale.reshape(POOL_GROUPS, 1, POOL_DG))


def _outproj_kernel(retc_ref, retl_ref, poolc_ref, pooll_ref, xc_ref, xl_ref, mod_ref, g_ref,
                    wt_ref, wb_ref, wrh_ref, wrl_ref, br_ref,
                    x1_ref, h2_ref, ri_ref, rp_ref, rr_ref, cnt_ref, run_ref):
    i = pl.program_id(0)

    @pl.when(i == 0)
    def _():
        run_ref[...] = jnp.zeros_like(run_ref)

    is_ctx = i < T_CTX // TM_OUT
    ret = jnp.where(is_ctx, retc_ref[...], retl_ref[...])
    pool = jnp.where(is_ctx, poolc_ref[...], pooll_ref[...])
    x = jnp.where(is_ctx, xc_ref[...], xl_ref[...])
    y = (jnp.dot(ret, wt_ref[...], preferred_element_type=F32)
         + jnp.dot(pool, wb_ref[...], preferred_element_type=F32))
    x1 = x + mod_ref[0, 2:3, :] * y
    x1_ref[...] = x1
    hn = x1 * lax.rsqrt(jnp.mean(x1 * x1, axis=-1, keepdims=True) + NORM_EPS) * g_ref[...]
    h2 = hn * (1.0 + mod_ref[0, 4:5, :]) + mod_ref[0, 3:4, :]
    hi = h2.astype(BF16)
    h2_ref[...] = hi
    lo = (h2 - hi.astype(F32)).astype(BF16)
    logits = (jnp.dot(hi, wrh_ref[...], preferred_element_type=F32)
              + jnp.dot(lo, wrh_ref[...], preferred_element_type=F32)
              + jnp.dot(hi, wrl_ref[...], preferred_element_type=F32)) + br_ref[...]

    tm = logits.shape[0]
    lane = lax.broadcasted_iota(jnp.int32, (tm, LANES), 1)
    work = logits
    vals, idxs, hots = [], [], []
    for _ in range(TOP_K):
        m = jnp.max(work, axis=-1, keepdims=True)
        idx = jnp.min(jnp.where(work == m, lane, LANES), axis=-1, keepdims=True)
        hot = lane == idx
        vals.append(m)
        idxs.append(idx)
        hots.append(hot)
        work = jnp.where(hot, -jnp.inf, work)
    exps = [jnp.exp(v - vals[0]) for v in vals]
    denom = exps[0] + exps[1] + exps[2] + exps[3]

    selected = jnp.zeros((tm, LANES), F32)
    for hot in hots:
        selected = selected + jnp.where(hot, 1.0, 0.0)
    r_i = lax.broadcasted_iota(jnp.int32, (tm, tm), 0)
    c_i = lax.broadcasted_iota(jnp.int32, (tm, tm), 1)
    tri = jnp.where(c_i < r_i, 1.0, 0.0).astype(BF16)
    before = jnp.dot(tri, selected.astype(BF16), preferred_element_type=F32) + run_ref[0:1, :]

    ri = jnp.zeros((tm, LANES), jnp.int32)
    rp = jnp.zeros((tm, LANES), F32)
    rr = jnp.zeros((tm, LANES), jnp.int32)
    for k in range(TOP_K):
        rank = jnp.sum(jnp.where(hots[k], before, 0.0), axis=-1, keepdims=True).astype(jnp.int32)
        ri = jnp.where(lane == k, idxs[k], ri)
        rp = jnp.where(lane == k, exps[k] / denom, rp)
        rr = jnp.where(lane == k, rank, rr)
    ri_ref[...] = ri
    rp_ref[...] = rp
    rr_ref[...] = rr

    run = run_ref[0:1, :] + jnp.sum(selected, axis=0, keepdims=True)
    run_ref[...] = jnp.broadcast_to(run, run_ref.shape)
    cnt_ref[...] = jnp.broadcast_to(run, cnt_ref.shape).astype(jnp.int32)


def _out_projection(ret_ctx, ret_lat, pool_ctx, pool_lat, x_ctx, x_lat, mod3, norm_g, w_out_bf16,
                    wr_hi, wr_lo, b_router_pad):
    tm = TM_OUT
    row = lambda i: (i, 0)
    const = lambda i: (0, 0)
    crow = lambda i: (_ctx_tile(i, tm), 0)
    lrow = lambda i: (_lat_tile(i, tm), 0)
    return pl.pallas_call(
        _outproj_kernel,
        grid=(T_ALL // tm,),
        in_specs=[pl.BlockSpec((tm, RET_WIDTH), crow),
                  pl.BlockSpec((tm, RET_WIDTH), lrow),
                  pl.BlockSpec((tm, POOL_WIDTH), crow),
                  pl.BlockSpec((tm, POOL_WIDTH), lrow),
                  pl.BlockSpec((tm, D_MODEL), crow),
                  pl.BlockSpec((tm, D_MODEL), lrow),
                  pl.BlockSpec((1, 6, D_MODEL), lambda i: (_cond_row(i, tm), 0, 0)),
                  pl.BlockSpec((1, D_MODEL), const),
                  pl.BlockSpec((RET_WIDTH, D_MODEL), const),
                  pl.BlockSpec((POOL_WIDTH, D_MODEL), lambda i: (1, 0)),
                  pl.BlockSpec((D_MODEL, LANES), const),
                  pl.BlockSpec((D_MODEL, LANES), const),
                  pl.BlockSpec((1, LANES), const)],
        out_specs=[pl.BlockSpec((tm, D_MODEL), row),
                   pl.BlockSpec((tm, D_MODEL), row),
                   pl.BlockSpec((tm, LANES), row),
                   pl.BlockSpec((tm, LANES), row),
                   pl.BlockSpec((tm, LANES), row),
                   pl.BlockSpec((8, LANES), const)],
        out_shape=[jax.ShapeDtypeStruct((T_ALL, D_MODEL), F32),
                   jax.ShapeDtypeStruct((T_ALL, D_MODEL), BF16),
                   jax.ShapeDtypeStruct((T_ALL, LANES), jnp.int32),
                   jax.ShapeDtypeStruct((T_ALL, LANES), F32),
                   jax.ShapeDtypeStruct((T_ALL, LANES), jnp.int32),
                   jax.ShapeDtypeStruct((8, LANES), jnp.int32)],
        scratch_shapes=[pltpu.VMEM((8, LANES), F32)],
        compiler_params=_cparams(("arbitrary",), 48 * 1024 * 1024),
        name="out_projection_router",
    )(ret_ctx, ret_lat, pool_ctx, pool_lat, x_ctx, x_lat, mod3, norm_g.reshape(1, D_MODEL),
      w_out_bf16, w_out_bf16, wr_hi, wr_lo, b_router_pad)


def _count_le(ends, v):
    return jnp.sum((ends[None, :] <= v[:, None]).astype(jnp.int32), axis=1)


def _expert_schedule(counts):
    i32 = jnp.int32
    ntile = (counts + TM_EXP - 1) // TM_EXP
    tile_off = jnp.cumsum(ntile) - ntile
    nsup = (ntile + SUP_EXP - 1) // SUP_EXP
    sup_end = jnp.cumsum(nsup)
    u = jnp.arange(U_EXP, dtype=i32)
    sup_valid = u < sup_end[-1]
    sup_e = jnp.minimum(_count_le(sup_end, u), N_EXPERTS - 1)
    sup_idx = u - (sup_end - nsup)[sup_e]
    sup_tile0 = tile_off[sup_e] + sup_idx * SUP_EXP
    sup_nt = jnp.where(sup_valid, jnp.clip(ntile[sup_e] - sup_idx * SUP_EXP, 0, SUP_EXP), 0)
    sup_steps_end = jnp.cumsum(J_EXP * sup_nt)
    total_steps = sup_steps_end[-1]
    s = jnp.arange(S_EXP, dtype=i32)
    sc = jnp.minimum(s, total_steps - 1)
    us = jnp.minimum(_count_le(sup_steps_end, sc), U_EXP - 1)
    r = sc - (sup_steps_end - J_EXP * sup_nt)[us]
    nt = jnp.maximum(sup_nt[us], 1)
    step_j = r // nt
    step_t = r % nt
    step_e = sup_e[us]
    step_tile = sup_tile0[us] + step_t
    step_valid = (s < total_steps).astype(i32)
    return tile_off * TM_EXP, (step_e.astype(i32), step_j.astype(i32), step_t.astype(i32),
                               step_tile.astype(i32), step_valid)


def _expert_kernel(se_ref, sj_ref, st_ref, stile_ref, sv_ref,
                   x_ref, wg_ref, wu_ref, bg_ref, bu_ref, wd_ref, bd_ref, o_ref,
                   wg_s, wu_s, wd_s, acc_ref):
    s = pl.program_id(0)
    j = sj_ref[s]
    t = st_ref[s]

    @pl.when(sv_ref[s] == 1)
    def _():
        @pl.when(t == 0)
        def _():
            wg_s[...] = wg_ref[0].astype(BF16)
            wu_s[...] = wu_ref[0].astype(BF16)
            wd_s[...] = wd_ref[0].astype(BF16)

        x = x_ref[...]
        gate = jnp.dot(x, wg_s[...], preferred_element_type=F32) + bg_ref[0]
        up = jnp.dot(x, wu_s[...], preferred_element_type=F32) + bu_ref[0]
        gate = jnp.minimum(gate, SWIGLU_LIMIT)
        up = jnp.clip(up, -SWIGLU_LIMIT, SWIGLU_LIMIT)
        act = (up + 1.0) * gate * _sigmoid(SWIGLU_ALPHA * gate)
        part = jnp.dot(act.astype(BF16), wd_s[...], preferred_element_type=F32)
        rows = pl.ds(pl.multiple_of(t * TM_EXP, TM_EXP), TM_EXP)

        @pl.when(j == 0)
        def _():
            acc_ref[rows, :] = part

        @pl.when((j > 0) & (j < J_EXP - 1))
        def _():
            acc_ref[rows, :] += part

        @pl.when(j == J_EXP - 1)
        def _():
            o_ref[...] = (acc_ref[rows, :] + part + bd_ref[0]).astype(BF16)


def _experts(xs, sched, w_gate_up, b_gate_up, w_down, b_down):
    n_rows = NT_EXP * TM_EXP
    last = J_EXP - 1

    def out_map(s, se, sj, st, stile, sv):
        return (jnp.where(sj[s] == last, stile[s], NT_EXP), 0)

    return pl.pallas_call(
        _expert_kernel,
        grid_spec=pltpu.PrefetchScalarGridSpec(
            num_scalar_prefetch=5,
            grid=(S_EXP,),
            in_specs=[
                pl.BlockSpec((TM_EXP, D_MODEL), lambda s, se, sj, st, stile, sv: (stile[s], 0)),
                pl.BlockSpec((1, D_MODEL, F_EXP), lambda s, se, sj, st, stile, sv: (se[s], 0, sj[s])),
                pl.BlockSpec((1, D_MODEL, F_EXP), lambda s, se, sj, st, stile, sv: (se[s], 0, J_EXP + sj[s])),
                pl.BlockSpec((1, 1, F_EXP), lambda s, se, sj, st, stile, sv: (se[s], 0, sj[s])),
                pl.BlockSpec((1, 1, F_EXP), lambda s, se, sj, st, stile, sv: (se[s], 0, J_EXP + sj[s])),
                pl.BlockSpec((1, F_EXP, D_MODEL), lambda s, se, sj, st, stile, sv: (se[s], sj[s], 0)),
                pl.BlockSpec((1, 1, D_MODEL), lambda s, se, sj, st, stile, sv: (se[s], 0, 0)),
            ],
            out_specs=pl.BlockSpec((TM_EXP, D_MODEL), out_map),
            scratch_shapes=[pltpu.VMEM((D_MODEL, F_EXP), BF16),
                            pltpu.VMEM((D_MODEL, F_EXP), BF16),
                            pltpu.VMEM((F_EXP, D_MODEL), BF16),
                            pltpu.VMEM((SUP_EXP * TM_EXP, D_MODEL), F32)]),
        out_shape=jax.ShapeDtypeStruct((n_rows + TM_EXP, D_MODEL), BF16),
        compiler_params=_cparams(("arbitrary",), VMEM_LIMIT),
        name="experts",
    )(*sched, xs, w_gate_up, w_gate_up,
      b_gate_up.reshape(N_EXPERTS, 1, 2 * D_FF), b_gate_up.reshape(N_EXPERTS, 1, 2 * D_FF),
      w_down, b_down.reshape(N_EXPERTS, 1, D_MODEL))


def _combine_kernel(x1_ref, y0_ref, y1_ref, y2_ref, y3_ref, rp_ref, mod_ref, g_ref, oc_ref, ol_ref):
    rp = rp_ref[...]
    moe = rp[:, 0:1] * y0_ref[...].astype(F32)
    for k, y_ref in enumerate((y1_ref, y2_ref, y3_ref), start=1):
        moe = moe + rp[:, k:k + 1] * y_ref[...].astype(F32)
    x2 = x1_ref[...] + mod_ref[0, 5:6, :] * moe
    out = x2 * lax.rsqrt(jnp.mean(x2 * x2, axis=-1, keepdims=True) + NORM_EPS) * g_ref[...]
    is_ctx = pl.program_id(0) < T_CTX // TM_CMB

    @pl.when(is_ctx)
    def _():
        oc_ref[...] = out

    @pl.when(jnp.logical_not(is_ctx))
    def _():
        ol_ref[...] = out


def _combine(x1, y4, route_p, mod3, final_g):
    tm = TM_CMB
    nt = T_ALL // tm
    y_specs = [pl.BlockSpec((tm, D_MODEL), functools.partial(lambda k, i: (k * nt + i, 0), k))
               for k in range(TOP_K)]
    return pl.pallas_call(
        _combine_kernel,
        grid=(nt,),
        in_specs=[pl.BlockSpec((tm, D_MODEL), lambda i: (i, 0))] + y_specs + [
            pl.BlockSpec((tm, LANES), lambda i: (i, 0)),
            pl.BlockSpec((1, 6, D_MODEL), lambda i: (_cond_row(i, tm), 0, 0)),
            pl.BlockSpec((1, D_MODEL), lambda i: (0, 0))],
        out_specs=[pl.BlockSpec((tm, D_MODEL), lambda i: (_ctx_tile(i, tm), 0)),
                   pl.BlockSpec((tm, D_MODEL), lambda i: (_lat_tile(i, tm), 0))],
        out_shape=[jax.ShapeDtypeStruct((T_CTX, D_MODEL), F32),
                   jax.ShapeDtypeStruct((T_LAT, D_MODEL), F32)],
        compiler_params=_cparams(("arbitrary",), 40 * 1024 * 1024),
        name="combine_final_norm",
    )(x1, y4, y4, y4, y4, route_p, mod3, final_g.reshape(1, D_MODEL))


def kernel(x_prompt, x_sample, c, c_ctx, state_ret, norm1_g, norm2_g, w_mod, b_mod, w_in, ret_decay,
           w_pool, pool_scale, w_out, w_router, b_router, w_gate_up, b_gate_up, w_down, b_down, final_g):
    assert w_mod.shape[0] == 1, "single trunk layer"
    x_ctx = x_prompt.reshape(T_CTX, D_MODEL)
    x_lat = x_sample.reshape(T_LAT, D_MODEL)
    cond = jnp.zeros((COND_ROWS, D_MODEL), F32).at[0].set(c_ctx).at[1:N_COND].set(c)

    mod = _modulation(cond, w_mod[0], b_mod[0])
    mod3 = mod.reshape(COND_ROWS, 6, D_MODEL)

    proj = _in_projection(x_ctx, x_lat, norm1_g[0], mod3, w_in[0].astype(BF16))

    ret_ctx, new_state = _retention(proj, 0, BATCH, SEQ, ret_decay[0], None, None, True)
    ret_lat, _ = _retention(proj, T_CTX, DEC_BATCH, DEC_SEQ, ret_decay[0], state_ret[:, 0],
                            _rope_tables(DEC_SEQ), False)
    w_pool_bf16 = w_pool[0].astype(BF16)
    pool_ctx = _pooling(proj, 0, BATCH, SEQ, w_pool_bf16, pool_scale[0], False)
    pool_lat = _pooling(proj, T_CTX, DEC_BATCH, DEC_SEQ, w_pool_bf16, pool_scale[0], True)

    wr = jnp.zeros((D_MODEL, LANES), F32).at[:, :N_EXPERTS].set(w_router[0])
    wr_hi = wr.astype(BF16)
    wr_lo = (wr - wr_hi.astype(F32)).astype(BF16)
    br = jnp.full((1, LANES), -1e30, F32).at[0, :N_EXPERTS].set(b_router[0])
    x1, h2, route_i, route_p, route_r, counts = _out_projection(
        ret_ctx, ret_lat, pool_ctx, pool_lat, x_ctx, x_lat, mod3, norm2_g[0], w_out[0].astype(BF16),
        wr_hi, wr_lo, br)

    row_off, sched = _expert_schedule(counts[0, :N_EXPERTS])
    slot = row_off[route_i[:, :TOP_K]] + route_r[:, :TOP_K]
    token = jnp.broadcast_to(jnp.arange(T_ALL, dtype=jnp.int32)[:, None], (T_ALL, TOP_K))
    token_of_slot = jnp.zeros((NT_EXP * TM_EXP,), jnp.int32).at[slot.reshape(-1)].set(
        token.reshape(-1), unique_indices=True, mode="promise_in_bounds")
    xs = h2.at[token_of_slot].get(mode="promise_in_bounds")
    ys = _experts(xs, sched, w_gate_up[0], b_gate_up[0], w_down[0], b_down[0])
    y4 = ys.at[slot.T.reshape(-1)].get(mode="promise_in_bounds", unique_indices=True)

    y_ctx, y_lat = _combine(x1, y4, route_p, mod3, final_g)
    return (y_ctx.reshape(BATCH, SEQ, D_MODEL), y_lat.reshape(DEC_BATCH, DEC_SEQ, D_MODEL),
            new_state.reshape(BATCH, 1, 2, RET_HEADS, RET_DK, RET_DV))
```

```python
import functools

import numpy as np
import jax
import jax.numpy as jnp
from jax import lax
from jax.experimental import pallas as pl
from jax.experimental.pallas import tpu as pltpu

F32 = jnp.float32
BF16 = jnp.bfloat16

D_MODEL = 2048
BATCH = 16
SEQ = 256
DEC_BATCH = 4
DEC_SEQ = 2048
GRID_W = 64
RET_HEADS = 4
RET_DK = 128
RET_DV = 256
RET_QK_WIDTH = RET_HEADS * RET_DK
RET_WIDTH = RET_HEADS * RET_DV
POOL_WINDOWS = (2, 4, 8, 16)
POOL_GROUPS = 4
POOL_DG = 256
POOL_WIDTH = POOL_GROUPS * POOL_DG
IN_WIDTH = 2 * RET_QK_WIDTH + 2 * RET_WIDTH + POOL_WIDTH
N_EXPERTS = 32
TOP_K = 4
D_FF = D_MODEL
SWIGLU_LIMIT = 7.0
SWIGLU_ALPHA = 1.702
ROPE_BASE = 10000.0
NORM_EPS = 1e-6
GN_EPS = 1e-6

T_CTX = BATCH * SEQ
T_LAT = DEC_BATCH * DEC_SEQ
T_ALL = T_CTX + T_LAT
N_COND = 1 + DEC_BATCH
COND_ROWS = 8
LANES = 128

RET_CHUNK = 256

TM_IN = 512
TN_IN = 1024
TM_OUT = 256
TM_CMB = 256
H2_ROWS = 16384
POOL_TILE = 256
POOL_PAD = (max(POOL_WINDOWS) // 2) * GRID_W

TM_EXP = 256
F_EXP = 512
J_EXP = D_FF // F_EXP
SUP_EXP = 6
NT_EXP = (T_ALL * TOP_K + N_EXPERTS * (TM_EXP - 1)) // TM_EXP
S_EXP = J_EXP * NT_EXP
U_EXP = N_EXPERTS + NT_EXP // SUP_EXP + 1

VMEM_LIMIT = 56 * 1024 * 1024


def _cparams(sem, vmem=None):
    return pltpu.CompilerParams(dimension_semantics=sem, vmem_limit_bytes=vmem)


def _cond_row(i, tm):
    nctx = T_CTX // tm
    return jnp.where(i < nctx, 0, 1 + (i - nctx) // (DEC_SEQ // tm))


def _sigmoid(x):
    return 1.0 / (1.0 + jnp.exp(-x))


def _mod_kernel(c_ref, w_ref, b_ref, o_ref):
    c = c_ref[...]
    s = (c * _sigmoid(c)).astype(BF16)
    o_ref[...] = jnp.dot(s, w_ref[...].astype(BF16), preferred_element_type=F32) + b_ref[...]


def _modulation(cond, w_mod, b_mod):
    n = w_mod.shape[1]
    tn = 1024
    return pl.pallas_call(
        _mod_kernel,
        grid=(n // tn,),
        in_specs=[pl.BlockSpec((COND_ROWS, D_MODEL), lambda j: (0, 0)),
                  pl.BlockSpec((D_MODEL, tn), lambda j: (0, j)),
                  pl.BlockSpec((1, tn), lambda j: (0, j))],
        out_specs=pl.BlockSpec((COND_ROWS, tn), lambda j: (0, j)),
        out_shape=jax.ShapeDtypeStruct((COND_ROWS, n), F32),
        compiler_params=_cparams(("arbitrary",), 40 * 1024 * 1024),
        name="modulation",
    )(cond, w_mod, b_mod.reshape(1, n))


def _ctx_tile(i, tm):
    return jnp.minimum(i, T_CTX // tm - 1)


def _lat_tile(i, tm):
    return jnp.maximum(i - T_CTX // tm, 0)


def _inproj_kernel(xc_ref, xl_ref, g_ref, mod_ref, w_ref, o_ref, h_ref):
    def modulated_norm(x_ref):
        x = x_ref[...]
        y = x * lax.rsqrt(jnp.mean(x * x, axis=-1, keepdims=True) + NORM_EPS) * g_ref[...]
        shift = mod_ref[0, 0:1, :]
        scale = mod_ref[0, 1:2, :]
        h_ref[...] = (y * (1.0 + scale) + shift).astype(BF16)

    first = pl.program_id(1) == 0
    is_ctx = pl.program_id(0) < T_CTX // TM_IN
    pl.when(first & is_ctx)(lambda: modulated_norm(xc_ref))
    pl.when(first & jnp.logical_not(is_ctx))(lambda: modulated_norm(xl_ref))

    o_ref[...] = jnp.dot(h_ref[...], w_ref[...], preferred_element_type=F32)


def _in_projection(x_ctx, x_lat, norm_g, mod3, w_in_bf16):
    return pl.pallas_call(
        _inproj_kernel,
        grid=(T_ALL // TM_IN, IN_WIDTH // TN_IN),
        in_specs=[pl.BlockSpec((TM_IN, D_MODEL), lambda i, j: (_ctx_tile(i, TM_IN), 0)),
                  pl.BlockSpec((TM_IN, D_MODEL), lambda i, j: (_lat_tile(i, TM_IN), 0)),
                  pl.BlockSpec((1, D_MODEL), lambda i, j: (0, 0)),
                  pl.BlockSpec((1, 6, D_MODEL), lambda i, j: (_cond_row(i, TM_IN), 0, 0)),
                  pl.BlockSpec((D_MODEL, TN_IN), lambda i, j: (0, j))],
        out_specs=pl.BlockSpec((TM_IN, TN_IN), lambda i, j: (i, j)),
        out_shape=jax.ShapeDtypeStruct((T_ALL, IN_WIDTH), F32),
        scratch_shapes=[pltpu.VMEM((TM_IN, D_MODEL), BF16)],
        compiler_params=_cparams(("arbitrary", "arbitrary"), 40 * 1024 * 1024),
        name="in_projection",
    )(x_ctx, x_lat, norm_g.reshape(1, D_MODEL), mod3, w_in_bf16)


def _rope_tables(seq_len):
    t = jnp.arange(seq_len)
    row = (t // GRID_W).astype(F32)
    col = (t % GRID_W).astype(F32)
    half = RET_DK // 2
    n_freq = half // 2
    inv = ROPE_BASE ** (-jnp.arange(n_freq, dtype=F32) / n_freq)
    ang_r = row[:, None] * inv
    ang_c = col[:, None] * inv
    cos = jnp.concatenate([jnp.cos(ang_r), jnp.cos(ang_r), jnp.cos(ang_c), jnp.cos(ang_c)], axis=-1)
    sin = jnp.concatenate([-jnp.sin(ang_r), jnp.sin(ang_r), -jnp.sin(ang_c), jnp.sin(ang_c)], axis=-1)
    return cos, sin


def _retention_kernel(rd_ref, *refs, seq_len, chunk, use_rope, has_state_in, has_state_out):
    refs = list(refs)
    q_ref, k_ref, v_ref, g_ref = refs[:4]
    pos = 4
    if has_state_in:
        s0_ref = refs[pos]
        pos += 1
    if use_rope:
        cos_ref, sin_ref = refs[pos], refs[pos + 1]
        pos += 2
    o_ref = refs[pos]
    pos += 1
    if has_state_out:
        st_ref = refs[pos]
        pos += 1
    qs_ref, ks_ref, acc_ref, sf_ref, sb_ref = refs[pos:pos + 5]

    C = chunk
    n_chunks = seq_len // C
    h = pl.program_id(1)

    lgf = -jnp.exp(jnp.full((C, 1), rd_ref[0, h], F32))
    lgb = -jnp.exp(jnp.full((C, 1), rd_ref[1, h], F32))
    ii = lax.broadcasted_iota(jnp.int32, (C, C), 0)
    jj = lax.broadcasted_iota(jnp.int32, (C, C), 1)
    diff = (ii - jj).astype(F32)
    decay = (jnp.where(diff >= 0, jnp.exp(lgf * jnp.maximum(diff, 0.0)), 0.0)
             + jnp.where(diff <= 0, jnp.exp(lgb * jnp.maximum(-diff, 0.0)), 0.0))
    p = lax.broadcasted_iota(jnp.int32, (C, 1), 0).astype(F32)
    xi_f = jnp.exp(lgf * (p + 1.0))
    zeta_f = jnp.exp(lgf * (C - 1.0 - p))
    xi_b = jnp.exp(lgb * (C - p))
    zeta_b = jnp.exp(lgb * p)
    cd_f = jnp.exp(lgf[0:1, :] * C)
    cd_b = jnp.exp(lgb[0:1, :] * C)

    if has_state_in:
        sf_ref[...] = s0_ref[0, 0, 0]
        sb_ref[...] = s0_ref[0, 1, 0]
    else:
        sf_ref[...] = jnp.zeros_like(sf_ref)
        sb_ref[...] = jnp.zeros_like(sb_ref)

    lane = lax.broadcasted_iota(jnp.int32, (C, RET_DK), 1)
    first_half = (lane & 32) == 0

    def rope(x, cs, sn):
        swapped = jnp.where(first_half, pltpu.roll(x, RET_DK - 32, axis=1), pltpu.roll(x, 32, axis=1))
        return x * cs + swapped * sn

    def state_update(s_ref, kz, v, cd):
        upd = lax.dot_general(kz, v, (((0,), (0,)), ((), ())), preferred_element_type=F32)
        s_ref[...] = cd * s_ref[...] + upd

    def fwd_body(c, carry):
        sl = pl.ds(pl.multiple_of(c * C, C), C)
        q = q_ref[sl, :]
        k = k_ref[sl, :] * (RET_DK ** -0.5)
        if use_rope:
            cs = cos_ref[sl, :]
            sn = sin_ref[sl, :]
            q = rope(q, cs, sn)
            k = rope(k, cs, sn)
        qb = q.astype(BF16)
        kb = k.astype(BF16)
        qs_ref[sl, :] = qb
        ks_ref[sl, :] = k
        v = v_ref[sl, :].astype(BF16)
        scores = lax.dot_general(qb, kb, (((1,), (1,)), ((), ())), preferred_element_type=F32)
        inner = jnp.dot((scores * decay).astype(BF16), v, preferred_element_type=F32)
        cross = jnp.dot(qb, sf_ref[...].astype(BF16), preferred_element_type=F32) * xi_f
        acc_ref[sl, :] = inner + cross
        state_update(sf_ref, (k * zeta_f).astype(BF16), v, cd_f)
        return carry

    lax.fori_loop(0, n_chunks, fwd_body, 0)

    def bwd_body(i, carry):
        c = n_chunks - 1 - i
        sl = pl.ds(pl.multiple_of(c * C, C), C)
        qb = qs_ref[sl, :]
        k = ks_ref[sl, :]
        v = v_ref[sl, :].astype(BF16)
        cross = jnp.dot(qb, sb_ref[...].astype(BF16), preferred_element_type=F32) * xi_b
        o = acc_ref[sl, :] + cross
        mu = jnp.mean(o, axis=-1, keepdims=True)
        oc = o - mu
        var = jnp.mean(oc * oc, axis=-1, keepdims=True)
        on = oc * lax.rsqrt(var + GN_EPS)
        g = g_ref[sl, :]
        o_ref[sl, :] = (on * (g * _sigmoid(g))).astype(BF16)
        state_update(sb_ref, (k * zeta_b).astype(BF16), v, cd_b)
        return carry

    lax.fori_loop(0, n_chunks, bwd_body, 0)

    if has_state_out:
        st_ref[0, 0, 0] = sf_ref[...]
        st_ref[0, 1, 0] = sb_ref[...]


def _retention(proj, row0, B, L, ret_decay, state_in, rope_tabs, want_state):
    rb = row0 // L
    use_rope = rope_tabs is not None
    has_state_in = state_in is not None
    kq = RET_QK_WIDTH // RET_DK
    kv = 2 * RET_QK_WIDTH // RET_DV
    kg = kv + RET_HEADS
    in_specs = [pl.BlockSpec((L, RET_DK), lambda b, h, rd: (rb + b, h)),
                pl.BlockSpec((L, RET_DK), lambda b, h, rd: (rb + b, kq + h)),
                pl.BlockSpec((L, RET_DV), lambda b, h, rd: (rb + b, kv + h)),
                pl.BlockSpec((L, RET_DV), lambda b, h, rd: (rb + b, kg + h))]
    args = [proj, proj, proj, proj]
    if has_state_in:
        in_specs.append(pl.BlockSpec((1, 2, 1, RET_DK, RET_DV), lambda b, h, rd: (b, 0, h, 0, 0)))
        args.append(state_in)
    if use_rope:
        in_specs += [pl.BlockSpec((L, RET_DK), lambda b, h, rd: (0, 0))] * 2
        args += list(rope_tabs)
    out_specs = [pl.BlockSpec((L, RET_DV), lambda b, h, rd: (b, h))]
    out_shape = [jax.ShapeDtypeStruct((B * L, RET_WIDTH), BF16)]
    if want_state:
        out_specs.append(pl.BlockSpec((1, 2, 1, RET_DK, RET_DV), lambda b, h, rd: (b, 0, h, 0, 0)))
        out_shape.append(jax.ShapeDtypeStruct((B, 2, RET_HEADS, RET_DK, RET_DV), F32))
    kern = functools.partial(_retention_kernel, seq_len=L, chunk=RET_CHUNK, use_rope=use_rope,
                             has_state_in=has_state_in, has_state_out=want_state)
    res = pl.pallas_call(
        kern,
        grid_spec=pltpu.PrefetchScalarGridSpec(
            num_scalar_prefetch=1,
            grid=(B, RET_HEADS),
            in_specs=in_specs,
            out_specs=out_specs,
            scratch_shapes=[pltpu.VMEM((L, RET_DK), BF16),
                            pltpu.VMEM((L, RET_DK), F32),
                            pltpu.VMEM((L, RET_DV), F32),
                            pltpu.VMEM((RET_DK, RET_DV), F32),
                            pltpu.VMEM((RET_DK, RET_DV), F32)]),
        out_shape=out_shape,
        compiler_params=_cparams(("arbitrary", "arbitrary"), 40 * 1024 * 1024),
        name="retention_grid" if use_rope else "retention_seq",
    )(ret_decay, *args)
    return res if want_state else (res[0], None)


def _split3(x):
    hi = x.astype(BF16)
    r1 = x - hi.astype(F32)
    mid = r1.astype(BF16)
    lo = (r1 - mid.astype(F32)).astype(BF16)
    return jnp.concatenate([hi, mid, lo], axis=-1)


def _pool_kernel(p_ref, w_ref, sc_ref, o_ref, pad_ref, *, seq_len, grid_mode):
    PT = POOL_TILE
    n_tiles = seq_len // PT
    ii = lax.broadcasted_iota(jnp.int32, (PT, PT), 0)
    jj = lax.broadcasted_iota(jnp.int32, (PT, PT), 1)
    d = jj - ii
    ti = lax.broadcasted_iota(jnp.int32, (PT, 1), 0)

    if grid_mode:
        zeros = jnp.zeros((POOL_PAD, POOL_DG), F32)
        pad_ref[0:POOL_PAD, :] = zeros
        pad_ref[POOL_PAD + seq_len:POOL_PAD + seq_len + POOL_PAD, :] = zeros

    for gi, w in enumerate(POOL_WINDOWS):
        lo_off = -(w // 2)
        hi_off = w - w // 2 - 1
        cols = slice(gi * POOL_DG, (gi + 1) * POOL_DG)
        in_window = (d >= lo_off) & (d <= hi_off)
        if grid_mode:
            in_window = in_window & ((ii // GRID_W) == (jj // GRID_W))
        band = jnp.where(in_window, 1.0, 0.0).astype(BF16)
        wg = w_ref[gi]
        scale = sc_ref[gi]

        def window_sum(t0):
            x = p_ref[pl.ds(t0, PT), cols]
            s3 = jnp.dot(band, _split3(x), preferred_element_type=F32)
            return s3[:, 0:POOL_DG] + s3[:, POOL_DG:2 * POOL_DG] + s3[:, 2 * POOL_DG:3 * POOL_DG]

        def finish(t0, s, cnt):
            x = p_ref[pl.ds(t0, PT), cols]
            diff = (s / cnt - x).astype(BF16)
            y = jnp.dot(diff, wg, preferred_element_type=F32) * scale
            o_ref[pl.ds(t0, PT), cols] = y.astype(BF16)

        def count(pos, n):
            lo = jnp.maximum(pos + lo_off, 0)
            hi = jnp.minimum(pos + lo_off + w, n)
            return (hi - lo).astype(F32)

        if grid_mode:
            for t in range(n_tiles):
                pad_ref[POOL_PAD + t * PT:POOL_PAD + (t + 1) * PT, :] = window_sum(t * PT)
            for t in range(n_tiles):
                base = POOL_PAD + t * PT
                s = pad_ref[base + lo_off * GRID_W:base + lo_off * GRID_W + PT, :]
                for r in range(lo_off + 1, hi_off + 1):
                    s = s + pad_ref[base + r * GRID_W:base + r * GRID_W + PT, :]
                tok = ti + t * PT
                cnt = count(tok // GRID_W, seq_len // GRID_W) * count(tok % GRID_W, GRID_W)
                finish(t * PT, s, cnt)
        else:
            for t in range(n_tiles):
                finish(t * PT, window_sum(t * PT), count(ti + t * PT, seq_len))


def _pooling(proj, row0, B, L, w_pool_bf16, pool_scale, grid_mode):
    if not grid_mode:
        assert L == POOL_TILE
    rb = row0 // L
    pcol = (IN_WIDTH - POOL_WIDTH) // POOL_WIDTH
    kern = functools.partial(_pool_kernel, seq_len=L, grid_mode=grid_mode)
    return pl.pallas_call(
        kern,
        grid=(B,),
        in_specs=[pl.BlockSpec((L, POOL_WIDTH), lambda b: (rb + b, pcol)),
                  pl.BlockSpec((POOL_GROUPS, POOL_DG, POOL_DG), lambda b: (0, 0, 0)),
                  pl.BlockSpec((POOL_GROUPS, 1, POOL_DG), lambda b: (0, 0, 0))],
        out_specs=pl.BlockSpec((L, POOL_WIDTH), lambda b: (b, 0)),
        out_shape=jax.ShapeDtypeStruct((B * L, POOL_WIDTH), BF16),
        scratch_shapes=[pltpu.VMEM((L + 2 * POOL_PAD, POOL_DG), F32)],
        compiler_params=_cparams(("arbitrary",), 48 * 1024 * 1024),
        name="pool_grid" if grid_mode else "pool_seq",
    )(proj, w_pool_bf16, pool_scale.reshape(POOL_GROUPS, 1, POOL_DG))


def _outproj_kernel(retc_ref, retl_ref, poolc_ref, pooll_ref, xc_ref, xl_ref, mod_ref, g_ref,
                    wt_ref, wb_ref, wrh_ref, wrl_ref, br_ref,
                    x1_ref, h2_ref, ri_ref, rp_ref, rr_ref, cnt_ref, run_ref):
    i = pl.program_id(0)

    @pl.when(i == 0)
    def _():
        run_ref[...] = jnp.zeros_like(run_ref)

    is_ctx = i < T_CTX // TM_OUT
    ret = jnp.where(is_ctx, retc_ref[...], retl_ref[...])
    pool = jnp.where(is_ctx, poolc_ref[...], pooll_ref[...])
    x = jnp.where(is_ctx, xc_ref[...], xl_ref[...])
    y = (jnp.dot(ret, wt_ref[...], preferred_element_type=F32)
         + jnp.dot(pool, wb_ref[...], preferred_element_type=F32))
    x1 = x + mod_ref[0, 2:3, :] * y
    x1_ref[...] = x1
    hn = x1 * lax.rsqrt(jnp.mean(x1 * x1, axis=-1, keepdims=True) + NORM_EPS) * g_ref[...]
    h2 = hn * (1.0 + mod_ref[0, 4:5, :]) + mod_ref[0, 3:4, :]
    hi = h2.astype(BF16)
    h2_ref[...] = hi
    lo = (h2 - hi.astype(F32)).astype(BF16)
    logits = (jnp.dot(hi, wrh_ref[...], preferred_element_type=F32)
              + jnp.dot(lo, wrh_ref[...], preferred_element_type=F32)
              + jnp.dot(hi, wrl_ref[...], preferred_element_type=F32)) + br_ref[...]

    tm = logits.shape[0]
    lane = lax.broadcasted_iota(jnp.int32, (tm, LANES), 1)
    work = logits
    vals, idxs, hots = [], [], []
    for _ in range(TOP_K):
        m = jnp.max(work, axis=-1, keepdims=True)
        idx = jnp.min(jnp.where(work == m, lane, LANES), axis=-1, keepdims=True)
        hot = lane == idx
        vals.append(m)
        idxs.append(idx)
        hots.append(hot)
        work = jnp.where(hot, -jnp.inf, work)
    exps = [jnp.exp(v - vals[0]) for v in vals]
    denom = exps[0] + exps[1] + exps[2] + exps[3]

    selected = jnp.zeros((tm, LANES), F32)
    for hot in hots:
        selected = selected + jnp.where(hot, 1.0, 0.0)
    r_i = lax.broadcasted_iota(jnp.int32, (tm, tm), 0)
    c_i = lax.broadcasted_iota(jnp.int32, (tm, tm), 1)
    tri = jnp.where(c_i < r_i, 1.0, 0.0).astype(BF16)
    before = jnp.dot(tri, selected.astype(BF16), preferred_element_type=F32) + run_ref[0:1, :]

    ri = jnp.zeros((tm, LANES), jnp.int32)
    rp = jnp.zeros((tm, LANES), F32)
    rr = jnp.zeros((tm, LANES), jnp.int32)
    for k in range(TOP_K):
        rank = jnp.sum(jnp.where(hots[k], before, 0.0), axis=-1, keepdims=True).astype(jnp.int32)
        ri = jnp.where(lane == k, idxs[k], ri)
        rp = jnp.where(lane == k, exps[k] / denom, rp)
        rr = jnp.where(lane == k, rank, rr)
    ri_ref[...] = ri
    rp_ref[...] = rp
    rr_ref[...] = rr

    run = run_ref[0:1, :] + jnp.sum(selected, axis=0, keepdims=True)
    run_ref[...] = jnp.broadcast_to(run, run_ref.shape)
    cnt_ref[...] = jnp.broadcast_to(run, cnt_ref.shape).astype(jnp.int32)


def _out_projection(ret_ctx, ret_lat, pool_ctx, pool_lat, x_ctx, x_lat, mod3, norm_g, w_out_bf16,
                    wr_hi, wr_lo, b_router_pad):
    tm = TM_OUT
    row = lambda i: (i, 0)
    const = lambda i: (0, 0)
    crow = lambda i: (_ctx_tile(i, tm), 0)
    lrow = lambda i: (_lat_tile(i, tm), 0)
    return pl.pallas_call(
        _outproj_kernel,
        grid=(T_ALL // tm,),
        in_specs=[pl.BlockSpec((tm, RET_WIDTH), crow),
                  pl.BlockSpec((tm, RET_WIDTH), lrow),
                  pl.BlockSpec((tm, POOL_WIDTH), crow),
                  pl.BlockSpec((tm, POOL_WIDTH), lrow),
                  pl.BlockSpec((tm, D_MODEL), crow),
                  pl.BlockSpec((tm, D_MODEL), lrow),
                  pl.BlockSpec((1, 6, D_MODEL), lambda i: (_cond_row(i, tm), 0, 0)),
                  pl.BlockSpec((1, D_MODEL), const),
                  pl.BlockSpec((RET_WIDTH, D_MODEL), const),
                  pl.BlockSpec((POOL_WIDTH, D_MODEL), lambda i: (1, 0)),
                  pl.BlockSpec((D_MODEL, LANES), const),
                  pl.BlockSpec((D_MODEL, LANES), const),
                  pl.BlockSpec((1, LANES), const)],
        out_specs=[pl.BlockSpec((tm, D_MODEL), row),
                   pl.BlockSpec((tm, D_MODEL), row),
                   pl.BlockSpec((tm, LANES), row),
                   pl.BlockSpec((tm, LANES), row),
                   pl.BlockSpec((tm, LANES), row),
                   pl.BlockSpec((8, LANES), const)],
        out_shape=[jax.ShapeDtypeStruct((T_ALL, D_MODEL), F32),
                   jax.ShapeDtypeStruct((H2_ROWS, D_MODEL), BF16),
                   jax.ShapeDtypeStruct((T_ALL, LANES), jnp.int32),
                   jax.ShapeDtypeStruct((T_ALL, LANES), F32),
                   jax.ShapeDtypeStruct((T_ALL, LANES), jnp.int32),
                   jax.ShapeDtypeStruct((8, LANES), jnp.int32)],
        scratch_shapes=[pltpu.VMEM((8, LANES), F32)],
        compiler_params=_cparams(("arbitrary",), 48 * 1024 * 1024),
        name="out_projection_router",
    )(ret_ctx, ret_lat, pool_ctx, pool_lat, x_ctx, x_lat, mod3, norm_g.reshape(1, D_MODEL),
      w_out_bf16, w_out_bf16, wr_hi, wr_lo, b_router_pad)


def _count_le(ends, v):
    return jnp.sum((ends[None, :] <= v[:, None]).astype(jnp.int32), axis=1)


def _expert_schedule(counts):
    i32 = jnp.int32
    ntile = (counts + TM_EXP - 1) // TM_EXP
    tile_off = jnp.cumsum(ntile) - ntile
    nsup = (ntile + SUP_EXP - 1) // SUP_EXP
    sup_end = jnp.cumsum(nsup)
    u = jnp.arange(U_EXP, dtype=i32)
    sup_valid = u < sup_end[-1]
    sup_e = jnp.minimum(_count_le(sup_end, u), N_EXPERTS - 1)
    sup_idx = u - (sup_end - nsup)[sup_e]
    sup_tile0 = tile_off[sup_e] + sup_idx * SUP_EXP
    sup_nt = jnp.where(sup_valid, jnp.clip(ntile[sup_e] - sup_idx * SUP_EXP, 0, SUP_EXP), 0)
    sup_steps_end = jnp.cumsum(J_EXP * sup_nt)
    total_steps = sup_steps_end[-1]
    s = jnp.arange(S_EXP, dtype=i32)
    sc = jnp.minimum(s, total_steps - 1)
    us = jnp.minimum(_count_le(sup_steps_end, sc), U_EXP - 1)
    r = sc - (sup_steps_end - J_EXP * sup_nt)[us]
    nt = jnp.maximum(sup_nt[us], 1)
    step_j = r // nt
    step_t = r % nt
    step_e = sup_e[us]
    step_tile = sup_tile0[us] + step_t
    step_valid = (s < total_steps).astype(i32)
    return tile_off * TM_EXP, (step_e.astype(i32), step_j.astype(i32), step_t.astype(i32),
                               step_tile.astype(i32), step_valid)


def _expert_kernel(se_ref, sj_ref, st_ref, stile_ref, sv_ref,
                   x_ref, wg_ref, wu_ref, bg_ref, bu_ref, wd_ref, bd_ref, o_ref,
                   wg_s, wu_s, wd_s, acc_ref):
    s = pl.program_id(0)
    j = sj_ref[s]
    t = st_ref[s]

    @pl.when(sv_ref[s] == 1)
    def _():
        @pl.when(t == 0)
        def _():
            wg_s[...] = wg_ref[0].astype(BF16)
            wu_s[...] = wu_ref[0].astype(BF16)
            wd_s[...] = wd_ref[0].astype(BF16)

        x = x_ref[...]
        gate = jnp.dot(x, wg_s[...], preferred_element_type=F32) + bg_ref[0]
        up = jnp.dot(x, wu_s[...], preferred_element_type=F32) + bu_ref[0]
        gate = jnp.minimum(gate, SWIGLU_LIMIT)
        up = jnp.clip(up, -SWIGLU_LIMIT, SWIGLU_LIMIT)
        act = (up + 1.0) * gate * _sigmoid(SWIGLU_ALPHA * gate)
        part = jnp.dot(act.astype(BF16), wd_s[...], preferred_element_type=F32)
        rows = pl.ds(pl.multiple_of(t * TM_EXP, TM_EXP), TM_EXP)

        @pl.when(j == 0)
        def _():
            acc_ref[rows, :] = part

        @pl.when((j > 0) & (j < J_EXP - 1))
        def _():
            acc_ref[rows, :] += part

        @pl.when(j == J_EXP - 1)
        def _():
            o_ref[...] = (acc_ref[rows, :] + part + bd_ref[0]).astype(BF16)


def _experts(xs, sched, w_gate_up, b_gate_up, w_down, b_down):
    n_rows = NT_EXP * TM_EXP
    last = J_EXP - 1

    def out_map(s, se, sj, st, stile, sv):
        return (jnp.where(sj[s] == last, stile[s], stile[s] - st[s]), 0)

    return pl.pallas_call(
        _expert_kernel,
        grid_spec=pltpu.PrefetchScalarGridSpec(
            num_scalar_prefetch=5,
            grid=(S_EXP,),
            in_specs=[
                pl.BlockSpec((TM_EXP, D_MODEL), lambda s, se, sj, st, stile, sv: (stile[s], 0)),
                pl.BlockSpec((1, D_MODEL, F_EXP), lambda s, se, sj, st, stile, sv: (se[s], 0, sj[s])),
                pl.BlockSpec((1, D_MODEL, F_EXP), lambda s, se, sj, st, stile, sv: (se[s], 0, J_EXP + sj[s])),
                pl.BlockSpec((1, 1, F_EXP), lambda s, se, sj, st, stile, sv: (se[s], 0, sj[s])),
                pl.BlockSpec((1, 1, F_EXP), lambda s, se, sj, st, stile, sv: (se[s], 0, J_EXP + sj[s])),
                pl.BlockSpec((1, F_EXP, D_MODEL), lambda s, se, sj, st, stile, sv: (se[s], sj[s], 0)),
                pl.BlockSpec((1, 1, D_MODEL), lambda s, se, sj, st, stile, sv: (se[s], 0, 0)),
            ],
            out_specs=pl.BlockSpec((TM_EXP, D_MODEL), out_map),
            scratch_shapes=[pltpu.VMEM((D_MODEL, F_EXP), BF16),
                            pltpu.VMEM((D_MODEL, F_EXP), BF16),
                            pltpu.VMEM((F_EXP, D_MODEL), BF16),
                            pltpu.VMEM((SUP_EXP * TM_EXP, D_MODEL), F32)]),
        out_shape=jax.ShapeDtypeStruct((n_rows, D_MODEL), BF16),
        compiler_params=_cparams(("arbitrary",), VMEM_LIMIT),
        name="experts",
    )(*sched, xs, w_gate_up, w_gate_up,
      b_gate_up.reshape(N_EXPERTS, 1, 2 * D_FF), b_gate_up.reshape(N_EXPERTS, 1, 2 * D_FF),
      w_down, b_down.reshape(N_EXPERTS, 1, D_MODEL))


def _combine_kernel(x1_ref, y0_ref, y1_ref, y2_ref, y3_ref, rp_ref, mod_ref, g_ref, oc_ref, ol_ref):
    rp = rp_ref[...]
    moe = rp[:, 0:1] * y0_ref[...].astype(F32)
    for k, y_ref in enumerate((y1_ref, y2_ref, y3_ref), start=1):
        moe = moe + rp[:, k:k + 1] * y_ref[...].astype(F32)
    x2 = x1_ref[...] + mod_ref[0, 5:6, :] * moe
    out = x2 * lax.rsqrt(jnp.mean(x2 * x2, axis=-1, keepdims=True) + NORM_EPS) * g_ref[...]
    is_ctx = pl.program_id(0) < T_CTX // TM_CMB

    @pl.when(is_ctx)
    def _():
        oc_ref[...] = out

    @pl.when(jnp.logical_not(is_ctx))
    def _():
        ol_ref[...] = out


def _combine(x1, y4, route_p, mod3, final_g):
    tm = TM_CMB
    nt = T_ALL // tm
    y_specs = [pl.BlockSpec((tm, D_MODEL), functools.partial(lambda k, i: (k * nt + i, 0), k))
               for k in range(TOP_K)]
    return pl.pallas_call(
        _combine_kernel,
        grid=(nt,),
        in_specs=[pl.BlockSpec((tm, D_MODEL), lambda i: (i, 0))] + y_specs + [
            pl.BlockSpec((tm, LANES), lambda i: (i, 0)),
            pl.BlockSpec((1, 6, D_MODEL), lambda i: (_cond_row(i, tm), 0, 0)),
            pl.BlockSpec((1, D_MODEL), lambda i: (0, 0))],
        out_specs=[pl.BlockSpec((tm, D_MODEL), lambda i: (_ctx_tile(i, tm), 0)),
                   pl.BlockSpec((tm, D_MODEL), lambda i: (_lat_tile(i, tm), 0))],
        out_shape=[jax.ShapeDtypeStruct((T_CTX, D_MODEL), F32),
                   jax.ShapeDtypeStruct((T_LAT, D_MODEL), F32)],
        compiler_params=_cparams(("arbitrary",), 40 * 1024 * 1024),
        name="combine_final_norm",
    )(x1, y4, y4, y4, y4, route_p, mod3, final_g.reshape(1, D_MODEL))


def kernel(x_prompt, x_sample, c, c_ctx, state_ret, norm1_g, norm2_g, w_mod, b_mod, w_in, ret_decay,
           w_pool, pool_scale, w_out, w_router, b_router, w_gate_up, b_gate_up, w_down, b_down, final_g):
    assert w_mod.shape[0] == 1, "single trunk layer"
    x_ctx = x_prompt.reshape(T_CTX, D_MODEL)
    x_lat = x_sample.reshape(T_LAT, D_MODEL)
    cond = jnp.zeros((COND_ROWS, D_MODEL), F32).at[0].set(c_ctx).at[1:N_COND].set(c)

    mod = _modulation(cond, w_mod[0], b_mod[0])
    mod3 = mod.reshape(COND_ROWS, 6, D_MODEL)

    proj = _in_projection(x_ctx, x_lat, norm1_g[0], mod3, w_in[0].astype(BF16))

    ret_ctx, new_state = _retention(proj, 0, BATCH, SEQ, ret_decay[0], None, None, True)
    ret_lat, _ = _retention(proj, T_CTX, DEC_BATCH, DEC_SEQ, ret_decay[0], state_ret[:, 0],
                            _rope_tables(DEC_SEQ), False)
    w_pool_bf16 = w_pool[0].astype(BF16)
    pool_ctx = _pooling(proj, 0, BATCH, SEQ, w_pool_bf16, pool_scale[0], False)
    pool_lat = _pooling(proj, T_CTX, DEC_BATCH, DEC_SEQ, w_pool_bf16, pool_scale[0], True)

    wr = jnp.zeros((D_MODEL, LANES), F32).at[:, :N_EXPERTS].set(w_router[0])
    wr_hi = wr.astype(BF16)
    wr_lo = (wr - wr_hi.astype(F32)).astype(BF16)
    br = jnp.full((1, LANES), -1e30, F32).at[0, :N_EXPERTS].set(b_router[0])
    x1, h2, route_i, route_p, route_r, counts = _out_projection(
        ret_ctx, ret_lat, pool_ctx, pool_lat, x_ctx, x_lat, mod3, norm2_g[0], w_out[0].astype(BF16),
        wr_hi, wr_lo, br)

    row_off, sched = _expert_schedule(counts[0, :N_EXPERTS])
    slot = row_off[route_i[:, :TOP_K]] + route_r[:, :TOP_K]
    token = jnp.broadcast_to(jnp.arange(T_ALL, dtype=jnp.int32)[:, None], (T_ALL, TOP_K))
    token_of_slot = jnp.zeros((NT_EXP * TM_EXP,), jnp.int32).at[slot.reshape(-1)].set(
        token.reshape(-1), unique_indices=True, mode="promise_in_bounds")
    xs = h2.at[token_of_slot].get(mode="promise_in_bounds")
    ys = _experts(xs, sched, w_gate_up[0], b_gate_up[0], w_down[0], b_down[0])
    y4 = ys.at[slot.T.reshape(-1)].get(mode="promise_in_bounds", unique_indices=True)

    y_ctx, y_lat = _combine(x1, y4, route_p, mod3, final_g)
    return (y_ctx.reshape(BATCH, SEQ, D_MODEL), y_lat.reshape(DEC_BATCH, DEC_SEQ, D_MODEL),
            new_state.reshape(BATCH, 1, 2, RET_HEADS, RET_DK, RET_DV))
```

```python
import functools

import numpy as np
import jax
import jax.numpy as jnp
from jax import lax
from jax.experimental import pallas as pl
from jax.experimental.pallas import tpu as pltpu

F32 = jnp.float32
BF16 = jnp.bfloat16

D_MODEL = 2048
BATCH = 16
SEQ = 256
DEC_BATCH = 4
DEC_SEQ = 2048
GRID_W = 64
RET_HEADS = 4
RET_DK = 128
RET_DV = 256
RET_QK_WIDTH = RET_HEADS * RET_DK
RET_WIDTH = RET_HEADS * RET_DV
POOL_WINDOWS = (2, 4, 8, 16)
POOL_GROUPS = 4
POOL_DG = 256
POOL_WIDTH = POOL_GROUPS * POOL_DG
IN_WIDTH = 2 * RET_QK_WIDTH + 2 * RET_WIDTH + POOL_WIDTH
N_EXPERTS = 32
TOP_K = 4
D_FF = D_MODEL
SWIGLU_LIMIT = 7.0
SWIGLU_ALPHA = 1.702
ROPE_BASE = 10000.0
NORM_EPS = 1e-6
GN_EPS = 1e-6

T_CTX = BATCH * SEQ
T_LAT = DEC_BATCH * DEC_SEQ
T_ALL = T_CTX + T_LAT
N_COND = 1 + DEC_BATCH
COND_ROWS = 8
LANES = 128

RET_CHUNK = 256

TM_IN = 512
TN_IN = 1024
TM_OUT = 256
TM_CMB = 256
H2_ROWS = 16384
POOL_TILE = 256
POOL_PAD = (max(POOL_WINDOWS) // 2) * GRID_W

TM_EXP = 256
F_EXP = 512
J_EXP = D_FF // F_EXP
SUP_EXP = 6
NT_EXP = (T_ALL * TOP_K + N_EXPERTS * (TM_EXP - 1)) // TM_EXP
U_EXP = N_EXPERTS + NT_EXP // SUP_EXP + 1
G_EXP = J_EXP * U_EXP
XS_ROWS = (NT_EXP + SUP_EXP) * TM_EXP

VMEM_LIMIT = 56 * 1024 * 1024


def _cparams(sem, vmem=None):
    return pltpu.CompilerParams(dimension_semantics=sem, vmem_limit_bytes=vmem)


def _cond_row(i, tm):
    nctx = T_CTX // tm
    return jnp.where(i < nctx, 0, 1 + (i - nctx) // (DEC_SEQ // tm))


def _sigmoid(x):
    return 1.0 / (1.0 + jnp.exp(-x))


def _mod_kernel(c_ref, w_ref, b_ref, o_ref):
    c = c_ref[...]
    s = (c * _sigmoid(c)).astype(BF16)
    o_ref[...] = jnp.dot(s, w_ref[...].astype(BF16), preferred_element_type=F32) + b_ref[...]


def _modulation(cond, w_mod, b_mod):
    n = w_mod.shape[1]
    tn = 1024
    return pl.pallas_call(
        _mod_kernel,
        grid=(n // tn,),
        in_specs=[pl.BlockSpec((COND_ROWS, D_MODEL), lambda j: (0, 0)),
                  pl.BlockSpec((D_MODEL, tn), lambda j: (0, j)),
                  pl.BlockSpec((1, tn), lambda j: (0, j))],
        out_specs=pl.BlockSpec((COND_ROWS, tn), lambda j: (0, j)),
        out_shape=jax.ShapeDtypeStruct((COND_ROWS, n), F32),
        compiler_params=_cparams(("arbitrary",), 40 * 1024 * 1024),
        name="modulation",
    )(cond, w_mod, b_mod.reshape(1, n))


def _ctx_tile(i, tm):
    return jnp.minimum(i, T_CTX // tm - 1)


def _lat_tile(i, tm):
    return jnp.maximum(i - T_CTX // tm, 0)


def _inproj_kernel(xc_ref, xl_ref, g_ref, mod_ref, w_ref, o_ref, h_ref):
    def modulated_norm(x_ref):
        x = x_ref[...]
        y = x * lax.rsqrt(jnp.mean(x * x, axis=-1, keepdims=True) + NORM_EPS) * g_ref[...]
        shift = mod_ref[0, 0:1, :]
        scale = mod_ref[0, 1:2, :]
        h_ref[...] = (y * (1.0 + scale) + shift).astype(BF16)

    first = pl.program_id(1) == 0
    is_ctx = pl.program_id(0) < T_CTX // TM_IN
    pl.when(first & is_ctx)(lambda: modulated_norm(xc_ref))
    pl.when(first & jnp.logical_not(is_ctx))(lambda: modulated_norm(xl_ref))

    o_ref[...] = jnp.dot(h_ref[...], w_ref[...], preferred_element_type=F32)


def _in_projection(x_ctx, x_lat, norm_g, mod3, w_in_bf16):
    return pl.pallas_call(
        _inproj_kernel,
        grid=(T_ALL // TM_IN, IN_WIDTH // TN_IN),
        in_specs=[pl.BlockSpec((TM_IN, D_MODEL), lambda i, j: (_ctx_tile(i, TM_IN), 0)),
                  pl.BlockSpec((TM_IN, D_MODEL), lambda i, j: (_lat_tile(i, TM_IN), 0)),
                  pl.BlockSpec((1, D_MODEL), lambda i, j: (0, 0)),
                  pl.BlockSpec((1, 6, D_MODEL), lambda i, j: (_cond_row(i, TM_IN), 0, 0)),
                  pl.BlockSpec((D_MODEL, TN_IN), lambda i, j: (0, j))],
        out_specs=pl.BlockSpec((TM_IN, TN_IN), lambda i, j: (i, j)),
        out_shape=jax.ShapeDtypeStruct((T_ALL, IN_WIDTH), F32),
        scratch_shapes=[pltpu.VMEM((TM_IN, D_MODEL), BF16)],
        compiler_params=_cparams(("arbitrary", "arbitrary"), 40 * 1024 * 1024),
        name="in_projection",
    )(x_ctx, x_lat, norm_g.reshape(1, D_MODEL), mod3, w_in_bf16)


def _rope_tables(seq_len):
    t = jnp.arange(seq_len)
    row = (t // GRID_W).astype(F32)
    col = (t % GRID_W).astype(F32)
    half = RET_DK // 2
    n_freq = half // 2
    inv = ROPE_BASE ** (-jnp.arange(n_freq, dtype=F32) / n_freq)
    ang_r = row[:, None] * inv
    ang_c = col[:, None] * inv
    cos = jnp.concatenate([jnp.cos(ang_r), jnp.cos(ang_r), jnp.cos(ang_c), jnp.cos(ang_c)], axis=-1)
    sin = jnp.concatenate([-jnp.sin(ang_r), jnp.sin(ang_r), -jnp.sin(ang_c), jnp.sin(ang_c)], axis=-1)
    return cos, sin


def _retention_kernel(rd_ref, *refs, seq_len, chunk, use_rope, has_state_in, has_state_out):
    refs = list(refs)
    q_ref, k_ref, v_ref, g_ref = refs[:4]
    pos = 4
    if has_state_in:
        s0_ref = refs[pos]
        pos += 1
    if use_rope:
        cos_ref, sin_ref = refs[pos], refs[pos + 1]
        pos += 2
    o_ref = refs[pos]
    pos += 1
    if has_state_out:
        st_ref = refs[pos]
        pos += 1
    qs_ref, ks_ref, acc_ref, sf_ref, sb_ref = refs[pos:pos + 5]

    C = chunk
    n_chunks = seq_len // C
    h = pl.program_id(1)

    lgf = -jnp.exp(jnp.full((C, 1), rd_ref[0, h], F32))
    lgb = -jnp.exp(jnp.full((C, 1), rd_ref[1, h], F32))
    ii = lax.broadcasted_iota(jnp.int32, (C, C), 0)
    jj = lax.broadcasted_iota(jnp.int32, (C, C), 1)
    diff = (ii - jj).astype(F32)
    decay = (jnp.where(diff >= 0, jnp.exp(lgf * jnp.maximum(diff, 0.0)), 0.0)
             + jnp.where(diff <= 0, jnp.exp(lgb * jnp.maximum(-diff, 0.0)), 0.0))
    p = lax.broadcasted_iota(jnp.int32, (C, 1), 0).astype(F32)
    xi_f = jnp.exp(lgf * (p + 1.0))
    zeta_f = jnp.exp(lgf * (C - 1.0 - p))
    xi_b = jnp.exp(lgb * (C - p))
    zeta_b = jnp.exp(lgb * p)
    cd_f = jnp.exp(lgf[0:1, :] * C)
    cd_b = jnp.exp(lgb[0:1, :] * C)

    if has_state_in:
        sf_ref[...] = s0_ref[0, 0, 0]
        sb_ref[...] = s0_ref[0, 1, 0]
    else:
        sf_ref[...] = jnp.zeros_like(sf_ref)
        sb_ref[...] = jnp.zeros_like(sb_ref)

    lane = lax.broadcasted_iota(jnp.int32, (C, RET_DK), 1)
    first_half = (lane & 32) == 0

    def rope(x, cs, sn):
        swapped = jnp.where(first_half, pltpu.roll(x, RET_DK - 32, axis=1), pltpu.roll(x, 32, axis=1))
        return x * cs + swapped * sn

    def state_update(s_ref, kz, v, cd):
        upd = lax.dot_general(kz, v, (((0,), (0,)), ((), ())), preferred_element_type=F32)
        s_ref[...] = cd * s_ref[...] + upd

    def fwd_body(c, carry):
        sl = pl.ds(pl.multiple_of(c * C, C), C)
        q = q_ref[sl, :]
        k = k_ref[sl, :] * (RET_DK ** -0.5)
        if use_rope:
            cs = cos_ref[sl, :]
            sn = sin_ref[sl, :]
            q = rope(q, cs, sn)
            k = rope(k, cs, sn)
        qb = q.astype(BF16)
        kb = k.astype(BF16)
        qs_ref[sl, :] = qb
        ks_ref[sl, :] = k
        v = v_ref[sl, :].astype(BF16)
        scores = lax.dot_general(qb, kb, (((1,), (1,)), ((), ())), preferred_element_type=F32)
        inner = jnp.dot((scores * decay).astype(BF16), v, preferred_element_type=F32)
        cross = jnp.dot(qb, sf_ref[...].astype(BF16), preferred_element_type=F32) * xi_f
        acc_ref[sl, :] = inner + cross
        state_update(sf_ref, (k * zeta_f).astype(BF16), v, cd_f)
        return carry

    lax.fori_loop(0, n_chunks, fwd_body, 0)

    def bwd_body(i, carry):
        c = n_chunks - 1 - i
        sl = pl.ds(pl.multiple_of(c * C, C), C)
        qb = qs_ref[sl, :]
        k = ks_ref[sl, :]
        v = v_ref[sl, :].astype(BF16)
        cross = jnp.dot(qb, sb_ref[...].astype(BF16), preferred_element_type=F32) * xi_b
        o = acc_ref[sl, :] + cross
        mu = jnp.mean(o, axis=-1, keepdims=True)
        oc = o - mu
        var = jnp.mean(oc * oc, axis=-1, keepdims=True)
        on = oc * lax.rsqrt(var + GN_EPS)
        g = g_ref[sl, :]
        o_ref[sl, :] = (on * (g * _sigmoid(g))).astype(BF16)
        state_update(sb_ref, (k * zeta_b).astype(BF16), v, cd_b)
        return carry

    lax.fori_loop(0, n_chunks, bwd_body, 0)

    if has_state_out:
        st_ref[0, 0, 0] = sf_ref[...]
        st_ref[0, 1, 0] = sb_ref[...]


def _retention(proj, row0, B, L, ret_decay, state_in, rope_tabs, want_state):
    rb = row0 // L
    use_rope = rope_tabs is not None
    has_state_in = state_in is not None
    kq = RET_QK_WIDTH // RET_DK
    kv = 2 * RET_QK_WIDTH // RET_DV
    kg = kv + RET_HEADS
    in_specs = [pl.BlockSpec((L, RET_DK), lambda b, h, rd: (rb + b, h)),
                pl.BlockSpec((L, RET_DK), lambda b, h, rd: (rb + b, kq + h)),
                pl.BlockSpec((L, RET_DV), lambda b, h, rd: (rb + b, kv + h)),
                pl.BlockSpec((L, RET_DV), lambda b, h, rd: (rb + b, kg + h))]
    args = [proj, proj, proj, proj]
    if has_state_in:
        in_specs.append(pl.BlockSpec((1, 2, 1, RET_DK, RET_DV), lambda b, h, rd: (b, 0, h, 0, 0)))
        args.append(state_in)
    if use_rope:
        in_specs += [pl.BlockSpec((L, RET_DK), lambda b, h, rd: (0, 0))] * 2
        args += list(rope_tabs)
    out_specs = [pl.BlockSpec((L, RET_DV), lambda b, h, rd: (b, h))]
    out_shape = [jax.ShapeDtypeStruct((B * L, RET_WIDTH), BF16)]
    if want_state:
        out_specs.append(pl.BlockSpec((1, 2, 1, RET_DK, RET_DV), lambda b, h, rd: (b, 0, h, 0, 0)))
        out_shape.append(jax.ShapeDtypeStruct((B, 2, RET_HEADS, RET_DK, RET_DV), F32))
    kern = functools.partial(_retention_kernel, seq_len=L, chunk=RET_CHUNK, use_rope=use_rope,
                             has_state_in=has_state_in, has_state_out=want_state)
    res = pl.pallas_call(
        kern,
        grid_spec=pltpu.PrefetchScalarGridSpec(
            num_scalar_prefetch=1,
            grid=(B, RET_HEADS),
            in_specs=in_specs,
            out_specs=out_specs,
            scratch_shapes=[pltpu.VMEM((L, RET_DK), BF16),
                            pltpu.VMEM((L, RET_DK), F32),
                            pltpu.VMEM((L, RET_DV), F32),
                            pltpu.VMEM((RET_DK, RET_DV), F32),
                            pltpu.VMEM((RET_DK, RET_DV), F32)]),
        out_shape=out_shape,
        compiler_params=_cparams(("arbitrary", "arbitrary"), 40 * 1024 * 1024),
        name="retention_grid" if use_rope else "retention_seq",
    )(ret_decay, *args)
    return res if want_state else (res[0], None)


def _split3(x):
    hi = x.astype(BF16)
    r1 = x - hi.astype(F32)
    mid = r1.astype(BF16)
    lo = (r1 - mid.astype(F32)).astype(BF16)
    return jnp.concatenate([hi, mid, lo], axis=-1)


def _pool_kernel(p_ref, w_ref, sc_ref, o_ref, pad_ref, *, seq_len, grid_mode):
    PT = POOL_TILE
    n_tiles = seq_len // PT
    ii = lax.broadcasted_iota(jnp.int32, (PT, PT), 0)
    jj = lax.broadcasted_iota(jnp.int32, (PT, PT), 1)
    d = jj - ii
    ti = lax.broadcasted_iota(jnp.int32, (PT, 1), 0)

    if grid_mode:
        zeros = jnp.zeros((POOL_PAD, POOL_DG), F32)
        pad_ref[0:POOL_PAD, :] = zeros
        pad_ref[POOL_PAD + seq_len:POOL_PAD + seq_len + POOL_PAD, :] = zeros

    for gi, w in enumerate(POOL_WINDOWS):
        lo_off = -(w // 2)
        hi_off = w - w // 2 - 1
        cols = slice(gi * POOL_DG, (gi + 1) * POOL_DG)
        in_window = (d >= lo_off) & (d <= hi_off)
        if grid_mode:
            in_window = in_window & ((ii // GRID_W) == (jj // GRID_W))
        band = jnp.where(in_window, 1.0, 0.0).astype(BF16)
        wg = w_ref[gi]
        scale = sc_ref[gi]

        def window_sum(t0):
            x = p_ref[pl.ds(t0, PT), cols]
            s3 = jnp.dot(band, _split3(x), preferred_element_type=F32)
            return s3[:, 0:POOL_DG] + s3[:, POOL_DG:2 * POOL_DG] + s3[:, 2 * POOL_DG:3 * POOL_DG]

        def finish(t0, s, cnt):
            x = p_ref[pl.ds(t0, PT), cols]
            diff = (s / cnt - x).astype(BF16)
            y = jnp.dot(diff, wg, preferred_element_type=F32) * scale
            o_ref[pl.ds(t0, PT), cols] = y.astype(BF16)

        def count(pos, n):
            lo = jnp.maximum(pos + lo_off, 0)
            hi = jnp.minimum(pos + lo_off + w, n)
            return (hi - lo).astype(F32)

        if grid_mode:
            for t in range(n_tiles):
                pad_ref[POOL_PAD + t * PT:POOL_PAD + (t + 1) * PT, :] = window_sum(t * PT)
            for t in range(n_tiles):
                base = POOL_PAD + t * PT
                s = pad_ref[base + lo_off * GRID_W:base + lo_off * GRID_W + PT, :]
                for r in range(lo_off + 1, hi_off + 1):
                    s = s + pad_ref[base + r * GRID_W:base + r * GRID_W + PT, :]
                tok = ti + t * PT
                cnt = count(tok // GRID_W, seq_len // GRID_W) * count(tok % GRID_W, GRID_W)
                finish(t * PT, s, cnt)
        else:
            for t in range(n_tiles):
                finish(t * PT, window_sum(t * PT), count(ti + t * PT, seq_len))


def _pooling(proj, row0, B, L, w_pool_bf16, pool_scale, grid_mode):
    if not grid_mode:
        assert L == POOL_TILE
    rb = row0 // L
    pcol = (IN_WIDTH - POOL_WIDTH) // POOL_WIDTH
    kern = functools.partial(_pool_kernel, seq_len=L, grid_mode=grid_mode)
    return pl.pallas_call(
        kern,
        grid=(B,),
        in_specs=[pl.BlockSpec((L, POOL_WIDTH), lambda b: (rb + b, pcol)),
                  pl.BlockSpec((POOL_GROUPS, POOL_DG, POOL_DG), lambda b: (0, 0, 0)),
                  pl.BlockSpec((POOL_GROUPS, 1, POOL_DG), lambda b: (0, 0, 0))],
        out_specs=pl.BlockSpec((L, POOL_WIDTH), lambda b: (b, 0)),
        out_shape=jax.ShapeDtypeStruct((B * L, POOL_WIDTH), BF16),
        scratch_shapes=[pltpu.VMEM((L + 2 * POOL_PAD, POOL_DG), F32)],
        compiler_params=_cparams(("arbitrary",), 48 * 1024 * 1024),
        name="pool_grid" if grid_mode else "pool_seq",
    )(proj, w_pool_bf16, pool_scale.reshape(POOL_GROUPS, 1, POOL_DG))


def _outproj_kernel(*refs):
    i = pl.program_id(0)
    h2_ref = refs[14]

    @pl.when(i >= T_ALL // TM_OUT)
    def _():
        h2_ref[...] = jnp.zeros_like(h2_ref)

    pl.when(i < T_ALL // TM_OUT)(lambda: _outproj_tile(*refs))


def _outproj_tile(retc_ref, retl_ref, poolc_ref, pooll_ref, xc_ref, xl_ref, mod_ref, g_ref,
                  wt_ref, wb_ref, wrh_ref, wrl_ref, br_ref,
                  x1_ref, h2_ref, ri_ref, rp_ref, rr_ref, cnt_ref, run_ref):
    i = pl.program_id(0)

    @pl.when(i == 0)
    def _():
        run_ref[...] = jnp.zeros_like(run_ref)

    is_ctx = i < T_CTX // TM_OUT
    ret = jnp.where(is_ctx, retc_ref[...], retl_ref[...])
    pool = jnp.where(is_ctx, poolc_ref[...], pooll_ref[...])
    x = jnp.where(is_ctx, xc_ref[...], xl_ref[...])
    y = (jnp.dot(ret, wt_ref[...], preferred_element_type=F32)
         + jnp.dot(pool, wb_ref[...], preferred_element_type=F32))
    x1 = x + mod_ref[0, 2:3, :] * y
    x1_ref[...] = x1
    hn = x1 * lax.rsqrt(jnp.mean(x1 * x1, axis=-1, keepdims=True) + NORM_EPS) * g_ref[...]
    h2 = hn * (1.0 + mod_ref[0, 4:5, :]) + mod_ref[0, 3:4, :]
    hi = h2.astype(BF16)
    h2_ref[...] = hi
    lo = (h2 - hi.astype(F32)).astype(BF16)
    logits = (jnp.dot(hi, wrh_ref[...], preferred_element_type=F32)
              + jnp.dot(lo, wrh_ref[...], preferred_element_type=F32)
              + jnp.dot(hi, wrl_ref[...], preferred_element_type=F32)) + br_ref[...]

    tm = logits.shape[0]
    lane = lax.broadcasted_iota(jnp.int32, (tm, LANES), 1)
    work = logits
    vals, idxs, hots = [], [], []
    for _ in range(TOP_K):
        m = jnp.max(work, axis=-1, keepdims=True)
        idx = jnp.min(jnp.where(work == m, lane, LANES), axis=-1, keepdims=True)
        hot = lane == idx
        vals.append(m)
        idxs.append(idx)
        hots.append(hot)
        work = jnp.where(hot, -jnp.inf, work)
    exps = [jnp.exp(v - vals[0]) for v in vals]
    denom = exps[0] + exps[1] + exps[2] + exps[3]

    selected = jnp.zeros((tm, LANES), F32)
    for hot in hots:
        selected = selected + jnp.where(hot, 1.0, 0.0)
    r_i = lax.broadcasted_iota(jnp.int32, (tm, tm), 0)
    c_i = lax.broadcasted_iota(jnp.int32, (tm, tm), 1)
    tri = jnp.where(c_i < r_i, 1.0, 0.0).astype(BF16)
    before = jnp.dot(tri, selected.astype(BF16), preferred_element_type=F32) + run_ref[0:1, :]

    ri = jnp.zeros((tm, LANES), jnp.int32)
    rp = jnp.zeros((tm, LANES), F32)
    rr = jnp.zeros((tm, LANES), jnp.int32)
    for k in range(TOP_K):
        rank = jnp.sum(jnp.where(hots[k], before, 0.0), axis=-1, keepdims=True).astype(jnp.int32)
        ri = jnp.where(lane == k, idxs[k], ri)
        rp = jnp.where(lane == k, exps[k] / denom, rp)
        rr = jnp.where(lane == k, rank, rr)
    ri_ref[...] = ri
    rp_ref[...] = rp
    rr_ref[...] = rr

    run = run_ref[0:1, :] + jnp.sum(selected, axis=0, keepdims=True)
    run_ref[...] = jnp.broadcast_to(run, run_ref.shape)
    cnt_ref[...] = jnp.broadcast_to(run, cnt_ref.shape).astype(jnp.int32)


def _out_projection(ret_ctx, ret_lat, pool_ctx, pool_lat, x_ctx, x_lat, mod3, norm_g, w_out_bf16,
                    wr_hi, wr_lo, b_router_pad):
    tm = TM_OUT
    real = lambda i: jnp.minimum(i, T_ALL // tm - 1)
    row = lambda i: (real(i), 0)
    const = lambda i: (0, 0)
    crow = lambda i: (_ctx_tile(real(i), tm), 0)
    lrow = lambda i: (_lat_tile(real(i), tm), 0)
    return pl.pallas_call(
        _outproj_kernel,
        grid=(H2_ROWS // tm,),
        in_specs=[pl.BlockSpec((tm, RET_WIDTH), crow),
                  pl.BlockSpec((tm, RET_WIDTH), lrow),
                  pl.BlockSpec((tm, POOL_WIDTH), crow),
                  pl.BlockSpec((tm, POOL_WIDTH), lrow),
                  pl.BlockSpec((tm, D_MODEL), crow),
                  pl.BlockSpec((tm, D_MODEL), lrow),
                  pl.BlockSpec((1, 6, D_MODEL), lambda i: (_cond_row(real(i), tm), 0, 0)),
                  pl.BlockSpec((1, D_MODEL), const),
                  pl.BlockSpec((RET_WIDTH, D_MODEL), const),
                  pl.BlockSpec((POOL_WIDTH, D_MODEL), lambda i: (1, 0)),
                  pl.BlockSpec((D_MODEL, LANES), const),
                  pl.BlockSpec((D_MODEL, LANES), const),
                  pl.BlockSpec((1, LANES), const)],
        out_specs=[pl.BlockSpec((tm, D_MODEL), row),
                   pl.BlockSpec((tm, D_MODEL), lambda i: (i, 0)),
                   pl.BlockSpec((tm, LANES), row),
                   pl.BlockSpec((tm, LANES), row),
                   pl.BlockSpec((tm, LANES), row),
                   pl.BlockSpec((8, LANES), const)],
        out_shape=[jax.ShapeDtypeStruct((T_ALL, D_MODEL), F32),
                   jax.ShapeDtypeStruct((H2_ROWS, D_MODEL), BF16),
                   jax.ShapeDtypeStruct((T_ALL, LANES), jnp.int32),
                   jax.ShapeDtypeStruct((T_ALL, LANES), F32),
                   jax.ShapeDtypeStruct((T_ALL, LANES), jnp.int32),
                   jax.ShapeDtypeStruct((8, LANES), jnp.int32)],
        scratch_shapes=[pltpu.VMEM((8, LANES), F32)],
        compiler_params=_cparams(("arbitrary",), 48 * 1024 * 1024),
        name="out_projection_router",
    )(ret_ctx, ret_lat, pool_ctx, pool_lat, x_ctx, x_lat, mod3, norm_g.reshape(1, D_MODEL),
      w_out_bf16, w_out_bf16, wr_hi, wr_lo, b_router_pad)


def _count_le(ends, v):
    return jnp.sum((ends[None, :] <= v[:, None]).astype(jnp.int32), axis=1)


def _expert_schedule(counts):
    i32 = jnp.int32
    ntile = (counts + TM_EXP - 1) // TM_EXP
    tile_off = jnp.cumsum(ntile) - ntile
    nsup = (ntile + SUP_EXP - 1) // SUP_EXP
    sup_end = jnp.cumsum(nsup)
    u = jnp.arange(U_EXP, dtype=i32)
    sup_valid = u < sup_end[-1]
    uc = jnp.minimum(u, sup_end[-1] - 1)
    sup_e = jnp.minimum(_count_le(sup_end, uc), N_EXPERTS - 1)
    sup_idx = uc - (sup_end - nsup)[sup_e]
    sup_row0 = (tile_off[sup_e] + sup_idx * SUP_EXP) * TM_EXP
    sup_nt = jnp.clip(ntile[sup_e] - sup_idx * SUP_EXP, 1, SUP_EXP)
    step_e = jnp.repeat(sup_e, J_EXP)
    step_valid = jnp.repeat(sup_valid.astype(i32), J_EXP)
    step_j = jnp.where(step_valid == 1, jnp.tile(jnp.arange(J_EXP, dtype=i32), U_EXP), J_EXP - 1)
    step_row0 = jnp.repeat(sup_row0, J_EXP)
    step_nt = jnp.repeat(sup_nt, J_EXP)
    return tile_off * TM_EXP, (step_e.astype(i32), step_j.astype(i32), step_row0.astype(i32),
                               step_nt.astype(i32), step_valid, jnp.sum(ntile).astype(i32).reshape(1))


def _expert_kernel(se_ref, sj_ref, srow_ref, snt_ref, sv_ref, stot_ref,
                   xs_hbm, wg_ref, wu_ref, bg_ref, bu_ref, wd_ref, bd_ref, ys_hbm,
                   wg_s, wu_s, wd_s, acc_ref, x_buf, o_buf, x_sem, o_sem):
    s = pl.program_id(0)
    j = sj_ref[s]
    n_tiles = snt_ref[s]
    row0 = pl.multiple_of(srow_ref[s], TM_EXP)
    last = J_EXP - 1

    def x_copy():
        return pltpu.make_async_copy(xs_hbm.at[pl.ds(row0, SUP_EXP * TM_EXP)], x_buf, x_sem)

    def out_copy(t):
        rows = pl.ds(pl.multiple_of(row0 + t * TM_EXP, TM_EXP), TM_EXP)
        return pltpu.make_async_copy(o_buf.at[t % 2], ys_hbm.at[rows], o_sem.at[t % 2])

    @pl.when(s == 0)
    def _():
        o_buf[0] = jnp.zeros((TM_EXP, D_MODEL), BF16)

        def zero_copy(t):
            rows = pl.ds(pl.multiple_of(t * TM_EXP, TM_EXP), TM_EXP)
            return pltpu.make_async_copy(o_buf.at[0], ys_hbm.at[rows], o_sem.at[0])

        def start(t, carry):
            zero_copy(t).start()
            return carry

        def wait(t, carry):
            zero_copy(t).wait()
            return carry

        lax.fori_loop(stot_ref[0], NT_EXP, start, 0)
        lax.fori_loop(stot_ref[0], NT_EXP, wait, 0)

    @pl.when(sv_ref[s] == 1)
    def _():
        @pl.when(j == 0)
        def _():
            x_copy().start()

        wg_s[...] = wg_ref[0].astype(BF16)
        wu_s[...] = wu_ref[0].astype(BF16)
        wd_s[...] = wd_ref[0].astype(BF16)

        @pl.when(j == 0)
        def _():
            x_copy().wait()

        def tile(t, carry):
            rows = pl.ds(pl.multiple_of(t * TM_EXP, TM_EXP), TM_EXP)
            x = x_buf[rows, :]
            gate = jnp.dot(x, wg_s[...], preferred_element_type=F32) + bg_ref[0]
            up = jnp.dot(x, wu_s[...], preferred_element_type=F32) + bu_ref[0]
            gate = jnp.minimum(gate, SWIGLU_LIMIT)
            up = jnp.clip(up, -SWIGLU_LIMIT, SWIGLU_LIMIT)
            act = (up + 1.0) * gate * _sigmoid(SWIGLU_ALPHA * gate)
            part = jnp.dot(act.astype(BF16), wd_s[...], preferred_element_type=F32)

            @pl.when(j == 0)
            def _():
                acc_ref[rows, :] = part

            @pl.when((j > 0) & (j < last))
            def _():
                acc_ref[rows, :] += part

            @pl.when(j == last)
            def _():
                @pl.when(t >= 2)
                def _():
                    out_copy(t - 2).wait()

                o_buf[t % 2] = (acc_ref[rows, :] + part + bd_ref[0]).astype(BF16)
                out_copy(t).start()

            return carry

        lax.fori_loop(0, n_tiles, tile, 0)

        @pl.when(j == last)
        def _():
            @pl.when(n_tiles >= 2)
            def _():
                out_copy(n_tiles - 2).wait()

            out_copy(n_tiles - 1).wait()


def _experts(xs, sched, w_gate_up, b_gate_up, w_down, b_down):
    wmap = lambda col0: (lambda s, se, sj, srow, snt, sv, stot: (se[s], 0, col0 + sj[s]))
    return pl.pallas_call(
        _expert_kernel,
        grid_spec=pltpu.PrefetchScalarGridSpec(
            num_scalar_prefetch=6,
            grid=(G_EXP,),
            in_specs=[
                pl.BlockSpec(memory_space=pl.ANY),
                pl.BlockSpec((1, D_MODEL, F_EXP), wmap(0)),
                pl.BlockSpec((1, D_MODEL, F_EXP), wmap(J_EXP)),
                pl.BlockSpec((1, 1, F_EXP), wmap(0)),
                pl.BlockSpec((1, 1, F_EXP), wmap(J_EXP)),
                pl.BlockSpec((1, F_EXP, D_MODEL), lambda s, se, sj, srow, snt, sv, stot: (se[s], sj[s], 0)),
                pl.BlockSpec((1, 1, D_MODEL), lambda s, se, sj, srow, snt, sv, stot: (se[s], 0, 0)),
            ],
            out_specs=pl.BlockSpec(memory_space=pl.ANY),
            scratch_shapes=[pltpu.VMEM((D_MODEL, F_EXP), BF16),
                            pltpu.VMEM((D_MODEL, F_EXP), BF16),
                            pltpu.VMEM((F_EXP, D_MODEL), BF16),
                            pltpu.VMEM((SUP_EXP * TM_EXP, D_MODEL), F32),
                            pltpu.VMEM((SUP_EXP * TM_EXP, D_MODEL), BF16),
                            pltpu.VMEM((2, TM_EXP, D_MODEL), BF16),
                            pltpu.SemaphoreType.DMA(()),
                            pltpu.SemaphoreType.DMA((2,))]),
        out_shape=jax.ShapeDtypeStruct((NT_EXP * TM_EXP, D_MODEL), BF16),
        compiler_params=_cparams(("arbitrary",), VMEM_LIMIT),
        name="experts",
    )(*sched, xs, w_gate_up, w_gate_up,
      b_gate_up.reshape(N_EXPERTS, 1, 2 * D_FF), b_gate_up.reshape(N_EXPERTS, 1, 2 * D_FF),
      w_down, b_down.reshape(N_EXPERTS, 1, D_MODEL))


def _combine_kernel(x1_ref, y0_ref, y1_ref, y2_ref, y3_ref, rp_ref, mod_ref, g_ref, oc_ref, ol_ref):
    rp = rp_ref[...]
    moe = rp[:, 0:1] * y0_ref[...].astype(F32)
    for k, y_ref in enumerate((y1_ref, y2_ref, y3_ref), start=1):
        moe = moe + rp[:, k:k + 1] * y_ref[...].astype(F32)
    x2 = x1_ref[...] + mod_ref[0, 5:6, :] * moe
    out = x2 * lax.rsqrt(jnp.mean(x2 * x2, axis=-1, keepdims=True) + NORM_EPS) * g_ref[...]
    is_ctx = pl.program_id(0) < T_CTX // TM_CMB

    @pl.when(is_ctx)
    def _():
        oc_ref[...] = out

    @pl.when(jnp.logical_not(is_ctx))
    def _():
        ol_ref[...] = out


def _combine(x1, y4, route_p, mod3, final_g):
    tm = TM_CMB
    nt = T_ALL // tm
    y_specs = [pl.BlockSpec((tm, D_MODEL), functools.partial(lambda k, i: (k * nt + i, 0), k))
               for k in range(TOP_K)]
    return pl.pallas_call(
        _combine_kernel,
        grid=(nt,),
        in_specs=[pl.BlockSpec((tm, D_MODEL), lambda i: (i, 0))] + y_specs + [
            pl.BlockSpec((tm, LANES), lambda i: (i, 0)),
            pl.BlockSpec((1, 6, D_MODEL), lambda i: (_cond_row(i, tm), 0, 0)),
            pl.BlockSpec((1, D_MODEL), lambda i: (0, 0))],
        out_specs=[pl.BlockSpec((tm, D_MODEL), lambda i: (_ctx_tile(i, tm), 0)),
                   pl.BlockSpec((tm, D_MODEL), lambda i: (_lat_tile(i, tm), 0))],
        out_shape=[jax.ShapeDtypeStruct((T_CTX, D_MODEL), F32),
                   jax.ShapeDtypeStruct((T_LAT, D_MODEL), F32)],
        compiler_params=_cparams(("arbitrary",), 40 * 1024 * 1024),
        name="combine_final_norm",
    )(x1, y4, y4, y4, y4, route_p, mod3, final_g.reshape(1, D_MODEL))


def kernel(x_prompt, x_sample, c, c_ctx, state_ret, norm1_g, norm2_g, w_mod, b_mod, w_in, ret_decay,
           w_pool, pool_scale, w_out, w_router, b_router, w_gate_up, b_gate_up, w_down, b_down, final_g):
    assert w_mod.shape[0] == 1, "single trunk layer"
    x_ctx = x_prompt.reshape(T_CTX, D_MODEL)
    x_lat = x_sample.reshape(T_LAT, D_MODEL)
    cond = jnp.zeros((COND_ROWS, D_MODEL), F32).at[0].set(c_ctx).at[1:N_COND].set(c)

    mod = _modulation(cond, w_mod[0], b_mod[0])
    mod3 = mod.reshape(COND_ROWS, 6, D_MODEL)

    proj = _in_projection(x_ctx, x_lat, norm1_g[0], mod3, w_in[0].astype(BF16))

    ret_ctx, new_state = _retention(proj, 0, BATCH, SEQ, ret_decay[0], None, None, True)
    ret_lat, _ = _retention(proj, T_CTX, DEC_BATCH, DEC_SEQ, ret_decay[0], state_ret[:, 0],
                            _rope_tables(DEC_SEQ), False)
    w_pool_bf16 = w_pool[0].astype(BF16)
    pool_ctx = _pooling(proj, 0, BATCH, SEQ, w_pool_bf16, pool_scale[0], False)
    pool_lat = _pooling(proj, T_CTX, DEC_BATCH, DEC_SEQ, w_pool_bf16, pool_scale[0], True)

    wr = jnp.zeros((D_MODEL, LANES), F32).at[:, :N_EXPERTS].set(w_router[0])
    wr_hi = wr.astype(BF16)
    wr_lo = (wr - wr_hi.astype(F32)).astype(BF16)
    br = jnp.full((1, LANES), -1e30, F32).at[0, :N_EXPERTS].set(b_router[0])
    x1, h2, route_i, route_p, route_r, counts = _out_projection(
        ret_ctx, ret_lat, pool_ctx, pool_lat, x_ctx, x_lat, mod3, norm2_g[0], w_out[0].astype(BF16),
        wr_hi, wr_lo, br)

    row_off, sched = _expert_schedule(counts[0, :N_EXPERTS])
    slot = row_off[route_i[:, :TOP_K]] + route_r[:, :TOP_K]
    token = jnp.broadcast_to(jnp.arange(T_ALL, dtype=jnp.int32)[:, None], (T_ALL, TOP_K))
    token_of_slot = (jnp.arange(XS_ROWS, dtype=jnp.int32) % T_ALL).at[slot.reshape(-1)].set(
        token.reshape(-1), unique_indices=True, mode="promise_in_bounds")
    xs = h2.at[token_of_slot].get(mode="promise_in_bounds")
    ys = _experts(xs, sched, w_gate_up[0], b_gate_up[0], w_down[0], b_down[0])
    y4 = ys.at[slot.T.reshape(-1)].get(mode="promise_in_bounds", unique_indices=True)

    y_ctx, y_lat = _combine(x1, y4, route_p, mod3, final_g)
    return (y_ctx.reshape(BATCH, SEQ, D_MODEL), y_lat.reshape(DEC_BATCH, DEC_SEQ, D_MODEL),
            new_state.reshape(BATCH, 1, 2, RET_HEADS, RET_DK, RET_DV))
```

```python
import functools

import numpy as np
import jax
import jax.numpy as jnp
from jax import lax
from jax.experimental import pallas as pl
from jax.experimental.pallas import tpu as pltpu

F32 = jnp.float32
BF16 = jnp.bfloat16

D_MODEL = 2048
BATCH = 16
SEQ = 256
DEC_BATCH = 4
DEC_SEQ = 2048
GRID_W = 64
RET_HEADS = 4
RET_DK = 128
RET_DV = 256
RET_QK_WIDTH = RET_HEADS * RET_DK
RET_WIDTH = RET_HEADS * RET_DV
POOL_WINDOWS = (2, 4, 8, 16)
POOL_GROUPS = 4
POOL_DG = 256
POOL_WIDTH = POOL_GROUPS * POOL_DG
IN_WIDTH = 2 * RET_QK_WIDTH + 2 * RET_WIDTH + POOL_WIDTH
N_EXPERTS = 32
TOP_K = 4
D_FF = D_MODEL
SWIGLU_LIMIT = 7.0
SWIGLU_ALPHA = 1.702
ROPE_BASE = 10000.0
NORM_EPS = 1e-6
GN_EPS = 1e-6

T_CTX = BATCH * SEQ
T_LAT = DEC_BATCH * DEC_SEQ
T_ALL = T_CTX + T_LAT
N_COND = 1 + DEC_BATCH
COND_ROWS = 8
LANES = 128

RET_CHUNK = 256

TM_IN = 512
TN_IN = 1024
TM_OUT = 256
TM_CMB = 256
H2_ROWS = 16384
POOL_TILE = 256
POOL_PAD = (max(POOL_WINDOWS) // 2) * GRID_W

TM_EXP = 256
F_EXP = 512
J_EXP = D_FF // F_EXP
SUP_EXP = 6
NT_EXP = (T_ALL * TOP_K + N_EXPERTS * (TM_EXP - 1)) // TM_EXP
U_EXP = N_EXPERTS + NT_EXP // SUP_EXP + 1
G_EXP = J_EXP * U_EXP
XS_ROWS = (NT_EXP + SUP_EXP) * TM_EXP

VMEM_LIMIT = 56 * 1024 * 1024


def _cparams(sem, vmem=None):
    return pltpu.CompilerParams(dimension_semantics=sem, vmem_limit_bytes=vmem)


def _cond_row(i, tm):
    nctx = T_CTX // tm
    return jnp.where(i < nctx, 0, 1 + (i - nctx) // (DEC_SEQ // tm))


def _sigmoid(x):
    return 1.0 / (1.0 + jnp.exp(-x))


def _mod_kernel(c_ref, w_ref, b_ref, o_ref):
    c = c_ref[...]
    s = (c * _sigmoid(c)).astype(BF16)
    o_ref[...] = jnp.dot(s, w_ref[...].astype(BF16), preferred_element_type=F32) + b_ref[...]


def _modulation(cond, w_mod, b_mod):
    n = w_mod.shape[1]
    tn = 1024
    return pl.pallas_call(
        _mod_kernel,
        grid=(n // tn,),
        in_specs=[pl.BlockSpec((COND_ROWS, D_MODEL), lambda j: (0, 0)),
                  pl.BlockSpec((D_MODEL, tn), lambda j: (0, j)),
                  pl.BlockSpec((1, tn), lambda j: (0, j))],
        out_specs=pl.BlockSpec((COND_ROWS, tn), lambda j: (0, j)),
        out_shape=jax.ShapeDtypeStruct((COND_ROWS, n), F32),
        compiler_params=_cparams(("arbitrary",), 40 * 1024 * 1024),
        name="modulation",
    )(cond, w_mod, b_mod.reshape(1, n))


def _ctx_tile(i, tm):
    return jnp.minimum(i, T_CTX // tm - 1)


def _lat_tile(i, tm):
    return jnp.maximum(i - T_CTX // tm, 0)


def _inproj_kernel(xc_ref, xl_ref, g_ref, mod_ref, w_ref, o_ref, h_ref):
    def modulated_norm(x_ref):
        x = x_ref[...]
        y = x * lax.rsqrt(jnp.mean(x * x, axis=-1, keepdims=True) + NORM_EPS) * g_ref[...]
        shift = mod_ref[0, 0:1, :]
        scale = mod_ref[0, 1:2, :]
        h_ref[...] = (y * (1.0 + scale) + shift).astype(BF16)

    first = pl.program_id(1) == 0
    is_ctx = pl.program_id(0) < T_CTX // TM_IN
    pl.when(first & is_ctx)(lambda: modulated_norm(xc_ref))
    pl.when(first & jnp.logical_not(is_ctx))(lambda: modulated_norm(xl_ref))

    o_ref[...] = jnp.dot(h_ref[...], w_ref[...], preferred_element_type=F32)


def _in_projection(x_ctx, x_lat, norm_g, mod3, w_in_bf16):
    return pl.pallas_call(
        _inproj_kernel,
        grid=(T_ALL // TM_IN, IN_WIDTH // TN_IN),
        in_specs=[pl.BlockSpec((TM_IN, D_MODEL), lambda i, j: (_ctx_tile(i, TM_IN), 0)),
                  pl.BlockSpec((TM_IN, D_MODEL), lambda i, j: (_lat_tile(i, TM_IN), 0)),
                  pl.BlockSpec((1, D_MODEL), lambda i, j: (0, 0)),
                  pl.BlockSpec((1, 6, D_MODEL), lambda i, j: (_cond_row(i, TM_IN), 0, 0)),
                  pl.BlockSpec((D_MODEL, TN_IN), lambda i, j: (0, j))],
        out_specs=pl.BlockSpec((TM_IN, TN_IN), lambda i, j: (i, j)),
        out_shape=jax.ShapeDtypeStruct((T_ALL, IN_WIDTH), F32),
        scratch_shapes=[pltpu.VMEM((TM_IN, D_MODEL), BF16)],
        compiler_params=_cparams(("arbitrary", "arbitrary"), 40 * 1024 * 1024),
        name="in_projection",
    )(x_ctx, x_lat, norm_g.reshape(1, D_MODEL), mod3, w_in_bf16)


def _rope_tables(seq_len):
    t = jnp.arange(seq_len)
    row = (t // GRID_W).astype(F32)
    col = (t % GRID_W).astype(F32)
    half = RET_DK // 2
    n_freq = half // 2
    inv = ROPE_BASE ** (-jnp.arange(n_freq, dtype=F32) / n_freq)
    ang_r = row[:, None] * inv
    ang_c = col[:, None] * inv
    cos = jnp.concatenate([jnp.cos(ang_r), jnp.cos(ang_r), jnp.cos(ang_c), jnp.cos(ang_c)], axis=-1)
    sin = jnp.concatenate([-jnp.sin(ang_r), jnp.sin(ang_r), -jnp.sin(ang_c), jnp.sin(ang_c)], axis=-1)
    return cos, sin


def _retention_kernel(rd_ref, *refs, seq_len, chunk, use_rope, has_state_in, has_state_out):
    refs = list(refs)
    q_ref, k_ref, v_ref, g_ref = refs[:4]
    pos = 4
    if has_state_in:
        s0_ref = refs[pos]
        pos += 1
    if use_rope:
        cos_ref, sin_ref = refs[pos], refs[pos + 1]
        pos += 2
    o_ref = refs[pos]
    pos += 1
    if has_state_out:
        st_ref = refs[pos]
        pos += 1
    qs_ref, ks_ref, acc_ref, sf_ref, sb_ref = refs[pos:pos + 5]

    C = chunk
    n_chunks = seq_len // C
    h = pl.program_id(1)

    lgf = -jnp.exp(jnp.full((C, 1), rd_ref[0, h], F32))
    lgb = -jnp.exp(jnp.full((C, 1), rd_ref[1, h], F32))
    ii = lax.broadcasted_iota(jnp.int32, (C, C), 0)
    jj = lax.broadcasted_iota(jnp.int32, (C, C), 1)
    diff = (ii - jj).astype(F32)
    decay = (jnp.where(diff >= 0, jnp.exp(lgf * jnp.maximum(diff, 0.0)), 0.0)
             + jnp.where(diff <= 0, jnp.exp(lgb * jnp.maximum(-diff, 0.0)), 0.0))
    p = lax.broadcasted_iota(jnp.int32, (C, 1), 0).astype(F32)
    xi_f = jnp.exp(lgf * (p + 1.0))
    zeta_f = jnp.exp(lgf * (C - 1.0 - p))
    xi_b = jnp.exp(lgb * (C - p))
    zeta_b = jnp.exp(lgb * p)
    cd_f = jnp.exp(lgf[0:1, :] * C)
    cd_b = jnp.exp(lgb[0:1, :] * C)

    if has_state_in:
        sf_ref[...] = s0_ref[0, 0, 0]
        sb_ref[...] = s0_ref[0, 1, 0]
    else:
        sf_ref[...] = jnp.zeros_like(sf_ref)
        sb_ref[...] = jnp.zeros_like(sb_ref)

    lane = lax.broadcasted_iota(jnp.int32, (C, RET_DK), 1)
    first_half = (lane & 32) == 0

    def rope(x, cs, sn):
        swapped = jnp.where(first_half, pltpu.roll(x, RET_DK - 32, axis=1), pltpu.roll(x, 32, axis=1))
        return x * cs + swapped * sn

    def state_update(s_ref, kz, v, cd):
        upd = lax.dot_general(kz, v, (((0,), (0,)), ((), ())), preferred_element_type=F32)
        s_ref[...] = cd * s_ref[...] + upd

    def fwd_body(c, carry):
        sl = pl.ds(pl.multiple_of(c * C, C), C)
        q = q_ref[sl, :]
        k = k_ref[sl, :] * (RET_DK ** -0.5)
        if use_rope:
            cs = cos_ref[sl, :]
            sn = sin_ref[sl, :]
            q = rope(q, cs, sn)
            k = rope(k, cs, sn)
        qb = q.astype(BF16)
        kb = k.astype(BF16)
        qs_ref[sl, :] = qb
        ks_ref[sl, :] = k
        v = v_ref[sl, :].astype(BF16)
        scores = lax.dot_general(qb, kb, (((1,), (1,)), ((), ())), preferred_element_type=F32)
        inner = jnp.dot((scores * decay).astype(BF16), v, preferred_element_type=F32)
        cross = jnp.dot(qb, sf_ref[...].astype(BF16), preferred_element_type=F32) * xi_f
        acc_ref[sl, :] = inner + cross
        state_update(sf_ref, (k * zeta_f).astype(BF16), v, cd_f)
        return carry

    lax.fori_loop(0, n_chunks, fwd_body, 0)

    def bwd_body(i, carry):
        c = n_chunks - 1 - i
        sl = pl.ds(pl.multiple_of(c * C, C), C)
        qb = qs_ref[sl, :]
        k = ks_ref[sl, :]
        v = v_ref[sl, :].astype(BF16)
        cross = jnp.dot(qb, sb_ref[...].astype(BF16), preferred_element_type=F32) * xi_b
        o = acc_ref[sl, :] + cross
        mu = jnp.mean(o, axis=-1, keepdims=True)
        oc = o - mu
        var = jnp.mean(oc * oc, axis=-1, keepdims=True)
        on = oc * lax.rsqrt(var + GN_EPS)
        g = g_ref[sl, :]
        o_ref[sl, :] = (on * (g * _sigmoid(g))).astype(BF16)
        state_update(sb_ref, (k * zeta_b).astype(BF16), v, cd_b)
        return carry

    lax.fori_loop(0, n_chunks, bwd_body, 0)

    if has_state_out:
        st_ref[0, 0, 0] = sf_ref[...]
        st_ref[0, 1, 0] = sb_ref[...]


def _retention(proj, row0, B, L, ret_decay, state_in, rope_tabs, want_state):
    rb = row0 // L
    use_rope = rope_tabs is not None
    has_state_in = state_in is not None
    kq = RET_QK_WIDTH // RET_DK
    kv = 2 * RET_QK_WIDTH // RET_DV
    kg = kv + RET_HEADS
    in_specs = [pl.BlockSpec((L, RET_DK), lambda b, h, rd: (rb + b, h)),
                pl.BlockSpec((L, RET_DK), lambda b, h, rd: (rb + b, kq + h)),
                pl.BlockSpec((L, RET_DV), lambda b, h, rd: (rb + b, kv + h)),
                pl.BlockSpec((L, RET_DV), lambda b, h, rd: (rb + b, kg + h))]
    args = [proj, proj, proj, proj]
    if has_state_in:
        in_specs.append(pl.BlockSpec((1, 2, 1, RET_DK, RET_DV), lambda b, h, rd: (b, 0, h, 0, 0)))
        args.append(state_in)
    if use_rope:
        in_specs += [pl.BlockSpec((L, RET_DK), lambda b, h, rd: (0, 0))] * 2
        args += list(rope_tabs)
    out_specs = [pl.BlockSpec((L, RET_DV), lambda b, h, rd: (b, h))]
    out_shape = [jax.ShapeDtypeStruct((B * L, RET_WIDTH), BF16)]
    if want_state:
        out_specs.append(pl.BlockSpec((1, 2, 1, RET_DK, RET_DV), lambda b, h, rd: (b, 0, h, 0, 0)))
        out_shape.append(jax.ShapeDtypeStruct((B, 2, RET_HEADS, RET_DK, RET_DV), F32))
    kern = functools.partial(_retention_kernel, seq_len=L, chunk=RET_CHUNK, use_rope=use_rope,
                             has_state_in=has_state_in, has_state_out=want_state)
    res = pl.pallas_call(
        kern,
        grid_spec=pltpu.PrefetchScalarGridSpec(
            num_scalar_prefetch=1,
            grid=(B, RET_HEADS),
            in_specs=in_specs,
            out_specs=out_specs,
            scratch_shapes=[pltpu.VMEM((L, RET_DK), BF16),
                            pltpu.VMEM((L, RET_DK), F32),
                            pltpu.VMEM((L, RET_DV), F32),
                            pltpu.VMEM((RET_DK, RET_DV), F32),
                            pltpu.VMEM((RET_DK, RET_DV), F32)]),
        out_shape=out_shape,
        compiler_params=_cparams(("arbitrary", "arbitrary"), 40 * 1024 * 1024),
        name="retention_grid" if use_rope else "retention_seq",
    )(ret_decay, *args)
    return res if want_state else (res[0], None)


def _split3(x):
    hi = x.astype(BF16)
    r1 = x - hi.astype(F32)
    mid = r1.astype(BF16)
    lo = (r1 - mid.astype(F32)).astype(BF16)
    return jnp.concatenate([hi, mid, lo], axis=-1)


def _pool_kernel(p_ref, w_ref, sc_ref, o_ref, pad_ref, *, seq_len, grid_mode):
    PT = POOL_TILE
    n_tiles = seq_len // PT
    ii = lax.broadcasted_iota(jnp.int32, (PT, PT), 0)
    jj = lax.broadcasted_iota(jnp.int32, (PT, PT), 1)
    d = jj - ii
    ti = lax.broadcasted_iota(jnp.int32, (PT, 1), 0)

    if grid_mode:
        zeros = jnp.zeros((POOL_PAD, POOL_DG), F32)
        pad_ref[0:POOL_PAD, :] = zeros
        pad_ref[POOL_PAD + seq_len:POOL_PAD + seq_len + POOL_PAD, :] = zeros

    for gi, w in enumerate(POOL_WINDOWS):
        lo_off = -(w // 2)
        hi_off = w - w // 2 - 1
        cols = slice(gi * POOL_DG, (gi + 1) * POOL_DG)
        in_window = (d >= lo_off) & (d <= hi_off)
        if grid_mode:
            in_window = in_window & ((ii // GRID_W) == (jj // GRID_W))
        band = jnp.where(in_window, 1.0, 0.0).astype(BF16)
        wg = w_ref[gi]
        scale = sc_ref[gi]

        def window_sum(t0):
            x = p_ref[pl.ds(t0, PT), cols]
            s3 = jnp.dot(band, _split3(x), preferred_element_type=F32)
            return s3[:, 0:POOL_DG] + s3[:, POOL_DG:2 * POOL_DG] + s3[:, 2 * POOL_DG:3 * POOL_DG]

        def finish(t0, s, cnt):
            x = p_ref[pl.ds(t0, PT), cols]
            diff = (s / cnt - x).astype(BF16)
            y = jnp.dot(diff, wg, preferred_element_type=F32) * scale
            o_ref[pl.ds(t0, PT), cols] = y.astype(BF16)

        def count(pos, n):
            lo = jnp.maximum(pos + lo_off, 0)
            hi = jnp.minimum(pos + lo_off + w, n)
            return (hi - lo).astype(F32)

        if grid_mode:
            for t in range(n_tiles):
                pad_ref[POOL_PAD + t * PT:POOL_PAD + (t + 1) * PT, :] = window_sum(t * PT)
            for t in range(n_tiles):
                base = POOL_PAD + t * PT
                s = pad_ref[base + lo_off * GRID_W:base + lo_off * GRID_W + PT, :]
                for r in range(lo_off + 1, hi_off + 1):
                    s = s + pad_ref[base + r * GRID_W:base + r * GRID_W + PT, :]
                tok = ti + t * PT
                cnt = count(tok // GRID_W, seq_len // GRID_W) * count(tok % GRID_W, GRID_W)
                finish(t * PT, s, cnt)
        else:
            for t in range(n_tiles):
                finish(t * PT, window_sum(t * PT), count(ti + t * PT, seq_len))


def _pooling(proj, row0, B, L, w_pool_bf16, pool_scale, grid_mode):
    if not grid_mode:
        assert L == POOL_TILE
    rb = row0 // L
    pcol = (IN_WIDTH - POOL_WIDTH) // POOL_WIDTH
    kern = functools.partial(_pool_kernel, seq_len=L, grid_mode=grid_mode)
    return pl.pallas_call(
        kern,
        grid=(B,),
        in_specs=[pl.BlockSpec((L, POOL_WIDTH), lambda b: (rb + b, pcol)),
                  pl.BlockSpec((POOL_GROUPS, POOL_DG, POOL_DG), lambda b: (0, 0, 0)),
                  pl.BlockSpec((POOL_GROUPS, 1, POOL_DG), lambda b: (0, 0, 0))],
        out_specs=pl.BlockSpec((L, POOL_WIDTH), lambda b: (b, 0)),
        out_shape=jax.ShapeDtypeStruct((B * L, POOL_WIDTH), BF16),
        scratch_shapes=[pltpu.VMEM((L + 2 * POOL_PAD, POOL_DG), F32)],
        compiler_params=_cparams(("arbitrary",), 48 * 1024 * 1024),
        name="pool_grid" if grid_mode else "pool_seq",
    )(proj, w_pool_bf16, pool_scale.reshape(POOL_GROUPS, 1, POOL_DG))


def _outproj_kernel(*refs):
    i = pl.program_id(0)
    h2_ref = refs[14]

    @pl.when(i >= T_ALL // TM_OUT)
    def _():
        h2_ref[...] = jnp.zeros_like(h2_ref)

    pl.when(i < T_ALL // TM_OUT)(lambda: _outproj_tile(*refs))


def _outproj_tile(retc_ref, retl_ref, poolc_ref, pooll_ref, xc_ref, xl_ref, mod_ref, g_ref,
                  wt_ref, wb_ref, wrh_ref, wrl_ref, br_ref,
                  x1_ref, h2_ref, ri_ref, rp_ref, rr_ref, cnt_ref, run_ref):
    i = pl.program_id(0)

    @pl.when(i == 0)
    def _():
        run_ref[...] = jnp.zeros_like(run_ref)

    is_ctx = i < T_CTX // TM_OUT
    ret = jnp.where(is_ctx, retc_ref[...], retl_ref[...])
    pool = jnp.where(is_ctx, poolc_ref[...], pooll_ref[...])
    x = jnp.where(is_ctx, xc_ref[...], xl_ref[...])
    y = (jnp.dot(ret, wt_ref[...], preferred_element_type=F32)
         + jnp.dot(pool, wb_ref[...], preferred_element_type=F32))
    x1 = x + mod_ref[0, 2:3, :] * y
    x1_ref[...] = x1
    hn = x1 * lax.rsqrt(jnp.mean(x1 * x1, axis=-1, keepdims=True) + NORM_EPS) * g_ref[...]
    h2 = hn * (1.0 + mod_ref[0, 4:5, :]) + mod_ref[0, 3:4, :]
    hi = h2.astype(BF16)
    h2_ref[...] = hi
    lo = (h2 - hi.astype(F32)).astype(BF16)
    logits = (jnp.dot(hi, wrh_ref[...], preferred_element_type=F32)
              + jnp.dot(lo, wrh_ref[...], preferred_element_type=F32)
              + jnp.dot(hi, wrl_ref[...], preferred_element_type=F32)) + br_ref[...]

    tm = logits.shape[0]
    lane = lax.broadcasted_iota(jnp.int32, (tm, LANES), 1)
    work = logits
    vals, idxs, hots = [], [], []
    for _ in range(TOP_K):
        m = jnp.max(work, axis=-1, keepdims=True)
        idx = jnp.min(jnp.where(work == m, lane, LANES), axis=-1, keepdims=True)
        hot = lane == idx
        vals.append(m)
        idxs.append(idx)
        hots.append(hot)
        work = jnp.where(hot, -jnp.inf, work)
    exps = [jnp.exp(v - vals[0]) for v in vals]
    denom = exps[0] + exps[1] + exps[2] + exps[3]

    selected = jnp.zeros((tm, LANES), F32)
    for hot in hots:
        selected = selected + jnp.where(hot, 1.0, 0.0)
    r_i = lax.broadcasted_iota(jnp.int32, (tm, tm), 0)
    c_i = lax.broadcasted_iota(jnp.int32, (tm, tm), 1)
    tri = jnp.where(c_i < r_i, 1.0, 0.0).astype(BF16)
    before = jnp.dot(tri, selected.astype(BF16), preferred_element_type=F32) + run_ref[0:1, :]

    ri = jnp.zeros((tm, LANES), jnp.int32)
    rp = jnp.zeros((tm, LANES), F32)
    rr = jnp.zeros((tm, LANES), jnp.int32)
    for k in range(TOP_K):
        rank = jnp.sum(jnp.where(hots[k], before, 0.0), axis=-1, keepdims=True).astype(jnp.int32)
        ri = jnp.where(lane == k, idxs[k], ri)
        rp = jnp.where(lane == k, exps[k] / denom, rp)
        rr = jnp.where(lane == k, rank, rr)
    ri_ref[...] = ri
    rp_ref[...] = rp
    rr_ref[...] = rr

    run = run_ref[0:1, :] + jnp.sum(selected, axis=0, keepdims=True)
    run_ref[...] = jnp.broadcast_to(run, run_ref.shape)
    cnt_ref[...] = jnp.broadcast_to(run, cnt_ref.shape).astype(jnp.int32)


def _out_projection(ret_ctx, ret_lat, pool_ctx, pool_lat, x_ctx, x_lat, mod3, norm_g, w_out_bf16,
                    wr_hi, wr_lo, b_router_pad):
    tm = TM_OUT
    real = lambda i: jnp.minimum(i, T_ALL // tm - 1)
    row = lambda i: (real(i), 0)
    const = lambda i: (0, 0)
    crow = lambda i: (_ctx_tile(real(i), tm), 0)
    lrow = lambda i: (_lat_tile(real(i), tm), 0)
    return pl.pallas_call(
        _outproj_kernel,
        grid=(H2_ROWS // tm,),
        in_specs=[pl.BlockSpec((tm, RET_WIDTH), crow),
                  pl.BlockSpec((tm, RET_WIDTH), lrow),
                  pl.BlockSpec((tm, POOL_WIDTH), crow),
                  pl.BlockSpec((tm, POOL_WIDTH), lrow),
                  pl.BlockSpec((tm, D_MODEL), crow),
                  pl.BlockSpec((tm, D_MODEL), lrow),
                  pl.BlockSpec((1, 6, D_MODEL), lambda i: (_cond_row(real(i), tm), 0, 0)),
                  pl.BlockSpec((1, D_MODEL), const),
                  pl.BlockSpec((RET_WIDTH, D_MODEL), const),
                  pl.BlockSpec((POOL_WIDTH, D_MODEL), lambda i: (1, 0)),
                  pl.BlockSpec((D_MODEL, LANES), const),
                  pl.BlockSpec((D_MODEL, LANES), const),
                  pl.BlockSpec((1, LANES), const)],
        out_specs=[pl.BlockSpec((tm, D_MODEL), row),
                   pl.BlockSpec((tm, D_MODEL), lambda i: (i, 0)),
                   pl.BlockSpec((tm, LANES), row),
                   pl.BlockSpec((tm, LANES), row),
                   pl.BlockSpec((tm, LANES), row),
                   pl.BlockSpec((8, LANES), const)],
        out_shape=[jax.ShapeDtypeStruct((T_ALL, D_MODEL), F32),
                   jax.ShapeDtypeStruct((H2_ROWS, D_MODEL), BF16),
                   jax.ShapeDtypeStruct((T_ALL, LANES), jnp.int32),
                   jax.ShapeDtypeStruct((T_ALL, LANES), F32),
                   jax.ShapeDtypeStruct((T_ALL, LANES), jnp.int32),
                   jax.ShapeDtypeStruct((8, LANES), jnp.int32)],
        scratch_shapes=[pltpu.VMEM((8, LANES), F32)],
        compiler_params=_cparams(("arbitrary",), 48 * 1024 * 1024),
        name="out_projection_router",
    )(ret_ctx, ret_lat, pool_ctx, pool_lat, x_ctx, x_lat, mod3, norm_g.reshape(1, D_MODEL),
      w_out_bf16, w_out_bf16, wr_hi, wr_lo, b_router_pad)


def _count_le(ends, v):
    return jnp.sum((ends[None, :] <= v[:, None]).astype(jnp.int32), axis=1)


def _expert_schedule(counts):
    i32 = jnp.int32
    ntile = (counts + TM_EXP - 1) // TM_EXP
    tile_off = jnp.cumsum(ntile) - ntile
    nsup = (ntile + SUP_EXP - 1) // SUP_EXP
    sup_end = jnp.cumsum(nsup)
    u = jnp.arange(U_EXP, dtype=i32)
    sup_valid = u < sup_end[-1]
    uc = jnp.minimum(u, sup_end[-1] - 1)
    sup_e = jnp.minimum(_count_le(sup_end, uc), N_EXPERTS - 1)
    sup_idx = uc - (sup_end - nsup)[sup_e]
    sup_row0 = (tile_off[sup_e] + sup_idx * SUP_EXP) * TM_EXP
    sup_nt = jnp.clip(ntile[sup_e] - sup_idx * SUP_EXP, 1, SUP_EXP)
    step_e = jnp.repeat(sup_e, J_EXP)
    step_valid = jnp.repeat(sup_valid.astype(i32), J_EXP)
    step_j = jnp.where(step_valid == 1, jnp.tile(jnp.arange(J_EXP, dtype=i32), U_EXP), J_EXP - 1)
    step_row0 = jnp.repeat(sup_row0, J_EXP)
    step_nt = jnp.repeat(sup_nt, J_EXP)
    return tile_off * TM_EXP, (step_e.astype(i32), step_j.astype(i32), step_row0.astype(i32),
                               step_nt.astype(i32), step_valid, jnp.sum(ntile).astype(i32).reshape(1))


def _expert_kernel(se_ref, sj_ref, srow_ref, snt_ref, sv_ref, stot_ref,
                   xs_hbm, wg_ref, wu_ref, bg_ref, bu_ref, wd_ref, bd_ref, ys_hbm,
                   wg_s, wu_s, wd_s, acc_ref, x_buf, o_buf, x_sem, o_sem):
    s = pl.program_id(0)
    j = sj_ref[s]
    n_tiles = snt_ref[s]
    row0 = pl.multiple_of(srow_ref[s], TM_EXP)
    last = J_EXP - 1

    def x_copy():
        return pltpu.make_async_copy(xs_hbm.at[pl.ds(row0, SUP_EXP * TM_EXP)], x_buf, x_sem)

    def out_copy(t):
        rows = pl.ds(pl.multiple_of(row0 + t * TM_EXP, TM_EXP), TM_EXP)
        return pltpu.make_async_copy(o_buf.at[t % 2], ys_hbm.at[rows], o_sem.at[t % 2])

    @pl.when(s == 0)
    def _():
        o_buf[0] = jnp.zeros((TM_EXP, D_MODEL), BF16)

        def zero_copy(t):
            rows = pl.ds(pl.multiple_of(t * TM_EXP, TM_EXP), TM_EXP)
            return pltpu.make_async_copy(o_buf.at[0], ys_hbm.at[rows], o_sem.at[0])

        def start(t, carry):
            zero_copy(t).start()
            return carry

        def wait(t, carry):
            zero_copy(t).wait()
            return carry

        lax.fori_loop(stot_ref[0], NT_EXP, start, 0)
        lax.fori_loop(stot_ref[0], NT_EXP, wait, 0)

    @pl.when(sv_ref[s] == 1)
    def _():
        @pl.when(j == 0)
        def _():
            x_copy().start()
            acc_ref[...] = jnp.zeros_like(acc_ref)

        wg_s[...] = wg_ref[0].astype(BF16)
        wu_s[...] = wu_ref[0].astype(BF16)
        wd_s[...] = wd_ref[0].astype(BF16)

        @pl.when(j == 0)
        def _():
            x_copy().wait()

        def accumulate(rows):
            x = x_buf[rows, :]
            gate = jnp.dot(x, wg_s[...], preferred_element_type=F32) + bg_ref[0]
            up = jnp.dot(x, wu_s[...], preferred_element_type=F32) + bu_ref[0]
            gate = jnp.minimum(gate, SWIGLU_LIMIT)
            up = jnp.clip(up, -SWIGLU_LIMIT, SWIGLU_LIMIT)
            act = (up + 1.0) * gate * _sigmoid(SWIGLU_ALPHA * gate)
            acc_ref[rows, :] += jnp.dot(act.astype(BF16), wd_s[...], preferred_element_type=F32)

        def tile_pair(p, carry):
            base = pl.multiple_of(p * (2 * TM_EXP), 2 * TM_EXP)
            accumulate(pl.ds(base, TM_EXP))
            accumulate(pl.ds(base + TM_EXP, TM_EXP))
            return carry

        lax.fori_loop(0, n_tiles // 2, tile_pair, 0)

        @pl.when(n_tiles % 2 == 1)
        def _():
            accumulate(pl.ds(pl.multiple_of((n_tiles - 1) * TM_EXP, TM_EXP), TM_EXP))

        @pl.when(j == last)
        def _():
            def finish(t, carry):
                @pl.when(t >= 2)
                def _():
                    out_copy(t - 2).wait()

                rows = pl.ds(pl.multiple_of(t * TM_EXP, TM_EXP), TM_EXP)
                o_buf[t % 2] = (acc_ref[rows, :] + bd_ref[0]).astype(BF16)
                out_copy(t).start()
                return carry

            lax.fori_loop(0, n_tiles, finish, 0)

            @pl.when(n_tiles >= 2)
            def _():
                out_copy(n_tiles - 2).wait()

            out_copy(n_tiles - 1).wait()


def _experts(xs, sched, w_gate_up, b_gate_up, w_down, b_down):
    wmap = lambda col0: (lambda s, se, sj, srow, snt, sv, stot: (se[s], 0, col0 + sj[s]))
    return pl.pallas_call(
        _expert_kernel,
        grid_spec=pltpu.PrefetchScalarGridSpec(
            num_scalar_prefetch=6,
            grid=(G_EXP,),
            in_specs=[
                pl.BlockSpec(memory_space=pl.ANY),
                pl.BlockSpec((1, D_MODEL, F_EXP), wmap(0)),
                pl.BlockSpec((1, D_MODEL, F_EXP), wmap(J_EXP)),
                pl.BlockSpec((1, 1, F_EXP), wmap(0)),
                pl.BlockSpec((1, 1, F_EXP), wmap(J_EXP)),
                pl.BlockSpec((1, F_EXP, D_MODEL), lambda s, se, sj, srow, snt, sv, stot: (se[s], sj[s], 0)),
                pl.BlockSpec((1, 1, D_MODEL), lambda s, se, sj, srow, snt, sv, stot: (se[s], 0, 0)),
            ],
            out_specs=pl.BlockSpec(memory_space=pl.ANY),
            scratch_shapes=[pltpu.VMEM((D_MODEL, F_EXP), BF16),
                            pltpu.VMEM((D_MODEL, F_EXP), BF16),
                            pltpu.VMEM((F_EXP, D_MODEL), BF16),
                            pltpu.VMEM((SUP_EXP * TM_EXP, D_MODEL), F32),
                            pltpu.VMEM((SUP_EXP * TM_EXP, D_MODEL), BF16),
                            pltpu.VMEM((2, TM_EXP, D_MODEL), BF16),
                            pltpu.SemaphoreType.DMA(()),
                            pltpu.SemaphoreType.DMA((2,))]),
        out_shape=jax.ShapeDtypeStruct((NT_EXP * TM_EXP, D_MODEL), BF16),
        compiler_params=_cparams(("arbitrary",), VMEM_LIMIT),
        name="experts",
    )(*sched, xs, w_gate_up, w_gate_up,
      b_gate_up.reshape(N_EXPERTS, 1, 2 * D_FF), b_gate_up.reshape(N_EXPERTS, 1, 2 * D_FF),
      w_down, b_down.reshape(N_EXPERTS, 1, D_MODEL))


def _combine_kernel(x1_ref, y0_ref, y1_ref, y2_ref, y3_ref, rp_ref, mod_ref, g_ref, oc_ref, ol_ref):
    rp = rp_ref[...]
    moe = rp[:, 0:1] * y0_ref[...].astype(F32)
    for k, y_ref in enumerate((y1_ref, y2_ref, y3_ref), start=1):
        moe = moe + rp[:, k:k + 1] * y_ref[...].astype(F32)
    x2 = x1_ref[...] + mod_ref[0, 5:6, :] * moe
    out = x2 * lax.rsqrt(jnp.mean(x2 * x2, axis=-1, keepdims=True) + NORM_EPS) * g_ref[...]
    is_ctx = pl.program_id(0) < T_CTX // TM_CMB

    @pl.when(is_ctx)
    def _():
        oc_ref[...] = out

    @pl.when(jnp.logical_not(is_ctx))
    def _():
        ol_ref[...] = out


def _combine(x1, y4, route_p, mod3, final_g):
    tm = TM_CMB
    nt = T_ALL // tm
    y_specs = [pl.BlockSpec((tm, D_MODEL), functools.partial(lambda k, i: (k * nt + i, 0), k))
               for k in range(TOP_K)]
    return pl.pallas_call(
        _combine_kernel,
        grid=(nt,),
        in_specs=[pl.BlockSpec((tm, D_MODEL), lambda i: (i, 0))] + y_specs + [
            pl.BlockSpec((tm, LANES), lambda i: (i, 0)),
            pl.BlockSpec((1, 6, D_MODEL), lambda i: (_cond_row(i, tm), 0, 0)),
            pl.BlockSpec((1, D_MODEL), lambda i: (0, 0))],
        out_specs=[pl.BlockSpec((tm, D_MODEL), lambda i: (_ctx_tile(i, tm), 0)),
                   pl.BlockSpec((tm, D_MODEL), lambda i: (_lat_tile(i, tm), 0))],
        out_shape=[jax.ShapeDtypeStruct((T_CTX, D_MODEL), F32),
                   jax.ShapeDtypeStruct((T_LAT, D_MODEL), F32)],
        compiler_params=_cparams(("arbitrary",), 40 * 1024 * 1024),
        name="combine_final_norm",
    )(x1, y4, y4, y4, y4, route_p, mod3, final_g.reshape(1, D_MODEL))


def kernel(x_prompt, x_sample, c, c_ctx, state_ret, norm1_g, norm2_g, w_mod, b_mod, w_in, ret_decay,
           w_pool, pool_scale, w_out, w_router, b_router, w_gate_up, b_gate_up, w_down, b_down, final_g):
    assert w_mod.shape[0] == 1, "single trunk layer"
    x_ctx = x_prompt.reshape(T_CTX, D_MODEL)
    x_lat = x_sample.reshape(T_LAT, D_MODEL)
    cond = jnp.zeros((COND_ROWS, D_MODEL), F32).at[0].set(c_ctx).at[1:N_COND].set(c)

    mod = _modulation(cond, w_mod[0], b_mod[0])
    mod3 = mod.reshape(COND_ROWS, 6, D_MODEL)

    proj = _in_projection(x_ctx, x_lat, norm1_g[0], mod3, w_in[0].astype(BF16))

    ret_ctx, new_state = _retention(proj, 0, BATCH, SEQ, ret_decay[0], None, None, True)
    ret_lat, _ = _retention(proj, T_CTX, DEC_BATCH, DEC_SEQ, ret_decay[0], state_ret[:, 0],
                            _rope_tables(DEC_SEQ), False)
    w_pool_bf16 = w_pool[0].astype(BF16)
    pool_ctx = _pooling(proj, 0, BATCH, SEQ, w_pool_bf16, pool_scale[0], False)
    pool_lat = _pooling(proj, T_CTX, DEC_BATCH, DEC_SEQ, w_pool_bf16, pool_scale[0], True)

    wr = jnp.zeros((D_MODEL, LANES), F32).at[:, :N_EXPERTS].set(w_router[0])
    wr_hi = wr.astype(BF16)
    wr_lo = (wr - wr_hi.astype(F32)).astype(BF16)
    br = jnp.full((1, LANES), -1e30, F32).at[0, :N_EXPERTS].set(b_router[0])
    x1, h2, route_i, route_p, route_r, counts = _out_projection(
        ret_ctx, ret_lat, pool_ctx, pool_lat, x_ctx, x_lat, mod3, norm2_g[0], w_out[0].astype(BF16),
        wr_hi, wr_lo, br)

    row_off, sched = _expert_schedule(counts[0, :N_EXPERTS])
    slot = row_off[route_i[:, :TOP_K]] + route_r[:, :TOP_K]
    token = jnp.broadcast_to(jnp.arange(T_ALL, dtype=jnp.int32)[:, None], (T_ALL, TOP_K))
    token_of_slot = (jnp.arange(XS_ROWS, dtype=jnp.int32) % T_ALL).at[slot.reshape(-1)].set(
        token.reshape(-1), unique_indices=True, mode="promise_in_bounds")
    xs = h2.at[token_of_slot].get(mode="promise_in_bounds")
    ys = _experts(xs, sched, w_gate_up[0], b_gate_up[0], w_down[0], b_down[0])
    y4 = ys.at[slot.T.reshape(-1)].get(mode="promise_in_bounds", unique_indices=True)

    y_ctx, y_lat = _combine(x1, y4, route_p, mod3, final_g)
    return (y_ctx.reshape(BATCH, SEQ, D_MODEL), y_lat.reshape(DEC_BATCH, DEC_SEQ, D_MODEL),
            new_state.reshape(BATCH, 1, 2, RET_HEADS, RET_DK, RET_DV))
```

```python
import functools

import numpy as np
import jax
import jax.numpy as jnp
from jax import lax
from jax.experimental import pallas as pl
from jax.experimental.pallas import tpu as pltpu

F32 = jnp.float32
BF16 = jnp.bfloat16

D_MODEL = 2048
BATCH = 16
SEQ = 256
DEC_BATCH = 4
DEC_SEQ = 2048
GRID_W = 64
RET_HEADS = 4
RET_DK = 128
RET_DV = 256
RET_QK_WIDTH = RET_HEADS * RET_DK
RET_WIDTH = RET_HEADS * RET_DV
POOL_WINDOWS = (2, 4, 8, 16)
POOL_GROUPS = 4
POOL_DG = 256
POOL_WIDTH = POOL_GROUPS * POOL_DG
IN_WIDTH = 2 * RET_QK_WIDTH + 2 * RET_WIDTH + POOL_WIDTH
N_EXPERTS = 32
TOP_K = 4
D_FF = D_MODEL
SWIGLU_LIMIT = 7.0
SWIGLU_ALPHA = 1.702
ROPE_BASE = 10000.0
NORM_EPS = 1e-6
GN_EPS = 1e-6

T_CTX = BATCH * SEQ
T_LAT = DEC_BATCH * DEC_SEQ
T_ALL = T_CTX + T_LAT
N_COND = 1 + DEC_BATCH
COND_ROWS = 8
LANES = 128

RET_CHUNK = 256

TM_IN = 512
TN_IN = 1024
TM_OUT = 256
TM_CMB = 256
H2_ROWS = 16384
POOL_TILE = 256
POOL_PAD = (max(POOL_WINDOWS) // 2) * GRID_W

TM_EXP = 256
F_EXP = 512
J_EXP = D_FF // F_EXP
SUP_EXP = 6
NT_EXP = (T_ALL * TOP_K + N_EXPERTS * (TM_EXP - 1)) // TM_EXP
U_EXP = N_EXPERTS + NT_EXP // SUP_EXP + 1
G_EXP = J_EXP * U_EXP
XS_ROWS = (NT_EXP + SUP_EXP) * TM_EXP

VMEM_LIMIT = 56 * 1024 * 1024


def _cparams(sem, vmem=None):
    return pltpu.CompilerParams(dimension_semantics=sem, vmem_limit_bytes=vmem)


def _cond_row(i, tm):
    nctx = T_CTX // tm
    return jnp.where(i < nctx, 0, 1 + (i - nctx) // (DEC_SEQ // tm))


def _sigmoid(x):
    return 1.0 / (1.0 + jnp.exp(-x))


def _mod_kernel(c_ref, w_ref, b_ref, o_ref):
    c = c_ref[...]
    s = (c * _sigmoid(c)).astype(BF16)
    o_ref[...] = jnp.dot(s, w_ref[...].astype(BF16), preferred_element_type=F32) + b_ref[...]


def _modulation(cond, w_mod, b_mod):
    n = w_mod.shape[1]
    tn = 1024
    return pl.pallas_call(
        _mod_kernel,
        grid=(n // tn,),
        in_specs=[pl.BlockSpec((COND_ROWS, D_MODEL), lambda j: (0, 0)),
                  pl.BlockSpec((D_MODEL, tn), lambda j: (0, j)),
                  pl.BlockSpec((1, tn), lambda j: (0, j))],
        out_specs=pl.BlockSpec((COND_ROWS, tn), lambda j: (0, j)),
        out_shape=jax.ShapeDtypeStruct((COND_ROWS, n), F32),
        compiler_params=_cparams(("arbitrary",), 40 * 1024 * 1024),
        name="modulation",
    )(cond, w_mod, b_mod.reshape(1, n))


def _ctx_tile(i, tm):
    return jnp.minimum(i, T_CTX // tm - 1)


def _lat_tile(i, tm):
    return jnp.maximum(i - T_CTX // tm, 0)


def _inproj_kernel(xc_ref, xl_ref, g_ref, mod_ref, w_ref, o_ref, h_ref):
    def modulated_norm(x_ref):
        x = x_ref[...]
        y = x * lax.rsqrt(jnp.mean(x * x, axis=-1, keepdims=True) + NORM_EPS) * g_ref[...]
        shift = mod_ref[0, 0:1, :]
        scale = mod_ref[0, 1:2, :]
        h_ref[...] = (y * (1.0 + scale) + shift).astype(BF16)

    first = pl.program_id(1) == 0
    is_ctx = pl.program_id(0) < T_CTX // TM_IN
    pl.when(first & is_ctx)(lambda: modulated_norm(xc_ref))
    pl.when(first & jnp.logical_not(is_ctx))(lambda: modulated_norm(xl_ref))

    o_ref[...] = jnp.dot(h_ref[...], w_ref[...], preferred_element_type=F32)


def _in_projection(x_ctx, x_lat, norm_g, mod3, w_in_bf16):
    return pl.pallas_call(
        _inproj_kernel,
        grid=(T_ALL // TM_IN, IN_WIDTH // TN_IN),
        in_specs=[pl.BlockSpec((TM_IN, D_MODEL), lambda i, j: (_ctx_tile(i, TM_IN), 0)),
                  pl.BlockSpec((TM_IN, D_MODEL), lambda i, j: (_lat_tile(i, TM_IN), 0)),
                  pl.BlockSpec((1, D_MODEL), lambda i, j: (0, 0)),
                  pl.BlockSpec((1, 6, D_MODEL), lambda i, j: (_cond_row(i, TM_IN), 0, 0)),
                  pl.BlockSpec((D_MODEL, TN_IN), lambda i, j: (0, j))],
        out_specs=pl.BlockSpec((TM_IN, TN_IN), lambda i, j: (i, j)),
        out_shape=jax.ShapeDtypeStruct((T_ALL, IN_WIDTH), F32),
        scratch_shapes=[pltpu.VMEM((TM_IN, D_MODEL), BF16)],
        compiler_params=_cparams(("arbitrary", "arbitrary"), 40 * 1024 * 1024),
        name="in_projection",
    )(x_ctx, x_lat, norm_g.reshape(1, D_MODEL), mod3, w_in_bf16)


def _rope_tables(seq_len):
    t = jnp.arange(seq_len)
    row = (t // GRID_W).astype(F32)
    col = (t % GRID_W).astype(F32)
    half = RET_DK // 2
    n_freq = half // 2
    inv = ROPE_BASE ** (-jnp.arange(n_freq, dtype=F32) / n_freq)
    ang_r = row[:, None] * inv
    ang_c = col[:, None] * inv
    cos = jnp.concatenate([jnp.cos(ang_r), jnp.cos(ang_r), jnp.cos(ang_c), jnp.cos(ang_c)], axis=-1)
    sin = jnp.concatenate([-jnp.sin(ang_r), jnp.sin(ang_r), -jnp.sin(ang_c), jnp.sin(ang_c)], axis=-1)
    return cos, sin


def _retention_kernel(rd_ref, *refs, seq_len, chunk, use_rope, has_state_in, has_state_out):
    refs = list(refs)
    q_ref, k_ref, v_ref, g_ref = refs[:4]
    pos = 4
    if has_state_in:
        s0_ref = refs[pos]
        pos += 1
    if use_rope:
        cos_ref, sin_ref = refs[pos], refs[pos + 1]
        pos += 2
    o_ref = refs[pos]
    pos += 1
    if has_state_out:
        st_ref = refs[pos]
        pos += 1
    qs_ref, ks_ref, acc_ref, sf_ref, sb_ref = refs[pos:pos + 5]

    C = chunk
    n_chunks = seq_len // C
    h = pl.program_id(1)

    lgf = -jnp.exp(jnp.full((C, 1), rd_ref[0, h], F32))
    lgb = -jnp.exp(jnp.full((C, 1), rd_ref[1, h], F32))
    ii = lax.broadcasted_iota(jnp.int32, (C, C), 0)
    jj = lax.broadcasted_iota(jnp.int32, (C, C), 1)
    diff = (ii - jj).astype(F32)
    decay = (jnp.where(diff >= 0, jnp.exp(lgf * jnp.maximum(diff, 0.0)), 0.0)
             + jnp.where(diff <= 0, jnp.exp(lgb * jnp.maximum(-diff, 0.0)), 0.0))
    p = lax.broadcasted_iota(jnp.int32, (C, 1), 0).astype(F32)
    xi_f = jnp.exp(lgf * (p + 1.0))
    zeta_f = jnp.exp(lgf * (C - 1.0 - p))
    xi_b = jnp.exp(lgb * (C - p))
    zeta_b = jnp.exp(lgb * p)
    cd_f = jnp.exp(lgf[0:1, :] * C)
    cd_b = jnp.exp(lgb[0:1, :] * C)

    if has_state_in:
        sf_ref[...] = s0_ref[0, 0, 0]
        sb_ref[...] = s0_ref[0, 1, 0]
    else:
        sf_ref[...] = jnp.zeros_like(sf_ref)
        sb_ref[...] = jnp.zeros_like(sb_ref)

    lane = lax.broadcasted_iota(jnp.int32, (C, RET_DK), 1)
    first_half = (lane & 32) == 0

    def rope(x, cs, sn):
        swapped = jnp.where(first_half, pltpu.roll(x, RET_DK - 32, axis=1), pltpu.roll(x, 32, axis=1))
        return x * cs + swapped * sn

    def state_update(s_ref, kz, v, cd):
        upd = lax.dot_general(kz, v, (((0,), (0,)), ((), ())), preferred_element_type=F32)
        s_ref[...] = cd * s_ref[...] + upd

    def fwd_body(c, carry):
        sl = pl.ds(pl.multiple_of(c * C, C), C)
        q = q_ref[sl, :]
        k = k_ref[sl, :] * (RET_DK ** -0.5)
        if use_rope:
            cs = cos_ref[sl, :]
            sn = sin_ref[sl, :]
            q = rope(q, cs, sn)
            k = rope(k, cs, sn)
        qb = q.astype(BF16)
        kb = k.astype(BF16)
        qs_ref[sl, :] = qb
        ks_ref[sl, :] = k
        v = v_ref[sl, :].astype(BF16)
        scores = lax.dot_general(qb, kb, (((1,), (1,)), ((), ())), preferred_element_type=F32)
        inner = jnp.dot((scores * decay).astype(BF16), v, preferred_element_type=F32)
        cross = jnp.dot(qb, sf_ref[...].astype(BF16), preferred_element_type=F32) * xi_f
        acc_ref[sl, :] = inner + cross
        state_update(sf_ref, (k * zeta_f).astype(BF16), v, cd_f)
        return carry

    lax.fori_loop(0, n_chunks, fwd_body, 0)

    def bwd_body(i, carry):
        c = n_chunks - 1 - i
        sl = pl.ds(pl.multiple_of(c * C, C), C)
        qb = qs_ref[sl, :]
        k = ks_ref[sl, :]
        v = v_ref[sl, :].astype(BF16)
        cross = jnp.dot(qb, sb_ref[...].astype(BF16), preferred_element_type=F32) * xi_b
        o = acc_ref[sl, :] + cross
        mu = jnp.mean(o, axis=-1, keepdims=True)
        oc = o - mu
        var = jnp.mean(oc * oc, axis=-1, keepdims=True)
        on = oc * lax.rsqrt(var + GN_EPS)
        g = g_ref[sl, :]
        o_ref[sl, :] = (on * (g * _sigmoid(g))).astype(BF16)
        state_update(sb_ref, (k * zeta_b).astype(BF16), v, cd_b)
        return carry

    lax.fori_loop(0, n_chunks, bwd_body, 0)

    if has_state_out:
        st_ref[0, 0, 0] = sf_ref[...]
        st_ref[0, 1, 0] = sb_ref[...]


def _retention(proj, row0, B, L, ret_decay, state_in, rope_tabs, want_state):
    rb = row0 // L
    use_rope = rope_tabs is not None
    has_state_in = state_in is not None
    kq = RET_QK_WIDTH // RET_DK
    kv = 2 * RET_QK_WIDTH // RET_DV
    kg = kv + RET_HEADS
    in_specs = [pl.BlockSpec((L, RET_DK), lambda b, h, rd: (rb + b, h)),
                pl.BlockSpec((L, RET_DK), lambda b, h, rd: (rb + b, kq + h)),
                pl.BlockSpec((L, RET_DV), lambda b, h, rd: (rb + b, kv + h)),
                pl.BlockSpec((L, RET_DV), lambda b, h, rd: (rb + b, kg + h))]
    args = [proj, proj, proj, proj]
    if has_state_in:
        in_specs.append(pl.BlockSpec((1, 2, 1, RET_DK, RET_DV), lambda b, h, rd: (b, 0, h, 0, 0)))
        args.append(state_in)
    if use_rope:
        in_specs += [pl.BlockSpec((L, RET_DK), lambda b, h, rd: (0, 0))] * 2
        args += list(rope_tabs)
    out_specs = [pl.BlockSpec((L, RET_DV), lambda b, h, rd: (b, h))]
    out_shape = [jax.ShapeDtypeStruct((B * L, RET_WIDTH), BF16)]
    if want_state:
        out_specs.append(pl.BlockSpec((1, 2, 1, RET_DK, RET_DV), lambda b, h, rd: (b, 0, h, 0, 0)))
        out_shape.append(jax.ShapeDtypeStruct((B, 2, RET_HEADS, RET_DK, RET_DV), F32))
    kern = functools.partial(_retention_kernel, seq_len=L, chunk=RET_CHUNK, use_rope=use_rope,
                             has_state_in=has_state_in, has_state_out=want_state)
    res = pl.pallas_call(
        kern,
        grid_spec=pltpu.PrefetchScalarGridSpec(
            num_scalar_prefetch=1,
            grid=(B, RET_HEADS),
            in_specs=in_specs,
            out_specs=out_specs,
            scratch_shapes=[pltpu.VMEM((L, RET_DK), BF16),
                            pltpu.VMEM((L, RET_DK), F32),
                            pltpu.VMEM((L, RET_DV), F32),
                            pltpu.VMEM((RET_DK, RET_DV), F32),
                            pltpu.VMEM((RET_DK, RET_DV), F32)]),
        out_shape=out_shape,
        compiler_params=_cparams(("arbitrary", "arbitrary"), 40 * 1024 * 1024),
        name="retention_grid" if use_rope else "retention_seq",
    )(ret_decay, *args)
    return res if want_state else (res[0], None)


def _split3(x):
    hi = x.astype(BF16)
    r1 = x - hi.astype(F32)
    mid = r1.astype(BF16)
    lo = (r1 - mid.astype(F32)).astype(BF16)
    return jnp.concatenate([hi, mid, lo], axis=-1)


def _pool_kernel(p_ref, w_ref, sc_ref, o_ref, pad_ref, *, seq_len, grid_mode):
    PT = POOL_TILE
    n_tiles = seq_len // PT
    ii = lax.broadcasted_iota(jnp.int32, (PT, PT), 0)
    jj = lax.broadcasted_iota(jnp.int32, (PT, PT), 1)
    d = jj - ii
    ti = lax.broadcasted_iota(jnp.int32, (PT, 1), 0)

    if grid_mode:
        zeros = jnp.zeros((POOL_PAD, POOL_DG), F32)
        pad_ref[0:POOL_PAD, :] = zeros
        pad_ref[POOL_PAD + seq_len:POOL_PAD + seq_len + POOL_PAD, :] = zeros

    for gi, w in enumerate(POOL_WINDOWS):
        lo_off = -(w // 2)
        hi_off = w - w // 2 - 1
        cols = slice(gi * POOL_DG, (gi + 1) * POOL_DG)
        in_window = (d >= lo_off) & (d <= hi_off)
        if grid_mode:
            in_window = in_window & ((ii // GRID_W) == (jj // GRID_W))
        band = jnp.where(in_window, 1.0, 0.0).astype(BF16)
        wg = w_ref[gi]
        scale = sc_ref[gi]

        def window_sum(t0):
            x = p_ref[pl.ds(t0, PT), cols]
            s3 = jnp.dot(band, _split3(x), preferred_element_type=F32)
            return s3[:, 0:POOL_DG] + s3[:, POOL_DG:2 * POOL_DG] + s3[:, 2 * POOL_DG:3 * POOL_DG]

        def finish(t0, s, cnt):
            x = p_ref[pl.ds(t0, PT), cols]
            diff = (s / cnt - x).astype(BF16)
            y = jnp.dot(diff, wg, preferred_element_type=F32) * scale
            o_ref[pl.ds(t0, PT), cols] = y.astype(BF16)

        def count(pos, n):
            lo = jnp.maximum(pos + lo_off, 0)
            hi = jnp.minimum(pos + lo_off + w, n)
            return (hi - lo).astype(F32)

        if grid_mode:
            for t in range(n_tiles):
                pad_ref[POOL_PAD + t * PT:POOL_PAD + (t + 1) * PT, :] = window_sum(t * PT)
            for t in range(n_tiles):
                base = POOL_PAD + t * PT
                s = pad_ref[base + lo_off * GRID_W:base + lo_off * GRID_W + PT, :]
                for r in range(lo_off + 1, hi_off + 1):
                    s = s + pad_ref[base + r * GRID_W:base + r * GRID_W + PT, :]
                tok = ti + t * PT
                cnt = count(tok // GRID_W, seq_len // GRID_W) * count(tok % GRID_W, GRID_W)
                finish(t * PT, s, cnt)
        else:
            for t in range(n_tiles):
                finish(t * PT, window_sum(t * PT), count(ti + t * PT, seq_len))


def _pooling(proj, row0, B, L, w_pool_bf16, pool_scale, grid_mode):
    if not grid_mode:
        assert L == POOL_TILE
    rb = row0 // L
    pcol = (IN_WIDTH - POOL_WIDTH) // POOL_WIDTH
    kern = functools.partial(_pool_kernel, seq_len=L, grid_mode=grid_mode)
    return pl.pallas_call(
        kern,
        grid=(B,),
        in_specs=[pl.BlockSpec((L, POOL_WIDTH), lambda b: (rb + b, pcol)),
                  pl.BlockSpec((POOL_GROUPS, POOL_DG, POOL_DG), lambda b: (0, 0, 0)),
                  pl.BlockSpec((POOL_GROUPS, 1, POOL_DG), lambda b: (0, 0, 0))],
        out_specs=pl.BlockSpec((L, POOL_WIDTH), lambda b: (b, 0)),
        out_shape=jax.ShapeDtypeStruct((B * L, POOL_WIDTH), BF16),
        scratch_shapes=[pltpu.VMEM((L + 2 * POOL_PAD, POOL_DG), F32)],
        compiler_params=_cparams(("arbitrary",), 48 * 1024 * 1024),
        name="pool_grid" if grid_mode else "pool_seq",
    )(proj, w_pool_bf16, pool_scale.reshape(POOL_GROUPS, 1, POOL_DG))


def _outproj_kernel(*refs):
    i = pl.program_id(0)
    h2_ref = refs[14]

    @pl.when(i >= T_ALL // TM_OUT)
    def _():
        h2_ref[...] = jnp.zeros_like(h2_ref)

    pl.when(i < T_ALL // TM_OUT)(lambda: _outproj_tile(*refs))


def _outproj_tile(retc_ref, retl_ref, poolc_ref, pooll_ref, xc_ref, xl_ref, mod_ref, g_ref,
                  wt_ref, wb_ref, wrh_ref, wrb_ref, br_ref,
                  x1_ref, h2_ref, ri_ref, rp_ref, rr_ref, cnt_ref, run_ref):
    i = pl.program_id(0)

    @pl.when(i == 0)
    def _():
        run_ref[...] = jnp.zeros_like(run_ref)

    is_ctx = i < T_CTX // TM_OUT
    ret = jnp.where(is_ctx, retc_ref[...], retl_ref[...])
    pool = jnp.where(is_ctx, poolc_ref[...], pooll_ref[...])
    x = jnp.where(is_ctx, xc_ref[...], xl_ref[...])
    y = (jnp.dot(ret, wt_ref[...], preferred_element_type=F32)
         + jnp.dot(pool, wb_ref[...], preferred_element_type=F32))
    x1 = x + mod_ref[0, 2:3, :] * y
    x1_ref[...] = x1
    hn = x1 * lax.rsqrt(jnp.mean(x1 * x1, axis=-1, keepdims=True) + NORM_EPS) * g_ref[...]
    h2 = hn * (1.0 + mod_ref[0, 4:5, :]) + mod_ref[0, 3:4, :]
    hi = h2.astype(BF16)
    h2_ref[...] = hi
    lo = (h2 - hi.astype(F32)).astype(BF16)
    both = jnp.dot(hi, wrb_ref[...], preferred_element_type=F32)
    logits = (both[:, 0:LANES] + jnp.dot(lo, wrh_ref[...], preferred_element_type=F32)
              + both[:, LANES:2 * LANES]) + br_ref[...]

    tm = logits.shape[0]
    lane = lax.broadcasted_iota(jnp.int32, (tm, LANES), 1)
    work = logits
    vals, idxs, hots = [], [], []
    for _ in range(TOP_K):
        m = jnp.max(work, axis=-1, keepdims=True)
        idx = jnp.min(jnp.where(work == m, lane, LANES), axis=-1, keepdims=True)
        hot = lane == idx
        vals.append(m)
        idxs.append(idx)
        hots.append(hot)
        work = jnp.where(hot, -jnp.inf, work)
    exps = [jnp.exp(v - vals[0]) for v in vals]
    denom = exps[0] + exps[1] + exps[2] + exps[3]

    selected = jnp.zeros((tm, LANES), F32)
    for hot in hots:
        selected = selected + jnp.where(hot, 1.0, 0.0)
    r_i = lax.broadcasted_iota(jnp.int32, (tm, tm), 0)
    c_i = lax.broadcasted_iota(jnp.int32, (tm, tm), 1)
    tri = jnp.where(c_i < r_i, 1.0, 0.0).astype(BF16)
    before = jnp.dot(tri, selected.astype(BF16), preferred_element_type=F32) + run_ref[0:1, :]

    ri = jnp.zeros((tm, LANES), jnp.int32)
    rp = jnp.zeros((tm, LANES), F32)
    rr = jnp.zeros((tm, LANES), jnp.int32)
    for k in range(TOP_K):
        rank = jnp.sum(jnp.where(hots[k], before, 0.0), axis=-1, keepdims=True).astype(jnp.int32)
        ri = jnp.where(lane == k, idxs[k], ri)
        rp = jnp.where(lane == k, exps[k] / denom, rp)
        rr = jnp.where(lane == k, rank, rr)
    ri_ref[...] = ri
    rp_ref[...] = rp
    rr_ref[...] = rr

    run = run_ref[0:1, :] + jnp.sum(selected, axis=0, keepdims=True)
    run_ref[...] = jnp.broadcast_to(run, run_ref.shape)
    cnt_ref[...] = jnp.broadcast_to(run, cnt_ref.shape).astype(jnp.int32)


def _out_projection(ret_ctx, ret_lat, pool_ctx, pool_lat, x_ctx, x_lat, mod3, norm_g, w_out_bf16,
                    wr_hi, wr_both, b_router_pad):
    tm = TM_OUT
    real = lambda i: jnp.minimum(i, T_ALL // tm - 1)
    row = lambda i: (real(i), 0)
    const = lambda i: (0, 0)
    crow = lambda i: (_ctx_tile(real(i), tm), 0)
    lrow = lambda i: (_lat_tile(real(i), tm), 0)
    return pl.pallas_call(
        _outproj_kernel,
        grid=(H2_ROWS // tm,),
        in_specs=[pl.BlockSpec((tm, RET_WIDTH), crow),
                  pl.BlockSpec((tm, RET_WIDTH), lrow),
                  pl.BlockSpec((tm, POOL_WIDTH), crow),
                  pl.BlockSpec((tm, POOL_WIDTH), lrow),
                  pl.BlockSpec((tm, D_MODEL), crow),
                  pl.BlockSpec((tm, D_MODEL), lrow),
                  pl.BlockSpec((1, 6, D_MODEL), lambda i: (_cond_row(real(i), tm), 0, 0)),
                  pl.BlockSpec((1, D_MODEL), const),
                  pl.BlockSpec((RET_WIDTH, D_MODEL), const),
                  pl.BlockSpec((POOL_WIDTH, D_MODEL), lambda i: (1, 0)),
                  pl.BlockSpec((D_MODEL, LANES), const),
                  pl.BlockSpec((D_MODEL, 2 * LANES), const),
                  pl.BlockSpec((1, LANES), const)],
        out_specs=[pl.BlockSpec((tm, D_MODEL), row),
                   pl.BlockSpec((tm, D_MODEL), lambda i: (i, 0)),
                   pl.BlockSpec((tm, LANES), row),
                   pl.BlockSpec((tm, LANES), row),
                   pl.BlockSpec((tm, LANES), row),
                   pl.BlockSpec((8, LANES), const)],
        out_shape=[jax.ShapeDtypeStruct((T_ALL, D_MODEL), F32),
                   jax.ShapeDtypeStruct((H2_ROWS, D_MODEL), BF16),
                   jax.ShapeDtypeStruct((T_ALL, LANES), jnp.int32),
                   jax.ShapeDtypeStruct((T_ALL, LANES), F32),
                   jax.ShapeDtypeStruct((T_ALL, LANES), jnp.int32),
                   jax.ShapeDtypeStruct((8, LANES), jnp.int32)],
        scratch_shapes=[pltpu.VMEM((8, LANES), F32)],
        compiler_params=_cparams(("arbitrary",), 48 * 1024 * 1024),
        name="out_projection_router",
    )(ret_ctx, ret_lat, pool_ctx, pool_lat, x_ctx, x_lat, mod3, norm_g.reshape(1, D_MODEL),
      w_out_bf16, w_out_bf16, wr_hi, wr_both, b_router_pad)


def _count_le(ends, v):
    return jnp.sum((ends[None, :] <= v[:, None]).astype(jnp.int32), axis=1)


def _expert_schedule(counts):
    i32 = jnp.int32
    ntile = (counts + TM_EXP - 1) // TM_EXP
    tile_off = jnp.cumsum(ntile) - ntile
    nsup = (ntile + SUP_EXP - 1) // SUP_EXP
    sup_end = jnp.cumsum(nsup)
    u = jnp.arange(U_EXP, dtype=i32)
    sup_valid = u < sup_end[-1]
    uc = jnp.minimum(u, sup_end[-1] - 1)
    sup_e = jnp.minimum(_count_le(sup_end, uc), N_EXPERTS - 1)
    sup_idx = uc - (sup_end - nsup)[sup_e]
    sup_row0 = (tile_off[sup_e] + sup_idx * SUP_EXP) * TM_EXP
    sup_nt = jnp.clip(ntile[sup_e] - sup_idx * SUP_EXP, 1, SUP_EXP)
    step_e = jnp.repeat(sup_e, J_EXP)
    step_valid = jnp.repeat(sup_valid.astype(i32), J_EXP)
    step_j = jnp.where(step_valid == 1, jnp.tile(jnp.arange(J_EXP, dtype=i32), U_EXP), J_EXP - 1)
    step_row0 = jnp.repeat(sup_row0, J_EXP)
    step_nt = jnp.repeat(sup_nt, J_EXP)
    last_rows = counts - (ntile - 1) * TM_EXP
    sup_half = (sup_idx == nsup[sup_e] - 1) & (last_rows[sup_e] <= TM_EXP // 2) & (counts[sup_e] > 0)
    step_half = jnp.repeat(sup_half.astype(i32), J_EXP)
    return tile_off * TM_EXP, (step_e.astype(i32), step_j.astype(i32), step_row0.astype(i32),
                               step_nt.astype(i32), step_half, step_valid,
                               jnp.sum(ntile).astype(i32).reshape(1))


def _expert_kernel(se_ref, sj_ref, srow_ref, snt_ref, shalf_ref, sv_ref, stot_ref,
                   xs_hbm, wg_ref, wu_ref, bg_ref, bu_ref, wd_ref, bd_ref, ys_hbm,
                   wg_s, wu_s, wd_s, acc_ref, x_buf, o_buf, x_sem, o_sem):
    s = pl.program_id(0)
    j = sj_ref[s]
    n_tiles = snt_ref[s]
    row0 = pl.multiple_of(srow_ref[s], TM_EXP)
    last = J_EXP - 1

    def x_copy():
        return pltpu.make_async_copy(xs_hbm.at[pl.ds(row0, SUP_EXP * TM_EXP)], x_buf, x_sem)

    def out_copy(t):
        rows = pl.ds(pl.multiple_of(row0 + t * TM_EXP, TM_EXP), TM_EXP)
        return pltpu.make_async_copy(o_buf.at[t % 2], ys_hbm.at[rows], o_sem.at[t % 2])

    @pl.when(s == 0)
    def _():
        o_buf[0] = jnp.zeros((TM_EXP, D_MODEL), BF16)

        def zero_copy(t):
            rows = pl.ds(pl.multiple_of(t * TM_EXP, TM_EXP), TM_EXP)
            return pltpu.make_async_copy(o_buf.at[0], ys_hbm.at[rows], o_sem.at[0])

        def start(t, carry):
            zero_copy(t).start()
            return carry

        def wait(t, carry):
            zero_copy(t).wait()
            return carry

        lax.fori_loop(stot_ref[0], NT_EXP, start, 0)
        lax.fori_loop(stot_ref[0], NT_EXP, wait, 0)

    @pl.when(sv_ref[s] == 1)
    def _():
        @pl.when(j == 0)
        def _():
            x_copy().start()
            acc_ref[...] = jnp.zeros_like(acc_ref)

        wg_s[...] = wg_ref[0].astype(BF16)
        wu_s[...] = wu_ref[0].astype(BF16)
        wd_s[...] = wd_ref[0].astype(BF16)

        @pl.when(j == 0)
        def _():
            x_copy().wait()

        def accumulate(rows):
            x = x_buf[rows, :]
            gate = jnp.dot(x, wg_s[...], preferred_element_type=F32) + bg_ref[0]
            up = jnp.dot(x, wu_s[...], preferred_element_type=F32) + bu_ref[0]
            gate = jnp.minimum(gate, SWIGLU_LIMIT)
            up = jnp.clip(up, -SWIGLU_LIMIT, SWIGLU_LIMIT)
            act = (up + 1.0) * gate * _sigmoid(SWIGLU_ALPHA * gate)
            acc_ref[rows, :] += jnp.dot(act.astype(BF16), wd_s[...], preferred_element_type=F32)

        def tile_pair(p, carry):
            base = pl.multiple_of(p * (2 * TM_EXP), 2 * TM_EXP)
            accumulate(pl.ds(base, TM_EXP))
            accumulate(pl.ds(base + TM_EXP, TM_EXP))
            return carry

        n_full = n_tiles - shalf_ref[s]
        lax.fori_loop(0, n_full // 2, tile_pair, 0)

        @pl.when(n_full % 2 == 1)
        def _():
            accumulate(pl.ds(pl.multiple_of((n_full - 1) * TM_EXP, TM_EXP), TM_EXP))

        @pl.when(shalf_ref[s] == 1)
        def _():
            accumulate(pl.ds(pl.multiple_of((n_tiles - 1) * TM_EXP, TM_EXP), TM_EXP // 2))

        @pl.when(j == last)
        def _():
            def finish(t, carry):
                @pl.when(t >= 2)
                def _():
                    out_copy(t - 2).wait()

                rows = pl.ds(pl.multiple_of(t * TM_EXP, TM_EXP), TM_EXP)
                o_buf[t % 2] = (acc_ref[rows, :] + bd_ref[0]).astype(BF16)
                out_copy(t).start()
                return carry

            lax.fori_loop(0, n_tiles, finish, 0)

            @pl.when(n_tiles >= 2)
            def _():
                out_copy(n_tiles - 2).wait()

            out_copy(n_tiles - 1).wait()


def _experts(xs, sched, w_gate_up, b_gate_up, w_down, b_down):
    wmap = lambda col0: (lambda s, se, sj, srow, snt, shalf, sv, stot: (se[s], 0, col0 + sj[s]))
    return pl.pallas_call(
        _expert_kernel,
        grid_spec=pltpu.PrefetchScalarGridSpec(
            num_scalar_prefetch=7,
            grid=(G_EXP,),
            in_specs=[
                pl.BlockSpec(memory_space=pl.ANY),
                pl.BlockSpec((1, D_MODEL, F_EXP), wmap(0)),
                pl.BlockSpec((1, D_MODEL, F_EXP), wmap(J_EXP)),
                pl.BlockSpec((1, 1, F_EXP), wmap(0)),
                pl.BlockSpec((1, 1, F_EXP), wmap(J_EXP)),
                pl.BlockSpec((1, F_EXP, D_MODEL), lambda s, se, sj, srow, snt, shalf, sv, stot: (se[s], sj[s], 0)),
                pl.BlockSpec((1, 1, D_MODEL), lambda s, se, sj, srow, snt, shalf, sv, stot: (se[s], 0, 0)),
            ],
            out_specs=pl.BlockSpec(memory_space=pl.ANY),
            scratch_shapes=[pltpu.VMEM((D_MODEL, F_EXP), BF16),
                            pltpu.VMEM((D_MODEL, F_EXP), BF16),
                            pltpu.VMEM((F_EXP, D_MODEL), BF16),
                            pltpu.VMEM((SUP_EXP * TM_EXP, D_MODEL), F32),
                            pltpu.VMEM((SUP_EXP * TM_EXP, D_MODEL), BF16),
                            pltpu.VMEM((2, TM_EXP, D_MODEL), BF16),
                            pltpu.SemaphoreType.DMA(()),
                            pltpu.SemaphoreType.DMA((2,))]),
        out_shape=jax.ShapeDtypeStruct((NT_EXP * TM_EXP, D_MODEL), BF16),
        compiler_params=_cparams(("arbitrary",), VMEM_LIMIT),
        name="experts",
    )(*sched, xs, w_gate_up, w_gate_up,
      b_gate_up.reshape(N_EXPERTS, 1, 2 * D_FF), b_gate_up.reshape(N_EXPERTS, 1, 2 * D_FF),
      w_down, b_down.reshape(N_EXPERTS, 1, D_MODEL))


def _combine_kernel(x1_ref, y0_ref, y1_ref, y2_ref, y3_ref, rp_ref, mod_ref, g_ref, o_ref):
    rp = rp_ref[...]
    moe = rp[:, 0:1] * y0_ref[...].astype(F32)
    for k, y_ref in enumerate((y1_ref, y2_ref, y3_ref), start=1):
        moe = moe + rp[:, k:k + 1] * y_ref[...].astype(F32)
    x2 = x1_ref[...] + mod_ref[0, 5:6, :] * moe
    o_ref[...] = x2 * lax.rsqrt(jnp.mean(x2 * x2, axis=-1, keepdims=True) + NORM_EPS) * g_ref[...]


def _combine(x1, y4, route_p, mod3, final_g, row0, n_rows):
    tm = TM_CMB
    nt = n_rows // tm
    t0 = row0 // tm
    y_specs = [pl.BlockSpec((tm, D_MODEL), functools.partial(lambda k, i: (k * nt + i, 0), k))
               for k in range(TOP_K)]
    return pl.pallas_call(
        _combine_kernel,
        grid=(nt,),
        in_specs=[pl.BlockSpec((tm, D_MODEL), lambda i: (t0 + i, 0))] + y_specs + [
            pl.BlockSpec((tm, LANES), lambda i: (t0 + i, 0)),
            pl.BlockSpec((1, 6, D_MODEL), lambda i: (_cond_row(t0 + i, tm), 0, 0)),
            pl.BlockSpec((1, D_MODEL), lambda i: (0, 0))],
        out_specs=pl.BlockSpec((tm, D_MODEL), lambda i: (i, 0)),
        out_shape=jax.ShapeDtypeStruct((n_rows, D_MODEL), F32),
        compiler_params=_cparams(("arbitrary",), 40 * 1024 * 1024),
        name="combine_final_norm",
    )(x1, y4, y4, y4, y4, route_p, mod3, final_g.reshape(1, D_MODEL))


def kernel(x_prompt, x_sample, c, c_ctx, state_ret, norm1_g, norm2_g, w_mod, b_mod, w_in, ret_decay,
           w_pool, pool_scale, w_out, w_router, b_router, w_gate_up, b_gate_up, w_down, b_down, final_g):
    assert w_mod.shape[0] == 1, "single trunk layer"
    x_ctx = x_prompt.reshape(T_CTX, D_MODEL)
    x_lat = x_sample.reshape(T_LAT, D_MODEL)
    cond = jnp.zeros((COND_ROWS, D_MODEL), F32).at[0].set(c_ctx).at[1:N_COND].set(c)

    mod = _modulation(cond, w_mod[0], b_mod[0])
    mod3 = mod.reshape(COND_ROWS, 6, D_MODEL)

    proj = _in_projection(x_ctx, x_lat, norm1_g[0], mod3, w_in[0].astype(BF16))

    ret_ctx, new_state = _retention(proj, 0, BATCH, SEQ, ret_decay[0], None, None, True)
    ret_lat, _ = _retention(proj, T_CTX, DEC_BATCH, DEC_SEQ, ret_decay[0], state_ret[:, 0],
                            _rope_tables(DEC_SEQ), False)
    w_pool_bf16 = w_pool[0].astype(BF16)
    pool_ctx = _pooling(proj, 0, BATCH, SEQ, w_pool_bf16, pool_scale[0], False)
    pool_lat = _pooling(proj, T_CTX, DEC_BATCH, DEC_SEQ, w_pool_bf16, pool_scale[0], True)

    wr = jnp.zeros((D_MODEL, LANES), F32).at[:, :N_EXPERTS].set(w_router[0])
    wr_hi = wr.astype(BF16)
    wr_both = jnp.concatenate([wr_hi, (wr - wr_hi.astype(F32)).astype(BF16)], axis=1)
    br = jnp.full((1, LANES), -1e30, F32).at[0, :N_EXPERTS].set(b_router[0])
    x1, h2, route_i, route_p, route_r, counts = _out_projection(
        ret_ctx, ret_lat, pool_ctx, pool_lat, x_ctx, x_lat, mod3, norm2_g[0], w_out[0].astype(BF16),
        wr_hi, wr_both, br)

    row_off, sched = _expert_schedule(counts[0, :N_EXPERTS])
    slot = row_off[route_i[:, :TOP_K]] + route_r[:, :TOP_K]
    token = jnp.broadcast_to(jnp.arange(T_ALL, dtype=jnp.int32)[:, None], (T_ALL, TOP_K))
    token_of_slot = (jnp.arange(XS_ROWS, dtype=jnp.int32) % T_ALL).at[slot.reshape(-1)].set(
        token.reshape(-1), unique_indices=True, mode="promise_in_bounds")
    xs = h2.at[token_of_slot].get(mode="promise_in_bounds")
    ys = _experts(xs, sched, w_gate_up[0], b_gate_up[0], w_down[0], b_down[0])
    y4_lat = ys.at[slot[T_CTX:].T.reshape(-1)].get(mode="promise_in_bounds", unique_indices=True)
    y4_ctx = ys.at[slot[:T_CTX].T.reshape(-1)].get(mode="promise_in_bounds", unique_indices=True)
    y_lat = _combine(x1, y4_lat, route_p, mod3, final_g, T_CTX, T_LAT)
    y_ctx = _combine(x1, y4_ctx, route_p, mod3, final_g, 0, T_CTX)
    return (y_ctx.reshape(BATCH, SEQ, D_MODEL), y_lat.reshape(DEC_BATCH, DEC_SEQ, D_MODEL),
            new_state.reshape(BATCH, 1, 2, RET_HEADS, RET_DK, RET_DV))
```

```python
import functools

import numpy as np
import jax
import jax.numpy as jnp
from jax import lax
from jax.experimental import pallas as pl
from jax.experimental.pallas import tpu as pltpu

F32 = jnp.float32
BF16 = jnp.bfloat16

D_MODEL = 2048
BATCH = 16
SEQ = 256
DEC_BATCH = 4
DEC_SEQ = 2048
GRID_W = 64
RET_HEADS = 4
RET_DK = 128
RET_DV = 256
RET_QK_WIDTH = RET_HEADS * RET_DK
RET_WIDTH = RET_HEADS * RET_DV
POOL_WINDOWS = (2, 4, 8, 16)
POOL_GROUPS = 4
POOL_DG = 256
POOL_WIDTH = POOL_GROUPS * POOL_DG
IN_WIDTH = 2 * RET_QK_WIDTH + 2 * RET_WIDTH + POOL_WIDTH
N_EXPERTS = 32
TOP_K = 4
D_FF = D_MODEL
SWIGLU_LIMIT = 7.0
SWIGLU_ALPHA = 1.702
ROPE_BASE = 10000.0
NORM_EPS = 1e-6
GN_EPS = 1e-6

T_CTX = BATCH * SEQ
T_LAT = DEC_BATCH * DEC_SEQ
T_ALL = T_CTX + T_LAT
N_COND = 1 + DEC_BATCH
COND_ROWS = 8
LANES = 128

RET_CHUNK = 256

TM_IN = 512
TN_IN = 1024
TM_OUT = 256
TM_CMB = 256
H2_ROWS = 16384
POOL_TILE = 256
POOL_PAD = (max(POOL_WINDOWS) // 2) * GRID_W

TM_EXP = 256
F_EXP = 512
J_EXP = D_FF // F_EXP
SUP_EXP = 6
NT_EXP = (T_ALL * TOP_K + N_EXPERTS * (TM_EXP - 1)) // TM_EXP
U_EXP = N_EXPERTS + NT_EXP // SUP_EXP + 1
G_EXP = J_EXP * U_EXP
XS_ROWS = (NT_EXP + SUP_EXP) * TM_EXP

VMEM_LIMIT = 56 * 1024 * 1024


def _cparams(sem, vmem=None):
    return pltpu.CompilerParams(dimension_semantics=sem, vmem_limit_bytes=vmem)


def _cond_row(i, tm):
    nctx = T_CTX // tm
    return jnp.where(i < nctx, 0, 1 + (i - nctx) // (DEC_SEQ // tm))


def _sigmoid(x):
    return 1.0 / (1.0 + jnp.exp(-x))


def _dot(a, b):
    return lax.dot_general(a, b, (((1,), (0,)), ((), ())), preferred_element_type=F32)


def _mod_kernel(c_ref, w_ref, b_ref, o_ref):
    c = c_ref[...]
    s = (c * _sigmoid(c)).astype(BF16)
    o_ref[...] = jnp.dot(s, w_ref[...].astype(BF16), preferred_element_type=F32) + b_ref[...]


def _modulation(cond, w_mod, b_mod):
    n = w_mod.shape[1]
    tn = 1024
    return pl.pallas_call(
        _mod_kernel,
        grid=(n // tn,),
        in_specs=[pl.BlockSpec((COND_ROWS, D_MODEL), lambda j: (0, 0)),
                  pl.BlockSpec((D_MODEL, tn), lambda j: (0, j)),
                  pl.BlockSpec((1, tn), lambda j: (0, j))],
        out_specs=pl.BlockSpec((COND_ROWS, tn), lambda j: (0, j)),
        out_shape=jax.ShapeDtypeStruct((COND_ROWS, n), F32),
        compiler_params=_cparams(("arbitrary",), 40 * 1024 * 1024),
        name="modulation",
    )(cond, w_mod, b_mod.reshape(1, n))


def _ctx_tile(i, tm):
    return jnp.minimum(i, T_CTX // tm - 1)


def _lat_tile(i, tm):
    return jnp.maximum(i - T_CTX // tm, 0)


def _inproj_kernel(xc_ref, xl_ref, g_ref, mod_ref, w_ref, o_ref, h_ref):
    def modulated_norm(x_ref):
        x = x_ref[...]
        y = x * lax.rsqrt(jnp.mean(x * x, axis=-1, keepdims=True) + NORM_EPS) * g_ref[...]
        shift = mod_ref[0, 0:1, :]
        scale = mod_ref[0, 1:2, :]
        h_ref[...] = (y * (1.0 + scale) + shift).astype(BF16)

    first = pl.program_id(1) == 0
    is_ctx = pl.program_id(0) < T_CTX // TM_IN
    pl.when(first & is_ctx)(lambda: modulated_norm(xc_ref))
    pl.when(first & jnp.logical_not(is_ctx))(lambda: modulated_norm(xl_ref))

    o_ref[...] = jnp.dot(h_ref[...], w_ref[...], preferred_element_type=F32)


def _in_projection(x_ctx, x_lat, norm_g, mod3, w_in_bf16):
    return pl.pallas_call(
        _inproj_kernel,
        grid=(T_ALL // TM_IN, IN_WIDTH // TN_IN),
        in_specs=[pl.BlockSpec((TM_IN, D_MODEL), lambda i, j: (_ctx_tile(i, TM_IN), 0)),
                  pl.BlockSpec((TM_IN, D_MODEL), lambda i, j: (_lat_tile(i, TM_IN), 0)),
                  pl.BlockSpec((1, D_MODEL), lambda i, j: (0, 0)),
                  pl.BlockSpec((1, 6, D_MODEL), lambda i, j: (_cond_row(i, TM_IN), 0, 0)),
                  pl.BlockSpec((D_MODEL, TN_IN), lambda i, j: (0, j))],
        out_specs=pl.BlockSpec((TM_IN, TN_IN), lambda i, j: (i, j)),
        out_shape=jax.ShapeDtypeStruct((T_ALL, IN_WIDTH), F32),
        scratch_shapes=[pltpu.VMEM((TM_IN, D_MODEL), BF16)],
        compiler_params=_cparams(("arbitrary", "arbitrary"), 40 * 1024 * 1024),
        name="in_projection",
    )(x_ctx, x_lat, norm_g.reshape(1, D_MODEL), mod3, w_in_bf16)


def _rope_tables(seq_len):
    t = jnp.arange(seq_len)
    row = (t // GRID_W).astype(F32)
    col = (t % GRID_W).astype(F32)
    half = RET_DK // 2
    n_freq = half // 2
    inv = ROPE_BASE ** (-jnp.arange(n_freq, dtype=F32) / n_freq)
    ang_r = row[:, None] * inv
    ang_c = col[:, None] * inv
    cos = jnp.concatenate([jnp.cos(ang_r), jnp.cos(ang_r), jnp.cos(ang_c), jnp.cos(ang_c)], axis=-1)
    sin = jnp.concatenate([-jnp.sin(ang_r), jnp.sin(ang_r), -jnp.sin(ang_c), jnp.sin(ang_c)], axis=-1)
    return cos, sin


def _retention_kernel(rd_ref, *refs, seq_len, chunk, use_rope, has_state_in, has_state_out):
    refs = list(refs)
    q_ref, k_ref, v_ref, g_ref = refs[:4]
    pos = 4
    if has_state_in:
        s0_ref = refs[pos]
        pos += 1
    if use_rope:
        cos_ref, sin_ref = refs[pos], refs[pos + 1]
        pos += 2
    o_ref = refs[pos]
    pos += 1
    if has_state_out:
        st_ref = refs[pos]
        pos += 1
    qs_ref, ks_ref, acc_ref, sf_ref, sb_ref = refs[pos:pos + 5]

    C = chunk
    n_chunks = seq_len // C
    h = pl.program_id(1)

    lgf = -jnp.exp(jnp.full((C, 1), rd_ref[0, h], F32))
    lgb = -jnp.exp(jnp.full((C, 1), rd_ref[1, h], F32))
    ii = lax.broadcasted_iota(jnp.int32, (C, C), 0)
    jj = lax.broadcasted_iota(jnp.int32, (C, C), 1)
    diff = (ii - jj).astype(F32)
    decay = (jnp.where(diff >= 0, jnp.exp(lgf * jnp.maximum(diff, 0.0)), 0.0)
             + jnp.where(diff <= 0, jnp.exp(lgb * jnp.maximum(-diff, 0.0)), 0.0))
    p = lax.broadcasted_iota(jnp.int32, (C, 1), 0).astype(F32)
    xi_f = jnp.exp(lgf * (p + 1.0))
    zeta_f = jnp.exp(lgf * (C - 1.0 - p))
    xi_b = jnp.exp(lgb * (C - p))
    zeta_b = jnp.exp(lgb * p)
    cd_f = jnp.exp(lgf[0:1, :] * C)
    cd_b = jnp.exp(lgb[0:1, :] * C)

    if has_state_in:
        sf_ref[...] = s0_ref[0, 0, 0]
        sb_ref[...] = s0_ref[0, 1, 0]
    else:
        sf_ref[...] = jnp.zeros_like(sf_ref)
        sb_ref[...] = jnp.zeros_like(sb_ref)

    lane = lax.broadcasted_iota(jnp.int32, (C, RET_DK), 1)
    first_half = (lane & 32) == 0

    def rope(x, cs, sn):
        swapped = jnp.where(first_half, pltpu.roll(x, RET_DK - 32, axis=1), pltpu.roll(x, 32, axis=1))
        return x * cs + swapped * sn

    def state_update(s_ref, kz, v, cd):
        upd = lax.dot_general(kz, v, (((0,), (0,)), ((), ())), preferred_element_type=F32)
        s_ref[...] = cd * s_ref[...] + upd

    def fwd_body(c, carry):
        sl = pl.ds(pl.multiple_of(c * C, C), C)
        q = q_ref[sl, :]
        k = k_ref[sl, :] * (RET_DK ** -0.5)
        if use_rope:
            cs = cos_ref[sl, :]
            sn = sin_ref[sl, :]
            q = rope(q, cs, sn)
            k = rope(k, cs, sn)
        qb = q.astype(BF16)
        kb = k.astype(BF16)
        qs_ref[sl, :] = qb
        ks_ref[sl, :] = k
        v = v_ref[sl, :].astype(BF16)
        scores = lax.dot_general(qb, kb, (((1,), (1,)), ((), ())), preferred_element_type=F32)
        inner = jnp.dot((scores * decay).astype(BF16), v, preferred_element_type=F32)
        cross = jnp.dot(qb, sf_ref[...].astype(BF16), preferred_element_type=F32) * xi_f
        acc_ref[sl, :] = inner + cross
        state_update(sf_ref, (k * zeta_f).astype(BF16), v, cd_f)
        return carry

    lax.fori_loop(0, n_chunks, fwd_body, 0)

    def bwd_body(i, carry):
        c = n_chunks - 1 - i
        sl = pl.ds(pl.multiple_of(c * C, C), C)
        qb = qs_ref[sl, :]
        k = ks_ref[sl, :]
        v = v_ref[sl, :].astype(BF16)
        cross = jnp.dot(qb, sb_ref[...].astype(BF16), preferred_element_type=F32) * xi_b
        o = acc_ref[sl, :] + cross
        mu = jnp.mean(o, axis=-1, keepdims=True)
        oc = o - mu
        var = jnp.mean(oc * oc, axis=-1, keepdims=True)
        on = oc * lax.rsqrt(var + GN_EPS)
        g = g_ref[sl, :]
        o_ref[sl, :] = (on * (g * _sigmoid(g))).astype(BF16)
        state_update(sb_ref, (k * zeta_b).astype(BF16), v, cd_b)
        return carry

    lax.fori_loop(0, n_chunks, bwd_body, 0)

    if has_state_out:
        st_ref[0, 0, 0] = sf_ref[...]
        st_ref[0, 1, 0] = sb_ref[...]


def _retention(proj, row0, B, L, ret_decay, state_in, rope_tabs, want_state):
    rb = row0 // L
    use_rope = rope_tabs is not None
    has_state_in = state_in is not None
    kq = RET_QK_WIDTH // RET_DK
    kv = 2 * RET_QK_WIDTH // RET_DV
    kg = kv + RET_HEADS
    in_specs = [pl.BlockSpec((L, RET_DK), lambda b, h, rd: (rb + b, h)),
                pl.BlockSpec((L, RET_DK), lambda b, h, rd: (rb + b, kq + h)),
                pl.BlockSpec((L, RET_DV), lambda b, h, rd: (rb + b, kv + h)),
                pl.BlockSpec((L, RET_DV), lambda b, h, rd: (rb + b, kg + h))]
    args = [proj, proj, proj, proj]
    if has_state_in:
        in_specs.append(pl.BlockSpec((1, 2, 1, RET_DK, RET_DV), lambda b, h, rd: (b, 0, h, 0, 0)))
        args.append(state_in)
    if use_rope:
        in_specs += [pl.BlockSpec((L, RET_DK), lambda b, h, rd: (0, 0))] * 2
        args += list(rope_tabs)
    out_specs = [pl.BlockSpec((L, RET_DV), lambda b, h, rd: (b, h))]
    out_shape = [jax.ShapeDtypeStruct((B * L, RET_WIDTH), BF16)]
    if want_state:
        out_specs.append(pl.BlockSpec((1, 2, 1, RET_DK, RET_DV), lambda b, h, rd: (b, 0, h, 0, 0)))
        out_shape.append(jax.ShapeDtypeStruct((B, 2, RET_HEADS, RET_DK, RET_DV), F32))
    kern = functools.partial(_retention_kernel, seq_len=L, chunk=RET_CHUNK, use_rope=use_rope,
                             has_state_in=has_state_in, has_state_out=want_state)
    res = pl.pallas_call(
        kern,
        grid_spec=pltpu.PrefetchScalarGridSpec(
            num_scalar_prefetch=1,
            grid=(B, RET_HEADS),
            in_specs=in_specs,
            out_specs=out_specs,
            scratch_shapes=[pltpu.VMEM((L, RET_DK), BF16),
                            pltpu.VMEM((L, RET_DK), F32),
                            pltpu.VMEM((L, RET_DV), F32),
                            pltpu.VMEM((RET_DK, RET_DV), F32),
                            pltpu.VMEM((RET_DK, RET_DV), F32)]),
        out_shape=out_shape,
        compiler_params=_cparams(("arbitrary", "arbitrary"), 40 * 1024 * 1024),
        name="retention_grid" if use_rope else "retention_seq",
    )(ret_decay, *args)
    return res if want_state else (res[0], None)


def _split3(x):
    hi = x.astype(BF16)
    r1 = x - hi.astype(F32)
    mid = r1.astype(BF16)
    lo = (r1 - mid.astype(F32)).astype(BF16)
    return jnp.concatenate([hi, mid, lo], axis=-1)


def _pool_kernel(p_ref, w_ref, sc_ref, o_ref, pad_ref, *, seq_len, grid_mode):
    PT = POOL_TILE
    n_tiles = seq_len // PT
    ii = lax.broadcasted_iota(jnp.int32, (PT, PT), 0)
    jj = lax.broadcasted_iota(jnp.int32, (PT, PT), 1)
    d = jj - ii
    ti = lax.broadcasted_iota(jnp.int32, (PT, 1), 0)

    if grid_mode:
        zeros = jnp.zeros((POOL_PAD, POOL_DG), F32)
        pad_ref[0:POOL_PAD, :] = zeros
        pad_ref[POOL_PAD + seq_len:POOL_PAD + seq_len + POOL_PAD, :] = zeros

    for gi, w in enumerate(POOL_WINDOWS):
        lo_off = -(w // 2)
        hi_off = w - w // 2 - 1
        cols = slice(gi * POOL_DG, (gi + 1) * POOL_DG)
        in_window = (d >= lo_off) & (d <= hi_off)
        if grid_mode:
            in_window = in_window & ((ii // GRID_W) == (jj // GRID_W))
        band = jnp.where(in_window, 1.0, 0.0).astype(BF16)
        wg = w_ref[gi]
        scale = sc_ref[gi]

        def window_sum(t0):
            x = p_ref[pl.ds(t0, PT), cols]
            s3 = jnp.dot(band, _split3(x), preferred_element_type=F32)
            return s3[:, 0:POOL_DG] + s3[:, POOL_DG:2 * POOL_DG] + s3[:, 2 * POOL_DG:3 * POOL_DG]

        def finish(t0, s, cnt):
            x = p_ref[pl.ds(t0, PT), cols]
            diff = (s / cnt - x).astype(BF16)
            y = jnp.dot(diff, wg, preferred_element_type=F32) * scale
            o_ref[pl.ds(t0, PT), cols] = y.astype(BF16)

        def count(pos, n):
            lo = jnp.maximum(pos + lo_off, 0)
            hi = jnp.minimum(pos + lo_off + w, n)
            return (hi - lo).astype(F32)

        if grid_mode:
            for t in range(n_tiles):
                pad_ref[POOL_PAD + t * PT:POOL_PAD + (t + 1) * PT, :] = window_sum(t * PT)
            for t in range(n_tiles):
                base = POOL_PAD + t * PT
                s = pad_ref[base + lo_off * GRID_W:base + lo_off * GRID_W + PT, :]
                for r in range(lo_off + 1, hi_off + 1):
                    s = s + pad_ref[base + r * GRID_W:base + r * GRID_W + PT, :]
                tok = ti + t * PT
                cnt = count(tok // GRID_W, seq_len // GRID_W) * count(tok % GRID_W, GRID_W)
                finish(t * PT, s, cnt)
        else:
            for t in range(n_tiles):
                finish(t * PT, window_sum(t * PT), count(ti + t * PT, seq_len))


def _pooling(proj, row0, B, L, w_pool_bf16, pool_scale, grid_mode):
    if not grid_mode:
        assert L == POOL_TILE
    rb = row0 // L
    pcol = (IN_WIDTH - POOL_WIDTH) // POOL_WIDTH
    kern = functools.partial(_pool_kernel, seq_len=L, grid_mode=grid_mode)
    return pl.pallas_call(
        kern,
        grid=(B,),
        in_specs=[pl.BlockSpec((L, POOL_WIDTH), lambda b: (rb + b, pcol)),
                  pl.BlockSpec((POOL_GROUPS, POOL_DG, POOL_DG), lambda b: (0, 0, 0)),
                  pl.BlockSpec((POOL_GROUPS, 1, POOL_DG), lambda b: (0, 0, 0))],
        out_specs=pl.BlockSpec((L, POOL_WIDTH), lambda b: (b, 0)),
        out_shape=jax.ShapeDtypeStruct((B * L, POOL_WIDTH), BF16),
        scratch_shapes=[pltpu.VMEM((L + 2 * POOL_PAD, POOL_DG), F32)],
        compiler_params=_cparams(("arbitrary",), 48 * 1024 * 1024),
        name="pool_grid" if grid_mode else "pool_seq",
    )(proj, w_pool_bf16, pool_scale.reshape(POOL_GROUPS, 1, POOL_DG))


def _outproj_kernel(*refs):
    i = pl.program_id(0)
    h2_ref = refs[14]

    @pl.when(i >= T_ALL // TM_OUT)
    def _():
        h2_ref[...] = jnp.zeros_like(h2_ref)

    pl.when(i < T_ALL // TM_OUT)(lambda: _outproj_tile(*refs))


def _outproj_tile(retc_ref, retl_ref, poolc_ref, pooll_ref, xc_ref, xl_ref, mod_ref, g_ref,
                  wt_ref, wb_ref, wrh_ref, wrb_ref, br_ref,
                  x1_ref, h2_ref, ri_ref, rp_ref, rr_ref, cnt_ref, run_ref):
    i = pl.program_id(0)

    @pl.when(i == 0)
    def _():
        run_ref[...] = jnp.zeros_like(run_ref)

    is_ctx = i < T_CTX // TM_OUT
    ret = jnp.where(is_ctx, retc_ref[...], retl_ref[...])
    pool = jnp.where(is_ctx, poolc_ref[...], pooll_ref[...])
    x = jnp.where(is_ctx, xc_ref[...], xl_ref[...])
    y = (jnp.dot(ret, wt_ref[...], preferred_element_type=F32)
         + jnp.dot(pool, wb_ref[...], preferred_element_type=F32))
    x1 = x + mod_ref[0, 2:3, :] * y
    x1_ref[...] = x1
    hn = x1 * lax.rsqrt(jnp.mean(x1 * x1, axis=-1, keepdims=True) + NORM_EPS) * g_ref[...]
    h2 = hn * (1.0 + mod_ref[0, 4:5, :]) + mod_ref[0, 3:4, :]
    hi = h2.astype(BF16)
    h2_ref[...] = hi
    lo = (h2 - hi.astype(F32)).astype(BF16)
    both = jnp.dot(hi, wrb_ref[...], preferred_element_type=F32)
    logits = (both[:, 0:LANES] + jnp.dot(lo, wrh_ref[...], preferred_element_type=F32)
              + both[:, LANES:2 * LANES]) + br_ref[...]

    tm = logits.shape[0]
    lane = lax.broadcasted_iota(jnp.int32, (tm, LANES), 1)
    work = logits
    vals, idxs, hots = [], [], []
    for _ in range(TOP_K):
        m = jnp.max(work, axis=-1, keepdims=True)
        idx = jnp.min(jnp.where(work == m, lane, LANES), axis=-1, keepdims=True)
        hot = lane == idx
        vals.append(m)
        idxs.append(idx)
        hots.append(hot)
        work = jnp.where(hot, -jnp.inf, work)
    exps = [jnp.exp(v - vals[0]) for v in vals]
    denom = exps[0] + exps[1] + exps[2] + exps[3]

    selected = jnp.zeros((tm, LANES), F32)
    for hot in hots:
        selected = selected + jnp.where(hot, 1.0, 0.0)
    r_i = lax.broadcasted_iota(jnp.int32, (tm, tm), 0)
    c_i = lax.broadcasted_iota(jnp.int32, (tm, tm), 1)
    tri = jnp.where(c_i < r_i, 1.0, 0.0).astype(BF16)
    before = jnp.dot(tri, selected.astype(BF16), preferred_element_type=F32) + run_ref[0:1, :]

    ri = jnp.zeros((tm, LANES), jnp.int32)
    rp = jnp.zeros((tm, LANES), F32)
    rr = jnp.zeros((tm, LANES), jnp.int32)
    for k in range(TOP_K):
        rank = jnp.sum(jnp.where(hots[k], before, 0.0), axis=-1, keepdims=True).astype(jnp.int32)
        ri = jnp.where(lane == k, idxs[k], ri)
        rp = jnp.where(lane == k, exps[k] / denom, rp)
        rr = jnp.where(lane == k, rank, rr)
    ri_ref[...] = jnp.transpose(ri)[0:8, :]
    rp_ref[...] = rp
    rr_ref[...] = jnp.transpose(rr)[0:8, :]

    run = run_ref[0:1, :] + jnp.sum(selected, axis=0, keepdims=True)
    run_ref[...] = jnp.broadcast_to(run, run_ref.shape)
    cnt_ref[...] = jnp.broadcast_to(run, cnt_ref.shape).astype(jnp.int32)


def _out_projection(ret_ctx, ret_lat, pool_ctx, pool_lat, x_ctx, x_lat, mod3, norm_g, w_out_bf16,
                    wr_hi, wr_both, b_router_pad):
    tm = TM_OUT
    real = lambda i: jnp.minimum(i, T_ALL // tm - 1)
    row = lambda i: (real(i), 0)
    const = lambda i: (0, 0)
    crow = lambda i: (_ctx_tile(real(i), tm), 0)
    lrow = lambda i: (_lat_tile(real(i), tm), 0)
    return pl.pallas_call(
        _outproj_kernel,
        grid=(H2_ROWS // tm,),
        in_specs=[pl.BlockSpec((tm, RET_WIDTH), crow),
                  pl.BlockSpec((tm, RET_WIDTH), lrow),
                  pl.BlockSpec((tm, POOL_WIDTH), crow),
                  pl.BlockSpec((tm, POOL_WIDTH), lrow),
                  pl.BlockSpec((tm, D_MODEL), crow),
                  pl.BlockSpec((tm, D_MODEL), lrow),
                  pl.BlockSpec((1, 6, D_MODEL), lambda i: (_cond_row(real(i), tm), 0, 0)),
                  pl.BlockSpec((1, D_MODEL), const),
                  pl.BlockSpec((RET_WIDTH, D_MODEL), const),
                  pl.BlockSpec((POOL_WIDTH, D_MODEL), lambda i: (1, 0)),
                  pl.BlockSpec((D_MODEL, LANES), const),
                  pl.BlockSpec((D_MODEL, 2 * LANES), const),
                  pl.BlockSpec((1, LANES), const)],
        out_specs=[pl.BlockSpec((tm, D_MODEL), row),
                   pl.BlockSpec((tm, D_MODEL), lambda i: (i, 0)),
                   pl.BlockSpec((8, tm), lambda i: (0, real(i))),
                   pl.BlockSpec((tm, LANES), row),
                   pl.BlockSpec((8, tm), lambda i: (0, real(i))),
                   pl.BlockSpec((8, LANES), const)],
        out_shape=[jax.ShapeDtypeStruct((T_ALL, D_MODEL), F32),
                   jax.ShapeDtypeStruct((H2_ROWS, D_MODEL), BF16),
                   jax.ShapeDtypeStruct((8, T_ALL), jnp.int32),
                   jax.ShapeDtypeStruct((T_ALL, LANES), F32),
                   jax.ShapeDtypeStruct((8, T_ALL), jnp.int32),
                   jax.ShapeDtypeStruct((8, LANES), jnp.int32)],
        scratch_shapes=[pltpu.VMEM((8, LANES), F32)],
        compiler_params=_cparams(("arbitrary",), 48 * 1024 * 1024),
        name="out_projection_router",
    )(ret_ctx, ret_lat, pool_ctx, pool_lat, x_ctx, x_lat, mod3, norm_g.reshape(1, D_MODEL),
      w_out_bf16, w_out_bf16, wr_hi, wr_both, b_router_pad)


def _count_le(ends, v):
    return jnp.sum((ends[None, :] <= v[:, None]).astype(jnp.int32), axis=1)


def _expert_schedule(counts):
    i32 = jnp.int32
    ntile = (counts + TM_EXP - 1) // TM_EXP
    tile_off = jnp.cumsum(ntile) - ntile
    nsup = (ntile + SUP_EXP - 1) // SUP_EXP
    sup_end = jnp.cumsum(nsup)
    u = jnp.arange(U_EXP, dtype=i32)
    sup_valid = u < sup_end[-1]
    uc = jnp.minimum(u, sup_end[-1] - 1)
    sup_e = jnp.minimum(_count_le(sup_end, uc), N_EXPERTS - 1)
    sup_idx = uc - (sup_end - nsup)[sup_e]
    sup_row0 = (tile_off[sup_e] + sup_idx * SUP_EXP) * TM_EXP
    sup_nt = jnp.clip(ntile[sup_e] - sup_idx * SUP_EXP, 1, SUP_EXP)
    step_e = jnp.repeat(sup_e, J_EXP)
    step_valid = jnp.repeat(sup_valid.astype(i32), J_EXP)
    step_j = jnp.where(step_valid == 1, jnp.tile(jnp.arange(J_EXP, dtype=i32), U_EXP), J_EXP - 1)
    step_row0 = jnp.repeat(sup_row0, J_EXP)
    step_nt = jnp.repeat(sup_nt, J_EXP)
    last_rows = counts - (ntile - 1) * TM_EXP
    sup_half = (sup_idx == nsup[sup_e] - 1) & (last_rows[sup_e] <= TM_EXP // 2) & (counts[sup_e] > 0)
    step_half = jnp.repeat(sup_half.astype(i32), J_EXP)
    return tile_off * TM_EXP, (step_e.astype(i32), step_j.astype(i32), step_row0.astype(i32),
                               step_nt.astype(i32), step_half, step_valid,
                               jnp.sum(ntile).astype(i32).reshape(1))


def _expert_kernel(se_ref, sj_ref, srow_ref, snt_ref, shalf_ref, sv_ref, stot_ref,
                   xs_hbm, wg_ref, wu_ref, bg_ref, bu_ref, wd_ref, bd_ref, ys_hbm,
                   acc_ref, x_buf, o_buf, x_sem, o_sem):
    s = pl.program_id(0)
    j = sj_ref[s]
    n_tiles = snt_ref[s]
    row0 = pl.multiple_of(srow_ref[s], TM_EXP)
    last = J_EXP - 1

    def x_copy():
        return pltpu.make_async_copy(xs_hbm.at[pl.ds(row0, SUP_EXP * TM_EXP)], x_buf, x_sem)

    def out_copy(t):
        rows = pl.ds(pl.multiple_of(row0 + t * TM_EXP, TM_EXP), TM_EXP)
        return pltpu.make_async_copy(o_buf.at[t % 2], ys_hbm.at[rows], o_sem.at[t % 2])

    @pl.when(s == 0)
    def _():
        o_buf[0] = jnp.zeros((TM_EXP, D_MODEL), BF16)

        def zero_copy(t):
            rows = pl.ds(pl.multiple_of(t * TM_EXP, TM_EXP), TM_EXP)
            return pltpu.make_async_copy(o_buf.at[0], ys_hbm.at[rows], o_sem.at[0])

        def start(t, carry):
            zero_copy(t).start()
            return carry

        def wait(t, carry):
            zero_copy(t).wait()
            return carry

        lax.fori_loop(stot_ref[0], NT_EXP, start, 0)
        lax.fori_loop(stot_ref[0], NT_EXP, wait, 0)

    @pl.when(sv_ref[s] == 1)
    def _():
        @pl.when(j == 0)
        def _():
            x_copy().start()
            acc_ref[...] = jnp.zeros_like(acc_ref)
            x_copy().wait()

        def accumulate(rows):
            x = x_buf[rows, :]
            gate = _dot(x, wg_ref[0]) + bg_ref[0]
            up = _dot(x, wu_ref[0]) + bu_ref[0]
            gate = jnp.minimum(gate, SWIGLU_LIMIT)
            up = jnp.clip(up, -SWIGLU_LIMIT, SWIGLU_LIMIT)
            act = (up + 1.0) * gate * _sigmoid(SWIGLU_ALPHA * gate)
            acc_ref[rows, :] += _dot(act, wd_ref[0])

        def tile_pair(p, carry):
            base = pl.multiple_of(p * (2 * TM_EXP), 2 * TM_EXP)
            accumulate(pl.ds(base, TM_EXP))
            accumulate(pl.ds(base + TM_EXP, TM_EXP))
            return carry

        n_full = n_tiles - shalf_ref[s]
        lax.fori_loop(0, n_full // 2, tile_pair, 0)

        @pl.when(n_full % 2 == 1)
        def _():
            accumulate(pl.ds(pl.multiple_of((n_full - 1) * TM_EXP, TM_EXP), TM_EXP))

        @pl.when(shalf_ref[s] == 1)
        def _():
            accumulate(pl.ds(pl.multiple_of((n_tiles - 1) * TM_EXP, TM_EXP), TM_EXP // 2))

        @pl.when(j == last)
        def _():
            def finish(t, carry):
                @pl.when(t >= 2)
                def _():
                    out_copy(t - 2).wait()

                rows = pl.ds(pl.multiple_of(t * TM_EXP, TM_EXP), TM_EXP)
                o_buf[t % 2] = (acc_ref[rows, :] + bd_ref[0]).astype(BF16)
                out_copy(t).start()
                return carry

            lax.fori_loop(0, n_tiles, finish, 0)

            @pl.when(n_tiles >= 2)
            def _():
                out_copy(n_tiles - 2).wait()

            out_copy(n_tiles - 1).wait()


def _experts(xs, sched, w_gate_up, b_gate_up, w_down, b_down):
    wmap = lambda col0: (lambda s, se, sj, srow, snt, shalf, sv, stot: (se[s], 0, col0 + sj[s]))
    return pl.pallas_call(
        _expert_kernel,
        grid_spec=pltpu.PrefetchScalarGridSpec(
            num_scalar_prefetch=7,
            grid=(G_EXP,),
            in_specs=[
                pl.BlockSpec(memory_space=pl.ANY),
                pl.BlockSpec((1, D_MODEL, F_EXP), wmap(0)),
                pl.BlockSpec((1, D_MODEL, F_EXP), wmap(J_EXP)),
                pl.BlockSpec((1, 1, F_EXP), wmap(0)),
                pl.BlockSpec((1, 1, F_EXP), wmap(J_EXP)),
                pl.BlockSpec((1, F_EXP, D_MODEL), lambda s, se, sj, srow, snt, shalf, sv, stot: (se[s], sj[s], 0)),
                pl.BlockSpec((1, 1, D_MODEL), lambda s, se, sj, srow, snt, shalf, sv, stot: (se[s], 0, 0)),
            ],
            out_specs=pl.BlockSpec(memory_space=pl.ANY),
            scratch_shapes=[pltpu.VMEM((SUP_EXP * TM_EXP, D_MODEL), F32),
                            pltpu.VMEM((SUP_EXP * TM_EXP, D_MODEL), BF16),
                            pltpu.VMEM((2, TM_EXP, D_MODEL), BF16),
                            pltpu.SemaphoreType.DMA(()),
                            pltpu.SemaphoreType.DMA((2,))]),
        out_shape=jax.ShapeDtypeStruct((NT_EXP * TM_EXP, D_MODEL), BF16),
        compiler_params=_cparams(("arbitrary",), VMEM_LIMIT),
        name="experts",
    )(*sched, xs, w_gate_up, w_gate_up,
      b_gate_up.reshape(N_EXPERTS, 1, 2 * D_FF), b_gate_up.reshape(N_EXPERTS, 1, 2 * D_FF),
      w_down, b_down.reshape(N_EXPERTS, 1, D_MODEL))


def _combine_kernel(x1_ref, y0_ref, y1_ref, y2_ref, y3_ref, rp_ref, mod_ref, g_ref, o_ref):
    rp = rp_ref[...]
    moe = rp[:, 0:1] * y0_ref[...].astype(F32)
    for k, y_ref in enumerate((y1_ref, y2_ref, y3_ref), start=1):
        moe = moe + rp[:, k:k + 1] * y_ref[...].astype(F32)
    x2 = x1_ref[...] + mod_ref[0, 5:6, :] * moe
    o_ref[...] = x2 * lax.rsqrt(jnp.mean(x2 * x2, axis=-1, keepdims=True) + NORM_EPS) * g_ref[...]


def _combine(x1, y4, route_p, mod3, final_g, row0, n_rows):
    tm = TM_CMB
    nt = n_rows // tm
    t0 = row0 // tm
    y_specs = [pl.BlockSpec((tm, D_MODEL), functools.partial(lambda k, i: (k * nt + i, 0), k))
               for k in range(TOP_K)]
    return pl.pallas_call(
        _combine_kernel,
        grid=(nt,),
        in_specs=[pl.BlockSpec((tm, D_MODEL), lambda i: (t0 + i, 0))] + y_specs + [
            pl.BlockSpec((tm, LANES), lambda i: (t0 + i, 0)),
            pl.BlockSpec((1, 6, D_MODEL), lambda i: (_cond_row(t0 + i, tm), 0, 0)),
            pl.BlockSpec((1, D_MODEL), lambda i: (0, 0))],
        out_specs=pl.BlockSpec((tm, D_MODEL), lambda i: (i, 0)),
        out_shape=jax.ShapeDtypeStruct((n_rows, D_MODEL), F32),
        compiler_params=_cparams(("arbitrary",), 40 * 1024 * 1024),
        name="combine_final_norm",
    )(x1, y4, y4, y4, y4, route_p, mod3, final_g.reshape(1, D_MODEL))


def kernel(x_prompt, x_sample, c, c_ctx, state_ret, norm1_g, norm2_g, w_mod, b_mod, w_in, ret_decay,
           w_pool, pool_scale, w_out, w_router, b_router, w_gate_up, b_gate_up, w_down, b_down, final_g):
    assert w_mod.shape[0] == 1, "single trunk layer"
    x_ctx = x_prompt.reshape(T_CTX, D_MODEL)
    x_lat = x_sample.reshape(T_LAT, D_MODEL)
    cond = jnp.zeros((COND_ROWS, D_MODEL), F32).at[0].set(c_ctx).at[1:N_COND].set(c)

    mod = _modulation(cond, w_mod[0], b_mod[0])
    mod3 = mod.reshape(COND_ROWS, 6, D_MODEL)

    proj = _in_projection(x_ctx, x_lat, norm1_g[0], mod3, w_in[0].astype(BF16))

    ret_ctx, new_state = _retention(proj, 0, BATCH, SEQ, ret_decay[0], None, None, True)
    ret_lat, _ = _retention(proj, T_CTX, DEC_BATCH, DEC_SEQ, ret_decay[0], state_ret[:, 0],
                            _rope_tables(DEC_SEQ), False)
    w_pool_bf16 = w_pool[0].astype(BF16)
    pool_ctx = _pooling(proj, 0, BATCH, SEQ, w_pool_bf16, pool_scale[0], False)
    pool_lat = _pooling(proj, T_CTX, DEC_BATCH, DEC_SEQ, w_pool_bf16, pool_scale[0], True)

    wr = jnp.zeros((D_MODEL, LANES), F32).at[:, :N_EXPERTS].set(w_router[0])
    wr_hi = wr.astype(BF16)
    wr_both = jnp.concatenate([wr_hi, (wr - wr_hi.astype(F32)).astype(BF16)], axis=1)
    br = jnp.full((1, LANES), -1e30, F32).at[0, :N_EXPERTS].set(b_router[0])
    x1, h2, route_i, route_p, route_r, counts = _out_projection(
        ret_ctx, ret_lat, pool_ctx, pool_lat, x_ctx, x_lat, mod3, norm2_g[0], w_out[0].astype(BF16),
        wr_hi, wr_both, br)

    row_off, sched = _expert_schedule(counts[0, :N_EXPERTS])
    slot = row_off[route_i[:TOP_K]] + route_r[:TOP_K]
    token = jnp.broadcast_to(jnp.arange(T_ALL, dtype=jnp.int32)[None, :], (TOP_K, T_ALL))
    token_of_slot = (jnp.arange(XS_ROWS, dtype=jnp.int32) % T_ALL).at[slot.reshape(-1)].set(
        token.reshape(-1), unique_indices=True, mode="promise_in_bounds")
    xs = h2.at[token_of_slot].get(mode="promise_in_bounds")
    ys = _experts(xs, sched, w_gate_up[0], b_gate_up[0], w_down[0], b_down[0])
    y4_lat = ys.at[slot[:, T_CTX:].reshape(-1)].get(mode="promise_in_bounds", unique_indices=True)
    y4_ctx = ys.at[slot[:, :T_CTX].reshape(-1)].get(mode="promise_in_bounds", unique_indices=True)
    y_lat = _combine(x1, y4_lat, route_p, mod3, final_g, T_CTX, T_LAT)
    y_ctx = _combine(x1, y4_ctx, route_p, mod3, final_g, 0, T_CTX)
    return (y_ctx.reshape(BATCH, SEQ, D_MODEL), y_lat.reshape(DEC_BATCH, DEC_SEQ, D_MODEL),
            new_state.reshape(BATCH, 1, 2, RET_HEADS, RET_DK, RET_DV))
```

```python
import functools

import numpy as np
import jax
import jax.numpy as jnp
from jax import lax
from jax.experimental import pallas as pl
from jax.experimental.pallas import tpu as pltpu

F32 = jnp.float32
BF16 = jnp.bfloat16

D_MODEL = 2048
BATCH = 16
SEQ = 256
DEC_BATCH = 4
DEC_SEQ = 2048
GRID_W = 64
RET_HEADS = 4
RET_DK = 128
RET_DV = 256
RET_QK_WIDTH = RET_HEADS * RET_DK
RET_WIDTH = RET_HEADS * RET_DV
POOL_WINDOWS = (2, 4, 8, 16)
POOL_GROUPS = 4
POOL_DG = 256
POOL_WIDTH = POOL_GROUPS * POOL_DG
IN_WIDTH = 2 * RET_QK_WIDTH + 2 * RET_WIDTH + POOL_WIDTH
N_EXPERTS = 32
TOP_K = 4
D_FF = D_MODEL
SWIGLU_LIMIT = 7.0
SWIGLU_ALPHA = 1.702
ROPE_BASE = 10000.0
NORM_EPS = 1e-6
GN_EPS = 1e-6

T_CTX = BATCH * SEQ
T_LAT = DEC_BATCH * DEC_SEQ
T_ALL = T_CTX + T_LAT
N_COND = 1 + DEC_BATCH
COND_ROWS = 8
LANES = 128

RET_CHUNK = 256

TM_IN = 512
TN_IN = 1024
TM_OUT = 256
TM_CMB = 256
H2_ROWS = 16384
POOL_TILE = 256
POOL_PAD = (max(POOL_WINDOWS) // 2) * GRID_W

TM_EXP = 256
F_EXP = 512
J_EXP = D_FF // F_EXP
SUP_EXP = 8
NT_EXP = (T_ALL * TOP_K + N_EXPERTS * (TM_EXP - 1)) // TM_EXP
U_EXP = N_EXPERTS + NT_EXP // SUP_EXP + 1
G_EXP = J_EXP * U_EXP
XS_ROWS = (NT_EXP + SUP_EXP) * TM_EXP

VMEM_LIMIT = 56 * 1024 * 1024


def _cparams(sem, vmem=None):
    return pltpu.CompilerParams(dimension_semantics=sem, vmem_limit_bytes=vmem)


def _cond_row(i, tm):
    nctx = T_CTX // tm
    return jnp.where(i < nctx, 0, 1 + (i - nctx) // (DEC_SEQ // tm))


def _sigmoid(x):
    return 1.0 / (1.0 + jnp.exp(-x))


def _dot(a, b):
    return lax.dot_general(a, b, (((1,), (0,)), ((), ())), preferred_element_type=F32)


def _mod_kernel(c_ref, w_ref, b_ref, o_ref):
    c = c_ref[...]
    s = (c * _sigmoid(c)).astype(BF16)
    o_ref[...] = jnp.dot(s, w_ref[...].astype(BF16), preferred_element_type=F32) + b_ref[...]


def _modulation(cond, w_mod, b_mod):
    n = w_mod.shape[1]
    tn = 1024
    return pl.pallas_call(
        _mod_kernel,
        grid=(n // tn,),
        in_specs=[pl.BlockSpec((COND_ROWS, D_MODEL), lambda j: (0, 0)),
                  pl.BlockSpec((D_MODEL, tn), lambda j: (0, j)),
                  pl.BlockSpec((1, tn), lambda j: (0, j))],
        out_specs=pl.BlockSpec((COND_ROWS, tn), lambda j: (0, j)),
        out_shape=jax.ShapeDtypeStruct((COND_ROWS, n), F32),
        compiler_params=_cparams(("arbitrary",), 40 * 1024 * 1024),
        name="modulation",
    )(cond, w_mod, b_mod.reshape(1, n))


def _ctx_tile(i, tm):
    return jnp.minimum(i, T_CTX // tm - 1)


def _lat_tile(i, tm):
    return jnp.maximum(i - T_CTX // tm, 0)


def _inproj_kernel(xc_ref, xl_ref, g_ref, mod_ref, w_ref, o_ref, h_ref):
    def modulated_norm(x_ref):
        x = x_ref[...]
        y = x * lax.rsqrt(jnp.mean(x * x, axis=-1, keepdims=True) + NORM_EPS) * g_ref[...]
        shift = mod_ref[0, 0:1, :]
        scale = mod_ref[0, 1:2, :]
        h_ref[...] = (y * (1.0 + scale) + shift).astype(BF16)

    first = pl.program_id(1) == 0
    is_ctx = pl.program_id(0) < T_CTX // TM_IN
    pl.when(first & is_ctx)(lambda: modulated_norm(xc_ref))
    pl.when(first & jnp.logical_not(is_ctx))(lambda: modulated_norm(xl_ref))

    o_ref[...] = jnp.dot(h_ref[...], w_ref[...], preferred_element_type=F32)


def _in_projection(x_ctx, x_lat, norm_g, mod3, w_in_bf16):
    return pl.pallas_call(
        _inproj_kernel,
        grid=(T_ALL // TM_IN, IN_WIDTH // TN_IN),
        in_specs=[pl.BlockSpec((TM_IN, D_MODEL), lambda i, j: (_ctx_tile(i, TM_IN), 0)),
                  pl.BlockSpec((TM_IN, D_MODEL), lambda i, j: (_lat_tile(i, TM_IN), 0)),
                  pl.BlockSpec((1, D_MODEL), lambda i, j: (0, 0)),
                  pl.BlockSpec((1, 6, D_MODEL), lambda i, j: (_cond_row(i, TM_IN), 0, 0)),
                  pl.BlockSpec((D_MODEL, TN_IN), lambda i, j: (0, j))],
        out_specs=pl.BlockSpec((TM_IN, TN_IN), lambda i, j: (i, j)),
        out_shape=jax.ShapeDtypeStruct((T_ALL, IN_WIDTH), F32),
        scratch_shapes=[pltpu.VMEM((TM_IN, D_MODEL), BF16)],
        compiler_params=_cparams(("arbitrary", "arbitrary"), 40 * 1024 * 1024),
        name="in_projection",
    )(x_ctx, x_lat, norm_g.reshape(1, D_MODEL), mod3, w_in_bf16)


def _rope_tables(seq_len):
    t = jnp.arange(seq_len)
    row = (t // GRID_W).astype(F32)
    col = (t % GRID_W).astype(F32)
    half = RET_DK // 2
    n_freq = half // 2
    inv = ROPE_BASE ** (-jnp.arange(n_freq, dtype=F32) / n_freq)
    ang_r = row[:, None] * inv
    ang_c = col[:, None] * inv
    cos = jnp.concatenate([jnp.cos(ang_r), jnp.cos(ang_r), jnp.cos(ang_c), jnp.cos(ang_c)], axis=-1)
    sin = jnp.concatenate([-jnp.sin(ang_r), jnp.sin(ang_r), -jnp.sin(ang_c), jnp.sin(ang_c)], axis=-1)
    return cos, sin


def _retention_kernel(rd_ref, *refs, seq_len, chunk, use_rope, has_state_in, has_state_out, heads):
    refs = list(refs)
    q_ref, k_ref, v_ref, g_ref = refs[:4]
    pos = 4
    if has_state_in:
        s0_ref = refs[pos]
        pos += 1
    if use_rope:
        cos_ref, sin_ref = refs[pos], refs[pos + 1]
        pos += 2
    o_ref = refs[pos]
    pos += 1
    if has_state_out:
        st_ref = refs[pos]
        pos += 1
    scratch = refs[pos:pos + 5]

    for hh in range(heads):
        _retention_head(rd_ref, pl.program_id(1) * heads + hh, hh,
                        q_ref, k_ref, v_ref, g_ref,
                        s0_ref if has_state_in else None,
                        (cos_ref, sin_ref) if use_rope else None,
                        o_ref, st_ref if has_state_out else None,
                        [r.at[hh] for r in scratch], seq_len, chunk)


def _retention_head(rd_ref, h, hh, q_ref, k_ref, v_ref, g_ref, s0_ref, rope_refs, o_ref, st_ref, scratch,
                    seq_len, chunk):
    qs_ref, ks_ref, acc_ref, sf_ref, sb_ref = scratch
    use_rope = rope_refs is not None
    kcols = slice(hh * RET_DK, (hh + 1) * RET_DK)
    vcols = slice(hh * RET_DV, (hh + 1) * RET_DV)
    C = chunk
    n_chunks = seq_len // C

    lgf = -jnp.exp(jnp.full((C, 1), rd_ref[0, h], F32))
    lgb = -jnp.exp(jnp.full((C, 1), rd_ref[1, h], F32))
    ii = lax.broadcasted_iota(jnp.int32, (C, C), 0)
    jj = lax.broadcasted_iota(jnp.int32, (C, C), 1)
    diff = (ii - jj).astype(F32)
    decay = (jnp.where(diff >= 0, jnp.exp(lgf * jnp.maximum(diff, 0.0)), 0.0)
             + jnp.where(diff <= 0, jnp.exp(lgb * jnp.maximum(-diff, 0.0)), 0.0))
    p = lax.broadcasted_iota(jnp.int32, (C, 1), 0).astype(F32)
    xi_f = jnp.exp(lgf * (p + 1.0))
    zeta_f = jnp.exp(lgf * (C - 1.0 - p))
    xi_b = jnp.exp(lgb * (C - p))
    zeta_b = jnp.exp(lgb * p)
    cd_f = jnp.exp(lgf[0:1, :] * C)
    cd_b = jnp.exp(lgb[0:1, :] * C)

    if s0_ref is not None:
        sf_ref[...] = s0_ref[0, 0, hh]
        sb_ref[...] = s0_ref[0, 1, hh]
    else:
        sf_ref[...] = jnp.zeros_like(sf_ref)
        sb_ref[...] = jnp.zeros_like(sb_ref)

    lane = lax.broadcasted_iota(jnp.int32, (C, RET_DK), 1)
    first_half = (lane & 32) == 0

    def rope(x, cs, sn):
        swapped = jnp.where(first_half, pltpu.roll(x, RET_DK - 32, axis=1), pltpu.roll(x, 32, axis=1))
        return x * cs + swapped * sn

    def state_update(s_ref, kz, v, cd):
        upd = lax.dot_general(kz, v, (((0,), (0,)), ((), ())), preferred_element_type=F32)
        s_ref[...] = cd * s_ref[...] + upd

    def chunk_rows(c):
        start = c * C
        return pl.ds(start if isinstance(start, int) else pl.multiple_of(start, C), C)

    def fwd_chunk(c):
        sl = chunk_rows(c)
        q = q_ref[sl, kcols]
        k = k_ref[sl, kcols] * (RET_DK ** -0.5)
        if use_rope:
            cs = rope_refs[0][sl, :]
            sn = rope_refs[1][sl, :]
            q = rope(q, cs, sn)
            k = rope(k, cs, sn)
        qb = q.astype(BF16)
        kb = k.astype(BF16)
        qs_ref[sl, :] = qb
        ks_ref[sl, :] = k
        v = v_ref[sl, vcols].astype(BF16)
        scores = lax.dot_general(qb, kb, (((1,), (1,)), ((), ())), preferred_element_type=F32)
        inner = jnp.dot((scores * decay).astype(BF16), v, preferred_element_type=F32)
        cross = jnp.dot(qb, sf_ref[...].astype(BF16), preferred_element_type=F32) * xi_f
        acc_ref[sl, :] = inner + cross
        state_update(sf_ref, (k * zeta_f).astype(BF16), v, cd_f)

    def bwd_chunk(c):
        sl = chunk_rows(c)
        qb = qs_ref[sl, :]
        k = ks_ref[sl, :]
        v = v_ref[sl, vcols].astype(BF16)
        cross = jnp.dot(qb, sb_ref[...].astype(BF16), preferred_element_type=F32) * xi_b
        o = acc_ref[sl, :] + cross
        mu = jnp.mean(o, axis=-1, keepdims=True)
        oc = o - mu
        var = jnp.mean(oc * oc, axis=-1, keepdims=True)
        on = oc * lax.rsqrt(var + GN_EPS)
        g = g_ref[sl, vcols]
        o_ref[sl, vcols] = (on * (g * _sigmoid(g))).astype(BF16)
        state_update(sb_ref, (k * zeta_b).astype(BF16), v, cd_b)

    if n_chunks == 1:
        fwd_chunk(0)
        bwd_chunk(0)
    else:
        assert n_chunks % 2 == 0

        def fwd_pair(i, carry):
            fwd_chunk(2 * i)
            fwd_chunk(2 * i + 1)
            return carry

        def bwd_pair(i, carry):
            bwd_chunk(n_chunks - 1 - 2 * i)
            bwd_chunk(n_chunks - 2 - 2 * i)
            return carry

        lax.fori_loop(0, n_chunks // 2, fwd_pair, 0)
        lax.fori_loop(0, n_chunks // 2, bwd_pair, 0)

    if st_ref is not None:
        st_ref[0, 0, hh] = sf_ref[...]
        st_ref[0, 1, hh] = sb_ref[...]


def _retention(proj, row0, B, L, ret_decay, state_in, rope_tabs, want_state, heads):
    rb = row0 // L
    use_rope = rope_tabs is not None
    has_state_in = state_in is not None
    wk = heads * RET_DK
    wv = heads * RET_DV
    kq = RET_QK_WIDTH // wk
    kv = 2 * RET_QK_WIDTH // wv
    kg = kv + RET_WIDTH // wv
    in_specs = [pl.BlockSpec((L, wk), lambda b, h, rd: (rb + b, h)),
                pl.BlockSpec((L, wk), lambda b, h, rd: (rb + b, kq + h)),
                pl.BlockSpec((L, wv), lambda b, h, rd: (rb + b, kv + h)),
                pl.BlockSpec((L, wv), lambda b, h, rd: (rb + b, kg + h))]
    args = [proj, proj, proj, proj]
    state_spec = pl.BlockSpec((1, 2, heads, RET_DK, RET_DV), lambda b, h, rd: (b, 0, h, 0, 0))
    if has_state_in:
        in_specs.append(state_spec)
        args.append(state_in)
    if use_rope:
        in_specs += [pl.BlockSpec((L, RET_DK), lambda b, h, rd: (0, 0))] * 2
        args += list(rope_tabs)
    out_specs = [pl.BlockSpec((L, wv), lambda b, h, rd: (b, h))]
    out_shape = [jax.ShapeDtypeStruct((B * L, RET_WIDTH), BF16)]
    if want_state:
        out_specs.append(state_spec)
        out_shape.append(jax.ShapeDtypeStruct((B, 2, RET_HEADS, RET_DK, RET_DV), F32))
    kern = functools.partial(_retention_kernel, seq_len=L, chunk=RET_CHUNK, use_rope=use_rope,
                             has_state_in=has_state_in, has_state_out=want_state, heads=heads)
    res = pl.pallas_call(
        kern,
        grid_spec=pltpu.PrefetchScalarGridSpec(
            num_scalar_prefetch=1,
            grid=(B, RET_HEADS // heads),
            in_specs=in_specs,
            out_specs=out_specs,
            scratch_shapes=[pltpu.VMEM((heads, L, RET_DK), BF16),
                            pltpu.VMEM((heads, L, RET_DK), F32),
                            pltpu.VMEM((heads, L, RET_DV), F32),
                            pltpu.VMEM((heads, RET_DK, RET_DV), F32),
                            pltpu.VMEM((heads, RET_DK, RET_DV), F32)]),
        out_shape=out_shape,
        compiler_params=_cparams(("arbitrary", "arbitrary"), 40 * 1024 * 1024),
        name="retention_grid" if use_rope else "retention_seq",
    )(ret_decay, *args)
    return res if want_state else (res[0], None)


def _split3(x):
    hi = x.astype(BF16)
    r1 = x - hi.astype(F32)
    mid = r1.astype(BF16)
    lo = (r1 - mid.astype(F32)).astype(BF16)
    return jnp.concatenate([hi, mid, lo], axis=-1)


def _pool_kernel(p_ref, w_ref, sc_ref, o_ref, pad_ref, *, seq_len, grid_mode):
    PT = POOL_TILE
    n_tiles = seq_len // PT
    ii = lax.broadcasted_iota(jnp.int32, (PT, PT), 0)
    jj = lax.broadcasted_iota(jnp.int32, (PT, PT), 1)
    d = jj - ii
    ti = lax.broadcasted_iota(jnp.int32, (PT, 1), 0)

    if grid_mode:
        zeros = jnp.zeros((POOL_PAD, POOL_DG), F32)
        pad_ref[0:POOL_PAD, :] = zeros
        pad_ref[POOL_PAD + seq_len:POOL_PAD + seq_len + POOL_PAD, :] = zeros

    for gi, w in enumerate(POOL_WINDOWS):
        lo_off = -(w // 2)
        hi_off = w - w // 2 - 1
        cols = slice(gi * POOL_DG, (gi + 1) * POOL_DG)
        in_window = (d >= lo_off) & (d <= hi_off)
        if grid_mode:
            in_window = in_window & ((ii // GRID_W) == (jj // GRID_W))
        band = jnp.where(in_window, 1.0, 0.0).astype(BF16)
        wg = w_ref[gi]
        scale = sc_ref[gi]

        def window_sum(t0):
            x = p_ref[pl.ds(t0, PT), cols]
            s3 = jnp.dot(band, _split3(x), preferred_element_type=F32)
            return s3[:, 0:POOL_DG] + s3[:, POOL_DG:2 * POOL_DG] + s3[:, 2 * POOL_DG:3 * POOL_DG]

        def finish(t0, s, cnt):
            x = p_ref[pl.ds(t0, PT), cols]
            diff = (s / cnt - x).astype(BF16)
            y = jnp.dot(diff, wg, preferred_element_type=F32) * scale
            o_ref[pl.ds(t0, PT), cols] = y.astype(BF16)

        def count(pos, n):
            lo = jnp.maximum(pos + lo_off, 0)
            hi = jnp.minimum(pos + lo_off + w, n)
            return (hi - lo).astype(F32)

        if grid_mode:
            for t in range(n_tiles):
                pad_ref[POOL_PAD + t * PT:POOL_PAD + (t + 1) * PT, :] = window_sum(t * PT)
            for t in range(n_tiles):
                base = POOL_PAD + t * PT
                s = pad_ref[base + lo_off * GRID_W:base + lo_off * GRID_W + PT, :]
                for r in range(lo_off + 1, hi_off + 1):
                    s = s + pad_ref[base + r * GRID_W:base + r * GRID_W + PT, :]
                tok = ti + t * PT
                cnt = count(tok // GRID_W, seq_len // GRID_W) * count(tok % GRID_W, GRID_W)
                finish(t * PT, s, cnt)
        else:
            for t in range(n_tiles):
                finish(t * PT, window_sum(t * PT), count(ti + t * PT, seq_len))


def _pooling(proj, row0, B, L, w_pool_bf16, pool_scale, grid_mode):
    if not grid_mode:
        assert L == POOL_TILE
    rb = row0 // L
    pcol = (IN_WIDTH - POOL_WIDTH) // POOL_WIDTH
    kern = functools.partial(_pool_kernel, seq_len=L, grid_mode=grid_mode)
    return pl.pallas_call(
        kern,
        grid=(B,),
        in_specs=[pl.BlockSpec((L, POOL_WIDTH), lambda b: (rb + b, pcol)),
                  pl.BlockSpec((POOL_GROUPS, POOL_DG, POOL_DG), lambda b: (0, 0, 0)),
                  pl.BlockSpec((POOL_GROUPS, 1, POOL_DG), lambda b: (0, 0, 0))],
        out_specs=pl.BlockSpec((L, POOL_WIDTH), lambda b: (b, 0)),
        out_shape=jax.ShapeDtypeStruct((B * L, POOL_WIDTH), BF16),
        scratch_shapes=[pltpu.VMEM((L + 2 * POOL_PAD, POOL_DG), F32)],
        compiler_params=_cparams(("arbitrary",), 48 * 1024 * 1024),
        name="pool_grid" if grid_mode else "pool_seq",
    )(proj, w_pool_bf16, pool_scale.reshape(POOL_GROUPS, 1, POOL_DG))


def _outproj_kernel(*refs):
    i = pl.program_id(0)
    h2_ref = refs[14]

    @pl.when(i >= T_ALL // TM_OUT)
    def _():
        h2_ref[...] = jnp.zeros_like(h2_ref)

    pl.when(i < T_ALL // TM_OUT)(lambda: _outproj_tile(*refs))


def _outproj_tile(retc_ref, retl_ref, poolc_ref, pooll_ref, xc_ref, xl_ref, mod_ref, g_ref,
                  wt_ref, wb_ref, wrh_ref, wrb_ref, br_ref,
                  x1_ref, h2_ref, ri_ref, rp_ref, rr_ref, cnt_ref, run_ref):
    i = pl.program_id(0)

    @pl.when(i == 0)
    def _():
        run_ref[...] = jnp.zeros_like(run_ref)

    is_ctx = i < T_CTX // TM_OUT
    ret = jnp.where(is_ctx, retc_ref[...], retl_ref[...])
    pool = jnp.where(is_ctx, poolc_ref[...], pooll_ref[...])
    x = jnp.where(is_ctx, xc_ref[...], xl_ref[...])
    y = (jnp.dot(ret, wt_ref[...], preferred_element_type=F32)
         + jnp.dot(pool, wb_ref[...], preferred_element_type=F32))
    x1 = x + mod_ref[0, 2:3, :] * y
    x1_ref[...] = x1
    hn = x1 * lax.rsqrt(jnp.mean(x1 * x1, axis=-1, keepdims=True) + NORM_EPS) * g_ref[...]
    h2 = hn * (1.0 + mod_ref[0, 4:5, :]) + mod_ref[0, 3:4, :]
    hi = h2.astype(BF16)
    h2_ref[...] = hi
    lo = (h2 - hi.astype(F32)).astype(BF16)
    both = jnp.dot(hi, wrb_ref[...], preferred_element_type=F32)
    logits = (both[:, 0:LANES] + jnp.dot(lo, wrh_ref[...], preferred_element_type=F32)
              + both[:, LANES:2 * LANES]) + br_ref[...]

    tm = logits.shape[0]
    lane = lax.broadcasted_iota(jnp.int32, (tm, LANES), 1)
    work = logits
    vals, idxs, hots = [], [], []
    for _ in range(TOP_K):
        m = jnp.max(work, axis=-1, keepdims=True)
        idx = jnp.min(jnp.where(work == m, lane, LANES), axis=-1, keepdims=True)
        hot = lane == idx
        vals.append(m)
        idxs.append(idx)
        hots.append(hot)
        work = jnp.where(hot, -jnp.inf, work)
    exps = [jnp.exp(v - vals[0]) for v in vals]
    denom = exps[0] + exps[1] + exps[2] + exps[3]

    selected = jnp.zeros((tm, LANES), F32)
    for hot in hots:
        selected = selected + jnp.where(hot, 1.0, 0.0)
    r_i = lax.broadcasted_iota(jnp.int32, (tm, tm), 0)
    c_i = lax.broadcasted_iota(jnp.int32, (tm, tm), 1)
    tri = jnp.where(c_i < r_i, 1.0, 0.0).astype(BF16)
    before = jnp.dot(tri, selected.astype(BF16), preferred_element_type=F32) + run_ref[0:1, :]

    ri = jnp.zeros((tm, LANES), jnp.int32)
    rp = jnp.zeros((tm, LANES), F32)
    rr = jnp.zeros((tm, LANES), jnp.int32)
    for k in range(TOP_K):
        rank = jnp.sum(jnp.where(hots[k], before, 0.0), axis=-1, keepdims=True).astype(jnp.int32)
        ri = jnp.where(lane == k, idxs[k], ri)
        rp = jnp.where(lane == k, exps[k] / denom, rp)
        rr = jnp.where(lane == k, rank, rr)
    ri_ref[...] = ri
    rp_ref[...] = rp
    rr_ref[...] = rr

    run = run_ref[0:1, :] + jnp.sum(selected, axis=0, keepdims=True)
    run_ref[...] = jnp.broadcast_to(run, run_ref.shape)
    cnt_ref[...] = jnp.broadcast_to(run, cnt_ref.shape).astype(jnp.int32)


def _out_projection(ret_ctx, ret_lat, pool_ctx, pool_lat, x_ctx, x_lat, mod3, norm_g, w_out_bf16,
                    wr_hi, wr_both, b_router_pad):
    tm = TM_OUT
    real = lambda i: jnp.minimum(i, T_ALL // tm - 1)
    row = lambda i: (real(i), 0)
    const = lambda i: (0, 0)
    crow = lambda i: (_ctx_tile(real(i), tm), 0)
    lrow = lambda i: (_lat_tile(real(i), tm), 0)
    return pl.pallas_call(
        _outproj_kernel,
        grid=(H2_ROWS // tm,),
        in_specs=[pl.BlockSpec((tm, RET_WIDTH), crow),
                  pl.BlockSpec((tm, RET_WIDTH), lrow),
                  pl.BlockSpec((tm, POOL_WIDTH), crow),
                  pl.BlockSpec((tm, POOL_WIDTH), lrow),
                  pl.BlockSpec((tm, D_MODEL), crow),
                  pl.BlockSpec((tm, D_MODEL), lrow),
                  pl.BlockSpec((1, 6, D_MODEL), lambda i: (_cond_row(real(i), tm), 0, 0)),
                  pl.BlockSpec((1, D_MODEL), const),
                  pl.BlockSpec((RET_WIDTH, D_MODEL), const),
                  pl.BlockSpec((POOL_WIDTH, D_MODEL), lambda i: (1, 0)),
                  pl.BlockSpec((D_MODEL, LANES), const),
                  pl.BlockSpec((D_MODEL, 2 * LANES), const),
                  pl.BlockSpec((1, LANES), const)],
        out_specs=[pl.BlockSpec((tm, D_MODEL), row),
                   pl.BlockSpec((tm, D_MODEL), lambda i: (i, 0)),
                   pl.BlockSpec((tm, LANES), row),
                   pl.BlockSpec((tm, LANES), row),
                   pl.BlockSpec((tm, LANES), row),
                   pl.BlockSpec((8, LANES), const)],
        out_shape=[jax.ShapeDtypeStruct((T_ALL, D_MODEL), F32),
                   jax.ShapeDtypeStruct((H2_ROWS, D_MODEL), BF16),
                   jax.ShapeDtypeStruct((T_ALL, LANES), jnp.int32),
                   jax.ShapeDtypeStruct((T_ALL, LANES), F32),
                   jax.ShapeDtypeStruct((T_ALL, LANES), jnp.int32),
                   jax.ShapeDtypeStruct((8, LANES), jnp.int32)],
        scratch_shapes=[pltpu.VMEM((8, LANES), F32)],
        compiler_params=_cparams(("arbitrary",), 48 * 1024 * 1024),
        name="out_projection_router",
    )(ret_ctx, ret_lat, pool_ctx, pool_lat, x_ctx, x_lat, mod3, norm_g.reshape(1, D_MODEL),
      w_out_bf16, w_out_bf16, wr_hi, wr_both, b_router_pad)


def _count_le(ends, v):
    return jnp.sum((ends[None, :] <= v[:, None]).astype(jnp.int32), axis=1)


def _expert_schedule(counts):
    i32 = jnp.int32
    ntile = (counts + TM_EXP - 1) // TM_EXP
    tile_off = jnp.cumsum(ntile) - ntile
    nsup = (ntile + SUP_EXP - 1) // SUP_EXP
    sup_end = jnp.cumsum(nsup)
    u = jnp.arange(U_EXP, dtype=i32)
    sup_valid = u < sup_end[-1]
    uc = jnp.minimum(u, sup_end[-1] - 1)
    sup_e = jnp.minimum(_count_le(sup_end, uc), N_EXPERTS - 1)
    sup_idx = uc - (sup_end - nsup)[sup_e]
    sup_row0 = (tile_off[sup_e] + sup_idx * SUP_EXP) * TM_EXP
    sup_nt = jnp.clip(ntile[sup_e] - sup_idx * SUP_EXP, 1, SUP_EXP)
    step_e = jnp.repeat(sup_e, J_EXP)
    step_valid = jnp.repeat(sup_valid.astype(i32), J_EXP)
    step_j = jnp.where(step_valid == 1, jnp.tile(jnp.arange(J_EXP, dtype=i32), U_EXP), J_EXP - 1)
    step_row0 = jnp.repeat(sup_row0, J_EXP)
    step_nt = jnp.repeat(sup_nt, J_EXP)
    last_rows = counts - (ntile - 1) * TM_EXP
    sup_half = (sup_idx == nsup[sup_e] - 1) & (last_rows[sup_e] <= TM_EXP // 2) & (counts[sup_e] > 0)
    step_half = jnp.repeat(sup_half.astype(i32), J_EXP)
    return tile_off * TM_EXP, (step_e.astype(i32), step_j.astype(i32), step_row0.astype(i32),
                               step_nt.astype(i32), step_half, step_valid,
                               jnp.sum(ntile).astype(i32).reshape(1))


def _expert_kernel(se_ref, sj_ref, srow_ref, snt_ref, shalf_ref, sv_ref, stot_ref,
                   xs_hbm, wg_ref, wu_ref, bg_ref, bu_ref, wd_ref, bd_ref, ys_hbm,
                   acc_ref, x_buf, o_buf, x_sem, o_sem):
    s = pl.program_id(0)
    j = sj_ref[s]
    n_tiles = snt_ref[s]
    row0 = pl.multiple_of(srow_ref[s], TM_EXP)
    last = J_EXP - 1

    def x_copy():
        return pltpu.make_async_copy(xs_hbm.at[pl.ds(row0, SUP_EXP * TM_EXP)], x_buf, x_sem)

    def out_copy(t):
        rows = pl.ds(pl.multiple_of(row0 + t * TM_EXP, TM_EXP), TM_EXP)
        return pltpu.make_async_copy(o_buf.at[t % 2], ys_hbm.at[rows], o_sem.at[t % 2])

    @pl.when(s == 0)
    def _():
        o_buf[0] = jnp.zeros((TM_EXP, D_MODEL), BF16)

        def zero_copy(t):
            rows = pl.ds(pl.multiple_of(t * TM_EXP, TM_EXP), TM_EXP)
            return pltpu.make_async_copy(o_buf.at[0], ys_hbm.at[rows], o_sem.at[0])

        def start(t, carry):
            zero_copy(t).start()
            return carry

        def wait(t, carry):
            zero_copy(t).wait()
            return carry

        lax.fori_loop(stot_ref[0], NT_EXP, start, 0)
        lax.fori_loop(stot_ref[0], NT_EXP, wait, 0)

    @pl.when(sv_ref[s] == 1)
    def _():
        @pl.when(j == 0)
        def _():
            x_copy().start()
            acc_ref[...] = jnp.zeros_like(acc_ref)
            x_copy().wait()

        def accumulate(rows):
            x = x_buf[rows, :]
            gate = _dot(x, wg_ref[0]) + bg_ref[0]
            up = _dot(x, wu_ref[0]) + bu_ref[0]
            gate = jnp.minimum(gate, SWIGLU_LIMIT)
            up = jnp.clip(up, -SWIGLU_LIMIT, SWIGLU_LIMIT)
            act = (up + 1.0) * gate * _sigmoid(SWIGLU_ALPHA * gate)
            acc_ref[rows, :] += _dot(act, wd_ref[0])

        def tile_pair(p, carry):
            base = pl.multiple_of(p * (2 * TM_EXP), 2 * TM_EXP)
            accumulate(pl.ds(base, TM_EXP))
            accumulate(pl.ds(base + TM_EXP, TM_EXP))
            return carry

        n_full = n_tiles - shalf_ref[s]
        lax.fori_loop(0, n_full // 2, tile_pair, 0)

        @pl.when(n_full % 2 == 1)
        def _():
            accumulate(pl.ds(pl.multiple_of((n_full - 1) * TM_EXP, TM_EXP), TM_EXP))

        @pl.when(shalf_ref[s] == 1)
        def _():
            accumulate(pl.ds(pl.multiple_of((n_tiles - 1) * TM_EXP, TM_EXP), TM_EXP // 2))

        @pl.when(j == last)
        def _():
            def finish(t, carry):
                @pl.when(t >= 2)
                def _():
                    out_copy(t - 2).wait()

                rows = pl.ds(pl.multiple_of(t * TM_EXP, TM_EXP), TM_EXP)
                o_buf[t % 2] = (acc_ref[rows, :] + bd_ref[0]).astype(BF16)
                out_copy(t).start()
                return carry

            lax.fori_loop(0, n_tiles, finish, 0)

            @pl.when(n_tiles >= 2)
            def _():
                out_copy(n_tiles - 2).wait()

            out_copy(n_tiles - 1).wait()


def _experts(xs, sched, w_gate_up, b_gate_up, w_down, b_down):
    wmap = lambda col0: (lambda s, se, sj, srow, snt, shalf, sv, stot: (se[s], 0, col0 + sj[s]))
    return pl.pallas_call(
        _expert_kernel,
        grid_spec=pltpu.PrefetchScalarGridSpec(
            num_scalar_prefetch=7,
            grid=(G_EXP,),
            in_specs=[
                pl.BlockSpec(memory_space=pl.ANY),
                pl.BlockSpec((1, D_MODEL, F_EXP), wmap(0)),
                pl.BlockSpec((1, D_MODEL, F_EXP), wmap(J_EXP)),
                pl.BlockSpec((1, 1, F_EXP), wmap(0)),
                pl.BlockSpec((1, 1, F_EXP), wmap(J_EXP)),
                pl.BlockSpec((1, F_EXP, D_MODEL), lambda s, se, sj, srow, snt, shalf, sv, stot: (se[s], sj[s], 0)),
                pl.BlockSpec((1, 1, D_MODEL), lambda s, se, sj, srow, snt, shalf, sv, stot: (se[s], 0, 0)),
            ],
            out_specs=pl.BlockSpec(memory_space=pl.ANY),
            scratch_shapes=[pltpu.VMEM((SUP_EXP * TM_EXP, D_MODEL), F32),
                            pltpu.VMEM((SUP_EXP * TM_EXP, D_MODEL), BF16),
                            pltpu.VMEM((2, TM_EXP, D_MODEL), BF16),
                            pltpu.SemaphoreType.DMA(()),
                            pltpu.SemaphoreType.DMA((2,))]),
        out_shape=jax.ShapeDtypeStruct((NT_EXP * TM_EXP, D_MODEL), BF16),
        compiler_params=_cparams(("arbitrary",), VMEM_LIMIT),
        name="experts",
    )(*sched, xs, w_gate_up, w_gate_up,
      b_gate_up.reshape(N_EXPERTS, 1, 2 * D_FF), b_gate_up.reshape(N_EXPERTS, 1, 2 * D_FF),
      w_down, b_down.reshape(N_EXPERTS, 1, D_MODEL))


def _combine_kernel(x1_ref, y0_ref, y1_ref, y2_ref, y3_ref, rp_ref, mod_ref, g_ref, o_ref):
    rp = rp_ref[...]
    moe = rp[:, 0:1] * y0_ref[...].astype(F32)
    for k, y_ref in enumerate((y1_ref, y2_ref, y3_ref), start=1):
        moe = moe + rp[:, k:k + 1] * y_ref[...].astype(F32)
    x2 = x1_ref[...] + mod_ref[0, 5:6, :] * moe
    o_ref[...] = x2 * lax.rsqrt(jnp.mean(x2 * x2, axis=-1, keepdims=True) + NORM_EPS) * g_ref[...]


def _combine(x1, y4, route_p, mod3, final_g, row0, n_rows):
    tm = TM_CMB
    nt = n_rows // tm
    t0 = row0 // tm
    y_specs = [pl.BlockSpec((tm, D_MODEL), functools.partial(lambda k, i: (k * nt + i, 0), k))
               for k in range(TOP_K)]
    return pl.pallas_call(
        _combine_kernel,
        grid=(nt,),
        in_specs=[pl.BlockSpec((tm, D_MODEL), lambda i: (t0 + i, 0))] + y_specs + [
            pl.BlockSpec((tm, LANES), lambda i: (t0 + i, 0)),
            pl.BlockSpec((1, 6, D_MODEL), lambda i: (_cond_row(t0 + i, tm), 0, 0)),
            pl.BlockSpec((1, D_MODEL), lambda i: (0, 0))],
        out_specs=pl.BlockSpec((tm, D_MODEL), lambda i: (i, 0)),
        out_shape=jax.ShapeDtypeStruct((n_rows, D_MODEL), F32),
        compiler_params=_cparams(("arbitrary",), 40 * 1024 * 1024),
        name="combine_final_norm",
    )(x1, y4, y4, y4, y4, route_p, mod3, final_g.reshape(1, D_MODEL))


def kernel(x_prompt, x_sample, c, c_ctx, state_ret, norm1_g, norm2_g, w_mod, b_mod, w_in, ret_decay,
           w_pool, pool_scale, w_out, w_router, b_router, w_gate_up, b_gate_up, w_down, b_down, final_g):
    assert w_mod.shape[0] == 1, "single trunk layer"
    x_ctx = x_prompt.reshape(T_CTX, D_MODEL)
    x_lat = x_sample.reshape(T_LAT, D_MODEL)
    cond = jnp.zeros((COND_ROWS, D_MODEL), F32).at[0].set(c_ctx).at[1:N_COND].set(c)

    mod = _modulation(cond, w_mod[0], b_mod[0])
    mod3 = mod.reshape(COND_ROWS, 6, D_MODEL)

    proj = _in_projection(x_ctx, x_lat, norm1_g[0], mod3, w_in[0].astype(BF16))

    ret_ctx, new_state = _retention(proj, 0, BATCH, SEQ, ret_decay[0], None, None, True, RET_HEADS)
    ret_lat, _ = _retention(proj, T_CTX, DEC_BATCH, DEC_SEQ, ret_decay[0], state_ret[:, 0],
                            _rope_tables(DEC_SEQ), False, 1)
    w_pool_bf16 = w_pool[0].astype(BF16)
    pool_ctx = _pooling(proj, 0, BATCH, SEQ, w_pool_bf16, pool_scale[0], False)
    pool_lat = _pooling(proj, T_CTX, DEC_BATCH, DEC_SEQ, w_pool_bf16, pool_scale[0], True)

    wr = jnp.zeros((D_MODEL, LANES), F32).at[:, :N_EXPERTS].set(w_router[0])
    wr_hi = wr.astype(BF16)
    wr_both = jnp.concatenate([wr_hi, (wr - wr_hi.astype(F32)).astype(BF16)], axis=1)
    br = jnp.full((1, LANES), -1e30, F32).at[0, :N_EXPERTS].set(b_router[0])
    x1, h2, route_i, route_p, route_r, counts = _out_projection(
        ret_ctx, ret_lat, pool_ctx, pool_lat, x_ctx, x_lat, mod3, norm2_g[0], w_out[0].astype(BF16),
        wr_hi, wr_both, br)

    row_off, sched = _expert_schedule(counts[0, :N_EXPERTS])
    slot = (row_off[route_i[:, :TOP_K]] + route_r[:, :TOP_K]).T
    token = jnp.broadcast_to(jnp.arange(T_ALL, dtype=jnp.int32)[None, :], (TOP_K, T_ALL))
    token_of_slot = (jnp.arange(XS_ROWS, dtype=jnp.int32) % T_ALL).at[slot.reshape(-1)].set(
        token.reshape(-1), unique_indices=True, mode="promise_in_bounds")
    xs = h2.at[token_of_slot].get(mode="promise_in_bounds")
    ys = _experts(xs, sched, w_gate_up[0], b_gate_up[0], w_down[0], b_down[0])
    y4_lat = ys.at[slot[:, T_CTX:].reshape(-1)].get(mode="promise_in_bounds", unique_indices=True)
    y4_ctx = ys.at[slot[:, :T_CTX].reshape(-1)].get(mode="promise_in_bounds", unique_indices=True)
    y_lat = _combine(x1, y4_lat, route_p, mod3, final_g, T_CTX, T_LAT)
    y_ctx = _combine(x1, y4_ctx, route_p, mod3, final_g, 0, T_CTX)
    return (y_ctx.reshape(BATCH, SEQ, D_MODEL), y_lat.reshape(DEC_BATCH, DEC_SEQ, D_MODEL),
            new_state.reshape(BATCH, 1, 2, RET_HEADS, RET_DK, RET_DV))
```

```python
import functools

import numpy as np
import jax
import jax.numpy as jnp
from jax import lax
from jax.experimental import pallas as pl
from jax.experimental.pallas import tpu as pltpu

F32 = jnp.float32
BF16 = jnp.bfloat16

D_MODEL = 2048
BATCH = 16
SEQ = 256
DEC_BATCH = 4
DEC_SEQ = 2048
GRID_W = 64
RET_HEADS = 4
RET_DK = 128
RET_DV = 256
RET_QK_WIDTH = RET_HEADS * RET_DK
RET_WIDTH = RET_HEADS * RET_DV
POOL_WINDOWS = (2, 4, 8, 16)
POOL_GROUPS = 4
POOL_DG = 256
POOL_WIDTH = POOL_GROUPS * POOL_DG
IN_WIDTH = 2 * RET_QK_WIDTH + 2 * RET_WIDTH + POOL_WIDTH
N_EXPERTS = 32
TOP_K = 4
D_FF = D_MODEL
SWIGLU_LIMIT = 7.0
SWIGLU_ALPHA = 1.702
ROPE_BASE = 10000.0
NORM_EPS = 1e-6
GN_EPS = 1e-6

T_CTX = BATCH * SEQ
T_LAT = DEC_BATCH * DEC_SEQ
T_ALL = T_CTX + T_LAT
N_COND = 1 + DEC_BATCH
COND_ROWS = 8
LANES = 128

RET_CHUNK = 256

TM_IN = 512
TN_IN = 1024
TM_OUT = 256
TM_CMB = 256
H2_ROWS = 16384
POOL_TILE = 256
POOL_PAD = (max(POOL_WINDOWS) // 2) * GRID_W

TM_EXP = 256
F_EXP = 512
J_EXP = D_FF // F_EXP
SUP_EXP = 8
NT_EXP = (T_ALL * TOP_K + N_EXPERTS * (TM_EXP - 1)) // TM_EXP
U_EXP = N_EXPERTS + NT_EXP // SUP_EXP + 1
XS_ROWS = (NT_EXP + SUP_EXP) * TM_EXP
CHUNKS_EXP = 3
CH_ROWS_EXP = -(-NT_EXP // CHUNKS_EXP) * TM_EXP

VMEM_LIMIT = 56 * 1024 * 1024


def _cparams(sem, vmem=None):
    return pltpu.CompilerParams(dimension_semantics=sem, vmem_limit_bytes=vmem)


def _cond_row(i, tm):
    nctx = T_CTX // tm
    return jnp.where(i < nctx, 0, 1 + (i - nctx) // (DEC_SEQ // tm))


def _sigmoid(x):
    return 1.0 / (1.0 + jnp.exp(-x))


def _dot(a, b):
    return lax.dot_general(a, b, (((1,), (0,)), ((), ())), preferred_element_type=F32)


def _mod_kernel(c_ref, w_ref, b_ref, o_ref):
    c = c_ref[...]
    s = (c * _sigmoid(c)).astype(BF16)
    o_ref[...] = jnp.dot(s, w_ref[...].astype(BF16), preferred_element_type=F32) + b_ref[...]


def _modulation(cond, w_mod, b_mod):
    n = w_mod.shape[1]
    tn = 1024
    return pl.pallas_call(
        _mod_kernel,
        grid=(n // tn,),
        in_specs=[pl.BlockSpec((COND_ROWS, D_MODEL), lambda j: (0, 0)),
                  pl.BlockSpec((D_MODEL, tn), lambda j: (0, j)),
                  pl.BlockSpec((1, tn), lambda j: (0, j))],
        out_specs=pl.BlockSpec((COND_ROWS, tn), lambda j: (0, j)),
        out_shape=jax.ShapeDtypeStruct((COND_ROWS, n), F32),
        compiler_params=_cparams(("arbitrary",), 40 * 1024 * 1024),
        name="modulation",
    )(cond, w_mod, b_mod.reshape(1, n))


def _ctx_tile(i, tm):
    return jnp.minimum(i, T_CTX // tm - 1)


def _lat_tile(i, tm):
    return jnp.maximum(i - T_CTX // tm, 0)


def _inproj_kernel(xc_ref, xl_ref, g_ref, mod_ref, w_ref, o_ref, h_ref):
    def modulated_norm(x_ref):
        x = x_ref[...]
        y = x * lax.rsqrt(jnp.mean(x * x, axis=-1, keepdims=True) + NORM_EPS) * g_ref[...]
        shift = mod_ref[0, 0:1, :]
        scale = mod_ref[0, 1:2, :]
        h_ref[...] = (y * (1.0 + scale) + shift).astype(BF16)

    first = pl.program_id(1) == 0
    is_ctx = pl.program_id(0) < T_CTX // TM_IN
    pl.when(first & is_ctx)(lambda: modulated_norm(xc_ref))
    pl.when(first & jnp.logical_not(is_ctx))(lambda: modulated_norm(xl_ref))

    o_ref[...] = jnp.dot(h_ref[...], w_ref[...], preferred_element_type=F32)


def _in_projection(x_ctx, x_lat, norm_g, mod3, w_in_bf16):
    return pl.pallas_call(
        _inproj_kernel,
        grid=(T_ALL // TM_IN, IN_WIDTH // TN_IN),
        in_specs=[pl.BlockSpec((TM_IN, D_MODEL), lambda i, j: (_ctx_tile(i, TM_IN), 0)),
                  pl.BlockSpec((TM_IN, D_MODEL), lambda i, j: (_lat_tile(i, TM_IN), 0)),
                  pl.BlockSpec((1, D_MODEL), lambda i, j: (0, 0)),
                  pl.BlockSpec((1, 6, D_MODEL), lambda i, j: (_cond_row(i, TM_IN), 0, 0)),
                  pl.BlockSpec((D_MODEL, TN_IN), lambda i, j: (0, j))],
        out_specs=pl.BlockSpec((TM_IN, TN_IN), lambda i, j: (i, j)),
        out_shape=jax.ShapeDtypeStruct((T_ALL, IN_WIDTH), F32),
        scratch_shapes=[pltpu.VMEM((TM_IN, D_MODEL), BF16)],
        compiler_params=_cparams(("arbitrary", "arbitrary"), 40 * 1024 * 1024),
        name="in_projection",
    )(x_ctx, x_lat, norm_g.reshape(1, D_MODEL), mod3, w_in_bf16)


def _rope_tables(seq_len):
    t = jnp.arange(seq_len)
    row = (t // GRID_W).astype(F32)
    col = (t % GRID_W).astype(F32)
    half = RET_DK // 2
    n_freq = half // 2
    inv = ROPE_BASE ** (-jnp.arange(n_freq, dtype=F32) / n_freq)
    ang_r = row[:, None] * inv
    ang_c = col[:, None] * inv
    cos = jnp.concatenate([jnp.cos(ang_r), jnp.cos(ang_r), jnp.cos(ang_c), jnp.cos(ang_c)], axis=-1)
    sin = jnp.concatenate([-jnp.sin(ang_r), jnp.sin(ang_r), -jnp.sin(ang_c), jnp.sin(ang_c)], axis=-1)
    return cos, sin


def _retention_kernel(rd_ref, *refs, seq_len, chunk, use_rope, has_state_in, has_state_out, heads):
    refs = list(refs)
    q_ref, k_ref, v_ref, g_ref = refs[:4]
    pos = 4
    if has_state_in:
        s0_ref = refs[pos]
        pos += 1
    if use_rope:
        cos_ref, sin_ref = refs[pos], refs[pos + 1]
        pos += 2
    o_ref = refs[pos]
    pos += 1
    if has_state_out:
        st_ref = refs[pos]
        pos += 1
    scratch = refs[pos:pos + 5]

    for hh in range(heads):
        _retention_head(rd_ref, pl.program_id(1) * heads + hh, hh,
                        q_ref, k_ref, v_ref, g_ref,
                        s0_ref if has_state_in else None,
                        (cos_ref, sin_ref) if use_rope else None,
                        o_ref, st_ref if has_state_out else None,
                        [r.at[hh] for r in scratch], seq_len, chunk)


def _retention_head(rd_ref, h, hh, q_ref, k_ref, v_ref, g_ref, s0_ref, rope_refs, o_ref, st_ref, scratch,
                    seq_len, chunk):
    qs_ref, ks_ref, acc_ref, sf_ref, sb_ref = scratch
    use_rope = rope_refs is not None
    kcols = slice(hh * RET_DK, (hh + 1) * RET_DK)
    vcols = slice(hh * RET_DV, (hh + 1) * RET_DV)
    C = chunk
    n_chunks = seq_len // C

    lgf = -jnp.exp(jnp.full((C, 1), rd_ref[0, h], F32))
    lgb = -jnp.exp(jnp.full((C, 1), rd_ref[1, h], F32))
    ii = lax.broadcasted_iota(jnp.int32, (C, C), 0)
    jj = lax.broadcasted_iota(jnp.int32, (C, C), 1)
    diff = (ii - jj).astype(F32)
    decay = (jnp.where(diff >= 0, jnp.exp(lgf * jnp.maximum(diff, 0.0)), 0.0)
             + jnp.where(diff <= 0, jnp.exp(lgb * jnp.maximum(-diff, 0.0)), 0.0))
    p = lax.broadcasted_iota(jnp.int32, (C, 1), 0).astype(F32)
    xi_f = jnp.exp(lgf * (p + 1.0))
    zeta_f = jnp.exp(lgf * (C - 1.0 - p))
    xi_b = jnp.exp(lgb * (C - p))
    zeta_b = jnp.exp(lgb * p)
    cd_f = jnp.exp(lgf[0:1, :] * C)
    cd_b = jnp.exp(lgb[0:1, :] * C)

    if s0_ref is not None:
        sf_ref[...] = s0_ref[0, 0, hh]
        sb_ref[...] = s0_ref[0, 1, hh]
    else:
        sf_ref[...] = jnp.zeros_like(sf_ref)
        sb_ref[...] = jnp.zeros_like(sb_ref)

    lane = lax.broadcasted_iota(jnp.int32, (C, RET_DK), 1)
    first_half = (lane & 32) == 0

    def rope(x, cs, sn):
        swapped = jnp.where(first_half, pltpu.roll(x, RET_DK - 32, axis=1), pltpu.roll(x, 32, axis=1))
        return x * cs + swapped * sn

    def state_update(s_ref, kz, v, cd):
        upd = lax.dot_general(kz, v, (((0,), (0,)), ((), ())), preferred_element_type=F32)
        s_ref[...] = cd * s_ref[...] + upd

    def chunk_rows(c):
        start = c * C
        return pl.ds(start if isinstance(start, int) else pl.multiple_of(start, C), C)

    def fwd_chunk(c):
        sl = chunk_rows(c)
        q = q_ref[sl, kcols]
        k = k_ref[sl, kcols] * (RET_DK ** -0.5)
        if use_rope:
            cs = rope_refs[0][sl, :]
            sn = rope_refs[1][sl, :]
            q = rope(q, cs, sn)
            k = rope(k, cs, sn)
        qb = q.astype(BF16)
        kb = k.astype(BF16)
        qs_ref[sl, :] = qb
        ks_ref[sl, :] = k
        v = v_ref[sl, vcols].astype(BF16)
        scores = lax.dot_general(qb, kb, (((1,), (1,)), ((), ())), preferred_element_type=F32)
        inner = jnp.dot((scores * decay).astype(BF16), v, preferred_element_type=F32)
        cross = jnp.dot(qb, sf_ref[...].astype(BF16), preferred_element_type=F32) * xi_f
        acc_ref[sl, :] = inner + cross
        state_update(sf_ref, (k * zeta_f).astype(BF16), v, cd_f)

    def bwd_chunk(c):
        sl = chunk_rows(c)
        qb = qs_ref[sl, :]
        k = ks_ref[sl, :]
        v = v_ref[sl, vcols].astype(BF16)
        cross = jnp.dot(qb, sb_ref[...].astype(BF16), preferred_element_type=F32) * xi_b
        o = acc_ref[sl, :] + cross
        mu = jnp.mean(o, axis=-1, keepdims=True)
        oc = o - mu
        var = jnp.mean(oc * oc, axis=-1, keepdims=True)
        on = oc * lax.rsqrt(var + GN_EPS)
        g = g_ref[sl, vcols]
        o_ref[sl, vcols] = (on * (g * _sigmoid(g))).astype(BF16)
        state_update(sb_ref, (k * zeta_b).astype(BF16), v, cd_b)

    if n_chunks == 1:
        fwd_chunk(0)
        bwd_chunk(0)
    else:
        assert n_chunks % 2 == 0

        def fwd_pair(i, carry):
            fwd_chunk(2 * i)
            fwd_chunk(2 * i + 1)
            return carry

        def bwd_pair(i, carry):
            bwd_chunk(n_chunks - 1 - 2 * i)
            bwd_chunk(n_chunks - 2 - 2 * i)
            return carry

        lax.fori_loop(0, n_chunks // 2, fwd_pair, 0)
        lax.fori_loop(0, n_chunks // 2, bwd_pair, 0)

    if st_ref is not None:
        st_ref[0, 0, hh] = sf_ref[...]
        st_ref[0, 1, hh] = sb_ref[...]


def _retention(proj, row0, B, L, ret_decay, state_in, rope_tabs, want_state, heads):
    rb = row0 // L
    use_rope = rope_tabs is not None
    has_state_in = state_in is not None
    wk = heads * RET_DK
    wv = heads * RET_DV
    kq = RET_QK_WIDTH // wk
    kv = 2 * RET_QK_WIDTH // wv
    kg = kv + RET_WIDTH // wv
    in_specs = [pl.BlockSpec((L, wk), lambda b, h, rd: (rb + b, h)),
                pl.BlockSpec((L, wk), lambda b, h, rd: (rb + b, kq + h)),
                pl.BlockSpec((L, wv), lambda b, h, rd: (rb + b, kv + h)),
                pl.BlockSpec((L, wv), lambda b, h, rd: (rb + b, kg + h))]
    args = [proj, proj, proj, proj]
    state_spec = pl.BlockSpec((1, 2, heads, RET_DK, RET_DV), lambda b, h, rd: (b, 0, h, 0, 0))
    if has_state_in:
        in_specs.append(state_spec)
        args.append(state_in)
    if use_rope:
        in_specs += [pl.BlockSpec((L, RET_DK), lambda b, h, rd: (0, 0))] * 2
        args += list(rope_tabs)
    out_specs = [pl.BlockSpec((L, wv), lambda b, h, rd: (b, h))]
    out_shape = [jax.ShapeDtypeStruct((B * L, RET_WIDTH), BF16)]
    if want_state:
        out_specs.append(state_spec)
        out_shape.append(jax.ShapeDtypeStruct((B, 2, RET_HEADS, RET_DK, RET_DV), F32))
    kern = functools.partial(_retention_kernel, seq_len=L, chunk=RET_CHUNK, use_rope=use_rope,
                             has_state_in=has_state_in, has_state_out=want_state, heads=heads)
    res = pl.pallas_call(
        kern,
        grid_spec=pltpu.PrefetchScalarGridSpec(
            num_scalar_prefetch=1,
            grid=(B, RET_HEADS // heads),
            in_specs=in_specs,
            out_specs=out_specs,
            scratch_shapes=[pltpu.VMEM((heads, L, RET_DK), BF16),
                            pltpu.VMEM((heads, L, RET_DK), F32),
                            pltpu.VMEM((heads, L, RET_DV), F32),
                            pltpu.VMEM((heads, RET_DK, RET_DV), F32),
                            pltpu.VMEM((heads, RET_DK, RET_DV), F32)]),
        out_shape=out_shape,
        compiler_params=_cparams(("arbitrary", "arbitrary"), 40 * 1024 * 1024),
        name="retention_grid" if use_rope else "retention_seq",
    )(ret_decay, *args)
    return res if want_state else (res[0], None)


def _split3(x):
    hi = x.astype(BF16)
    r1 = x - hi.astype(F32)
    mid = r1.astype(BF16)
    lo = (r1 - mid.astype(F32)).astype(BF16)
    return jnp.concatenate([hi, mid, lo], axis=-1)


def _pool_kernel(p_ref, w_ref, sc_ref, o_ref, pad_ref, *, seq_len, grid_mode):
    PT = POOL_TILE
    n_tiles = seq_len // PT
    ii = lax.broadcasted_iota(jnp.int32, (PT, PT), 0)
    jj = lax.broadcasted_iota(jnp.int32, (PT, PT), 1)
    d = jj - ii
    ti = lax.broadcasted_iota(jnp.int32, (PT, 1), 0)

    if grid_mode:
        zeros = jnp.zeros((POOL_PAD, POOL_DG), F32)
        pad_ref[0:POOL_PAD, :] = zeros
        pad_ref[POOL_PAD + seq_len:POOL_PAD + seq_len + POOL_PAD, :] = zeros

    for gi, w in enumerate(POOL_WINDOWS):
        lo_off = -(w // 2)
        hi_off = w - w // 2 - 1
        cols = slice(gi * POOL_DG, (gi + 1) * POOL_DG)
        in_window = (d >= lo_off) & (d <= hi_off)
        if grid_mode:
            in_window = in_window & ((ii // GRID_W) == (jj // GRID_W))
        band = jnp.where(in_window, 1.0, 0.0).astype(BF16)
        wg = w_ref[gi]
        scale = sc_ref[gi]

        def window_sum(t0):
            x = p_ref[pl.ds(t0, PT), cols]
            s3 = jnp.dot(band, _split3(x), preferred_element_type=F32)
            return s3[:, 0:POOL_DG] + s3[:, POOL_DG:2 * POOL_DG] + s3[:, 2 * POOL_DG:3 * POOL_DG]

        def finish(t0, s, cnt):
            x = p_ref[pl.ds(t0, PT), cols]
            diff = (s / cnt - x).astype(BF16)
            y = jnp.dot(diff, wg, preferred_element_type=F32) * scale
            o_ref[pl.ds(t0, PT), cols] = y.astype(BF16)

        def count(pos, n):
            lo = jnp.maximum(pos + lo_off, 0)
            hi = jnp.minimum(pos + lo_off + w, n)
            return (hi - lo).astype(F32)

        if grid_mode:
            for t in range(n_tiles):
                pad_ref[POOL_PAD + t * PT:POOL_PAD + (t + 1) * PT, :] = window_sum(t * PT)
            for t in range(n_tiles):
                base = POOL_PAD + t * PT
                s = pad_ref[base + lo_off * GRID_W:base + lo_off * GRID_W + PT, :]
                for r in range(lo_off + 1, hi_off + 1):
                    s = s + pad_ref[base + r * GRID_W:base + r * GRID_W + PT, :]
                tok = ti + t * PT
                cnt = count(tok // GRID_W, seq_len // GRID_W) * count(tok % GRID_W, GRID_W)
                finish(t * PT, s, cnt)
        else:
            for t in range(n_tiles):
                finish(t * PT, window_sum(t * PT), count(ti + t * PT, seq_len))


def _pooling(proj, row0, B, L, w_pool_bf16, pool_scale, grid_mode):
    if not grid_mode:
        assert L == POOL_TILE
    rb = row0 // L
    pcol = (IN_WIDTH - POOL_WIDTH) // POOL_WIDTH
    kern = functools.partial(_pool_kernel, seq_len=L, grid_mode=grid_mode)
    return pl.pallas_call(
        kern,
        grid=(B,),
        in_specs=[pl.BlockSpec((L, POOL_WIDTH), lambda b: (rb + b, pcol)),
                  pl.BlockSpec((POOL_GROUPS, POOL_DG, POOL_DG), lambda b: (0, 0, 0)),
                  pl.BlockSpec((POOL_GROUPS, 1, POOL_DG), lambda b: (0, 0, 0))],
        out_specs=pl.BlockSpec((L, POOL_WIDTH), lambda b: (b, 0)),
        out_shape=jax.ShapeDtypeStruct((B * L, POOL_WIDTH), BF16),
        scratch_shapes=[pltpu.VMEM((L + 2 * POOL_PAD, POOL_DG), F32)],
        compiler_params=_cparams(("arbitrary",), 48 * 1024 * 1024),
        name="pool_grid" if grid_mode else "pool_seq",
    )(proj, w_pool_bf16, pool_scale.reshape(POOL_GROUPS, 1, POOL_DG))


def _outproj_kernel(*refs):
    i = pl.program_id(0)
    h2_ref = refs[14]

    @pl.when(i >= T_ALL // TM_OUT)
    def _():
        h2_ref[...] = jnp.zeros_like(h2_ref)

    pl.when(i < T_ALL // TM_OUT)(lambda: _outproj_tile(*refs))


def _outproj_tile(retc_ref, retl_ref, poolc_ref, pooll_ref, xc_ref, xl_ref, mod_ref, g_ref,
                  wt_ref, wb_ref, wrh_ref, wrb_ref, br_ref,
                  x1_ref, h2_ref, ri_ref, rp_ref, rr_ref, cnt_ref, run_ref):
    i = pl.program_id(0)

    @pl.when(i == 0)
    def _():
        run_ref[...] = jnp.zeros_like(run_ref)

    is_ctx = i < T_CTX // TM_OUT
    ret = jnp.where(is_ctx, retc_ref[...], retl_ref[...])
    pool = jnp.where(is_ctx, poolc_ref[...], pooll_ref[...])
    x = jnp.where(is_ctx, xc_ref[...], xl_ref[...])
    y = (jnp.dot(ret, wt_ref[...], preferred_element_type=F32)
         + jnp.dot(pool, wb_ref[...], preferred_element_type=F32))
    x1 = x + mod_ref[0, 2:3, :] * y
    x1_ref[...] = x1
    hn = x1 * lax.rsqrt(jnp.mean(x1 * x1, axis=-1, keepdims=True) + NORM_EPS) * g_ref[...]
    h2 = hn * (1.0 + mod_ref[0, 4:5, :]) + mod_ref[0, 3:4, :]
    hi = h2.astype(BF16)
    h2_ref[...] = hi
    lo = (h2 - hi.astype(F32)).astype(BF16)
    both = jnp.dot(hi, wrb_ref[...], preferred_element_type=F32)
    logits = (both[:, 0:LANES] + jnp.dot(lo, wrh_ref[...], preferred_element_type=F32)
              + both[:, LANES:2 * LANES]) + br_ref[...]

    tm = logits.shape[0]
    lane = lax.broadcasted_iota(jnp.int32, (tm, LANES), 1)
    work = logits
    vals, idxs, hots = [], [], []
    for _ in range(TOP_K):
        m = jnp.max(work, axis=-1, keepdims=True)
        idx = jnp.min(jnp.where(work == m, lane, LANES), axis=-1, keepdims=True)
        hot = lane == idx
        vals.append(m)
        idxs.append(idx)
        hots.append(hot)
        work = jnp.where(hot, -jnp.inf, work)
    exps = [jnp.exp(v - vals[0]) for v in vals]
    denom = exps[0] + exps[1] + exps[2] + exps[3]

    selected = jnp.zeros((tm, LANES), F32)
    for hot in hots:
        selected = selected + jnp.where(hot, 1.0, 0.0)
    r_i = lax.broadcasted_iota(jnp.int32, (tm, tm), 0)
    c_i = lax.broadcasted_iota(jnp.int32, (tm, tm), 1)
    tri = jnp.where(c_i < r_i, 1.0, 0.0).astype(BF16)
    before = jnp.dot(tri, selected.astype(BF16), preferred_element_type=F32) + run_ref[0:1, :]

    ri = jnp.zeros((tm, LANES), jnp.int32)
    rp = jnp.zeros((tm, LANES), F32)
    rr = jnp.zeros((tm, LANES), jnp.int32)
    for k in range(TOP_K):
        rank = jnp.sum(jnp.where(hots[k], before, 0.0), axis=-1, keepdims=True).astype(jnp.int32)
        ri = jnp.where(lane == k, idxs[k], ri)
        rp = jnp.where(lane == k, exps[k] / denom, rp)
        rr = jnp.where(lane == k, rank, rr)
    ri_ref[...] = ri
    rp_ref[...] = rp
    rr_ref[...] = rr

    run = run_ref[0:1, :] + jnp.sum(selected, axis=0, keepdims=True)
    run_ref[...] = jnp.broadcast_to(run, run_ref.shape)
    cnt_ref[...] = jnp.broadcast_to(run, cnt_ref.shape).astype(jnp.int32)


def _out_projection(ret_ctx, ret_lat, pool_ctx, pool_lat, x_ctx, x_lat, mod3, norm_g, w_out_bf16,
                    wr_hi, wr_both, b_router_pad):
    tm = TM_OUT
    real = lambda i: jnp.minimum(i, T_ALL // tm - 1)
    row = lambda i: (real(i), 0)
    const = lambda i: (0, 0)
    crow = lambda i: (_ctx_tile(real(i), tm), 0)
    lrow = lambda i: (_lat_tile(real(i), tm), 0)
    return pl.pallas_call(
        _outproj_kernel,
        grid=(H2_ROWS // tm,),
        in_specs=[pl.BlockSpec((tm, RET_WIDTH), crow),
                  pl.BlockSpec((tm, RET_WIDTH), lrow),
                  pl.BlockSpec((tm, POOL_WIDTH), crow),
                  pl.BlockSpec((tm, POOL_WIDTH), lrow),
                  pl.BlockSpec((tm, D_MODEL), crow),
                  pl.BlockSpec((tm, D_MODEL), lrow),
                  pl.BlockSpec((1, 6, D_MODEL), lambda i: (_cond_row(real(i), tm), 0, 0)),
                  pl.BlockSpec((1, D_MODEL), const),
                  pl.BlockSpec((RET_WIDTH, D_MODEL), const),
                  pl.BlockSpec((POOL_WIDTH, D_MODEL), lambda i: (1, 0)),
                  pl.BlockSpec((D_MODEL, LANES), const),
                  pl.BlockSpec((D_MODEL, 2 * LANES), const),
                  pl.BlockSpec((1, LANES), const)],
        out_specs=[pl.BlockSpec((tm, D_MODEL), row),
                   pl.BlockSpec((tm, D_MODEL), lambda i: (i, 0)),
                   pl.BlockSpec((tm, LANES), row),
                   pl.BlockSpec((tm, LANES), row),
                   pl.BlockSpec((tm, LANES), row),
                   pl.BlockSpec((8, LANES), const)],
        out_shape=[jax.ShapeDtypeStruct((T_ALL, D_MODEL), F32),
                   jax.ShapeDtypeStruct((H2_ROWS, D_MODEL), BF16),
                   jax.ShapeDtypeStruct((T_ALL, LANES), jnp.int32),
                   jax.ShapeDtypeStruct((T_ALL, LANES), F32),
                   jax.ShapeDtypeStruct((T_ALL, LANES), jnp.int32),
                   jax.ShapeDtypeStruct((8, LANES), jnp.int32)],
        scratch_shapes=[pltpu.VMEM((8, LANES), F32)],
        compiler_params=_cparams(("arbitrary",), 48 * 1024 * 1024),
        name="out_projection_router",
    )(ret_ctx, ret_lat, pool_ctx, pool_lat, x_ctx, x_lat, mod3, norm_g.reshape(1, D_MODEL),
      w_out_bf16, w_out_bf16, wr_hi, wr_both, b_router_pad)


def _count_le(ends, v):
    return jnp.sum((ends[None, :] <= v[:, None]).astype(jnp.int32), axis=1)


def _expert_schedule(counts):
    i32 = jnp.int32
    ntile = (counts + TM_EXP - 1) // TM_EXP
    tile_off = jnp.cumsum(ntile) - ntile
    nsup = (ntile + SUP_EXP - 1) // SUP_EXP
    sup_end = jnp.cumsum(nsup)
    u = jnp.arange(U_EXP, dtype=i32)
    sup_valid = u < sup_end[-1]
    uc = jnp.minimum(u, sup_end[-1] - 1)
    sup_e = jnp.minimum(_count_le(sup_end, uc), N_EXPERTS - 1)
    sup_idx = uc - (sup_end - nsup)[sup_e]
    per_sup = (ntile + jnp.maximum(nsup, 1) - 1) // jnp.maximum(nsup, 1)
    sup_row0 = (tile_off[sup_e] + sup_idx * per_sup[sup_e]) * TM_EXP
    sup_nt = jnp.clip(ntile[sup_e] - sup_idx * per_sup[sup_e], 1, per_sup[sup_e])
    step_e = jnp.repeat(sup_e, J_EXP)
    step_j = jnp.tile(jnp.arange(J_EXP, dtype=i32), U_EXP)
    step_row0 = jnp.repeat(sup_row0, J_EXP)
    step_nt = jnp.repeat(sup_nt, J_EXP)
    last_rows = counts - (ntile - 1) * TM_EXP
    sup_half = (sup_idx == nsup[sup_e] - 1) & (last_rows[sup_e] <= TM_EXP // 2) & (counts[sup_e] > 0)
    step_half = jnp.repeat(sup_half.astype(i32), J_EXP)
    bounds = jnp.arange(1, CHUNKS_EXP, dtype=i32) * CH_ROWS_EXP
    before = jnp.sum((sup_valid[None, :] & (sup_row0[None, :] < bounds[:, None])).astype(i32), axis=1) * J_EXP
    n_steps = jnp.sum(sup_valid.astype(i32)) * J_EXP
    chunk_start = jnp.concatenate([jnp.zeros((1,), i32), before])
    chunk_steps = jnp.concatenate([before, n_steps.reshape(1)]) - chunk_start
    tables = (step_e.astype(i32), step_j.astype(i32), step_row0.astype(i32), step_nt.astype(i32), step_half)
    return tile_off * TM_EXP, tables, chunk_start, chunk_steps, jnp.sum(ntile).astype(i32)


def _expert_kernel(se_ref, sj_ref, srow_ref, snt_ref, shalf_ref, meta_ref,
                   xs_hbm, wg_ref, wu_ref, bg_ref, bu_ref, wd_ref, bd_ref, *rest, first_chunk):
    ys_hbm, acc_ref, x_buf, o_buf, x_sem, o_sem = rest if first_chunk else rest[1:]
    s = pl.program_id(0) + meta_ref[0]
    j = sj_ref[s]
    n_tiles = snt_ref[s]
    row0 = pl.multiple_of(srow_ref[s], TM_EXP)
    last = J_EXP - 1

    def x_copy():
        rows = pl.ds(pl.multiple_of(row0 - meta_ref[1], TM_EXP), SUP_EXP * TM_EXP)
        return pltpu.make_async_copy(xs_hbm.at[rows], x_buf, x_sem)

    def out_copy(t):
        rows = pl.ds(pl.multiple_of(row0 + t * TM_EXP, TM_EXP), TM_EXP)
        return pltpu.make_async_copy(o_buf.at[t % 2], ys_hbm.at[rows], o_sem.at[t % 2])

    @pl.when((pl.program_id(0) == 0) & first_chunk)
    def _():
        o_buf[0] = jnp.zeros((TM_EXP, D_MODEL), BF16)

        def zero_copy(t):
            rows = pl.ds(pl.multiple_of(t * TM_EXP, TM_EXP), TM_EXP)
            return pltpu.make_async_copy(o_buf.at[0], ys_hbm.at[rows], o_sem.at[0])

        def start(t, carry):
            zero_copy(t).start()
            return carry

        def wait(t, carry):
            zero_copy(t).wait()
            return carry

        lax.fori_loop(meta_ref[2], NT_EXP, start, 0)
        lax.fori_loop(meta_ref[2], NT_EXP, wait, 0)

    def step():
        @pl.when(j == 0)
        def _():
            x_copy().start()
            acc_ref[...] = jnp.zeros_like(acc_ref)
            x_copy().wait()

        def accumulate(rows):
            x = x_buf[rows, :]
            gate = _dot(x, wg_ref[0]) + bg_ref[0]
            up = _dot(x, wu_ref[0]) + bu_ref[0]
            gate = jnp.minimum(gate, SWIGLU_LIMIT)
            up = jnp.clip(up, -SWIGLU_LIMIT, SWIGLU_LIMIT)
            act = (up + 1.0) * gate * _sigmoid(SWIGLU_ALPHA * gate)
            acc_ref[rows, :] += _dot(act, wd_ref[0])

        def tile_pair(p, carry):
            base = pl.multiple_of(p * (2 * TM_EXP), 2 * TM_EXP)
            accumulate(pl.ds(base, TM_EXP))
            accumulate(pl.ds(base + TM_EXP, TM_EXP))
            return carry

        n_full = n_tiles - shalf_ref[s]
        lax.fori_loop(0, n_full // 2, tile_pair, 0)

        @pl.when(n_full % 2 == 1)
        def _():
            accumulate(pl.ds(pl.multiple_of((n_full - 1) * TM_EXP, TM_EXP), TM_EXP))

        @pl.when(shalf_ref[s] == 1)
        def _():
            accumulate(pl.ds(pl.multiple_of((n_tiles - 1) * TM_EXP, TM_EXP), TM_EXP // 2))

        @pl.when(j == last)
        def _():
            def finish(t, carry):
                @pl.when(t >= 2)
                def _():
                    out_copy(t - 2).wait()

                rows = pl.ds(pl.multiple_of(t * TM_EXP, TM_EXP), TM_EXP)
                o_buf[t % 2] = (acc_ref[rows, :] + bd_ref[0]).astype(BF16)
                out_copy(t).start()
                return carry

            lax.fori_loop(0, n_tiles, finish, 0)

            @pl.when(n_tiles >= 2)
            def _():
                out_copy(n_tiles - 2).wait()

            out_copy(n_tiles - 1).wait()

    step()


def _experts_chunk(chunk, xs_part, ys, tables, meta, n_steps, w_gate_up, b_gate_up, w_down, b_down):
    first = ys is None
    wmap = lambda col0: (lambda s, se, sj, srow, snt, shalf, meta: (se[s + meta[0]], 0, col0 + sj[s + meta[0]]))
    n_prefetch = len(tables) + 1
    in_specs = [
        pl.BlockSpec(memory_space=pl.ANY),
        pl.BlockSpec((1, D_MODEL, F_EXP), wmap(0)),
        pl.BlockSpec((1, D_MODEL, F_EXP), wmap(J_EXP)),
        pl.BlockSpec((1, 1, F_EXP), wmap(0)),
        pl.BlockSpec((1, 1, F_EXP), wmap(J_EXP)),
        pl.BlockSpec((1, F_EXP, D_MODEL),
                     lambda s, se, sj, srow, snt, shalf, meta: (se[s + meta[0]], sj[s + meta[0]], 0)),
        pl.BlockSpec((1, 1, D_MODEL), lambda s, se, sj, srow, snt, shalf, meta: (se[s + meta[0]], 0, 0)),
    ]
    args = [xs_part, w_gate_up, w_gate_up,
            b_gate_up.reshape(N_EXPERTS, 1, 2 * D_FF), b_gate_up.reshape(N_EXPERTS, 1, 2 * D_FF),
            w_down, b_down.reshape(N_EXPERTS, 1, D_MODEL)]
    aliases = {}
    if not first:
        in_specs.append(pl.BlockSpec(memory_space=pl.ANY))
        args.append(ys)
        aliases = {n_prefetch + len(args) - 1: 0}
    return pl.pallas_call(
        functools.partial(_expert_kernel, first_chunk=first),
        grid_spec=pltpu.PrefetchScalarGridSpec(
            num_scalar_prefetch=n_prefetch,
            grid=(n_steps,),
            in_specs=in_specs,
            out_specs=pl.BlockSpec(memory_space=pl.ANY),
            scratch_shapes=[pltpu.VMEM((SUP_EXP * TM_EXP, D_MODEL), F32),
                            pltpu.VMEM((SUP_EXP * TM_EXP, D_MODEL), BF16),
                            pltpu.VMEM((2, TM_EXP, D_MODEL), BF16),
                            pltpu.SemaphoreType.DMA(()),
                            pltpu.SemaphoreType.DMA((2,))]),
        out_shape=jax.ShapeDtypeStruct((NT_EXP * TM_EXP, D_MODEL), BF16),
        input_output_aliases=aliases,
        compiler_params=_cparams(("arbitrary",), VMEM_LIMIT),
        name=f"experts_chunk{chunk}",
    )(*tables, meta, *args)


def _combine_kernel(x1_ref, y0_ref, y1_ref, y2_ref, y3_ref, rp_ref, mod_ref, g_ref, o_ref):
    rp = rp_ref[...]
    moe = rp[:, 0:1] * y0_ref[...].astype(F32)
    for k, y_ref in enumerate((y1_ref, y2_ref, y3_ref), start=1):
        moe = moe + rp[:, k:k + 1] * y_ref[...].astype(F32)
    x2 = x1_ref[...] + mod_ref[0, 5:6, :] * moe
    o_ref[...] = x2 * lax.rsqrt(jnp.mean(x2 * x2, axis=-1, keepdims=True) + NORM_EPS) * g_ref[...]


def _combine(x1, y4, route_p, mod3, final_g, row0, n_rows):
    tm = TM_CMB
    nt = n_rows // tm
    t0 = row0 // tm
    y_specs = [pl.BlockSpec((tm, D_MODEL), functools.partial(lambda k, i: (k * nt + i, 0), k))
               for k in range(TOP_K)]
    return pl.pallas_call(
        _combine_kernel,
        grid=(nt,),
        in_specs=[pl.BlockSpec((tm, D_MODEL), lambda i: (t0 + i, 0))] + y_specs + [
            pl.BlockSpec((tm, LANES), lambda i: (t0 + i, 0)),
            pl.BlockSpec((1, 6, D_MODEL), lambda i: (_cond_row(t0 + i, tm), 0, 0)),
            pl.BlockSpec((1, D_MODEL), lambda i: (0, 0))],
        out_specs=pl.BlockSpec((tm, D_MODEL), lambda i: (i, 0)),
        out_shape=jax.ShapeDtypeStruct((n_rows, D_MODEL), F32),
        compiler_params=_cparams(("arbitrary",), 40 * 1024 * 1024),
        name="combine_final_norm",
    )(x1, y4, y4, y4, y4, route_p, mod3, final_g.reshape(1, D_MODEL))


def kernel(x_prompt, x_sample, c, c_ctx, state_ret, norm1_g, norm2_g, w_mod, b_mod, w_in, ret_decay,
           w_pool, pool_scale, w_out, w_router, b_router, w_gate_up, b_gate_up, w_down, b_down, final_g):
    assert w_mod.shape[0] == 1, "single trunk layer"
    x_ctx = x_prompt.reshape(T_CTX, D_MODEL)
    x_lat = x_sample.reshape(T_LAT, D_MODEL)
    cond = jnp.zeros((COND_ROWS, D_MODEL), F32).at[0].set(c_ctx).at[1:N_COND].set(c)

    mod = _modulation(cond, w_mod[0], b_mod[0])
    mod3 = mod.reshape(COND_ROWS, 6, D_MODEL)

    proj = _in_projection(x_ctx, x_lat, norm1_g[0], mod3, w_in[0].astype(BF16))

    ret_ctx, new_state = _retention(proj, 0, BATCH, SEQ, ret_decay[0], None, None, True, RET_HEADS)
    ret_lat, _ = _retention(proj, T_CTX, DEC_BATCH, DEC_SEQ, ret_decay[0], state_ret[:, 0],
                            _rope_tables(DEC_SEQ), False, 1)
    w_pool_bf16 = w_pool[0].astype(BF16)
    pool_ctx = _pooling(proj, 0, BATCH, SEQ, w_pool_bf16, pool_scale[0], False)
    pool_lat = _pooling(proj, T_CTX, DEC_BATCH, DEC_SEQ, w_pool_bf16, pool_scale[0], True)

    wr = jnp.zeros((D_MODEL, LANES), F32).at[:, :N_EXPERTS].set(w_router[0])
    wr_hi = wr.astype(BF16)
    wr_both = jnp.concatenate([wr_hi, (wr - wr_hi.astype(F32)).astype(BF16)], axis=1)
    br = jnp.full((1, LANES), -1e30, F32).at[0, :N_EXPERTS].set(b_router[0])
    x1, h2, route_i, route_p, route_r, counts = _out_projection(
        ret_ctx, ret_lat, pool_ctx, pool_lat, x_ctx, x_lat, mod3, norm2_g[0], w_out[0].astype(BF16),
        wr_hi, wr_both, br)

    row_off, tables, chunk_start, chunk_steps, n_row_tiles = _expert_schedule(counts[0, :N_EXPERTS])
    slot = (row_off[route_i[:, :TOP_K]] + route_r[:, :TOP_K]).T
    token = jnp.broadcast_to(jnp.arange(T_ALL, dtype=jnp.int32)[None, :], (TOP_K, T_ALL))
    token_of_slot = (jnp.arange(XS_ROWS, dtype=jnp.int32) % T_ALL).at[slot.reshape(-1)].set(
        token.reshape(-1), unique_indices=True, mode="promise_in_bounds")
    ys = None
    for c in range(CHUNKS_EXP):
        lo = c * CH_ROWS_EXP
        hi = min(lo + CH_ROWS_EXP + SUP_EXP * TM_EXP, XS_ROWS)
        xs_part = h2.at[token_of_slot[lo:hi]].get(mode="promise_in_bounds")
        meta = jnp.stack([chunk_start[c], jnp.int32(lo), n_row_tiles]).astype(jnp.int32)
        ys = _experts_chunk(c, xs_part, ys, tables, meta, chunk_steps[c],
                            w_gate_up[0], b_gate_up[0], w_down[0], b_down[0])
    y4_lat = ys.at[slot[:, T_CTX:].reshape(-1)].get(mode="promise_in_bounds", unique_indices=True)
    y4_ctx = ys.at[slot[:, :T_CTX].reshape(-1)].get(mode="promise_in_bounds", unique_indices=True)
    y_lat = _combine(x1, y4_lat, route_p, mod3, final_g, T_CTX, T_LAT)
    y_ctx = _combine(x1, y4_ctx, route_p, mod3, final_g, 0, T_CTX)
    return (y_ctx.reshape(BATCH, SEQ, D_MODEL), y_lat.reshape(DEC_BATCH, DEC_SEQ, D_MODEL),
            new_state.reshape(BATCH, 1, 2, RET_HEADS, RET_DK, RET_DV))
```

```python
import functools

import numpy as np
import jax
import jax.numpy as jnp
from jax import lax
from jax.experimental import pallas as pl
from jax.experimental.pallas import tpu as pltpu

F32 = jnp.float32
BF16 = jnp.bfloat16

D_MODEL = 2048
BATCH = 16
SEQ = 256
DEC_BATCH = 4
DEC_SEQ = 2048
GRID_W = 64
RET_HEADS = 4
RET_DK = 128
RET_DV = 256
RET_QK_WIDTH = RET_HEADS * RET_DK
RET_WIDTH = RET_HEADS * RET_DV
POOL_WINDOWS = (2, 4, 8, 16)
POOL_GROUPS = 4
POOL_DG = 256
POOL_WIDTH = POOL_GROUPS * POOL_DG
IN_WIDTH = 2 * RET_QK_WIDTH + 2 * RET_WIDTH + POOL_WIDTH
N_EXPERTS = 32
TOP_K = 4
D_FF = D_MODEL
SWIGLU_LIMIT = 7.0
SWIGLU_ALPHA = 1.702
ROPE_BASE = 10000.0
NORM_EPS = 1e-6
GN_EPS = 1e-6

T_CTX = BATCH * SEQ
T_LAT = DEC_BATCH * DEC_SEQ
T_ALL = T_CTX + T_LAT
N_COND = 1 + DEC_BATCH
COND_ROWS = 8
LANES = 128

RET_CHUNK = 256

TM_IN = 512
TN_IN = 1024
TM_OUT = 256
TM_CMB = 256
H2_ROWS = 16384
POOL_TILE = 256
POOL_PAD = (max(POOL_WINDOWS) // 2) * GRID_W

TM_EXP = 256
F_EXP = 512
J_EXP = D_FF // F_EXP
SUP_EXP = 8
NT_EXP = (T_ALL * TOP_K + N_EXPERTS * (TM_EXP - 1)) // TM_EXP
U_EXP = N_EXPERTS + NT_EXP // SUP_EXP + 1
XS_ROWS = (NT_EXP + SUP_EXP) * TM_EXP
CH_START_EXP = tuple(t * TM_EXP for t in (0, 24, 120))
CHUNKS_EXP = len(CH_START_EXP)

VMEM_LIMIT = 56 * 1024 * 1024


def _cparams(sem, vmem=None):
    return pltpu.CompilerParams(dimension_semantics=sem, vmem_limit_bytes=vmem)


def _cond_row(i, tm):
    nctx = T_CTX // tm
    return jnp.where(i < nctx, 0, 1 + (i - nctx) // (DEC_SEQ // tm))


def _sigmoid(x):
    return 1.0 / (1.0 + jnp.exp(-x))


def _dot(a, b):
    return lax.dot_general(a, b, (((1,), (0,)), ((), ())), preferred_element_type=F32)


def _mod_kernel(c_ref, w_ref, b_ref, o_ref):
    c = c_ref[...]
    s = (c * _sigmoid(c)).astype(BF16)
    o_ref[...] = jnp.dot(s, w_ref[...].astype(BF16), preferred_element_type=F32) + b_ref[...]


def _modulation(cond, w_mod, b_mod):
    n = w_mod.shape[1]
    tn = 1024
    return pl.pallas_call(
        _mod_kernel,
        grid=(n // tn,),
        in_specs=[pl.BlockSpec((COND_ROWS, D_MODEL), lambda j: (0, 0)),
                  pl.BlockSpec((D_MODEL, tn), lambda j: (0, j)),
                  pl.BlockSpec((1, tn), lambda j: (0, j))],
        out_specs=pl.BlockSpec((COND_ROWS, tn), lambda j: (0, j)),
        out_shape=jax.ShapeDtypeStruct((COND_ROWS, n), F32),
        compiler_params=_cparams(("arbitrary",), 40 * 1024 * 1024),
        name="modulation",
    )(cond, w_mod, b_mod.reshape(1, n))


def _ctx_tile(i, tm):
    return jnp.minimum(i, T_CTX // tm - 1)


def _lat_tile(i, tm):
    return jnp.maximum(i - T_CTX // tm, 0)


def _inproj_kernel(xc_ref, xl_ref, g_ref, mod_ref, w_ref, o_ref, h_ref):
    def modulated_norm(x_ref):
        x = x_ref[...]
        y = x * lax.rsqrt(jnp.mean(x * x, axis=-1, keepdims=True) + NORM_EPS) * g_ref[...]
        shift = mod_ref[0, 0:1, :]
        scale = mod_ref[0, 1:2, :]
        h_ref[...] = (y * (1.0 + scale) + shift).astype(BF16)

    first = pl.program_id(1) == 0
    is_ctx = pl.program_id(0) < T_CTX // TM_IN
    pl.when(first & is_ctx)(lambda: modulated_norm(xc_ref))
    pl.when(first & jnp.logical_not(is_ctx))(lambda: modulated_norm(xl_ref))

    o_ref[...] = jnp.dot(h_ref[...], w_ref[...], preferred_element_type=F32)


def _in_projection(x_ctx, x_lat, norm_g, mod3, w_in_bf16):
    return pl.pallas_call(
        _inproj_kernel,
        grid=(T_ALL // TM_IN, IN_WIDTH // TN_IN),
        in_specs=[pl.BlockSpec((TM_IN, D_MODEL), lambda i, j: (_ctx_tile(i, TM_IN), 0)),
                  pl.BlockSpec((TM_IN, D_MODEL), lambda i, j: (_lat_tile(i, TM_IN), 0)),
                  pl.BlockSpec((1, D_MODEL), lambda i, j: (0, 0)),
                  pl.BlockSpec((1, 6, D_MODEL), lambda i, j: (_cond_row(i, TM_IN), 0, 0)),
                  pl.BlockSpec((D_MODEL, TN_IN), lambda i, j: (0, j))],
        out_specs=pl.BlockSpec((TM_IN, TN_IN), lambda i, j: (i, j)),
        out_shape=jax.ShapeDtypeStruct((T_ALL, IN_WIDTH), F32),
        scratch_shapes=[pltpu.VMEM((TM_IN, D_MODEL), BF16)],
        compiler_params=_cparams(("arbitrary", "arbitrary"), 40 * 1024 * 1024),
        name="in_projection",
    )(x_ctx, x_lat, norm_g.reshape(1, D_MODEL), mod3, w_in_bf16)


def _rope_tables(seq_len):
    t = jnp.arange(seq_len)
    row = (t // GRID_W).astype(F32)
    col = (t % GRID_W).astype(F32)
    half = RET_DK // 2
    n_freq = half // 2
    inv = ROPE_BASE ** (-jnp.arange(n_freq, dtype=F32) / n_freq)
    ang_r = row[:, None] * inv
    ang_c = col[:, None] * inv
    cos = jnp.concatenate([jnp.cos(ang_r), jnp.cos(ang_r), jnp.cos(ang_c), jnp.cos(ang_c)], axis=-1)
    sin = jnp.concatenate([-jnp.sin(ang_r), jnp.sin(ang_r), -jnp.sin(ang_c), jnp.sin(ang_c)], axis=-1)
    return cos, sin


def _retention_kernel(rd_ref, *refs, seq_len, chunk, use_rope, has_state_in, has_state_out, heads):
    refs = list(refs)
    q_ref, k_ref, v_ref, g_ref = refs[:4]
    pos = 4
    if has_state_in:
        s0_ref = refs[pos]
        pos += 1
    if use_rope:
        cos_ref, sin_ref = refs[pos], refs[pos + 1]
        pos += 2
    o_ref = refs[pos]
    pos += 1
    if has_state_out:
        st_ref = refs[pos]
        pos += 1
    scratch = refs[pos:pos + 5]

    for hh in range(heads):
        _retention_head(rd_ref, pl.program_id(1) * heads + hh, hh,
                        q_ref, k_ref, v_ref, g_ref,
                        s0_ref if has_state_in else None,
                        (cos_ref, sin_ref) if use_rope else None,
                        o_ref, st_ref if has_state_out else None,
                        [r.at[hh] for r in scratch], seq_len, chunk)


def _retention_head(rd_ref, h, hh, q_ref, k_ref, v_ref, g_ref, s0_ref, rope_refs, o_ref, st_ref, scratch,
                    seq_len, chunk):
    qs_ref, ks_ref, acc_ref, sf_ref, sb_ref = scratch
    use_rope = rope_refs is not None
    kcols = slice(hh * RET_DK, (hh + 1) * RET_DK)
    vcols = slice(hh * RET_DV, (hh + 1) * RET_DV)
    C = chunk
    n_chunks = seq_len // C

    lgf = -jnp.exp(jnp.full((C, 1), rd_ref[0, h], F32))
    lgb = -jnp.exp(jnp.full((C, 1), rd_ref[1, h], F32))
    ii = lax.broadcasted_iota(jnp.int32, (C, C), 0)
    jj = lax.broadcasted_iota(jnp.int32, (C, C), 1)
    diff = (ii - jj).astype(F32)
    decay = (jnp.where(diff >= 0, jnp.exp(lgf * jnp.maximum(diff, 0.0)), 0.0)
             + jnp.where(diff <= 0, jnp.exp(lgb * jnp.maximum(-diff, 0.0)), 0.0))
    p = lax.broadcasted_iota(jnp.int32, (C, 1), 0).astype(F32)
    xi_f = jnp.exp(lgf * (p + 1.0))
    zeta_f = jnp.exp(lgf * (C - 1.0 - p))
    xi_b = jnp.exp(lgb * (C - p))
    zeta_b = jnp.exp(lgb * p)
    cd_f = jnp.exp(lgf[0:1, :] * C)
    cd_b = jnp.exp(lgb[0:1, :] * C)

    if s0_ref is not None:
        sf_ref[...] = s0_ref[0, 0, hh]
        sb_ref[...] = s0_ref[0, 1, hh]
    else:
        sf_ref[...] = jnp.zeros_like(sf_ref)
        sb_ref[...] = jnp.zeros_like(sb_ref)

    lane = lax.broadcasted_iota(jnp.int32, (C, RET_DK), 1)
    first_half = (lane & 32) == 0

    def rope(x, cs, sn):
        swapped = jnp.where(first_half, pltpu.roll(x, RET_DK - 32, axis=1), pltpu.roll(x, 32, axis=1))
        return x * cs + swapped * sn

    def state_update(s_ref, kz, v, cd):
        upd = lax.dot_general(kz, v, (((0,), (0,)), ((), ())), preferred_element_type=F32)
        s_ref[...] = cd * s_ref[...] + upd

    def chunk_rows(c):
        start = c * C
        return pl.ds(start if isinstance(start, int) else pl.multiple_of(start, C), C)

    def fwd_chunk(c):
        sl = chunk_rows(c)
        q = q_ref[sl, kcols]
        k = k_ref[sl, kcols] * (RET_DK ** -0.5)
        if use_rope:
            cs = rope_refs[0][sl, :]
            sn = rope_refs[1][sl, :]
            q = rope(q, cs, sn)
            k = rope(k, cs, sn)
        qb = q.astype(BF16)
        kb = k.astype(BF16)
        qs_ref[sl, :] = qb
        ks_ref[sl, :] = k
        v = v_ref[sl, vcols].astype(BF16)
        scores = lax.dot_general(qb, kb, (((1,), (1,)), ((), ())), preferred_element_type=F32)
        inner = jnp.dot((scores * decay).astype(BF16), v, preferred_element_type=F32)
        cross = jnp.dot(qb, sf_ref[...].astype(BF16), preferred_element_type=F32) * xi_f
        acc_ref[sl, :] = inner + cross
        state_update(sf_ref, (k * zeta_f).astype(BF16), v, cd_f)

    def bwd_chunk(c):
        sl = chunk_rows(c)
        qb = qs_ref[sl, :]
        k = ks_ref[sl, :]
        v = v_ref[sl, vcols].astype(BF16)
        cross = jnp.dot(qb, sb_ref[...].astype(BF16), preferred_element_type=F32) * xi_b
        o = acc_ref[sl, :] + cross
        mu = jnp.mean(o, axis=-1, keepdims=True)
        oc = o - mu
        var = jnp.mean(oc * oc, axis=-1, keepdims=True)
        on = oc * lax.rsqrt(var + GN_EPS)
        g = g_ref[sl, vcols]
        o_ref[sl, vcols] = (on * (g * _sigmoid(g))).astype(BF16)
        state_update(sb_ref, (k * zeta_b).astype(BF16), v, cd_b)

    if n_chunks == 1:
        fwd_chunk(0)
        bwd_chunk(0)
    else:
        assert n_chunks % 2 == 0

        def fwd_pair(i, carry):
            fwd_chunk(2 * i)
            fwd_chunk(2 * i + 1)
            return carry

        def bwd_pair(i, carry):
            bwd_chunk(n_chunks - 1 - 2 * i)
            bwd_chunk(n_chunks - 2 - 2 * i)
            return carry

        lax.fori_loop(0, n_chunks // 2, fwd_pair, 0)
        lax.fori_loop(0, n_chunks // 2, bwd_pair, 0)

    if st_ref is not None:
        st_ref[0, 0, hh] = sf_ref[...]
        st_ref[0, 1, hh] = sb_ref[...]


def _retention(proj, row0, B, L, ret_decay, state_in, rope_tabs, want_state, heads):
    rb = row0 // L
    use_rope = rope_tabs is not None
    has_state_in = state_in is not None
    wk = heads * RET_DK
    wv = heads * RET_DV
    kq = RET_QK_WIDTH // wk
    kv = 2 * RET_QK_WIDTH // wv
    kg = kv + RET_WIDTH // wv
    in_specs = [pl.BlockSpec((L, wk), lambda b, h, rd: (rb + b, h)),
                pl.BlockSpec((L, wk), lambda b, h, rd: (rb + b, kq + h)),
                pl.BlockSpec((L, wv), lambda b, h, rd: (rb + b, kv + h)),
                pl.BlockSpec((L, wv), lambda b, h, rd: (rb + b, kg + h))]
    args = [proj, proj, proj, proj]
    state_spec = pl.BlockSpec((1, 2, heads, RET_DK, RET_DV), lambda b, h, rd: (b, 0, h, 0, 0))
    if has_state_in:
        in_specs.append(state_spec)
        args.append(state_in)
    if use_rope:
        in_specs += [pl.BlockSpec((L, RET_DK), lambda b, h, rd: (0, 0))] * 2
        args += list(rope_tabs)
    out_specs = [pl.BlockSpec((L, wv), lambda b, h, rd: (b, h))]
    out_shape = [jax.ShapeDtypeStruct((B * L, RET_WIDTH), BF16)]
    if want_state:
        out_specs.append(state_spec)
        out_shape.append(jax.ShapeDtypeStruct((B, 2, RET_HEADS, RET_DK, RET_DV), F32))
    kern = functools.partial(_retention_kernel, seq_len=L, chunk=RET_CHUNK, use_rope=use_rope,
                             has_state_in=has_state_in, has_state_out=want_state, heads=heads)
    res = pl.pallas_call(
        kern,
        grid_spec=pltpu.PrefetchScalarGridSpec(
            num_scalar_prefetch=1,
            grid=(B, RET_HEADS // heads),
            in_specs=in_specs,
            out_specs=out_specs,
            scratch_shapes=[pltpu.VMEM((heads, L, RET_DK), BF16),
                            pltpu.VMEM((heads, L, RET_DK), F32),
                            pltpu.VMEM((heads, L, RET_DV), F32),
                            pltpu.VMEM((heads, RET_DK, RET_DV), F32),
                            pltpu.VMEM((heads, RET_DK, RET_DV), F32)]),
        out_shape=out_shape,
        compiler_params=_cparams(("arbitrary", "arbitrary"), 40 * 1024 * 1024),
        name="retention_grid" if use_rope else "retention_seq",
    )(ret_decay, *args)
    return res if want_state else (res[0], None)


def _split3(x):
    hi = x.astype(BF16)
    r1 = x - hi.astype(F32)
    mid = r1.astype(BF16)
    lo = (r1 - mid.astype(F32)).astype(BF16)
    return jnp.concatenate([hi, mid, lo], axis=-1)


def _pool_kernel(p_ref, w_ref, sc_ref, o_ref, pad_ref, *, seq_len, grid_mode):
    PT = POOL_TILE
    n_tiles = seq_len // PT
    ii = lax.broadcasted_iota(jnp.int32, (PT, PT), 0)
    jj = lax.broadcasted_iota(jnp.int32, (PT, PT), 1)
    d = jj - ii
    ti = lax.broadcasted_iota(jnp.int32, (PT, 1), 0)

    if grid_mode:
        zeros = jnp.zeros((POOL_PAD, POOL_DG), F32)
        pad_ref[0:POOL_PAD, :] = zeros
        pad_ref[POOL_PAD + seq_len:POOL_PAD + seq_len + POOL_PAD, :] = zeros

    for gi, w in enumerate(POOL_WINDOWS):
        lo_off = -(w // 2)
        hi_off = w - w // 2 - 1
        cols = slice(gi * POOL_DG, (gi + 1) * POOL_DG)
        in_window = (d >= lo_off) & (d <= hi_off)
        if grid_mode:
            in_window = in_window & ((ii // GRID_W) == (jj // GRID_W))
        band = jnp.where(in_window, 1.0, 0.0).astype(BF16)
        wg = w_ref[gi]
        scale = sc_ref[gi]

        def window_sum(t0):
            x = p_ref[pl.ds(t0, PT), cols]
            s3 = jnp.dot(band, _split3(x), preferred_element_type=F32)
            return s3[:, 0:POOL_DG] + s3[:, POOL_DG:2 * POOL_DG] + s3[:, 2 * POOL_DG:3 * POOL_DG]

        def finish(t0, s, cnt):
            x = p_ref[pl.ds(t0, PT), cols]
            diff = (s / cnt - x).astype(BF16)
            y = jnp.dot(diff, wg, preferred_element_type=F32) * scale
            o_ref[pl.ds(t0, PT), cols] = y.astype(BF16)

        def count(pos, n):
            lo = jnp.maximum(pos + lo_off, 0)
            hi = jnp.minimum(pos + lo_off + w, n)
            return (hi - lo).astype(F32)

        if grid_mode:
            for t in range(n_tiles):
                pad_ref[POOL_PAD + t * PT:POOL_PAD + (t + 1) * PT, :] = window_sum(t * PT)
            for t in range(n_tiles):
                base = POOL_PAD + t * PT
                s = pad_ref[base + lo_off * GRID_W:base + lo_off * GRID_W + PT, :]
                for r in range(lo_off + 1, hi_off + 1):
                    s = s + pad_ref[base + r * GRID_W:base + r * GRID_W + PT, :]
                tok = ti + t * PT
                cnt = count(tok // GRID_W, seq_len // GRID_W) * count(tok % GRID_W, GRID_W)
                finish(t * PT, s, cnt)
        else:
            for t in range(n_tiles):
                finish(t * PT, window_sum(t * PT), count(ti + t * PT, seq_len))


def _pooling(proj, row0, B, L, w_pool_bf16, pool_scale, grid_mode):
    if not grid_mode:
        assert L == POOL_TILE
    rb = row0 // L
    pcol = (IN_WIDTH - POOL_WIDTH) // POOL_WIDTH
    kern = functools.partial(_pool_kernel, seq_len=L, grid_mode=grid_mode)
    return pl.pallas_call(
        kern,
        grid=(B,),
        in_specs=[pl.BlockSpec((L, POOL_WIDTH), lambda b: (rb + b, pcol)),
                  pl.BlockSpec((POOL_GROUPS, POOL_DG, POOL_DG), lambda b: (0, 0, 0)),
                  pl.BlockSpec((POOL_GROUPS, 1, POOL_DG), lambda b: (0, 0, 0))],
        out_specs=pl.BlockSpec((L, POOL_WIDTH), lambda b: (b, 0)),
        out_shape=jax.ShapeDtypeStruct((B * L, POOL_WIDTH), BF16),
        scratch_shapes=[pltpu.VMEM((L + 2 * POOL_PAD, POOL_DG), F32)],
        compiler_params=_cparams(("arbitrary",), 48 * 1024 * 1024),
        name="pool_grid" if grid_mode else "pool_seq",
    )(proj, w_pool_bf16, pool_scale.reshape(POOL_GROUPS, 1, POOL_DG))


def _outproj_kernel(*refs):
    i = pl.program_id(0)
    h2_ref = refs[14]

    @pl.when(i >= T_ALL // TM_OUT)
    def _():
        h2_ref[...] = jnp.zeros_like(h2_ref)

    pl.when(i < T_ALL // TM_OUT)(lambda: _outproj_tile(*refs))


def _outproj_tile(retc_ref, retl_ref, poolc_ref, pooll_ref, xc_ref, xl_ref, mod_ref, g_ref,
                  wt_ref, wb_ref, wrh_ref, wrb_ref, br_ref,
                  x1_ref, h2_ref, ri_ref, rp_ref, rr_ref, cnt_ref, run_ref):
    i = pl.program_id(0)

    @pl.when(i == 0)
    def _():
        run_ref[...] = jnp.zeros_like(run_ref)

    is_ctx = i < T_CTX // TM_OUT
    ret = jnp.where(is_ctx, retc_ref[...], retl_ref[...])
    pool = jnp.where(is_ctx, poolc_ref[...], pooll_ref[...])
    x = jnp.where(is_ctx, xc_ref[...], xl_ref[...])
    y = (jnp.dot(ret, wt_ref[...], preferred_element_type=F32)
         + jnp.dot(pool, wb_ref[...], preferred_element_type=F32))
    x1 = x + mod_ref[0, 2:3, :] * y
    x1_ref[...] = x1
    hn = x1 * lax.rsqrt(jnp.mean(x1 * x1, axis=-1, keepdims=True) + NORM_EPS) * g_ref[...]
    h2 = hn * (1.0 + mod_ref[0, 4:5, :]) + mod_ref[0, 3:4, :]
    hi = h2.astype(BF16)
    h2_ref[...] = hi
    lo = (h2 - hi.astype(F32)).astype(BF16)
    both = jnp.dot(hi, wrb_ref[...], preferred_element_type=F32)
    logits = (both[:, 0:LANES] + jnp.dot(lo, wrh_ref[...], preferred_element_type=F32)
              + both[:, LANES:2 * LANES]) + br_ref[...]

    tm = logits.shape[0]
    lane = lax.broadcasted_iota(jnp.int32, (tm, LANES), 1)
    work = logits
    vals, idxs, hots = [], [], []
    for _ in range(TOP_K):
        m = jnp.max(work, axis=-1, keepdims=True)
        idx = jnp.min(jnp.where(work == m, lane, LANES), axis=-1, keepdims=True)
        hot = lane == idx
        vals.append(m)
        idxs.append(idx)
        hots.append(hot)
        work = jnp.where(hot, -jnp.inf, work)
    exps = [jnp.exp(v - vals[0]) for v in vals]
    denom = exps[0] + exps[1] + exps[2] + exps[3]

    selected = jnp.zeros((tm, LANES), F32)
    for hot in hots:
        selected = selected + jnp.where(hot, 1.0, 0.0)
    r_i = lax.broadcasted_iota(jnp.int32, (tm, tm), 0)
    c_i = lax.broadcasted_iota(jnp.int32, (tm, tm), 1)
    tri = jnp.where(c_i < r_i, 1.0, 0.0).astype(BF16)
    before = jnp.dot(tri, selected.astype(BF16), preferred_element_type=F32) + run_ref[0:1, :]

    ri = jnp.zeros((tm, LANES), jnp.int32)
    rp = jnp.zeros((tm, LANES), F32)
    rr = jnp.zeros((tm, LANES), jnp.int32)
    for k in range(TOP_K):
        rank = jnp.sum(jnp.where(hots[k], before, 0.0), axis=-1, keepdims=True).astype(jnp.int32)
        ri = jnp.where(lane == k, idxs[k], ri)
        rp = jnp.where(lane == k, exps[k] / denom, rp)
        rr = jnp.where(lane == k, rank, rr)
    ri_ref[...] = ri
    rp_ref[...] = rp
    rr_ref[...] = rr

    run = run_ref[0:1, :] + jnp.sum(selected, axis=0, keepdims=True)
    run_ref[...] = jnp.broadcast_to(run, run_ref.shape)
    cnt_ref[...] = jnp.broadcast_to(run, cnt_ref.shape).astype(jnp.int32)


def _out_projection(ret_ctx, ret_lat, pool_ctx, pool_lat, x_ctx, x_lat, mod3, norm_g, w_out_bf16,
                    wr_hi, wr_both, b_router_pad):
    tm = TM_OUT
    real = lambda i: jnp.minimum(i, T_ALL // tm - 1)
    row = lambda i: (real(i), 0)
    const = lambda i: (0, 0)
    crow = lambda i: (_ctx_tile(real(i), tm), 0)
    lrow = lambda i: (_lat_tile(real(i), tm), 0)
    return pl.pallas_call(
        _outproj_kernel,
        grid=(H2_ROWS // tm,),
        in_specs=[pl.BlockSpec((tm, RET_WIDTH), crow),
                  pl.BlockSpec((tm, RET_WIDTH), lrow),
                  pl.BlockSpec((tm, POOL_WIDTH), crow),
                  pl.BlockSpec((tm, POOL_WIDTH), lrow),
                  pl.BlockSpec((tm, D_MODEL), crow),
                  pl.BlockSpec((tm, D_MODEL), lrow),
                  pl.BlockSpec((1, 6, D_MODEL), lambda i: (_cond_row(real(i), tm), 0, 0)),
                  pl.BlockSpec((1, D_MODEL), const),
                  pl.BlockSpec((RET_WIDTH, D_MODEL), const),
                  pl.BlockSpec((POOL_WIDTH, D_MODEL), lambda i: (1, 0)),
                  pl.BlockSpec((D_MODEL, LANES), const),
                  pl.BlockSpec((D_MODEL, 2 * LANES), const),
                  pl.BlockSpec((1, LANES), const)],
        out_specs=[pl.BlockSpec((tm, D_MODEL), row),
                   pl.BlockSpec((tm, D_MODEL), lambda i: (i, 0)),
                   pl.BlockSpec((tm, LANES), row),
                   pl.BlockSpec((tm, LANES), row),
                   pl.BlockSpec((tm, LANES), row),
                   pl.BlockSpec((8, LANES), const)],
        out_shape=[jax.ShapeDtypeStruct((T_ALL, D_MODEL), F32),
                   jax.ShapeDtypeStruct((H2_ROWS, D_MODEL), BF16),
                   jax.ShapeDtypeStruct((T_ALL, LANES), jnp.int32),
                   jax.ShapeDtypeStruct((T_ALL, LANES), F32),
                   jax.ShapeDtypeStruct((T_ALL, LANES), jnp.int32),
                   jax.ShapeDtypeStruct((8, LANES), jnp.int32)],
        scratch_shapes=[pltpu.VMEM((8, LANES), F32)],
        compiler_params=_cparams(("arbitrary",), 48 * 1024 * 1024),
        name="out_projection_router",
    )(ret_ctx, ret_lat, pool_ctx, pool_lat, x_ctx, x_lat, mod3, norm_g.reshape(1, D_MODEL),
      w_out_bf16, w_out_bf16, wr_hi, wr_both, b_router_pad)


def _count_le(ends, v):
    return jnp.sum((ends[None, :] <= v[:, None]).astype(jnp.int32), axis=1)


def _expert_schedule(counts):
    i32 = jnp.int32
    ntile = (counts + TM_EXP - 1) // TM_EXP
    tile_off = jnp.cumsum(ntile) - ntile
    nsup = (ntile + SUP_EXP - 1) // SUP_EXP
    sup_end = jnp.cumsum(nsup)
    u = jnp.arange(U_EXP, dtype=i32)
    sup_valid = u < sup_end[-1]
    uc = jnp.minimum(u, sup_end[-1] - 1)
    sup_e = jnp.minimum(_count_le(sup_end, uc), N_EXPERTS - 1)
    sup_idx = uc - (sup_end - nsup)[sup_e]
    per_sup = (ntile + jnp.maximum(nsup, 1) - 1) // jnp.maximum(nsup, 1)
    sup_row0 = (tile_off[sup_e] + sup_idx * per_sup[sup_e]) * TM_EXP
    sup_nt = jnp.clip(ntile[sup_e] - sup_idx * per_sup[sup_e], 1, per_sup[sup_e])
    step_e = jnp.repeat(sup_e, J_EXP)
    step_j = jnp.tile(jnp.arange(J_EXP, dtype=i32), U_EXP)
    step_row0 = jnp.repeat(sup_row0, J_EXP)
    step_nt = jnp.repeat(sup_nt, J_EXP)
    last_rows = counts - (ntile - 1) * TM_EXP
    sup_half = (sup_idx == nsup[sup_e] - 1) & (last_rows[sup_e] <= TM_EXP // 2) & (counts[sup_e] > 0)
    step_half = jnp.repeat(sup_half.astype(i32), J_EXP)
    bounds = jnp.asarray(CH_START_EXP[1:], dtype=i32)
    before = jnp.sum((sup_valid[None, :] & (sup_row0[None, :] < bounds[:, None])).astype(i32), axis=1) * J_EXP
    n_steps = jnp.sum(sup_valid.astype(i32)) * J_EXP
    chunk_start = jnp.concatenate([jnp.zeros((1,), i32), before])
    chunk_steps = jnp.concatenate([before, n_steps.reshape(1)]) - chunk_start
    tables = (step_e.astype(i32), step_j.astype(i32), step_row0.astype(i32), step_nt.astype(i32), step_half)
    return tile_off * TM_EXP, tables, chunk_start, chunk_steps, jnp.sum(ntile).astype(i32)


def _expert_kernel(se_ref, sj_ref, srow_ref, snt_ref, shalf_ref, meta_ref,
                   xs_hbm, wg_ref, wu_ref, bg_ref, bu_ref, wd_ref, bd_ref, *rest, first_chunk):
    ys_hbm, acc_ref, x_buf, o_buf, x_sem, o_sem = rest if first_chunk else rest[1:]
    s = pl.program_id(0) + meta_ref[0]
    j = sj_ref[s]
    n_tiles = snt_ref[s]
    row0 = pl.multiple_of(srow_ref[s], TM_EXP)
    last = J_EXP - 1

    def x_copy():
        rows = pl.ds(pl.multiple_of(row0 - meta_ref[1], TM_EXP), SUP_EXP * TM_EXP)
        return pltpu.make_async_copy(xs_hbm.at[rows], x_buf, x_sem)

    def out_copy(t):
        rows = pl.ds(pl.multiple_of(row0 + t * TM_EXP, TM_EXP), TM_EXP)
        return pltpu.make_async_copy(o_buf.at[t % 2], ys_hbm.at[rows], o_sem.at[t % 2])

    @pl.when((pl.program_id(0) == 0) & first_chunk)
    def _():
        o_buf[0] = jnp.zeros((TM_EXP, D_MODEL), BF16)

        def zero_copy(t):
            rows = pl.ds(pl.multiple_of(t * TM_EXP, TM_EXP), TM_EXP)
            return pltpu.make_async_copy(o_buf.at[0], ys_hbm.at[rows], o_sem.at[0])

        def start(t, carry):
            zero_copy(t).start()
            return carry

        def wait(t, carry):
            zero_copy(t).wait()
            return carry

        lax.fori_loop(meta_ref[2], NT_EXP, start, 0)
        lax.fori_loop(meta_ref[2], NT_EXP, wait, 0)

    def step():
        @pl.when(j == 0)
        def _():
            x_copy().start()
            acc_ref[...] = jnp.zeros_like(acc_ref)
            x_copy().wait()

        def accumulate(rows):
            x = x_buf[rows, :]
            gate = _dot(x, wg_ref[0]) + bg_ref[0]
            up = _dot(x, wu_ref[0]) + bu_ref[0]
            gate = jnp.minimum(gate, SWIGLU_LIMIT)
            up = jnp.clip(up, -SWIGLU_LIMIT, SWIGLU_LIMIT)
            act = (up + 1.0) * gate * _sigmoid(SWIGLU_ALPHA * gate)
            acc_ref[rows, :] += _dot(act, wd_ref[0])

        def tile_pair(p, carry):
            base = pl.multiple_of(p * (2 * TM_EXP), 2 * TM_EXP)
            accumulate(pl.ds(base, TM_EXP))
            accumulate(pl.ds(base + TM_EXP, TM_EXP))
            return carry

        n_full = n_tiles - shalf_ref[s]
        lax.fori_loop(0, n_full // 2, tile_pair, 0)

        @pl.when(n_full % 2 == 1)
        def _():
            accumulate(pl.ds(pl.multiple_of((n_full - 1) * TM_EXP, TM_EXP), TM_EXP))

        @pl.when(shalf_ref[s] == 1)
        def _():
            accumulate(pl.ds(pl.multiple_of((n_tiles - 1) * TM_EXP, TM_EXP), TM_EXP // 2))

        @pl.when(j == last)
        def _():
            def finish(t, carry):
                @pl.when(t >= 2)
                def _():
                    out_copy(t - 2).wait()

                rows = pl.ds(pl.multiple_of(t * TM_EXP, TM_EXP), TM_EXP)
                o_buf[t % 2] = (acc_ref[rows, :] + bd_ref[0]).astype(BF16)
                out_copy(t).start()
                return carry

            lax.fori_loop(0, n_tiles, finish, 0)

            @pl.when(n_tiles >= 2)
            def _():
                out_copy(n_tiles - 2).wait()

            out_copy(n_tiles - 1).wait()

    step()


def _experts_chunk(chunk, xs_part, ys, tables, meta, n_steps, w_gate_up, b_gate_up, w_down, b_down):
    first = ys is None
    wmap = lambda col0: (lambda s, se, sj, srow, snt, shalf, meta: (se[s + meta[0]], 0, col0 + sj[s + meta[0]]))
    n_prefetch = len(tables) + 1
    in_specs = [
        pl.BlockSpec(memory_space=pl.ANY),
        pl.BlockSpec((1, D_MODEL, F_EXP), wmap(0)),
        pl.BlockSpec((1, D_MODEL, F_EXP), wmap(J_EXP)),
        pl.BlockSpec((1, 1, F_EXP), wmap(0)),
        pl.BlockSpec((1, 1, F_EXP), wmap(J_EXP)),
        pl.BlockSpec((1, F_EXP, D_MODEL),
                     lambda s, se, sj, srow, snt, shalf, meta: (se[s + meta[0]], sj[s + meta[0]], 0)),
        pl.BlockSpec((1, 1, D_MODEL), lambda s, se, sj, srow, snt, shalf, meta: (se[s + meta[0]], 0, 0)),
    ]
    args = [xs_part, w_gate_up, w_gate_up,
            b_gate_up.reshape(N_EXPERTS, 1, 2 * D_FF), b_gate_up.reshape(N_EXPERTS, 1, 2 * D_FF),
            w_down, b_down.reshape(N_EXPERTS, 1, D_MODEL)]
    aliases = {}
    if not first:
        in_specs.append(pl.BlockSpec(memory_space=pl.ANY))
        args.append(ys)
        aliases = {n_prefetch + len(args) - 1: 0}
    return pl.pallas_call(
        functools.partial(_expert_kernel, first_chunk=first),
        grid_spec=pltpu.PrefetchScalarGridSpec(
            num_scalar_prefetch=n_prefetch,
            grid=(n_steps,),
            in_specs=in_specs,
            out_specs=pl.BlockSpec(memory_space=pl.ANY),
            scratch_shapes=[pltpu.VMEM((SUP_EXP * TM_EXP, D_MODEL), F32),
                            pltpu.VMEM((SUP_EXP * TM_EXP, D_MODEL), BF16),
                            pltpu.VMEM((2, TM_EXP, D_MODEL), BF16),
                            pltpu.SemaphoreType.DMA(()),
                            pltpu.SemaphoreType.DMA((2,))]),
        out_shape=jax.ShapeDtypeStruct((NT_EXP * TM_EXP, D_MODEL), BF16),
        input_output_aliases=aliases,
        compiler_params=_cparams(("arbitrary",), VMEM_LIMIT),
        name=f"experts_chunk{chunk}",
    )(*tables, meta, *args)


def _combine_kernel(x1_ref, y0_ref, y1_ref, y2_ref, y3_ref, rp_ref, mod_ref, g_ref, o_ref):
    rp = rp_ref[...]
    moe = rp[:, 0:1] * y0_ref[...].astype(F32)
    for k, y_ref in enumerate((y1_ref, y2_ref, y3_ref), start=1):
        moe = moe + rp[:, k:k + 1] * y_ref[...].astype(F32)
    x2 = x1_ref[...] + mod_ref[0, 5:6, :] * moe
    o_ref[...] = x2 * lax.rsqrt(jnp.mean(x2 * x2, axis=-1, keepdims=True) + NORM_EPS) * g_ref[...]


def _combine(x1, y4, route_p, mod3, final_g, row0, n_rows):
    tm = TM_CMB
    nt = n_rows // tm
    t0 = row0 // tm
    y_specs = [pl.BlockSpec((tm, D_MODEL), functools.partial(lambda k, i: (k * nt + i, 0), k))
               for k in range(TOP_K)]
    return pl.pallas_call(
        _combine_kernel,
        grid=(nt,),
        in_specs=[pl.BlockSpec((tm, D_MODEL), lambda i: (t0 + i, 0))] + y_specs + [
            pl.BlockSpec((tm, LANES), lambda i: (t0 + i, 0)),
            pl.BlockSpec((1, 6, D_MODEL), lambda i: (_cond_row(t0 + i, tm), 0, 0)),
            pl.BlockSpec((1, D_MODEL), lambda i: (0, 0))],
        out_specs=pl.BlockSpec((tm, D_MODEL), lambda i: (i, 0)),
        out_shape=jax.ShapeDtypeStruct((n_rows, D_MODEL), F32),
        compiler_params=_cparams(("arbitrary",), 40 * 1024 * 1024),
        name="combine_final_norm",
    )(x1, y4, y4, y4, y4, route_p, mod3, final_g.reshape(1, D_MODEL))


def kernel(x_prompt, x_sample, c, c_ctx, state_ret, norm1_g, norm2_g, w_mod, b_mod, w_in, ret_decay,
           w_pool, pool_scale, w_out, w_router, b_router, w_gate_up, b_gate_up, w_down, b_down, final_g):
    assert w_mod.shape[0] == 1, "single trunk layer"
    x_ctx = x_prompt.reshape(T_CTX, D_MODEL)
    x_lat = x_sample.reshape(T_LAT, D_MODEL)
    cond = jnp.zeros((COND_ROWS, D_MODEL), F32).at[0].set(c_ctx).at[1:N_COND].set(c)

    mod = _modulation(cond, w_mod[0], b_mod[0])
    mod3 = mod.reshape(COND_ROWS, 6, D_MODEL)

    proj = _in_projection(x_ctx, x_lat, norm1_g[0], mod3, w_in[0].astype(BF16))

    ret_ctx, new_state = _retention(proj, 0, BATCH, SEQ, ret_decay[0], None, None, True, RET_HEADS)
    ret_lat, _ = _retention(proj, T_CTX, DEC_BATCH, DEC_SEQ, ret_decay[0], state_ret[:, 0],
                            _rope_tables(DEC_SEQ), False, 1)
    w_pool_bf16 = w_pool[0].astype(BF16)
    pool_ctx = _pooling(proj, 0, BATCH, SEQ, w_pool_bf16, pool_scale[0], False)
    pool_lat = _pooling(proj, T_CTX, DEC_BATCH, DEC_SEQ, w_pool_bf16, pool_scale[0], True)

    wr = jnp.zeros((D_MODEL, LANES), F32).at[:, :N_EXPERTS].set(w_router[0])
    wr_hi = wr.astype(BF16)
    wr_both = jnp.concatenate([wr_hi, (wr - wr_hi.astype(F32)).astype(BF16)], axis=1)
    br = jnp.full((1, LANES), -1e30, F32).at[0, :N_EXPERTS].set(b_router[0])
    x1, h2, route_i, route_p, route_r, counts = _out_projection(
        ret_ctx, ret_lat, pool_ctx, pool_lat, x_ctx, x_lat, mod3, norm2_g[0], w_out[0].astype(BF16),
        wr_hi, wr_both, br)

    row_off, tables, chunk_start, chunk_steps, n_row_tiles = _expert_schedule(counts[0, :N_EXPERTS])
    slot = (row_off[route_i[:, :TOP_K]] + route_r[:, :TOP_K]).T
    token = jnp.broadcast_to(jnp.arange(T_ALL, dtype=jnp.int32)[None, :], (TOP_K, T_ALL))
    token_of_slot = (jnp.arange(XS_ROWS, dtype=jnp.int32) % T_ALL).at[slot.reshape(-1)].set(
        token.reshape(-1), unique_indices=True, mode="promise_in_bounds")
    ys = None
    for c in range(CHUNKS_EXP):
        lo = CH_START_EXP[c]
        nxt = CH_START_EXP[c + 1] if c + 1 < CHUNKS_EXP else XS_ROWS
        hi = min(nxt + SUP_EXP * TM_EXP, XS_ROWS)
        xs_part = h2.at[token_of_slot[lo:hi]].get(mode="promise_in_bounds")
        meta = jnp.stack([chunk_start[c], jnp.int32(lo), n_row_tiles]).astype(jnp.int32)
        ys = _experts_chunk(c, xs_part, ys, tables, meta, chunk_steps[c],
                            w_gate_up[0], b_gate_up[0], w_down[0], b_down[0])
    y4_lat = ys.at[slot[:, T_CTX:].reshape(-1)].get(mode="promise_in_bounds", unique_indices=True)
    y4_ctx = ys.at[slot[:, :T_CTX].reshape(-1)].get(mode="promise_in_bounds", unique_indices=True)
    y_lat = _combine(x1, y4_lat, route_p, mod3, final_g, T_CTX, T_LAT)
    y_ctx = _combine(x1, y4_ctx, route_p, mod3, final_g, 0, T_CTX)
    return (y_ctx.reshape(BATCH, SEQ, D_MODEL), y_lat.reshape(DEC_BATCH, DEC_SEQ, D_MODEL),
            new_state.reshape(BATCH, 1, 2, RET_HEADS, RET_DK, RET_DV))
```

```python
import functools

import numpy as np
import jax
import jax.numpy as jnp
from jax import lax
from jax.experimental import pallas as pl
from jax.experimental.pallas import tpu as pltpu

F32 = jnp.float32
BF16 = jnp.bfloat16

D_MODEL = 2048
BATCH = 16
SEQ = 256
DEC_BATCH = 4
DEC_SEQ = 2048
GRID_W = 64
RET_HEADS = 4
RET_DK = 128
RET_DV = 256
RET_QK_WIDTH = RET_HEADS * RET_DK
RET_WIDTH = RET_HEADS * RET_DV
POOL_WINDOWS = (2, 4, 8, 16)
POOL_GROUPS = 4
POOL_DG = 256
POOL_WIDTH = POOL_GROUPS * POOL_DG
IN_WIDTH = 2 * RET_QK_WIDTH + 2 * RET_WIDTH + POOL_WIDTH
N_EXPERTS = 32
TOP_K = 4
D_FF = D_MODEL
SWIGLU_LIMIT = 7.0
SWIGLU_ALPHA = 1.702
ROPE_BASE = 10000.0
NORM_EPS = 1e-6
GN_EPS = 1e-6

T_CTX = BATCH * SEQ
T_LAT = DEC_BATCH * DEC_SEQ
T_ALL = T_CTX + T_LAT
N_COND = 1 + DEC_BATCH
COND_ROWS = 8
LANES = 128

RET_CHUNK = 256

TM_IN = 512
TN_IN = 1024
TM_OUT = 256
TM_CMB = 256
H2_ROWS = 16384
POOL_TILE = 256
POOL_PAD = (max(POOL_WINDOWS) // 2) * GRID_W

TM_EXP = 256
F_EXP = 512
J_EXP = D_FF // F_EXP
SUP_EXP = 8
NT_EXP = (T_ALL * TOP_K + N_EXPERTS * (TM_EXP - 1)) // TM_EXP
U_EXP = N_EXPERTS + NT_EXP // SUP_EXP + 1
XS_ROWS = (NT_EXP + SUP_EXP) * TM_EXP
CH_START_EXP = tuple(t * TM_EXP for t in (0, 24))
CHUNKS_EXP = len(CH_START_EXP)

VMEM_LIMIT = 56 * 1024 * 1024


def _cparams(sem, vmem=None):
    return pltpu.CompilerParams(dimension_semantics=sem, vmem_limit_bytes=vmem)


def _cond_row(i, tm):
    nctx = T_CTX // tm
    return jnp.where(i < nctx, 0, 1 + (i - nctx) // (DEC_SEQ // tm))


def _sigmoid(x):
    return 1.0 / (1.0 + jnp.exp(-x))


def _dot(a, b):
    return lax.dot_general(a, b, (((1,), (0,)), ((), ())), preferred_element_type=F32)


def _mod_kernel(c_ref, w_ref, b_ref, o_ref):
    c = c_ref[...]
    s = (c * _sigmoid(c)).astype(BF16)
    o_ref[...] = jnp.dot(s, w_ref[...].astype(BF16), preferred_element_type=F32) + b_ref[...]


def _modulation(cond, w_mod, b_mod):
    n = w_mod.shape[1]
    tn = 1024
    return pl.pallas_call(
        _mod_kernel,
        grid=(n // tn,),
        in_specs=[pl.BlockSpec((COND_ROWS, D_MODEL), lambda j: (0, 0)),
                  pl.BlockSpec((D_MODEL, tn), lambda j: (0, j)),
                  pl.BlockSpec((1, tn), lambda j: (0, j))],
        out_specs=pl.BlockSpec((COND_ROWS, tn), lambda j: (0, j)),
        out_shape=jax.ShapeDtypeStruct((COND_ROWS, n), F32),
        compiler_params=_cparams(("arbitrary",), 40 * 1024 * 1024),
        name="modulation",
    )(cond, w_mod, b_mod.reshape(1, n))


def _ctx_tile(i, tm):
    return jnp.minimum(i, T_CTX // tm - 1)


def _lat_tile(i, tm):
    return jnp.maximum(i - T_CTX // tm, 0)


def _inproj_kernel(xc_ref, xl_ref, g_ref, mod_ref, w_ref, o_ref, h_ref):
    def modulated_norm(x_ref):
        x = x_ref[...]
        y = x * lax.rsqrt(jnp.mean(x * x, axis=-1, keepdims=True) + NORM_EPS) * g_ref[...]
        shift = mod_ref[0, 0:1, :]
        scale = mod_ref[0, 1:2, :]
        h_ref[...] = (y * (1.0 + scale) + shift).astype(BF16)

    first = pl.program_id(1) == 0
    is_ctx = pl.program_id(0) < T_CTX // TM_IN
    pl.when(first & is_ctx)(lambda: modulated_norm(xc_ref))
    pl.when(first & jnp.logical_not(is_ctx))(lambda: modulated_norm(xl_ref))

    o_ref[...] = jnp.dot(h_ref[...], w_ref[...], preferred_element_type=F32)


def _in_projection(x_ctx, x_lat, norm_g, mod3, w_in_bf16):
    return pl.pallas_call(
        _inproj_kernel,
        grid=(T_ALL // TM_IN, IN_WIDTH // TN_IN),
        in_specs=[pl.BlockSpec((TM_IN, D_MODEL), lambda i, j: (_ctx_tile(i, TM_IN), 0)),
                  pl.BlockSpec((TM_IN, D_MODEL), lambda i, j: (_lat_tile(i, TM_IN), 0)),
                  pl.BlockSpec((1, D_MODEL), lambda i, j: (0, 0)),
                  pl.BlockSpec((1, 6, D_MODEL), lambda i, j: (_cond_row(i, TM_IN), 0, 0)),
                  pl.BlockSpec((D_MODEL, TN_IN), lambda i, j: (0, j))],
        out_specs=pl.BlockSpec((TM_IN, TN_IN), lambda i, j: (i, j)),
        out_shape=jax.ShapeDtypeStruct((T_ALL, IN_WIDTH), F32),
        scratch_shapes=[pltpu.VMEM((TM_IN, D_MODEL), BF16)],
        compiler_params=_cparams(("arbitrary", "arbitrary"), 40 * 1024 * 1024),
        name="in_projection",
    )(x_ctx, x_lat, norm_g.reshape(1, D_MODEL), mod3, w_in_bf16)


def _rope_tables(seq_len):
    t = jnp.arange(seq_len)
    row = (t // GRID_W).astype(F32)
    col = (t % GRID_W).astype(F32)
    half = RET_DK // 2
    n_freq = half // 2
    inv = ROPE_BASE ** (-jnp.arange(n_freq, dtype=F32) / n_freq)
    ang_r = row[:, None] * inv
    ang_c = col[:, None] * inv
    cos = jnp.concatenate([jnp.cos(ang_r), jnp.cos(ang_r), jnp.cos(ang_c), jnp.cos(ang_c)], axis=-1)
    sin = jnp.concatenate([-jnp.sin(ang_r), jnp.sin(ang_r), -jnp.sin(ang_c), jnp.sin(ang_c)], axis=-1)
    return cos, sin


def _retention_kernel(rd_ref, *refs, seq_len, chunk, use_rope, has_state_in, has_state_out, heads):
    refs = list(refs)
    q_ref, k_ref, v_ref, g_ref = refs[:4]
    pos = 4
    if has_state_in:
        s0_ref = refs[pos]
        pos += 1
    if use_rope:
        cos_ref, sin_ref = refs[pos], refs[pos + 1]
        pos += 2
    o_ref = refs[pos]
    pos += 1
    if has_state_out:
        st_ref = refs[pos]
        pos += 1
    scratch = refs[pos:pos + 5]

    for hh in range(heads):
        _retention_head(rd_ref, pl.program_id(1) * heads + hh, hh,
                        q_ref, k_ref, v_ref, g_ref,
                        s0_ref if has_state_in else None,
                        (cos_ref, sin_ref) if use_rope else None,
                        o_ref, st_ref if has_state_out else None,
                        [r.at[hh] for r in scratch], seq_len, chunk)


def _retention_head(rd_ref, h, hh, q_ref, k_ref, v_ref, g_ref, s0_ref, rope_refs, o_ref, st_ref, scratch,
                    seq_len, chunk):
    qs_ref, ks_ref, acc_ref, sf_ref, sb_ref = scratch
    use_rope = rope_refs is not None
    kcols = slice(hh * RET_DK, (hh + 1) * RET_DK)
    vcols = slice(hh * RET_DV, (hh + 1) * RET_DV)
    C = chunk
    n_chunks = seq_len // C

    lgf = -jnp.exp(jnp.full((C, 1), rd_ref[0, h], F32))
    lgb = -jnp.exp(jnp.full((C, 1), rd_ref[1, h], F32))
    ii = lax.broadcasted_iota(jnp.int32, (C, C), 0)
    jj = lax.broadcasted_iota(jnp.int32, (C, C), 1)
    diff = (ii - jj).astype(F32)
    decay = (jnp.where(diff >= 0, jnp.exp(lgf * jnp.maximum(diff, 0.0)), 0.0)
             + jnp.where(diff <= 0, jnp.exp(lgb * jnp.maximum(-diff, 0.0)), 0.0))
    p = lax.broadcasted_iota(jnp.int32, (C, 1), 0).astype(F32)
    xi_f = jnp.exp(lgf * (p + 1.0))
    zeta_f = jnp.exp(lgf * (C - 1.0 - p))
    xi_b = jnp.exp(lgb * (C - p))
    zeta_b = jnp.exp(lgb * p)
    cd_f = jnp.exp(lgf[0:1, :] * C)
    cd_b = jnp.exp(lgb[0:1, :] * C)

    if s0_ref is not None:
        sf_ref[...] = s0_ref[0, 0, hh]
        sb_ref[...] = s0_ref[0, 1, hh]
    else:
        sf_ref[...] = jnp.zeros_like(sf_ref)
        sb_ref[...] = jnp.zeros_like(sb_ref)

    lane = lax.broadcasted_iota(jnp.int32, (C, RET_DK), 1)
    first_half = (lane & 32) == 0

    def rope(x, cs, sn):
        swapped = jnp.where(first_half, pltpu.roll(x, RET_DK - 32, axis=1), pltpu.roll(x, 32, axis=1))
        return x * cs + swapped * sn

    def state_update(s_ref, kz, v, cd):
        upd = lax.dot_general(kz, v, (((0,), (0,)), ((), ())), preferred_element_type=F32)
        s_ref[...] = cd * s_ref[...] + upd

    def chunk_rows(c):
        start = c * C
        return pl.ds(start if isinstance(start, int) else pl.multiple_of(start, C), C)

    def fwd_chunk(c):
        sl = chunk_rows(c)
        q = q_ref[sl, kcols]
        k = k_ref[sl, kcols] * (RET_DK ** -0.5)
        if use_rope:
            cs = rope_refs[0][sl, :]
            sn = rope_refs[1][sl, :]
            q = rope(q, cs, sn)
            k = rope(k, cs, sn)
        qb = q.astype(BF16)
        kb = k.astype(BF16)
        qs_ref[sl, :] = qb
        ks_ref[sl, :] = k
        v = v_ref[sl, vcols].astype(BF16)
        scores = lax.dot_general(qb, kb, (((1,), (1,)), ((), ())), preferred_element_type=F32)
        inner = jnp.dot((scores * decay).astype(BF16), v, preferred_element_type=F32)
        cross = jnp.dot(qb, sf_ref[...].astype(BF16), preferred_element_type=F32) * xi_f
        acc_ref[sl, :] = inner + cross
        state_update(sf_ref, (k * zeta_f).astype(BF16), v, cd_f)

    def bwd_chunk(c):
        sl = chunk_rows(c)
        qb = qs_ref[sl, :]
        k = ks_ref[sl, :]
        v = v_ref[sl, vcols].astype(BF16)
        cross = jnp.dot(qb, sb_ref[...].astype(BF16), preferred_element_type=F32) * xi_b
        o = acc_ref[sl, :] + cross
        mu = jnp.mean(o, axis=-1, keepdims=True)
        oc = o - mu
        var = jnp.mean(oc * oc, axis=-1, keepdims=True)
        on = oc * lax.rsqrt(var + GN_EPS)
        g = g_ref[sl, vcols]
        o_ref[sl, vcols] = (on * (g * _sigmoid(g))).astype(BF16)
        state_update(sb_ref, (k * zeta_b).astype(BF16), v, cd_b)

    if n_chunks == 1:
        fwd_chunk(0)
        bwd_chunk(0)
    else:
        assert n_chunks % 2 == 0

        def fwd_pair(i, carry):
            fwd_chunk(2 * i)
            fwd_chunk(2 * i + 1)
            return carry

        def bwd_pair(i, carry):
            bwd_chunk(n_chunks - 1 - 2 * i)
            bwd_chunk(n_chunks - 2 - 2 * i)
            return carry

        lax.fori_loop(0, n_chunks // 2, fwd_pair, 0)
        lax.fori_loop(0, n_chunks // 2, bwd_pair, 0)

    if st_ref is not None:
        st_ref[0, 0, hh] = sf_ref[...]
        st_ref[0, 1, hh] = sb_ref[...]


def _retention(proj, row0, B, L, ret_decay, state_in, rope_tabs, want_state, heads):
    rb = row0 // L
    use_rope = rope_tabs is not None
    has_state_in = state_in is not None
    wk = heads * RET_DK
    wv = heads * RET_DV
    kq = RET_QK_WIDTH // wk
    kv = 2 * RET_QK_WIDTH // wv
    kg = kv + RET_WIDTH // wv
    in_specs = [pl.BlockSpec((L, wk), lambda b, h, rd: (rb + b, h)),
                pl.BlockSpec((L, wk), lambda b, h, rd: (rb + b, kq + h)),
                pl.BlockSpec((L, wv), lambda b, h, rd: (rb + b, kv + h)),
                pl.BlockSpec((L, wv), lambda b, h, rd: (rb + b, kg + h))]
    args = [proj, proj, proj, proj]
    state_spec = pl.BlockSpec((1, 2, heads, RET_DK, RET_DV), lambda b, h, rd: (b, 0, h, 0, 0))
    if has_state_in:
        in_specs.append(state_spec)
        args.append(state_in)
    if use_rope:
        in_specs += [pl.BlockSpec((L, RET_DK), lambda b, h, rd: (0, 0))] * 2
        args += list(rope_tabs)
    out_specs = [pl.BlockSpec((L, wv), lambda b, h, rd: (b, h))]
    out_shape = [jax.ShapeDtypeStruct((B * L, RET_WIDTH), BF16)]
    if want_state:
        out_specs.append(state_spec)
        out_shape.append(jax.ShapeDtypeStruct((B, 2, RET_HEADS, RET_DK, RET_DV), F32))
    kern = functools.partial(_retention_kernel, seq_len=L, chunk=RET_CHUNK, use_rope=use_rope,
                             has_state_in=has_state_in, has_state_out=want_state, heads=heads)
    res = pl.pallas_call(
        kern,
        grid_spec=pltpu.PrefetchScalarGridSpec(
            num_scalar_prefetch=1,
            grid=(B, RET_HEADS // heads),
            in_specs=in_specs,
            out_specs=out_specs,
            scratch_shapes=[pltpu.VMEM((heads, L, RET_DK), BF16),
                            pltpu.VMEM((heads, L, RET_DK), F32),
                            pltpu.VMEM((heads, L, RET_DV), F32),
                            pltpu.VMEM((heads, RET_DK, RET_DV), F32),
                            pltpu.VMEM((heads, RET_DK, RET_DV), F32)]),
        out_shape=out_shape,
        compiler_params=_cparams(("arbitrary", "arbitrary"), 40 * 1024 * 1024),
        name="retention_grid" if use_rope else "retention_seq",
    )(ret_decay, *args)
    return res if want_state else (res[0], None)


def _split3(x):
    hi = x.astype(BF16)
    r1 = x - hi.astype(F32)
    mid = r1.astype(BF16)
    lo = (r1 - mid.astype(F32)).astype(BF16)
    return jnp.concatenate([hi, mid, lo], axis=-1)


def _pool_kernel(p_ref, w_ref, sc_ref, o_ref, pad_ref, *, seq_len, grid_mode):
    PT = POOL_TILE
    n_tiles = seq_len // PT
    ii = lax.broadcasted_iota(jnp.int32, (PT, PT), 0)
    jj = lax.broadcasted_iota(jnp.int32, (PT, PT), 1)
    d = jj - ii
    ti = lax.broadcasted_iota(jnp.int32, (PT, 1), 0)

    if grid_mode:
        zeros = jnp.zeros((POOL_PAD, POOL_DG), F32)
        pad_ref[0:POOL_PAD, :] = zeros
        pad_ref[POOL_PAD + seq_len:POOL_PAD + seq_len + POOL_PAD, :] = zeros

    for gi, w in enumerate(POOL_WINDOWS):
        lo_off = -(w // 2)
        hi_off = w - w // 2 - 1
        cols = slice(gi * POOL_DG, (gi + 1) * POOL_DG)
        in_window = (d >= lo_off) & (d <= hi_off)
        if grid_mode:
            in_window = in_window & ((ii // GRID_W) == (jj // GRID_W))
        band = jnp.where(in_window, 1.0, 0.0).astype(BF16)
        wg = w_ref[gi]
        scale = sc_ref[gi]

        def window_sum(t0):
            x = p_ref[pl.ds(t0, PT), cols]
            s3 = jnp.dot(band, _split3(x), preferred_element_type=F32)
            return s3[:, 0:POOL_DG] + s3[:, POOL_DG:2 * POOL_DG] + s3[:, 2 * POOL_DG:3 * POOL_DG]

        def finish(t0, s, cnt):
            x = p_ref[pl.ds(t0, PT), cols]
            diff = (s / cnt - x).astype(BF16)
            y = jnp.dot(diff, wg, preferred_element_type=F32) * scale
            o_ref[pl.ds(t0, PT), cols] = y.astype(BF16)

        def count(pos, n):
            lo = jnp.maximum(pos + lo_off, 0)
            hi = jnp.minimum(pos + lo_off + w, n)
            return (hi - lo).astype(F32)

        if grid_mode:
            for t in range(n_tiles):
                pad_ref[POOL_PAD + t * PT:POOL_PAD + (t + 1) * PT, :] = window_sum(t * PT)
            for t in range(n_tiles):
                base = POOL_PAD + t * PT
                s = pad_ref[base + lo_off * GRID_W:base + lo_off * GRID_W + PT, :]
                for r in range(lo_off + 1, hi_off + 1):
                    s = s + pad_ref[base + r * GRID_W:base + r * GRID_W + PT, :]
                tok = ti + t * PT
                cnt = count(tok // GRID_W, seq_len // GRID_W) * count(tok % GRID_W, GRID_W)
                finish(t * PT, s, cnt)
        else:
            for t in range(n_tiles):
                finish(t * PT, window_sum(t * PT), count(ti + t * PT, seq_len))


def _pooling(proj, row0, B, L, w_pool_bf16, pool_scale, grid_mode):
    if not grid_mode:
        assert L == POOL_TILE
    rb = row0 // L
    pcol = (IN_WIDTH - POOL_WIDTH) // POOL_WIDTH
    kern = functools.partial(_pool_kernel, seq_len=L, grid_mode=grid_mode)
    return pl.pallas_call(
        kern,
        grid=(B,),
        in_specs=[pl.BlockSpec((L, POOL_WIDTH), lambda b: (rb + b, pcol)),
                  pl.BlockSpec((POOL_GROUPS, POOL_DG, POOL_DG), lambda b: (0, 0, 0)),
                  pl.BlockSpec((POOL_GROUPS, 1, POOL_DG), lambda b: (0, 0, 0))],
        out_specs=pl.BlockSpec((L, POOL_WIDTH), lambda b: (b, 0)),
        out_shape=jax.ShapeDtypeStruct((B * L, POOL_WIDTH), BF16),
        scratch_shapes=[pltpu.VMEM((L + 2 * POOL_PAD, POOL_DG), F32)],
        compiler_params=_cparams(("arbitrary",), 48 * 1024 * 1024),
        name="pool_grid" if grid_mode else "pool_seq",
    )(proj, w_pool_bf16, pool_scale.reshape(POOL_GROUPS, 1, POOL_DG))


def _outproj_kernel(*refs):
    i = pl.program_id(0)
    h2_ref = refs[14]

    @pl.when(i >= T_ALL // TM_OUT)
    def _():
        h2_ref[...] = jnp.zeros_like(h2_ref)

    pl.when(i < T_ALL // TM_OUT)(lambda: _outproj_tile(*refs))


def _outproj_tile(retc_ref, retl_ref, poolc_ref, pooll_ref, xc_ref, xl_ref, mod_ref, g_ref,
                  wt_ref, wb_ref, wrh_ref, wrb_ref, br_ref,
                  x1_ref, h2_ref, ri_ref, rp_ref, rr_ref, cnt_ref, run_ref):
    i = pl.program_id(0)

    @pl.when(i == 0)
    def _():
        run_ref[...] = jnp.zeros_like(run_ref)

    is_ctx = i < T_CTX // TM_OUT
    ret = jnp.where(is_ctx, retc_ref[...], retl_ref[...])
    pool = jnp.where(is_ctx, poolc_ref[...], pooll_ref[...])
    x = jnp.where(is_ctx, xc_ref[...], xl_ref[...])
    y = (jnp.dot(ret, wt_ref[...], preferred_element_type=F32)
         + jnp.dot(pool, wb_ref[...], preferred_element_type=F32))
    x1 = x + mod_ref[0, 2:3, :] * y
    x1_ref[...] = x1
    hn = x1 * lax.rsqrt(jnp.mean(x1 * x1, axis=-1, keepdims=True) + NORM_EPS) * g_ref[...]
    h2 = hn * (1.0 + mod_ref[0, 4:5, :]) + mod_ref[0, 3:4, :]
    hi = h2.astype(BF16)
    h2_ref[...] = hi
    lo = (h2 - hi.astype(F32)).astype(BF16)
    both = jnp.dot(hi, wrb_ref[...], preferred_element_type=F32)
    logits = (both[:, 0:LANES] + jnp.dot(lo, wrh_ref[...], preferred_element_type=F32)
              + both[:, LANES:2 * LANES]) + br_ref[...]

    tm = logits.shape[0]
    lane = lax.broadcasted_iota(jnp.int32, (tm, LANES), 1)
    work = logits
    vals, idxs, hots = [], [], []
    for _ in range(TOP_K):
        m = jnp.max(work, axis=-1, keepdims=True)
        idx = jnp.min(jnp.where(work == m, lane, LANES), axis=-1, keepdims=True)
        hot = lane == idx
        vals.append(m)
        idxs.append(idx)
        hots.append(hot)
        work = jnp.where(hot, -jnp.inf, work)
    exps = [jnp.exp(v - vals[0]) for v in vals]
    denom = exps[0] + exps[1] + exps[2] + exps[3]

    selected = jnp.zeros((tm, LANES), F32)
    for hot in hots:
        selected = selected + jnp.where(hot, 1.0, 0.0)
    r_i = lax.broadcasted_iota(jnp.int32, (tm, tm), 0)
    c_i = lax.broadcasted_iota(jnp.int32, (tm, tm), 1)
    tri = jnp.where(c_i < r_i, 1.0, 0.0).astype(BF16)
    before = jnp.dot(tri, selected.astype(BF16), preferred_element_type=F32) + run_ref[0:1, :]

    ri = jnp.zeros((tm, LANES), jnp.int32)
    rp = jnp.zeros((tm, LANES), F32)
    rr = jnp.zeros((tm, LANES), jnp.int32)
    for k in range(TOP_K):
        rank = jnp.sum(jnp.where(hots[k], before, 0.0), axis=-1, keepdims=True).astype(jnp.int32)
        ri = jnp.where(lane == k, idxs[k], ri)
        rp = jnp.where(lane == k, exps[k] / denom, rp)
        rr = jnp.where(lane == k, rank, rr)
    ri_ref[...] = ri
    rp_ref[...] = rp
    rr_ref[...] = rr

    run = run_ref[0:1, :] + jnp.sum(selected, axis=0, keepdims=True)
    run_ref[...] = jnp.broadcast_to(run, run_ref.shape)
    cnt_ref[...] = jnp.broadcast_to(run, cnt_ref.shape).astype(jnp.int32)


def _out_projection(ret_ctx, ret_lat, pool_ctx, pool_lat, x_ctx, x_lat, mod3, norm_g, w_out_bf16,
                    wr_hi, wr_both, b_router_pad):
    tm = TM_OUT
    real = lambda i: jnp.minimum(i, T_ALL // tm - 1)
    row = lambda i: (real(i), 0)
    const = lambda i: (0, 0)
    crow = lambda i: (_ctx_tile(real(i), tm), 0)
    lrow = lambda i: (_lat_tile(real(i), tm), 0)
    return pl.pallas_call(
        _outproj_kernel,
        grid=(H2_ROWS // tm,),
        in_specs=[pl.BlockSpec((tm, RET_WIDTH), crow),
                  pl.BlockSpec((tm, RET_WIDTH), lrow),
                  pl.BlockSpec((tm, POOL_WIDTH), crow),
                  pl.BlockSpec((tm, POOL_WIDTH), lrow),
                  pl.BlockSpec((tm, D_MODEL), crow),
                  pl.BlockSpec((tm, D_MODEL), lrow),
                  pl.BlockSpec((1, 6, D_MODEL), lambda i: (_cond_row(real(i), tm), 0, 0)),
                  pl.BlockSpec((1, D_MODEL), const),
                  pl.BlockSpec((RET_WIDTH, D_MODEL), const),
                  pl.BlockSpec((POOL_WIDTH, D_MODEL), lambda i: (1, 0)),
                  pl.BlockSpec((D_MODEL, LANES), const),
                  pl.BlockSpec((D_MODEL, 2 * LANES), const),
                  pl.BlockSpec((1, LANES), const)],
        out_specs=[pl.BlockSpec((tm, D_MODEL), row),
                   pl.BlockSpec((tm, D_MODEL), lambda i: (i, 0)),
                   pl.BlockSpec((tm, LANES), row),
                   pl.BlockSpec((tm, LANES), row),
                   pl.BlockSpec((tm, LANES), row),
                   pl.BlockSpec((8, LANES), const)],
        out_shape=[jax.ShapeDtypeStruct((T_ALL, D_MODEL), F32),
                   jax.ShapeDtypeStruct((H2_ROWS, D_MODEL), BF16),
                   jax.ShapeDtypeStruct((T_ALL, LANES), jnp.int32),
                   jax.ShapeDtypeStruct((T_ALL, LANES), F32),
                   jax.ShapeDtypeStruct((T_ALL, LANES), jnp.int32),
                   jax.ShapeDtypeStruct((8, LANES), jnp.int32)],
        scratch_shapes=[pltpu.VMEM((8, LANES), F32)],
        compiler_params=_cparams(("arbitrary",), 48 * 1024 * 1024),
        name="out_projection_router",
    )(ret_ctx, ret_lat, pool_ctx, pool_lat, x_ctx, x_lat, mod3, norm_g.reshape(1, D_MODEL),
      w_out_bf16, w_out_bf16, wr_hi, wr_both, b_router_pad)


def _count_le(ends, v):
    return jnp.sum((ends[None, :] <= v[:, None]).astype(jnp.int32), axis=1)


def _expert_schedule(counts):
    i32 = jnp.int32
    ntile = (counts + TM_EXP - 1) // TM_EXP
    tile_off = jnp.cumsum(ntile) - ntile
    nsup = (ntile + SUP_EXP - 1) // SUP_EXP
    sup_end = jnp.cumsum(nsup)
    u = jnp.arange(U_EXP, dtype=i32)
    sup_valid = u < sup_end[-1]
    uc = jnp.minimum(u, sup_end[-1] - 1)
    sup_e = jnp.minimum(_count_le(sup_end, uc), N_EXPERTS - 1)
    sup_idx = uc - (sup_end - nsup)[sup_e]
    per_sup = (ntile + jnp.maximum(nsup, 1) - 1) // jnp.maximum(nsup, 1)
    sup_row0 = (tile_off[sup_e] + sup_idx * per_sup[sup_e]) * TM_EXP
    sup_nt = jnp.clip(ntile[sup_e] - sup_idx * per_sup[sup_e], 1, per_sup[sup_e])
    step_e = jnp.repeat(sup_e, J_EXP)
    step_j = jnp.tile(jnp.arange(J_EXP, dtype=i32), U_EXP)
    step_row0 = jnp.repeat(sup_row0, J_EXP)
    step_nt = jnp.repeat(sup_nt, J_EXP)
    last_rows = counts - (ntile - 1) * TM_EXP
    sup_half = (sup_idx == nsup[sup_e] - 1) & (last_rows[sup_e] <= TM_EXP // 2) & (counts[sup_e] > 0)
    step_half = jnp.repeat(sup_half.astype(i32), J_EXP)
    bounds = jnp.asarray(CH_START_EXP[1:], dtype=i32)
    before = jnp.sum((sup_valid[None, :] & (sup_row0[None, :] < bounds[:, None])).astype(i32), axis=1) * J_EXP
    n_steps = jnp.sum(sup_valid.astype(i32)) * J_EXP
    chunk_start = jnp.concatenate([jnp.zeros((1,), i32), before])
    chunk_steps = jnp.concatenate([before, n_steps.reshape(1)]) - chunk_start
    tables = (step_e.astype(i32), step_j.astype(i32), step_row0.astype(i32), step_nt.astype(i32), step_half)
    return tile_off * TM_EXP, tables, chunk_start, chunk_steps, jnp.sum(ntile).astype(i32)


def _expert_kernel(se_ref, sj_ref, srow_ref, snt_ref, shalf_ref, meta_ref,
                   xs_hbm, wg_ref, wu_ref, bg_ref, bu_ref, wd_ref, bd_ref, *rest, first_chunk):
    ys_hbm, acc_ref, x_buf, o_buf, x_sem, o_sem = rest if first_chunk else rest[1:]
    s = pl.program_id(0) + meta_ref[0]
    j = sj_ref[s]
    n_tiles = snt_ref[s]
    row0 = pl.multiple_of(srow_ref[s], TM_EXP)
    last = J_EXP - 1

    def x_copy():
        rows = pl.ds(pl.multiple_of(row0 - meta_ref[1], TM_EXP), SUP_EXP * TM_EXP)
        return pltpu.make_async_copy(xs_hbm.at[rows], x_buf, x_sem)

    def out_copy(t):
        rows = pl.ds(pl.multiple_of(row0 + t * TM_EXP, TM_EXP), TM_EXP)
        return pltpu.make_async_copy(o_buf.at[t % 2], ys_hbm.at[rows], o_sem.at[t % 2])

    @pl.when((pl.program_id(0) == 0) & first_chunk)
    def _():
        o_buf[0] = jnp.zeros((TM_EXP, D_MODEL), BF16)

        def zero_copy(t):
            rows = pl.ds(pl.multiple_of(t * TM_EXP, TM_EXP), TM_EXP)
            return pltpu.make_async_copy(o_buf.at[0], ys_hbm.at[rows], o_sem.at[0])

        def start(t, carry):
            zero_copy(t).start()
            return carry

        def wait(t, carry):
            zero_copy(t).wait()
            return carry

        lax.fori_loop(meta_ref[2], NT_EXP, start, 0)
        lax.fori_loop(meta_ref[2], NT_EXP, wait, 0)

    def step():
        @pl.when(j == 0)
        def _():
            x_copy().start()
            acc_ref[...] = jnp.zeros_like(acc_ref)
            x_copy().wait()

        def accumulate(rows):
            x = x_buf[rows, :]
            gate = _dot(x, wg_ref[0]) + bg_ref[0]
            up = _dot(x, wu_ref[0]) + bu_ref[0]
            gate = jnp.minimum(gate, SWIGLU_LIMIT)
            up = jnp.clip(up, -SWIGLU_LIMIT, SWIGLU_LIMIT)
            act = (up + 1.0) * gate * _sigmoid(SWIGLU_ALPHA * gate)
            acc_ref[rows, :] += _dot(act, wd_ref[0])

        def tile_pair(p, carry):
            base = pl.multiple_of(p * (2 * TM_EXP), 2 * TM_EXP)
            accumulate(pl.ds(base, TM_EXP))
            accumulate(pl.ds(base + TM_EXP, TM_EXP))
            return carry

        n_full = n_tiles - shalf_ref[s]
        lax.fori_loop(0, n_full // 2, tile_pair, 0)

        @pl.when(n_full % 2 == 1)
        def _():
            accumulate(pl.ds(pl.multiple_of((n_full - 1) * TM_EXP, TM_EXP), TM_EXP))

        @pl.when(shalf_ref[s] == 1)
        def _():
            accumulate(pl.ds(pl.multiple_of((n_tiles - 1) * TM_EXP, TM_EXP), TM_EXP // 2))

        @pl.when(j == last)
        def _():
            def finish(t, carry):
                @pl.when(t >= 2)
                def _():
                    out_copy(t - 2).wait()

                rows = pl.ds(pl.multiple_of(t * TM_EXP, TM_EXP), TM_EXP)
                o_buf[t % 2] = (acc_ref[rows, :] + bd_ref[0]).astype(BF16)
                out_copy(t).start()
                return carry

            lax.fori_loop(0, n_tiles, finish, 0)

            @pl.when(n_tiles >= 2)
            def _():
                out_copy(n_tiles - 2).wait()

            out_copy(n_tiles - 1).wait()

    step()


def _experts_chunk(chunk, xs_part, ys, tables, meta, n_steps, w_gate_up, b_gate_up, w_down, b_down):
    first = ys is None
    wmap = lambda col0: (lambda s, se, sj, srow, snt, shalf, meta: (se[s + meta[0]], 0, col0 + sj[s + meta[0]]))
    n_prefetch = len(tables) + 1
    in_specs = [
        pl.BlockSpec(memory_space=pl.ANY),
        pl.BlockSpec((1, D_MODEL, F_EXP), wmap(0)),
        pl.BlockSpec((1, D_MODEL, F_EXP), wmap(J_EXP)),
        pl.BlockSpec((1, 1, F_EXP), wmap(0)),
        pl.BlockSpec((1, 1, F_EXP), wmap(J_EXP)),
        pl.BlockSpec((1, F_EXP, D_MODEL),
                     lambda s, se, sj, srow, snt, shalf, meta: (se[s + meta[0]], sj[s + meta[0]], 0)),
        pl.BlockSpec((1, 1, D_MODEL), lambda s, se, sj, srow, snt, shalf, meta: (se[s + meta[0]], 0, 0)),
    ]
    args = [xs_part, w_gate_up, w_gate_up,
            b_gate_up.reshape(N_EXPERTS, 1, 2 * D_FF), b_gate_up.reshape(N_EXPERTS, 1, 2 * D_FF),
            w_down, b_down.reshape(N_EXPERTS, 1, D_MODEL)]
    aliases = {}
    if not first:
        in_specs.append(pl.BlockSpec(memory_space=pl.ANY))
        args.append(ys)
        aliases = {n_prefetch + len(args) - 1: 0}
    return pl.pallas_call(
        functools.partial(_expert_kernel, first_chunk=first),
        grid_spec=pltpu.PrefetchScalarGridSpec(
            num_scalar_prefetch=n_prefetch,
            grid=(n_steps,),
            in_specs=in_specs,
            out_specs=pl.BlockSpec(memory_space=pl.ANY),
            scratch_shapes=[pltpu.VMEM((SUP_EXP * TM_EXP, D_MODEL), F32),
                            pltpu.VMEM((SUP_EXP * TM_EXP, D_MODEL), BF16),
                            pltpu.VMEM((2, TM_EXP, D_MODEL), BF16),
                            pltpu.SemaphoreType.DMA(()),
                            pltpu.SemaphoreType.DMA((2,))]),
        out_shape=jax.ShapeDtypeStruct((NT_EXP * TM_EXP, D_MODEL), BF16),
        input_output_aliases=aliases,
        compiler_params=_cparams(("arbitrary",), VMEM_LIMIT),
        name=f"experts_chunk{chunk}",
    )(*tables, meta, *args)


def _combine_kernel(x1_ref, y0_ref, y1_ref, y2_ref, y3_ref, rp_ref, mod_ref, g_ref, o_ref):
    rp = rp_ref[...]
    moe = rp[:, 0:1] * y0_ref[...].astype(F32)
    for k, y_ref in enumerate((y1_ref, y2_ref, y3_ref), start=1):
        moe = moe + rp[:, k:k + 1] * y_ref[...].astype(F32)
    x2 = x1_ref[...] + mod_ref[0, 5:6, :] * moe
    o_ref[...] = x2 * lax.rsqrt(jnp.mean(x2 * x2, axis=-1, keepdims=True) + NORM_EPS) * g_ref[...]


def _combine(x1, y4, route_p, mod3, final_g, row0, n_rows):
    tm = TM_CMB
    nt = n_rows // tm
    t0 = row0 // tm
    y_specs = [pl.BlockSpec((tm, D_MODEL), functools.partial(lambda k, i: (k * nt + i, 0), k))
               for k in range(TOP_K)]
    return pl.pallas_call(
        _combine_kernel,
        grid=(nt,),
        in_specs=[pl.BlockSpec((tm, D_MODEL), lambda i: (t0 + i, 0))] + y_specs + [
            pl.BlockSpec((tm, LANES), lambda i: (t0 + i, 0)),
            pl.BlockSpec((1, 6, D_MODEL), lambda i: (_cond_row(t0 + i, tm), 0, 0)),
            pl.BlockSpec((1, D_MODEL), lambda i: (0, 0))],
        out_specs=pl.BlockSpec((tm, D_MODEL), lambda i: (i, 0)),
        out_shape=jax.ShapeDtypeStruct((n_rows, D_MODEL), F32),
        compiler_params=_cparams(("arbitrary",), 40 * 1024 * 1024),
        name="combine_final_norm",
    )(x1, y4, y4, y4, y4, route_p, mod3, final_g.reshape(1, D_MODEL))


def kernel(x_prompt, x_sample, c, c_ctx, state_ret, norm1_g, norm2_g, w_mod, b_mod, w_in, ret_decay,
           w_pool, pool_scale, w_out, w_router, b_router, w_gate_up, b_gate_up, w_down, b_down, final_g):
    assert w_mod.shape[0] == 1, "single trunk layer"
    x_ctx = x_prompt.reshape(T_CTX, D_MODEL)
    x_lat = x_sample.reshape(T_LAT, D_MODEL)
    cond = jnp.zeros((COND_ROWS, D_MODEL), F32).at[0].set(c_ctx).at[1:N_COND].set(c)

    mod = _modulation(cond, w_mod[0], b_mod[0])
    mod3 = mod.reshape(COND_ROWS, 6, D_MODEL)

    proj = _in_projection(x_ctx, x_lat, norm1_g[0], mod3, w_in[0].astype(BF16))

    ret_ctx, new_state = _retention(proj, 0, BATCH, SEQ, ret_decay[0], None, None, True, RET_HEADS)
    ret_lat, _ = _retention(proj, T_CTX, DEC_BATCH, DEC_SEQ, ret_decay[0], state_ret[:, 0],
                            _rope_tables(DEC_SEQ), False, 1)
    w_pool_bf16 = w_pool[0].astype(BF16)
    pool_ctx = _pooling(proj, 0, BATCH, SEQ, w_pool_bf16, pool_scale[0], False)
    pool_lat = _pooling(proj, T_CTX, DEC_BATCH, DEC_SEQ, w_pool_bf16, pool_scale[0], True)

    wr = jnp.zeros((D_MODEL, LANES), F32).at[:, :N_EXPERTS].set(w_router[0])
    wr_hi = wr.astype(BF16)
    wr_both = jnp.concatenate([wr_hi, (wr - wr_hi.astype(F32)).astype(BF16)], axis=1)
    br = jnp.full((1, LANES), -1e30, F32).at[0, :N_EXPERTS].set(b_router[0])
    x1, h2, route_i, route_p, route_r, counts = _out_projection(
        ret_ctx, ret_lat, pool_ctx, pool_lat, x_ctx, x_lat, mod3, norm2_g[0], w_out[0].astype(BF16),
        wr_hi, wr_both, br)

    row_off, tables, chunk_start, chunk_steps, n_row_tiles = _expert_schedule(counts[0, :N_EXPERTS])
    slot = (row_off[route_i[:, :TOP_K]] + route_r[:, :TOP_K]).T
    token = jnp.broadcast_to(jnp.arange(T_ALL, dtype=jnp.int32)[None, :], (TOP_K, T_ALL))
    token_of_slot = (jnp.arange(XS_ROWS, dtype=jnp.int32) % T_ALL).at[slot.reshape(-1)].set(
        token.reshape(-1), unique_indices=True, mode="promise_in_bounds")
    ys = None
    for c in range(CHUNKS_EXP):
        lo = CH_START_EXP[c]
        nxt = CH_START_EXP[c + 1] if c + 1 < CHUNKS_EXP else XS_ROWS
        hi = min(nxt + SUP_EXP * TM_EXP, XS_ROWS)
        xs_part = h2.at[token_of_slot[lo:hi]].get(mode="promise_in_bounds")
        meta = jnp.stack([chunk_start[c], jnp.int32(lo), n_row_tiles]).astype(jnp.int32)
        ys = _experts_chunk(c, xs_part, ys, tables, meta, chunk_steps[c],
                            w_gate_up[0], b_gate_up[0], w_down[0], b_down[0])
    y4_lat = ys.at[slot[:, T_CTX:].reshape(-1)].get(mode="promise_in_bounds", unique_indices=True)
    y4_ctx = ys.at[slot[:, :T_CTX].reshape(-1)].get(mode="promise_in_bounds", unique_indices=True)
    y_lat = _combine(x1, y4_lat, route_p, mod3, final_g, T_CTX, T_LAT)
    y_ctx = _combine(x1, y4_ctx, route_p, mod3, final_g, 0, T_CTX)
    return (y_ctx.reshape(BATCH, SEQ, D_MODEL), y_lat.reshape(DEC_BATCH, DEC_SEQ, D_MODEL),
            new_state.reshape(BATCH, 1, 2, RET_HEADS, RET_DK, RET_DV))
```

```python
import functools

import numpy as np
import jax
import jax.numpy as jnp
from jax import lax
from jax.experimental import pallas as pl
from jax.experimental.pallas import tpu as pltpu

F32 = jnp.float32
BF16 = jnp.bfloat16

D_MODEL = 2048
BATCH = 16
SEQ = 256
DEC_BATCH = 4
DEC_SEQ = 2048
GRID_W = 64
RET_HEADS = 4
RET_DK = 128
RET_DV = 256
RET_QK_WIDTH = RET_HEADS * RET_DK
RET_WIDTH = RET_HEADS * RET_DV
POOL_WINDOWS = (2, 4, 8, 16)
POOL_GROUPS = 4
POOL_DG = 256
POOL_WIDTH = POOL_GROUPS * POOL_DG
IN_WIDTH = 2 * RET_QK_WIDTH + 2 * RET_WIDTH + POOL_WIDTH
N_EXPERTS = 32
TOP_K = 4
D_FF = D_MODEL
SWIGLU_LIMIT = 7.0
SWIGLU_ALPHA = 1.702
ROPE_BASE = 10000.0
NORM_EPS = 1e-6
GN_EPS = 1e-6

T_CTX = BATCH * SEQ
T_LAT = DEC_BATCH * DEC_SEQ
T_ALL = T_CTX + T_LAT
N_COND = 1 + DEC_BATCH
COND_ROWS = 8
LANES = 128

RET_CHUNK = 256

TM_IN = 512
TN_IN = 1024
TM_OUT = 256
TM_CMB = 256
H2_ROWS = 16384
POOL_TILE = 256
POOL_PAD = (max(POOL_WINDOWS) // 2) * GRID_W

TM_EXP = 256
F_EXP = 512
J_EXP = D_FF // F_EXP
SUP_EXP = 8
NT_EXP = (T_ALL * TOP_K + N_EXPERTS * (TM_EXP - 1)) // TM_EXP
U_EXP = N_EXPERTS + NT_EXP // SUP_EXP + 1
XS_ROWS = (NT_EXP + SUP_EXP) * TM_EXP
CH_START_EXP = tuple(t * TM_EXP for t in (0, 24))
CHUNKS_EXP = len(CH_START_EXP)

VMEM_LIMIT = 56 * 1024 * 1024


def _cparams(sem, vmem=None):
    return pltpu.CompilerParams(dimension_semantics=sem, vmem_limit_bytes=vmem)


def _cond_row(i, tm):
    nctx = T_CTX // tm
    return jnp.where(i < nctx, 0, 1 + (i - nctx) // (DEC_SEQ // tm))


def _sigmoid(x):
    return 1.0 / (1.0 + jnp.exp(-x))


def _dot(a, b):
    return lax.dot_general(a, b, (((1,), (0,)), ((), ())), preferred_element_type=F32)


def _mod_kernel(c_ref, w_ref, b_ref, o_ref):
    c = c_ref[...]
    s = (c * _sigmoid(c)).astype(BF16)
    o_ref[...] = jnp.dot(s, w_ref[...].astype(BF16), preferred_element_type=F32) + b_ref[...]


def _modulation(cond, w_mod, b_mod):
    n = w_mod.shape[1]
    tn = 1024
    return pl.pallas_call(
        _mod_kernel,
        grid=(n // tn,),
        in_specs=[pl.BlockSpec((COND_ROWS, D_MODEL), lambda j: (0, 0)),
                  pl.BlockSpec((D_MODEL, tn), lambda j: (0, j)),
                  pl.BlockSpec((1, tn), lambda j: (0, j))],
        out_specs=pl.BlockSpec((COND_ROWS, tn), lambda j: (0, j)),
        out_shape=jax.ShapeDtypeStruct((COND_ROWS, n), F32),
        compiler_params=_cparams(("arbitrary",), 40 * 1024 * 1024),
        name="modulation",
    )(cond, w_mod, b_mod.reshape(1, n))


def _ctx_tile(i, tm):
    return jnp.minimum(i, T_CTX // tm - 1)


def _lat_tile(i, tm):
    return jnp.maximum(i - T_CTX // tm, 0)


def _inproj_kernel(xc_ref, xl_ref, g_ref, mod_ref, w_ref, o_ref, h_ref):
    def modulated_norm(x_ref):
        x = x_ref[...]
        y = x * lax.rsqrt(jnp.mean(x * x, axis=-1, keepdims=True) + NORM_EPS) * g_ref[...]
        shift = mod_ref[0, 0:1, :]
        scale = mod_ref[0, 1:2, :]
        h_ref[...] = (y * (1.0 + scale) + shift).astype(BF16)

    first = pl.program_id(1) == 0
    is_ctx = pl.program_id(0) < T_CTX // TM_IN
    pl.when(first & is_ctx)(lambda: modulated_norm(xc_ref))
    pl.when(first & jnp.logical_not(is_ctx))(lambda: modulated_norm(xl_ref))

    o_ref[...] = jnp.dot(h_ref[...], w_ref[...], preferred_element_type=F32)


def _in_projection(x_ctx, x_lat, norm_g, mod3, w_in_bf16):
    return pl.pallas_call(
        _inproj_kernel,
        grid=(T_ALL // TM_IN, IN_WIDTH // TN_IN),
        in_specs=[pl.BlockSpec((TM_IN, D_MODEL), lambda i, j: (_ctx_tile(i, TM_IN), 0)),
                  pl.BlockSpec((TM_IN, D_MODEL), lambda i, j: (_lat_tile(i, TM_IN), 0)),
                  pl.BlockSpec((1, D_MODEL), lambda i, j: (0, 0)),
                  pl.BlockSpec((1, 6, D_MODEL), lambda i, j: (_cond_row(i, TM_IN), 0, 0)),
                  pl.BlockSpec((D_MODEL, TN_IN), lambda i, j: (0, j))],
        out_specs=pl.BlockSpec((TM_IN, TN_IN), lambda i, j: (i, j)),
        out_shape=jax.ShapeDtypeStruct((T_ALL, IN_WIDTH), F32),
        scratch_shapes=[pltpu.VMEM((TM_IN, D_MODEL), BF16)],
        compiler_params=_cparams(("arbitrary", "arbitrary"), 40 * 1024 * 1024),
        name="in_projection",
    )(x_ctx, x_lat, norm_g.reshape(1, D_MODEL), mod3, w_in_bf16)


def _rope_tables(seq_len):
    t = jnp.arange(seq_len)
    row = (t // GRID_W).astype(F32)
    col = (t % GRID_W).astype(F32)
    half = RET_DK // 2
    n_freq = half // 2
    inv = ROPE_BASE ** (-jnp.arange(n_freq, dtype=F32) / n_freq)
    ang_r = row[:, None] * inv
    ang_c = col[:, None] * inv
    cos = jnp.concatenate([jnp.cos(ang_r), jnp.cos(ang_r), jnp.cos(ang_c), jnp.cos(ang_c)], axis=-1)
    sin = jnp.concatenate([-jnp.sin(ang_r), jnp.sin(ang_r), -jnp.sin(ang_c), jnp.sin(ang_c)], axis=-1)
    return cos, sin


def _retention_kernel(rd_ref, *refs, seq_len, chunk, use_rope, has_state_in, has_state_out, heads):
    refs = list(refs)
    q_ref, k_ref, v_ref, g_ref = refs[:4]
    pos = 4
    if has_state_in:
        s0_ref = refs[pos]
        pos += 1
    if use_rope:
        cos_ref, sin_ref = refs[pos], refs[pos + 1]
        pos += 2
    o_ref = refs[pos]
    pos += 1
    if has_state_out:
        st_ref = refs[pos]
        pos += 1
    scratch = refs[pos:pos + 5]

    progs = [_retention_head(rd_ref, pl.program_id(1) * heads + hh, hh,
                             q_ref, k_ref, v_ref, g_ref,
                             s0_ref if has_state_in else None,
                             (cos_ref, sin_ref) if use_rope else None,
                             o_ref, st_ref if has_state_out else None,
                             [r.at[hh] for r in scratch], seq_len, chunk)
             for hh in range(heads)]
    n_chunks = seq_len // chunk
    if n_chunks == 1:
        for fwd_chunk, _, _ in progs:
            fwd_chunk(0)
        for _, bwd_chunk, _ in progs:
            bwd_chunk(0)
    else:
        assert n_chunks % 2 == 0

        def fwd_pair(i, carry):
            for fwd_chunk, _, _ in progs:
                fwd_chunk(2 * i)
                fwd_chunk(2 * i + 1)
            return carry

        def bwd_pair(i, carry):
            for _, bwd_chunk, _ in progs:
                bwd_chunk(n_chunks - 1 - 2 * i)
                bwd_chunk(n_chunks - 2 - 2 * i)
            return carry

        lax.fori_loop(0, n_chunks // 2, fwd_pair, 0)
        lax.fori_loop(0, n_chunks // 2, bwd_pair, 0)
    for _, _, finish in progs:
        finish()


def _retention_head(rd_ref, h, hh, q_ref, k_ref, v_ref, g_ref, s0_ref, rope_refs, o_ref, st_ref, scratch,
                    seq_len, chunk):
    qs_ref, ks_ref, acc_ref, sf_ref, sb_ref = scratch
    use_rope = rope_refs is not None
    kcols = slice(hh * RET_DK, (hh + 1) * RET_DK)
    vcols = slice(hh * RET_DV, (hh + 1) * RET_DV)
    C = chunk
    n_chunks = seq_len // C

    lgf = -jnp.exp(jnp.full((C, 1), rd_ref[0, h], F32))
    lgb = -jnp.exp(jnp.full((C, 1), rd_ref[1, h], F32))
    ii = lax.broadcasted_iota(jnp.int32, (C, C), 0)
    jj = lax.broadcasted_iota(jnp.int32, (C, C), 1)
    diff = (ii - jj).astype(F32)
    decay = (jnp.where(diff >= 0, jnp.exp(lgf * jnp.maximum(diff, 0.0)), 0.0)
             + jnp.where(diff <= 0, jnp.exp(lgb * jnp.maximum(-diff, 0.0)), 0.0))
    p = lax.broadcasted_iota(jnp.int32, (C, 1), 0).astype(F32)
    xi_f = jnp.exp(lgf * (p + 1.0))
    zeta_f = jnp.exp(lgf * (C - 1.0 - p))
    xi_b = jnp.exp(lgb * (C - p))
    zeta_b = jnp.exp(lgb * p)
    cd_f = jnp.exp(lgf[0:1, :] * C)
    cd_b = jnp.exp(lgb[0:1, :] * C)

    if s0_ref is not None:
        sf_ref[...] = s0_ref[0, 0, hh]
        sb_ref[...] = s0_ref[0, 1, hh]
    else:
        sf_ref[...] = jnp.zeros_like(sf_ref)
        sb_ref[...] = jnp.zeros_like(sb_ref)

    lane = lax.broadcasted_iota(jnp.int32, (C, RET_DK), 1)
    first_half = (lane & 32) == 0

    def rope(x, cs, sn):
        swapped = jnp.where(first_half, pltpu.roll(x, RET_DK - 32, axis=1), pltpu.roll(x, 32, axis=1))
        return x * cs + swapped * sn

    def state_update(s_ref, kz, v, cd):
        upd = lax.dot_general(kz, v, (((0,), (0,)), ((), ())), preferred_element_type=F32)
        s_ref[...] = cd * s_ref[...] + upd

    def chunk_rows(c):
        start = c * C
        return pl.ds(start if isinstance(start, int) else pl.multiple_of(start, C), C)

    def fwd_chunk(c):
        sl = chunk_rows(c)
        q = q_ref[sl, kcols]
        k = k_ref[sl, kcols] * (RET_DK ** -0.5)
        if use_rope:
            cs = rope_refs[0][sl, :]
            sn = rope_refs[1][sl, :]
            q = rope(q, cs, sn)
            k = rope(k, cs, sn)
        qb = q.astype(BF16)
        kb = k.astype(BF16)
        qs_ref[sl, :] = qb
        ks_ref[sl, :] = k
        v = v_ref[sl, vcols].astype(BF16)
        scores = lax.dot_general(qb, kb, (((1,), (1,)), ((), ())), preferred_element_type=F32)
        inner = jnp.dot((scores * decay).astype(BF16), v, preferred_element_type=F32)
        cross = jnp.dot(qb, sf_ref[...].astype(BF16), preferred_element_type=F32) * xi_f
        acc_ref[sl, :] = inner + cross
        state_update(sf_ref, (k * zeta_f).astype(BF16), v, cd_f)

    def bwd_chunk(c):
        sl = chunk_rows(c)
        qb = qs_ref[sl, :]
        k = ks_ref[sl, :]
        v = v_ref[sl, vcols].astype(BF16)
        cross = jnp.dot(qb, sb_ref[...].astype(BF16), preferred_element_type=F32) * xi_b
        o = acc_ref[sl, :] + cross
        mu = jnp.mean(o, axis=-1, keepdims=True)
        oc = o - mu
        var = jnp.mean(oc * oc, axis=-1, keepdims=True)
        on = oc * lax.rsqrt(var + GN_EPS)
        g = g_ref[sl, vcols]
        o_ref[sl, vcols] = (on * (g * _sigmoid(g))).astype(BF16)
        state_update(sb_ref, (k * zeta_b).astype(BF16), v, cd_b)

    def finish():
        if st_ref is not None:
            st_ref[0, 0, hh] = sf_ref[...]
            st_ref[0, 1, hh] = sb_ref[...]

    return fwd_chunk, bwd_chunk, finish


def _retention(proj, row0, B, L, ret_decay, state_in, rope_tabs, want_state, heads):
    rb = row0 // L
    use_rope = rope_tabs is not None
    has_state_in = state_in is not None
    wk = heads * RET_DK
    wv = heads * RET_DV
    kq = RET_QK_WIDTH // wk
    kv = 2 * RET_QK_WIDTH // wv
    kg = kv + RET_WIDTH // wv
    in_specs = [pl.BlockSpec((L, wk), lambda b, h, rd: (rb + b, h)),
                pl.BlockSpec((L, wk), lambda b, h, rd: (rb + b, kq + h)),
                pl.BlockSpec((L, wv), lambda b, h, rd: (rb + b, kv + h)),
                pl.BlockSpec((L, wv), lambda b, h, rd: (rb + b, kg + h))]
    args = [proj, proj, proj, proj]
    state_spec = pl.BlockSpec((1, 2, heads, RET_DK, RET_DV), lambda b, h, rd: (b, 0, h, 0, 0))
    if has_state_in:
        in_specs.append(state_spec)
        args.append(state_in)
    if use_rope:
        in_specs += [pl.BlockSpec((L, RET_DK), lambda b, h, rd: (0, 0))] * 2
        args += list(rope_tabs)
    out_specs = [pl.BlockSpec((L, wv), lambda b, h, rd: (b, h))]
    out_shape = [jax.ShapeDtypeStruct((B * L, RET_WIDTH), BF16)]
    if want_state:
        out_specs.append(state_spec)
        out_shape.append(jax.ShapeDtypeStruct((B, 2, RET_HEADS, RET_DK, RET_DV), F32))
    kern = functools.partial(_retention_kernel, seq_len=L, chunk=RET_CHUNK, use_rope=use_rope,
                             has_state_in=has_state_in, has_state_out=want_state, heads=heads)
    res = pl.pallas_call(
        kern,
        grid_spec=pltpu.PrefetchScalarGridSpec(
            num_scalar_prefetch=1,
            grid=(B, RET_HEADS // heads),
            in_specs=in_specs,
            out_specs=out_specs,
            scratch_shapes=[pltpu.VMEM((heads, L, RET_DK), BF16),
                            pltpu.VMEM((heads, L, RET_DK), F32),
                            pltpu.VMEM((heads, L, RET_DV), F32),
                            pltpu.VMEM((heads, RET_DK, RET_DV), F32),
                            pltpu.VMEM((heads, RET_DK, RET_DV), F32)]),
        out_shape=out_shape,
        compiler_params=_cparams(("arbitrary", "arbitrary"), 40 * 1024 * 1024),
        name="retention_grid" if use_rope else "retention_seq",
    )(ret_decay, *args)
    return res if want_state else (res[0], None)


def _split3(x):
    hi = x.astype(BF16)
    r1 = x - hi.astype(F32)
    mid = r1.astype(BF16)
    lo = (r1 - mid.astype(F32)).astype(BF16)
    return jnp.concatenate([hi, mid, lo], axis=-1)


def _pool_kernel(p_ref, w_ref, sc_ref, o_ref, pad_ref, *, seq_len, grid_mode):
    PT = POOL_TILE
    n_tiles = seq_len // PT
    ii = lax.broadcasted_iota(jnp.int32, (PT, PT), 0)
    jj = lax.broadcasted_iota(jnp.int32, (PT, PT), 1)
    d = jj - ii
    ti = lax.broadcasted_iota(jnp.int32, (PT, 1), 0)

    if grid_mode:
        zeros = jnp.zeros((POOL_PAD, POOL_DG), F32)
        pad_ref[0:POOL_PAD, :] = zeros
        pad_ref[POOL_PAD + seq_len:POOL_PAD + seq_len + POOL_PAD, :] = zeros

    for gi, w in enumerate(POOL_WINDOWS):
        lo_off = -(w // 2)
        hi_off = w - w // 2 - 1
        cols = slice(gi * POOL_DG, (gi + 1) * POOL_DG)
        in_window = (d >= lo_off) & (d <= hi_off)
        if grid_mode:
            in_window = in_window & ((ii // GRID_W) == (jj // GRID_W))
        band = jnp.where(in_window, 1.0, 0.0).astype(BF16)
        wg = w_ref[gi]
        scale = sc_ref[gi]

        def window_sum(t0):
            x = p_ref[pl.ds(t0, PT), cols]
            s3 = jnp.dot(band, _split3(x), preferred_element_type=F32)
            return s3[:, 0:POOL_DG] + s3[:, POOL_DG:2 * POOL_DG] + s3[:, 2 * POOL_DG:3 * POOL_DG]

        def finish(t0, s, cnt):
            x = p_ref[pl.ds(t0, PT), cols]
            diff = (s / cnt - x).astype(BF16)
            y = jnp.dot(diff, wg, preferred_element_type=F32) * scale
            o_ref[pl.ds(t0, PT), cols] = y.astype(BF16)

        def count(pos, n):
            lo = jnp.maximum(pos + lo_off, 0)
            hi = jnp.minimum(pos + lo_off + w, n)
            return (hi - lo).astype(F32)

        if grid_mode:
            for t in range(n_tiles):
                pad_ref[POOL_PAD + t * PT:POOL_PAD + (t + 1) * PT, :] = window_sum(t * PT)
            for t in range(n_tiles):
                base = POOL_PAD + t * PT
                s = pad_ref[base + lo_off * GRID_W:base + lo_off * GRID_W + PT, :]
                for r in range(lo_off + 1, hi_off + 1):
                    s = s + pad_ref[base + r * GRID_W:base + r * GRID_W + PT, :]
                tok = ti + t * PT
                cnt = count(tok // GRID_W, seq_len // GRID_W) * count(tok % GRID_W, GRID_W)
                finish(t * PT, s, cnt)
        else:
            for t in range(n_tiles):
                finish(t * PT, window_sum(t * PT), count(ti + t * PT, seq_len))


def _pooling(proj, row0, B, L, w_pool_bf16, pool_scale, grid_mode):
    if not grid_mode:
        assert L == POOL_TILE
    rb = row0 // L
    pcol = (IN_WIDTH - POOL_WIDTH) // POOL_WIDTH
    kern = functools.partial(_pool_kernel, seq_len=L, grid_mode=grid_mode)
    return pl.pallas_call(
        kern,
        grid=(B,),
        in_specs=[pl.BlockSpec((L, POOL_WIDTH), lambda b: (rb + b, pcol)),
                  pl.BlockSpec((POOL_GROUPS, POOL_DG, POOL_DG), lambda b: (0, 0, 0)),
                  pl.BlockSpec((POOL_GROUPS, 1, POOL_DG), lambda b: (0, 0, 0))],
        out_specs=pl.BlockSpec((L, POOL_WIDTH), lambda b: (b, 0)),
        out_shape=jax.ShapeDtypeStruct((B * L, POOL_WIDTH), BF16),
        scratch_shapes=[pltpu.VMEM((L + 2 * POOL_PAD, POOL_DG), F32)],
        compiler_params=_cparams(("arbitrary",), 48 * 1024 * 1024),
        name="pool_grid" if grid_mode else "pool_seq",
    )(proj, w_pool_bf16, pool_scale.reshape(POOL_GROUPS, 1, POOL_DG))


def _outproj_kernel(*refs):
    i = pl.program_id(0)
    h2_ref = refs[14]

    @pl.when(i >= T_ALL // TM_OUT)
    def _():
        h2_ref[...] = jnp.zeros_like(h2_ref)

    pl.when(i < T_ALL // TM_OUT)(lambda: _outproj_tile(*refs))


def _outproj_tile(retc_ref, retl_ref, poolc_ref, pooll_ref, xc_ref, xl_ref, mod_ref, g_ref,
                  wt_ref, wb_ref, wrh_ref, wrb_ref, br_ref,
                  x1_ref, h2_ref, ri_ref, rp_ref, rr_ref, cnt_ref, run_ref):
    i = pl.program_id(0)

    @pl.when(i == 0)
    def _():
        run_ref[...] = jnp.zeros_like(run_ref)

    is_ctx = i < T_CTX // TM_OUT
    ret = jnp.where(is_ctx, retc_ref[...], retl_ref[...])
    pool = jnp.where(is_ctx, poolc_ref[...], pooll_ref[...])
    x = jnp.where(is_ctx, xc_ref[...], xl_ref[...])
    y = (jnp.dot(ret, wt_ref[...], preferred_element_type=F32)
         + jnp.dot(pool, wb_ref[...], preferred_element_type=F32))
    x1 = x + mod_ref[0, 2:3, :] * y
    x1_ref[...] = x1
    hn = x1 * lax.rsqrt(jnp.mean(x1 * x1, axis=-1, keepdims=True) + NORM_EPS) * g_ref[...]
    h2 = hn * (1.0 + mod_ref[0, 4:5, :]) + mod_ref[0, 3:4, :]
    hi = h2.astype(BF16)
    h2_ref[...] = hi
    lo = (h2 - hi.astype(F32)).astype(BF16)
    both = jnp.dot(hi, wrb_ref[...], preferred_element_type=F32)
    logits = (both[:, 0:LANES] + jnp.dot(lo, wrh_ref[...], preferred_element_type=F32)
              + both[:, LANES:2 * LANES]) + br_ref[...]

    tm = logits.shape[0]
    lane = lax.broadcasted_iota(jnp.int32, (tm, LANES), 1)
    work = logits
    vals, idxs, hots = [], [], []
    for _ in range(TOP_K):
        m = jnp.max(work, axis=-1, keepdims=True)
        idx = jnp.min(jnp.where(work == m, lane, LANES), axis=-1, keepdims=True)
        hot = lane == idx
        vals.append(m)
        idxs.append(idx)
        hots.append(hot)
        work = jnp.where(hot, -jnp.inf, work)
    exps = [jnp.exp(v - vals[0]) for v in vals]
    denom = exps[0] + exps[1] + exps[2] + exps[3]

    selected = jnp.zeros((tm, LANES), F32)
    for hot in hots:
        selected = selected + jnp.where(hot, 1.0, 0.0)
    r_i = lax.broadcasted_iota(jnp.int32, (tm, tm), 0)
    c_i = lax.broadcasted_iota(jnp.int32, (tm, tm), 1)
    tri = jnp.where(c_i < r_i, 1.0, 0.0).astype(BF16)
    before = jnp.dot(tri, selected.astype(BF16), preferred_element_type=F32) + run_ref[0:1, :]

    ri = jnp.zeros((tm, LANES), jnp.int32)
    rp = jnp.zeros((tm, LANES), F32)
    rr = jnp.zeros((tm, LANES), jnp.int32)
    for k in range(TOP_K):
        rank = jnp.sum(jnp.where(hots[k], before, 0.0), axis=-1, keepdims=True).astype(jnp.int32)
        ri = jnp.where(lane == k, idxs[k], ri)
        rp = jnp.where(lane == k, exps[k] / denom, rp)
        rr = jnp.where(lane == k, rank, rr)
    ri_ref[...] = ri
    rp_ref[...] = rp
    rr_ref[...] = rr

    run = run_ref[0:1, :] + jnp.sum(selected, axis=0, keepdims=True)
    run_ref[...] = jnp.broadcast_to(run, run_ref.shape)
    cnt_ref[...] = jnp.broadcast_to(run, cnt_ref.shape).astype(jnp.int32)


def _out_projection(ret_ctx, ret_lat, pool_ctx, pool_lat, x_ctx, x_lat, mod3, norm_g, w_out_bf16,
                    wr_hi, wr_both, b_router_pad):
    tm = TM_OUT
    real = lambda i: jnp.minimum(i, T_ALL // tm - 1)
    row = lambda i: (real(i), 0)
    const = lambda i: (0, 0)
    crow = lambda i: (_ctx_tile(real(i), tm), 0)
    lrow = lambda i: (_lat_tile(real(i), tm), 0)
    return pl.pallas_call(
        _outproj_kernel,
        grid=(H2_ROWS // tm,),
        in_specs=[pl.BlockSpec((tm, RET_WIDTH), crow),
                  pl.BlockSpec((tm, RET_WIDTH), lrow),
                  pl.BlockSpec((tm, POOL_WIDTH), crow),
                  pl.BlockSpec((tm, POOL_WIDTH), lrow),
                  pl.BlockSpec((tm, D_MODEL), crow),
                  pl.BlockSpec((tm, D_MODEL), lrow),
                  pl.BlockSpec((1, 6, D_MODEL), lambda i: (_cond_row(real(i), tm), 0, 0)),
                  pl.BlockSpec((1, D_MODEL), const),
                  pl.BlockSpec((RET_WIDTH, D_MODEL), const),
                  pl.BlockSpec((POOL_WIDTH, D_MODEL), lambda i: (1, 0)),
                  pl.BlockSpec((D_MODEL, LANES), const),
                  pl.BlockSpec((D_MODEL, 2 * LANES), const),
                  pl.BlockSpec((1, LANES), const)],
        out_specs=[pl.BlockSpec((tm, D_MODEL), row),
                   pl.BlockSpec((tm, D_MODEL), lambda i: (i, 0)),
                   pl.BlockSpec((tm, LANES), row),
                   pl.BlockSpec((tm, LANES), row),
                   pl.BlockSpec((tm, LANES), row),
                   pl.BlockSpec((8, LANES), const)],
        out_shape=[jax.ShapeDtypeStruct((T_ALL, D_MODEL), F32),
                   jax.ShapeDtypeStruct((H2_ROWS, D_MODEL), BF16),
                   jax.ShapeDtypeStruct((T_ALL, LANES), jnp.int32),
                   jax.ShapeDtypeStruct((T_ALL, LANES), F32),
                   jax.ShapeDtypeStruct((T_ALL, LANES), jnp.int32),
                   jax.ShapeDtypeStruct((8, LANES), jnp.int32)],
        scratch_shapes=[pltpu.VMEM((8, LANES), F32)],
        compiler_params=_cparams(("arbitrary",), 48 * 1024 * 1024),
        name="out_projection_router",
    )(ret_ctx, ret_lat, pool_ctx, pool_lat, x_ctx, x_lat, mod3, norm_g.reshape(1, D_MODEL),
      w_out_bf16, w_out_bf16, wr_hi, wr_both, b_router_pad)


def _count_le(ends, v):
    return jnp.sum((ends[None, :] <= v[:, None]).astype(jnp.int32), axis=1)


def _expert_schedule(counts):
    i32 = jnp.int32
    ntile = (counts + TM_EXP - 1) // TM_EXP
    tile_off = jnp.cumsum(ntile) - ntile
    nsup = (ntile + SUP_EXP - 1) // SUP_EXP
    sup_end = jnp.cumsum(nsup)
    u = jnp.arange(U_EXP, dtype=i32)
    sup_valid = u < sup_end[-1]
    uc = jnp.minimum(u, sup_end[-1] - 1)
    sup_e = jnp.minimum(_count_le(sup_end, uc), N_EXPERTS - 1)
    sup_idx = uc - (sup_end - nsup)[sup_e]
    per_sup = (ntile + jnp.maximum(nsup, 1) - 1) // jnp.maximum(nsup, 1)
    sup_row0 = (tile_off[sup_e] + sup_idx * per_sup[sup_e]) * TM_EXP
    sup_nt = jnp.clip(ntile[sup_e] - sup_idx * per_sup[sup_e], 1, per_sup[sup_e])
    step_e = jnp.repeat(sup_e, J_EXP)
    step_j = jnp.tile(jnp.arange(J_EXP, dtype=i32), U_EXP)
    step_row0 = jnp.repeat(sup_row0, J_EXP)
    step_nt = jnp.repeat(sup_nt, J_EXP)
    last_rows = counts - (ntile - 1) * TM_EXP
    sup_half = (sup_idx == nsup[sup_e] - 1) & (last_rows[sup_e] <= TM_EXP // 2) & (counts[sup_e] > 0)
    step_half = jnp.repeat(sup_half.astype(i32), J_EXP)
    bounds = jnp.asarray(CH_START_EXP[1:], dtype=i32)
    before = jnp.sum((sup_valid[None, :] & (sup_row0[None, :] < bounds[:, None])).astype(i32), axis=1) * J_EXP
    n_steps = jnp.sum(sup_valid.astype(i32)) * J_EXP
    chunk_start = jnp.concatenate([jnp.zeros((1,), i32), before])
    chunk_steps = jnp.concatenate([before, n_steps.reshape(1)]) - chunk_start
    tables = (step_e.astype(i32), step_j.astype(i32), step_row0.astype(i32), step_nt.astype(i32), step_half)
    return tile_off * TM_EXP, tables, chunk_start, chunk_steps, jnp.sum(ntile).astype(i32)


def _expert_kernel(se_ref, sj_ref, srow_ref, snt_ref, shalf_ref, meta_ref,
                   xs_hbm, wg_ref, wu_ref, bg_ref, bu_ref, wd_ref, bd_ref, *rest, first_chunk):
    ys_hbm, acc_ref, x_buf, o_buf, x_sem, o_sem = rest if first_chunk else rest[1:]
    s = pl.program_id(0) + meta_ref[0]
    j = sj_ref[s]
    n_tiles = snt_ref[s]
    row0 = pl.multiple_of(srow_ref[s], TM_EXP)
    last = J_EXP - 1

    def x_copy():
        rows = pl.ds(pl.multiple_of(row0 - meta_ref[1], TM_EXP), SUP_EXP * TM_EXP)
        return pltpu.make_async_copy(xs_hbm.at[rows], x_buf, x_sem)

    def out_copy(t):
        rows = pl.ds(pl.multiple_of(row0 + t * TM_EXP, TM_EXP), TM_EXP)
        return pltpu.make_async_copy(o_buf.at[t % 2], ys_hbm.at[rows], o_sem.at[t % 2])

    @pl.when((pl.program_id(0) == 0) & first_chunk)
    def _():
        o_buf[0] = jnp.zeros((TM_EXP, D_MODEL), BF16)

        def zero_copy(t):
            rows = pl.ds(pl.multiple_of(t * TM_EXP, TM_EXP), TM_EXP)
            return pltpu.make_async_copy(o_buf.at[0], ys_hbm.at[rows], o_sem.at[0])

        def start(t, carry):
            zero_copy(t).start()
            return carry

        def wait(t, carry):
            zero_copy(t).wait()
            return carry

        lax.fori_loop(meta_ref[2], NT_EXP, start, 0)
        lax.fori_loop(meta_ref[2], NT_EXP, wait, 0)

    def step():
        @pl.when(j == 0)
        def _():
            x_copy().start()
            acc_ref[...] = jnp.zeros_like(acc_ref)
            x_copy().wait()

        def accumulate(rows):
            x = x_buf[rows, :]
            gate = _dot(x, wg_ref[0]) + bg_ref[0]
            up = _dot(x, wu_ref[0]) + bu_ref[0]
            gate = jnp.minimum(gate, SWIGLU_LIMIT)
            up = jnp.clip(up, -SWIGLU_LIMIT, SWIGLU_LIMIT)
            act = (up + 1.0) * gate * _sigmoid(SWIGLU_ALPHA * gate)
            acc_ref[rows, :] += _dot(act, wd_ref[0])

        def tile_pair(p, carry):
            base = pl.multiple_of(p * (2 * TM_EXP), 2 * TM_EXP)
            accumulate(pl.ds(base, TM_EXP))
            accumulate(pl.ds(base + TM_EXP, TM_EXP))
            return carry

        n_full = n_tiles - shalf_ref[s]
        lax.fori_loop(0, n_full // 2, tile_pair, 0)

        @pl.when(n_full % 2 == 1)
        def _():
            accumulate(pl.ds(pl.multiple_of((n_full - 1) * TM_EXP, TM_EXP), TM_EXP))

        @pl.when(shalf_ref[s] == 1)
        def _():
            accumulate(pl.ds(pl.multiple_of((n_tiles - 1) * TM_EXP, TM_EXP), TM_EXP // 2))

        @pl.when(j == last)
        def _():
            def finish(t, carry):
                @pl.when(t >= 2)
                def _():
                    out_copy(t - 2).wait()

                rows = pl.ds(pl.multiple_of(t * TM_EXP, TM_EXP), TM_EXP)
                o_buf[t % 2] = (acc_ref[rows, :] + bd_ref[0]).astype(BF16)
                out_copy(t).start()
                return carry

            lax.fori_loop(0, n_tiles, finish, 0)

            @pl.when(n_tiles >= 2)
            def _():
                out_copy(n_tiles - 2).wait()

            out_copy(n_tiles - 1).wait()

    step()


def _experts_chunk(chunk, xs_part, ys, tables, meta, n_steps, w_gate_up, b_gate_up, w_down, b_down):
    first = ys is None
    wmap = lambda col0: (lambda s, se, sj, srow, snt, shalf, meta: (se[s + meta[0]], 0, col0 + sj[s + meta[0]]))
    n_prefetch = len(tables) + 1
    in_specs = [
        pl.BlockSpec(memory_space=pl.ANY),
        pl.BlockSpec((1, D_MODEL, F_EXP), wmap(0)),
        pl.BlockSpec((1, D_MODEL, F_EXP), wmap(J_EXP)),
        pl.BlockSpec((1, 1, F_EXP), wmap(0)),
        pl.BlockSpec((1, 1, F_EXP), wmap(J_EXP)),
        pl.BlockSpec((1, F_EXP, D_MODEL),
                     lambda s, se, sj, srow, snt, shalf, meta: (se[s + meta[0]], sj[s + meta[0]], 0)),
        pl.BlockSpec((1, 1, D_MODEL), lambda s, se, sj, srow, snt, shalf, meta: (se[s + meta[0]], 0, 0)),
    ]
    args = [xs_part, w_gate_up, w_gate_up,
            b_gate_up.reshape(N_EXPERTS, 1, 2 * D_FF), b_gate_up.reshape(N_EXPERTS, 1, 2 * D_FF),
            w_down, b_down.reshape(N_EXPERTS, 1, D_MODEL)]
    aliases = {}
    if not first:
        in_specs.append(pl.BlockSpec(memory_space=pl.ANY))
        args.append(ys)
        aliases = {n_prefetch + len(args) - 1: 0}
    return pl.pallas_call(
        functools.partial(_expert_kernel, first_chunk=first),
        grid_spec=pltpu.PrefetchScalarGridSpec(
            num_scalar_prefetch=n_prefetch,
            grid=(n_steps,),
            in_specs=in_specs,
            out_specs=pl.BlockSpec(memory_space=pl.ANY),
            scratch_shapes=[pltpu.VMEM((SUP_EXP * TM_EXP, D_MODEL), F32),
                            pltpu.VMEM((SUP_EXP * TM_EXP, D_MODEL), BF16),
                            pltpu.VMEM((2, TM_EXP, D_MODEL), BF16),
                            pltpu.SemaphoreType.DMA(()),
                            pltpu.SemaphoreType.DMA((2,))]),
        out_shape=jax.ShapeDtypeStruct((NT_EXP * TM_EXP, D_MODEL), BF16),
        input_output_aliases=aliases,
        compiler_params=_cparams(("arbitrary",), VMEM_LIMIT),
        name=f"experts_chunk{chunk}",
    )(*tables, meta, *args)


def _combine_kernel(x1_ref, y0_ref, y1_ref, y2_ref, y3_ref, rp_ref, mod_ref, g_ref, o_ref):
    rp = rp_ref[...]
    moe = rp[:, 0:1] * y0_ref[...].astype(F32)
    for k, y_ref in enumerate((y1_ref, y2_ref, y3_ref), start=1):
        moe = moe + rp[:, k:k + 1] * y_ref[...].astype(F32)
    x2 = x1_ref[...] + mod_ref[0, 5:6, :] * moe
    o_ref[...] = x2 * lax.rsqrt(jnp.mean(x2 * x2, axis=-1, keepdims=True) + NORM_EPS) * g_ref[...]


def _combine(x1, y4, route_p, mod3, final_g, row0, n_rows):
    tm = TM_CMB
    nt = n_rows // tm
    t0 = row0 // tm
    y_specs = [pl.BlockSpec((tm, D_MODEL), functools.partial(lambda k, i: (k * nt + i, 0), k))
               for k in range(TOP_K)]
    return pl.pallas_call(
        _combine_kernel,
        grid=(nt,),
        in_specs=[pl.BlockSpec((tm, D_MODEL), lambda i: (t0 + i, 0))] + y_specs + [
            pl.BlockSpec((tm, LANES), lambda i: (t0 + i, 0)),
            pl.BlockSpec((1, 6, D_MODEL), lambda i: (_cond_row(t0 + i, tm), 0, 0)),
            pl.BlockSpec((1, D_MODEL), lambda i: (0, 0))],
        out_specs=pl.BlockSpec((tm, D_MODEL), lambda i: (i, 0)),
        out_shape=jax.ShapeDtypeStruct((n_rows, D_MODEL), F32),
        compiler_params=_cparams(("arbitrary",), 40 * 1024 * 1024),
        name="combine_final_norm",
    )(x1, y4, y4, y4, y4, route_p, mod3, final_g.reshape(1, D_MODEL))


def kernel(x_prompt, x_sample, c, c_ctx, state_ret, norm1_g, norm2_g, w_mod, b_mod, w_in, ret_decay,
           w_pool, pool_scale, w_out, w_router, b_router, w_gate_up, b_gate_up, w_down, b_down, final_g):
    assert w_mod.shape[0] == 1, "single trunk layer"
    x_ctx = x_prompt.reshape(T_CTX, D_MODEL)
    x_lat = x_sample.reshape(T_LAT, D_MODEL)
    cond = jnp.zeros((COND_ROWS, D_MODEL), F32).at[0].set(c_ctx).at[1:N_COND].set(c)

    mod = _modulation(cond, w_mod[0], b_mod[0])
    mod3 = mod.reshape(COND_ROWS, 6, D_MODEL)

    proj = _in_projection(x_ctx, x_lat, norm1_g[0], mod3, w_in[0].astype(BF16))

    ret_ctx, new_state = _retention(proj, 0, BATCH, SEQ, ret_decay[0], None, None, True, RET_HEADS)
    ret_lat, _ = _retention(proj, T_CTX, DEC_BATCH, DEC_SEQ, ret_decay[0], state_ret[:, 0],
                            _rope_tables(DEC_SEQ), False, 2)
    w_pool_bf16 = w_pool[0].astype(BF16)
    pool_ctx = _pooling(proj, 0, BATCH, SEQ, w_pool_bf16, pool_scale[0], False)
    pool_lat = _pooling(proj, T_CTX, DEC_BATCH, DEC_SEQ, w_pool_bf16, pool_scale[0], True)

    wr = jnp.zeros((D_MODEL, LANES), F32).at[:, :N_EXPERTS].set(w_router[0])
    wr_hi = wr.astype(BF16)
    wr_both = jnp.concatenate([wr_hi, (wr - wr_hi.astype(F32)).astype(BF16)], axis=1)
    br = jnp.full((1, LANES), -1e30, F32).at[0, :N_EXPERTS].set(b_router[0])
    x1, h2, route_i, route_p, route_r, counts = _out_projection(
        ret_ctx, ret_lat, pool_ctx, pool_lat, x_ctx, x_lat, mod3, norm2_g[0], w_out[0].astype(BF16),
        wr_hi, wr_both, br)

    row_off, tables, chunk_start, chunk_steps, n_row_tiles = _expert_schedule(counts[0, :N_EXPERTS])
    slot = (row_off[route_i[:, :TOP_K]] + route_r[:, :TOP_K]).T
    token = jnp.broadcast_to(jnp.arange(T_ALL, dtype=jnp.int32)[None, :], (TOP_K, T_ALL))
    token_of_slot = (jnp.arange(XS_ROWS, dtype=jnp.int32) % T_ALL).at[slot.reshape(-1)].set(
        token.reshape(-1), unique_indices=True, mode="promise_in_bounds")
    ys = None
    for c in range(CHUNKS_EXP):
        lo = CH_START_EXP[c]
        nxt = CH_START_EXP[c + 1] if c + 1 < CHUNKS_EXP else XS_ROWS
        hi = min(nxt + SUP_EXP * TM_EXP, XS_ROWS)
        xs_part = h2.at[token_of_slot[lo:hi]].get(mode="promise_in_bounds")
        meta = jnp.stack([chunk_start[c], jnp.int32(lo), n_row_tiles]).astype(jnp.int32)
        ys = _experts_chunk(c, xs_part, ys, tables, meta, chunk_steps[c],
                            w_gate_up[0], b_gate_up[0], w_down[0], b_down[0])
    y4_lat = ys.at[slot[:, T_CTX:].reshape(-1)].get(mode="promise_in_bounds", unique_indices=True)
    y4_ctx = ys.at[slot[:, :T_CTX].reshape(-1)].get(mode="promise_in_bounds", unique_indices=True)
    y_lat = _combine(x1, y4_lat, route_p, mod3, final_g, T_CTX, T_LAT)
    y_ctx = _combine(x1, y4_ctx, route_p, mod3, final_g, 0, T_CTX)
    return (y_ctx.reshape(BATCH, SEQ, D_MODEL), y_lat.reshape(DEC_BATCH, DEC_SEQ, D_MODEL),
            new_state.reshape(BATCH, 1, 2, RET_HEADS, RET_DK, RET_DV))
```

```python
import functools

import numpy as np
import jax
import jax.numpy as jnp
from jax import lax
from jax.experimental import pallas as pl
from jax.experimental.pallas import tpu as pltpu

F32 = jnp.float32
BF16 = jnp.bfloat16

D_MODEL = 2048
BATCH = 16
SEQ = 256
DEC_BATCH = 4
DEC_SEQ = 2048
GRID_W = 64
RET_HEADS = 4
RET_DK = 128
RET_DV = 256
RET_QK_WIDTH = RET_HEADS * RET_DK
RET_WIDTH = RET_HEADS * RET_DV
POOL_WINDOWS = (2, 4, 8, 16)
POOL_GROUPS = 4
POOL_DG = 256
POOL_WIDTH = POOL_GROUPS * POOL_DG
IN_WIDTH = 2 * RET_QK_WIDTH + 2 * RET_WIDTH + POOL_WIDTH
N_EXPERTS = 32
TOP_K = 4
D_FF = D_MODEL
SWIGLU_LIMIT = 7.0
SWIGLU_ALPHA = 1.702
ROPE_BASE = 10000.0
NORM_EPS = 1e-6
GN_EPS = 1e-6

T_CTX = BATCH * SEQ
T_LAT = DEC_BATCH * DEC_SEQ
T_ALL = T_CTX + T_LAT
N_COND = 1 + DEC_BATCH
COND_ROWS = 8
LANES = 128

RET_CHUNK = 256

TM_IN = 512
TN_IN = 1024
TM_OUT = 256
TM_CMB = 256
H2_ROWS = 16384
POOL_TILE = 256
POOL_PAD = (max(POOL_WINDOWS) // 2) * GRID_W

TM_EXP = 256
F_EXP = 512
J_EXP = D_FF // F_EXP
SUP_EXP = 8
NT_EXP = (T_ALL * TOP_K + N_EXPERTS * (TM_EXP - 1)) // TM_EXP
U_EXP = N_EXPERTS + NT_EXP // SUP_EXP + 1
XS_ROWS = (NT_EXP + SUP_EXP) * TM_EXP
CH_START_EXP = tuple(t * TM_EXP for t in (0, 24))
CHUNKS_EXP = len(CH_START_EXP)

VMEM_LIMIT = 56 * 1024 * 1024


def _cparams(sem, vmem=None):
    return pltpu.CompilerParams(dimension_semantics=sem, vmem_limit_bytes=vmem)


def _cond_row(i, tm):
    nctx = T_CTX // tm
    return jnp.where(i < nctx, 0, 1 + (i - nctx) // (DEC_SEQ // tm))


def _sigmoid(x):
    return 1.0 / (1.0 + jnp.exp(-x))


def _dot(a, b):
    return lax.dot_general(a, b, (((1,), (0,)), ((), ())), preferred_element_type=F32)


def _mod_kernel(c_ref, w_ref, b_ref, o_ref):
    c = c_ref[...]
    s = (c * _sigmoid(c)).astype(BF16)
    o_ref[...] = jnp.dot(s, w_ref[...].astype(BF16), preferred_element_type=F32) + b_ref[...]


def _modulation(cond, w_mod, b_mod):
    n = w_mod.shape[1]
    tn = 1024
    return pl.pallas_call(
        _mod_kernel,
        grid=(n // tn,),
        in_specs=[pl.BlockSpec((COND_ROWS, D_MODEL), lambda j: (0, 0)),
                  pl.BlockSpec((D_MODEL, tn), lambda j: (0, j)),
                  pl.BlockSpec((1, tn), lambda j: (0, j))],
        out_specs=pl.BlockSpec((COND_ROWS, tn), lambda j: (0, j)),
        out_shape=jax.ShapeDtypeStruct((COND_ROWS, n), F32),
        compiler_params=_cparams(("arbitrary",), 40 * 1024 * 1024),
        name="modulation",
    )(cond, w_mod, b_mod.reshape(1, n))


def _ctx_tile(i, tm):
    return jnp.minimum(i, T_CTX // tm - 1)


def _lat_tile(i, tm):
    return jnp.maximum(i - T_CTX // tm, 0)


def _inproj_kernel(xc_ref, xl_ref, g_ref, mod_ref, w_ref, o_ref, h_ref):
    def modulated_norm(x_ref):
        x = x_ref[...]
        y = x * lax.rsqrt(jnp.mean(x * x, axis=-1, keepdims=True) + NORM_EPS) * g_ref[...]
        shift = mod_ref[0, 0:1, :]
        scale = mod_ref[0, 1:2, :]
        h_ref[...] = (y * (1.0 + scale) + shift).astype(BF16)

    first = pl.program_id(1) == 0
    is_ctx = pl.program_id(0) < T_CTX // TM_IN
    pl.when(first & is_ctx)(lambda: modulated_norm(xc_ref))
    pl.when(first & jnp.logical_not(is_ctx))(lambda: modulated_norm(xl_ref))

    o_ref[...] = jnp.dot(h_ref[...], w_ref[...], preferred_element_type=F32)


def _in_projection(x_ctx, x_lat, norm_g, mod3, w_in_bf16):
    return pl.pallas_call(
        _inproj_kernel,
        grid=(T_ALL // TM_IN, IN_WIDTH // TN_IN),
        in_specs=[pl.BlockSpec((TM_IN, D_MODEL), lambda i, j: (_ctx_tile(i, TM_IN), 0)),
                  pl.BlockSpec((TM_IN, D_MODEL), lambda i, j: (_lat_tile(i, TM_IN), 0)),
                  pl.BlockSpec((1, D_MODEL), lambda i, j: (0, 0)),
                  pl.BlockSpec((1, 6, D_MODEL), lambda i, j: (_cond_row(i, TM_IN), 0, 0)),
                  pl.BlockSpec((D_MODEL, TN_IN), lambda i, j: (0, j))],
        out_specs=pl.BlockSpec((TM_IN, TN_IN), lambda i, j: (i, j)),
        out_shape=jax.ShapeDtypeStruct((T_ALL, IN_WIDTH), F32),
        scratch_shapes=[pltpu.VMEM((TM_IN, D_MODEL), BF16)],
        compiler_params=_cparams(("arbitrary", "arbitrary"), 40 * 1024 * 1024),
        name="in_projection",
    )(x_ctx, x_lat, norm_g.reshape(1, D_MODEL), mod3, w_in_bf16)


def _rope_tables(seq_len):
    t = jnp.arange(seq_len)
    row = (t // GRID_W).astype(F32)
    col = (t % GRID_W).astype(F32)
    half = RET_DK // 2
    n_freq = half // 2
    inv = ROPE_BASE ** (-jnp.arange(n_freq, dtype=F32) / n_freq)
    ang_r = row[:, None] * inv
    ang_c = col[:, None] * inv
    cos = jnp.concatenate([jnp.cos(ang_r), jnp.cos(ang_r), jnp.cos(ang_c), jnp.cos(ang_c)], axis=-1)
    sin = jnp.concatenate([-jnp.sin(ang_r), jnp.sin(ang_r), -jnp.sin(ang_c), jnp.sin(ang_c)], axis=-1)
    return cos, sin


def _retention_kernel(rd_ref, *refs, seq_len, chunk, use_rope, has_state_in, has_state_out, heads):
    refs = list(refs)
    q_ref, k_ref, v_ref, g_ref = refs[:4]
    pos = 4
    if has_state_in:
        s0_ref = refs[pos]
        pos += 1
    if use_rope:
        cos_ref, sin_ref = refs[pos], refs[pos + 1]
        pos += 2
    o_ref = refs[pos]
    pos += 1
    if has_state_out:
        st_ref = refs[pos]
        pos += 1
    scratch = refs[pos:pos + 5]

    progs = [_retention_head(rd_ref, pl.program_id(1) * heads + hh, hh,
                             q_ref, k_ref, v_ref, g_ref,
                             s0_ref if has_state_in else None,
                             (cos_ref, sin_ref) if use_rope else None,
                             o_ref, st_ref if has_state_out else None,
                             [r.at[hh] for r in scratch], seq_len, chunk)
             for hh in range(heads)]
    n_chunks = seq_len // chunk
    if n_chunks == 1:
        for fwd_chunk, _, _ in progs:
            fwd_chunk(0)
        for _, bwd_chunk, _ in progs:
            bwd_chunk(0)
    else:
        assert n_chunks % 2 == 0

        def fwd_pair(i, carry):
            for fwd_chunk, _, _ in progs:
                fwd_chunk(2 * i)
                fwd_chunk(2 * i + 1)
            return carry

        def bwd_pair(i, carry):
            for _, bwd_chunk, _ in progs:
                bwd_chunk(n_chunks - 1 - 2 * i)
                bwd_chunk(n_chunks - 2 - 2 * i)
            return carry

        lax.fori_loop(0, n_chunks // 2, fwd_pair, 0)
        lax.fori_loop(0, n_chunks // 2, bwd_pair, 0)
    for _, _, finish in progs:
        finish()


def _retention_head(rd_ref, h, hh, q_ref, k_ref, v_ref, g_ref, s0_ref, rope_refs, o_ref, st_ref, scratch,
                    seq_len, chunk):
    qs_ref, ks_ref, acc_ref, sf_ref, sb_ref = scratch
    use_rope = rope_refs is not None
    kcols = slice(hh * RET_DK, (hh + 1) * RET_DK)
    vcols = slice(hh * RET_DV, (hh + 1) * RET_DV)
    C = chunk
    n_chunks = seq_len // C

    lgf = -jnp.exp(jnp.full((C, 1), rd_ref[0, h], F32))
    lgb = -jnp.exp(jnp.full((C, 1), rd_ref[1, h], F32))
    ii = lax.broadcasted_iota(jnp.int32, (C, C), 0)
    jj = lax.broadcasted_iota(jnp.int32, (C, C), 1)
    diff = (ii - jj).astype(F32)
    decay = (jnp.where(diff >= 0, jnp.exp(lgf * jnp.maximum(diff, 0.0)), 0.0)
             + jnp.where(diff <= 0, jnp.exp(lgb * jnp.maximum(-diff, 0.0)), 0.0))
    p = lax.broadcasted_iota(jnp.int32, (C, 1), 0).astype(F32)
    xi_f = jnp.exp(lgf * (p + 1.0))
    zeta_f = jnp.exp(lgf * (C - 1.0 - p))
    xi_b = jnp.exp(lgb * (C - p))
    zeta_b = jnp.exp(lgb * p)
    cd_f = jnp.exp(lgf[0:1, :] * C)
    cd_b = jnp.exp(lgb[0:1, :] * C)

    if s0_ref is not None:
        sf_ref[...] = s0_ref[0, 0, hh]
        sb_ref[...] = s0_ref[0, 1, hh]
    else:
        sf_ref[...] = jnp.zeros_like(sf_ref)
        sb_ref[...] = jnp.zeros_like(sb_ref)

    lane = lax.broadcasted_iota(jnp.int32, (C, RET_DK), 1)
    first_half = (lane & 32) == 0

    def rope(x, cs, sn):
        swapped = jnp.where(first_half, pltpu.roll(x, RET_DK - 32, axis=1), pltpu.roll(x, 32, axis=1))
        return x * cs + swapped * sn

    def state_update(s_ref, kz, v, cd):
        upd = lax.dot_general(kz, v, (((0,), (0,)), ((), ())), preferred_element_type=F32)
        s_ref[...] = cd * s_ref[...] + upd

    def chunk_rows(c):
        start = c * C
        return pl.ds(start if isinstance(start, int) else pl.multiple_of(start, C), C)

    def fwd_chunk(c):
        sl = chunk_rows(c)
        q = q_ref[sl, kcols]
        k = k_ref[sl, kcols] * (RET_DK ** -0.5)
        if use_rope:
            cs = rope_refs[0][sl, :]
            sn = rope_refs[1][sl, :]
            q = rope(q, cs, sn)
            k = rope(k, cs, sn)
        qb = q.astype(BF16)
        kb = k.astype(BF16)
        qs_ref[sl, :] = qb
        ks_ref[sl, :] = k
        v = v_ref[sl, vcols].astype(BF16)
        scores = lax.dot_general(qb, kb, (((1,), (1,)), ((), ())), preferred_element_type=F32)
        inner = jnp.dot((scores * decay).astype(BF16), v, preferred_element_type=F32)
        cross = jnp.dot(qb, sf_ref[...].astype(BF16), preferred_element_type=F32) * xi_f
        acc_ref[sl, :] = inner + cross
        state_update(sf_ref, (k * zeta_f).astype(BF16), v, cd_f)

    def bwd_chunk(c):
        sl = chunk_rows(c)
        qb = qs_ref[sl, :]
        k = ks_ref[sl, :]
        v = v_ref[sl, vcols].astype(BF16)
        cross = jnp.dot(qb, sb_ref[...].astype(BF16), preferred_element_type=F32) * xi_b
        o = acc_ref[sl, :] + cross
        mu = jnp.mean(o, axis=-1, keepdims=True)
        oc = o - mu
        var = jnp.mean(oc * oc, axis=-1, keepdims=True)
        on = oc * lax.rsqrt(var + GN_EPS)
        g = g_ref[sl, vcols]
        o_ref[sl, vcols] = (on * (g * _sigmoid(g))).astype(BF16)
        state_update(sb_ref, (k * zeta_b).astype(BF16), v, cd_b)

    def finish():
        if st_ref is not None:
            st_ref[0, 0, hh] = sf_ref[...]
            st_ref[0, 1, hh] = sb_ref[...]

    return fwd_chunk, bwd_chunk, finish


def _retention(proj, row0, B, L, ret_decay, state_in, rope_tabs, want_state, heads):
    rb = row0 // L
    use_rope = rope_tabs is not None
    has_state_in = state_in is not None
    wk = heads * RET_DK
    wv = heads * RET_DV
    kq = RET_QK_WIDTH // wk
    kv = 2 * RET_QK_WIDTH // wv
    kg = kv + RET_WIDTH // wv
    in_specs = [pl.BlockSpec((L, wk), lambda b, h, rd: (rb + b, h)),
                pl.BlockSpec((L, wk), lambda b, h, rd: (rb + b, kq + h)),
                pl.BlockSpec((L, wv), lambda b, h, rd: (rb + b, kv + h)),
                pl.BlockSpec((L, wv), lambda b, h, rd: (rb + b, kg + h))]
    args = [proj, proj, proj, proj]
    state_spec = pl.BlockSpec((1, 2, heads, RET_DK, RET_DV), lambda b, h, rd: (b, 0, h, 0, 0))
    if has_state_in:
        in_specs.append(state_spec)
        args.append(state_in)
    if use_rope:
        in_specs += [pl.BlockSpec((L, RET_DK), lambda b, h, rd: (0, 0))] * 2
        args += list(rope_tabs)
    out_specs = [pl.BlockSpec((L, wv), lambda b, h, rd: (b, h))]
    out_shape = [jax.ShapeDtypeStruct((B * L, RET_WIDTH), BF16)]
    if want_state:
        out_specs.append(state_spec)
        out_shape.append(jax.ShapeDtypeStruct((B, 2, RET_HEADS, RET_DK, RET_DV), F32))
    kern = functools.partial(_retention_kernel, seq_len=L, chunk=RET_CHUNK, use_rope=use_rope,
                             has_state_in=has_state_in, has_state_out=want_state, heads=heads)
    res = pl.pallas_call(
        kern,
        grid_spec=pltpu.PrefetchScalarGridSpec(
            num_scalar_prefetch=1,
            grid=(B, RET_HEADS // heads),
            in_specs=in_specs,
            out_specs=out_specs,
            scratch_shapes=[pltpu.VMEM((heads, L, RET_DK), BF16),
                            pltpu.VMEM((heads, L, RET_DK), F32),
                            pltpu.VMEM((heads, L, RET_DV), F32),
                            pltpu.VMEM((heads, RET_DK, RET_DV), F32),
                            pltpu.VMEM((heads, RET_DK, RET_DV), F32)]),
        out_shape=out_shape,
        compiler_params=_cparams(("arbitrary", "arbitrary"), 40 * 1024 * 1024),
        name="retention_grid" if use_rope else "retention_seq",
    )(ret_decay, *args)
    return res if want_state else (res[0], None)


def _split3(x):
    hi = x.astype(BF16)
    r1 = x - hi.astype(F32)
    mid = r1.astype(BF16)
    lo = (r1 - mid.astype(F32)).astype(BF16)
    return jnp.concatenate([hi, mid, lo], axis=-1)


def _pool_kernel(p_ref, w_ref, sc_ref, o_ref, pad_ref, *, seq_len, grid_mode):
    PT = POOL_TILE
    n_tiles = seq_len // PT
    ii = lax.broadcasted_iota(jnp.int32, (PT, PT), 0)
    jj = lax.broadcasted_iota(jnp.int32, (PT, PT), 1)
    d = jj - ii
    ti = lax.broadcasted_iota(jnp.int32, (PT, 1), 0)

    if grid_mode:
        zeros = jnp.zeros((POOL_PAD, POOL_DG), F32)
        pad_ref[0:POOL_PAD, :] = zeros
        pad_ref[POOL_PAD + seq_len:POOL_PAD + seq_len + POOL_PAD, :] = zeros

    for gi, w in enumerate(POOL_WINDOWS):
        lo_off = -(w // 2)
        hi_off = w - w // 2 - 1
        cols = slice(gi * POOL_DG, (gi + 1) * POOL_DG)
        in_window = (d >= lo_off) & (d <= hi_off)
        if grid_mode:
            in_window = in_window & ((ii // GRID_W) == (jj // GRID_W))
        band = jnp.where(in_window, 1.0, 0.0).astype(BF16)
        wg = w_ref[gi]
        scale = sc_ref[gi]

        def window_sum(t0):
            x = p_ref[pl.ds(t0, PT), cols]
            s3 = jnp.dot(band, _split3(x), preferred_element_type=F32)
            return s3[:, 0:POOL_DG] + s3[:, POOL_DG:2 * POOL_DG] + s3[:, 2 * POOL_DG:3 * POOL_DG]

        def finish(t0, s, cnt):
            x = p_ref[pl.ds(t0, PT), cols]
            diff = (s / cnt - x).astype(BF16)
            y = jnp.dot(diff, wg, preferred_element_type=F32) * scale
            o_ref[pl.ds(t0, PT), cols] = y.astype(BF16)

        def count(pos, n):
            lo = jnp.maximum(pos + lo_off, 0)
            hi = jnp.minimum(pos + lo_off + w, n)
            return (hi - lo).astype(F32)

        if grid_mode:
            for t in range(n_tiles):
                pad_ref[POOL_PAD + t * PT:POOL_PAD + (t + 1) * PT, :] = window_sum(t * PT)
            for t in range(n_tiles):
                base = POOL_PAD + t * PT
                s = pad_ref[base + lo_off * GRID_W:base + lo_off * GRID_W + PT, :]
                for r in range(lo_off + 1, hi_off + 1):
                    s = s + pad_ref[base + r * GRID_W:base + r * GRID_W + PT, :]
                tok = ti + t * PT
                cnt = count(tok // GRID_W, seq_len // GRID_W) * count(tok % GRID_W, GRID_W)
                finish(t * PT, s, cnt)
        else:
            for t in range(n_tiles):
                finish(t * PT, window_sum(t * PT), count(ti + t * PT, seq_len))


def _pooling(proj, row0, B, L, w_pool_bf16, pool_scale, grid_mode):
    if not grid_mode:
        assert L == POOL_TILE
    rb = row0 // L
    pcol = (IN_WIDTH - POOL_WIDTH) // POOL_WIDTH
    kern = functools.partial(_pool_kernel, seq_len=L, grid_mode=grid_mode)
    return pl.pallas_call(
        kern,
        grid=(B,),
        in_specs=[pl.BlockSpec((L, POOL_WIDTH), lambda b: (rb + b, pcol)),
                  pl.BlockSpec((POOL_GROUPS, POOL_DG, POOL_DG), lambda b: (0, 0, 0)),
                  pl.BlockSpec((POOL_GROUPS, 1, POOL_DG), lambda b: (0, 0, 0))],
        out_specs=pl.BlockSpec((L, POOL_WIDTH), lambda b: (b, 0)),
        out_shape=jax.ShapeDtypeStruct((B * L, POOL_WIDTH), BF16),
        scratch_shapes=[pltpu.VMEM((L + 2 * POOL_PAD, POOL_DG), F32)],
        compiler_params=_cparams(("arbitrary",), 48 * 1024 * 1024),
        name="pool_grid" if grid_mode else "pool_seq",
    )(proj, w_pool_bf16, pool_scale.reshape(POOL_GROUPS, 1, POOL_DG))


def _outproj_kernel(*refs):
    i = pl.program_id(0)
    h2_ref = refs[14]

    @pl.when(i >= T_ALL // TM_OUT)
    def _():
        h2_ref[...] = jnp.zeros_like(h2_ref)

    pl.when(i < T_ALL // TM_OUT)(lambda: _outproj_tile(*refs))


def _outproj_tile(retc_ref, retl_ref, poolc_ref, pooll_ref, xc_ref, xl_ref, mod_ref, g_ref,
                  wt_ref, wb_ref, wrh_ref, wrb_ref, br_ref,
                  x1_ref, h2_ref, ri_ref, rp_ref, rr_ref, cnt_ref, run_ref):
    i = pl.program_id(0)

    @pl.when(i == 0)
    def _():
        run_ref[...] = jnp.zeros_like(run_ref)

    is_ctx = i < T_CTX // TM_OUT
    ret = jnp.where(is_ctx, retc_ref[...], retl_ref[...])
    pool = jnp.where(is_ctx, poolc_ref[...], pooll_ref[...])
    x = jnp.where(is_ctx, xc_ref[...], xl_ref[...])
    y = (jnp.dot(ret, wt_ref[...], preferred_element_type=F32)
         + jnp.dot(pool, wb_ref[...], preferred_element_type=F32))
    x1 = x + mod_ref[0, 2:3, :] * y
    x1_ref[...] = x1
    hn = x1 * lax.rsqrt(jnp.mean(x1 * x1, axis=-1, keepdims=True) + NORM_EPS) * g_ref[...]
    h2 = hn * (1.0 + mod_ref[0, 4:5, :]) + mod_ref[0, 3:4, :]
    hi = h2.astype(BF16)
    h2_ref[...] = hi
    lo = (h2 - hi.astype(F32)).astype(BF16)
    both = jnp.dot(hi, wrb_ref[...], preferred_element_type=F32)
    logits = (both[:, 0:LANES] + jnp.dot(lo, wrh_ref[...], preferred_element_type=F32)
              + both[:, LANES:2 * LANES]) + br_ref[...]

    tm = logits.shape[0]
    lane = lax.broadcasted_iota(jnp.int32, (tm, LANES), 1)
    work = logits
    vals, idxs, hots = [], [], []
    for _ in range(TOP_K):
        m = jnp.max(work, axis=-1, keepdims=True)
        idx = jnp.min(jnp.where(work == m, lane, LANES), axis=-1, keepdims=True)
        hot = lane == idx
        vals.append(m)
        idxs.append(idx)
        hots.append(hot)
        work = jnp.where(hot, -jnp.inf, work)
    exps = [jnp.exp(v - vals[0]) for v in vals]
    denom = exps[0] + exps[1] + exps[2] + exps[3]

    selected = jnp.zeros((tm, LANES), F32)
    for hot in hots:
        selected = selected + jnp.where(hot, 1.0, 0.0)
    r_i = lax.broadcasted_iota(jnp.int32, (tm, tm), 0)
    c_i = lax.broadcasted_iota(jnp.int32, (tm, tm), 1)
    tri = jnp.where(c_i < r_i, 1.0, 0.0).astype(BF16)
    before = jnp.dot(tri, selected.astype(BF16), preferred_element_type=F32) + run_ref[0:1, :]

    ri = jnp.zeros((tm, LANES), jnp.int32)
    rp = jnp.zeros((tm, LANES), F32)
    rr = jnp.zeros((tm, LANES), jnp.int32)
    for k in range(TOP_K):
        rank = jnp.sum(jnp.where(hots[k], before, 0.0), axis=-1, keepdims=True).astype(jnp.int32)
        ri = jnp.where(lane == k, idxs[k], ri)
        rp = jnp.where(lane == k, exps[k] / denom, rp)
        rr = jnp.where(lane == k, rank, rr)
    ri_ref[...] = ri
    rp_ref[...] = rp
    rr_ref[...] = rr

    run = run_ref[0:1, :] + jnp.sum(selected, axis=0, keepdims=True)
    run_ref[...] = jnp.broadcast_to(run, run_ref.shape)
    cnt_ref[...] = jnp.broadcast_to(run, cnt_ref.shape).astype(jnp.int32)


def _out_projection(ret_ctx, ret_lat, pool_ctx, pool_lat, x_ctx, x_lat, mod3, norm_g, w_out_bf16,
                    wr_hi, wr_both, b_router_pad):
    tm = TM_OUT
    real = lambda i: jnp.minimum(i, T_ALL // tm - 1)
    row = lambda i: (real(i), 0)
    const = lambda i: (0, 0)
    crow = lambda i: (_ctx_tile(real(i), tm), 0)
    lrow = lambda i: (_lat_tile(real(i), tm), 0)
    return pl.pallas_call(
        _outproj_kernel,
        grid=(H2_ROWS // tm,),
        in_specs=[pl.BlockSpec((tm, RET_WIDTH), crow),
                  pl.BlockSpec((tm, RET_WIDTH), lrow),
                  pl.BlockSpec((tm, POOL_WIDTH), crow),
                  pl.BlockSpec((tm, POOL_WIDTH), lrow),
                  pl.BlockSpec((tm, D_MODEL), crow),
                  pl.BlockSpec((tm, D_MODEL), lrow),
                  pl.BlockSpec((1, 6, D_MODEL), lambda i: (_cond_row(real(i), tm), 0, 0)),
                  pl.BlockSpec((1, D_MODEL), const),
                  pl.BlockSpec((RET_WIDTH, D_MODEL), const),
                  pl.BlockSpec((POOL_WIDTH, D_MODEL), lambda i: (1, 0)),
                  pl.BlockSpec((D_MODEL, LANES), const),
                  pl.BlockSpec((D_MODEL, 2 * LANES), const),
                  pl.BlockSpec((1, LANES), const)],
        out_specs=[pl.BlockSpec((tm, D_MODEL), row),
                   pl.BlockSpec((tm, D_MODEL), lambda i: (i, 0)),
                   pl.BlockSpec((tm, LANES), row),
                   pl.BlockSpec((tm, LANES), row),
                   pl.BlockSpec((tm, LANES), row),
                   pl.BlockSpec((8, LANES), const)],
        out_shape=[jax.ShapeDtypeStruct((T_ALL, D_MODEL), F32),
                   jax.ShapeDtypeStruct((H2_ROWS, D_MODEL), BF16),
                   jax.ShapeDtypeStruct((T_ALL, LANES), jnp.int32),
                   jax.ShapeDtypeStruct((T_ALL, LANES), F32),
                   jax.ShapeDtypeStruct((T_ALL, LANES), jnp.int32),
                   jax.ShapeDtypeStruct((8, LANES), jnp.int32)],
        scratch_shapes=[pltpu.VMEM((8, LANES), F32)],
        compiler_params=_cparams(("arbitrary",), 48 * 1024 * 1024),
        name="out_projection_router",
    )(ret_ctx, ret_lat, pool_ctx, pool_lat, x_ctx, x_lat, mod3, norm_g.reshape(1, D_MODEL),
      w_out_bf16, w_out_bf16, wr_hi, wr_both, b_router_pad)


def _count_le(ends, v):
    return jnp.sum((ends[None, :] <= v[:, None]).astype(jnp.int32), axis=1)


def _expert_schedule(counts):
    i32 = jnp.int32
    ntile = (counts + TM_EXP - 1) // TM_EXP
    tile_off = jnp.cumsum(ntile) - ntile
    nsup = (ntile + SUP_EXP - 1) // SUP_EXP
    sup_end = jnp.cumsum(nsup)
    u = jnp.arange(U_EXP, dtype=i32)
    sup_valid = u < sup_end[-1]
    uc = jnp.minimum(u, sup_end[-1] - 1)
    sup_e = jnp.minimum(_count_le(sup_end, uc), N_EXPERTS - 1)
    sup_idx = uc - (sup_end - nsup)[sup_e]
    per_sup = (ntile + jnp.maximum(nsup, 1) - 1) // jnp.maximum(nsup, 1)
    sup_row0 = (tile_off[sup_e] + sup_idx * per_sup[sup_e]) * TM_EXP
    sup_nt = jnp.clip(ntile[sup_e] - sup_idx * per_sup[sup_e], 1, per_sup[sup_e])
    step_e = jnp.repeat(sup_e, J_EXP)
    step_j = jnp.tile(jnp.arange(J_EXP, dtype=i32), U_EXP)
    step_row0 = jnp.repeat(sup_row0, J_EXP)
    step_nt = jnp.repeat(sup_nt, J_EXP)
    last_rows = counts - (ntile - 1) * TM_EXP
    sup_half = (sup_idx == nsup[sup_e] - 1) & (last_rows[sup_e] <= TM_EXP // 2) & (counts[sup_e] > 0)
    step_half = jnp.repeat(sup_half.astype(i32), J_EXP)
    bounds = jnp.asarray(CH_START_EXP[1:], dtype=i32)
    before = jnp.sum((sup_valid[None, :] & (sup_row0[None, :] < bounds[:, None])).astype(i32), axis=1) * J_EXP
    n_steps = jnp.sum(sup_valid.astype(i32)) * J_EXP
    chunk_start = jnp.concatenate([jnp.zeros((1,), i32), before])
    chunk_steps = jnp.concatenate([before, n_steps.reshape(1)]) - chunk_start
    tables = (step_e.astype(i32), step_j.astype(i32), step_row0.astype(i32), step_nt.astype(i32), step_half)
    return tile_off * TM_EXP, tables, chunk_start, chunk_steps, jnp.sum(ntile).astype(i32)


def _expert_kernel(se_ref, sj_ref, srow_ref, snt_ref, shalf_ref, meta_ref,
                   xs_hbm, wg_ref, wu_ref, bg_ref, bu_ref, wd_ref, bd_ref, *rest, first_chunk):
    ys_hbm, acc_ref, x_buf, o_buf, x_sem, o_sem = rest if first_chunk else rest[1:]
    s = pl.program_id(0) + meta_ref[0]
    j = sj_ref[s]
    n_tiles = snt_ref[s]
    row0 = pl.multiple_of(srow_ref[s], TM_EXP)
    last = J_EXP - 1

    def x_copy(first_row):
        rows = pl.ds(pl.multiple_of(first_row - meta_ref[1], TM_EXP), SUP_EXP * TM_EXP)
        return pltpu.make_async_copy(xs_hbm.at[rows], x_buf, x_sem)

    def out_copy(t):
        rows = pl.ds(pl.multiple_of(row0 + t * TM_EXP, TM_EXP), TM_EXP)
        return pltpu.make_async_copy(o_buf.at[t % 2], ys_hbm.at[rows], o_sem.at[t % 2])

    @pl.when((pl.program_id(0) == 0) & first_chunk)
    def _():
        o_buf[0] = jnp.zeros((TM_EXP, D_MODEL), BF16)

        def zero_copy(t):
            rows = pl.ds(pl.multiple_of(t * TM_EXP, TM_EXP), TM_EXP)
            return pltpu.make_async_copy(o_buf.at[0], ys_hbm.at[rows], o_sem.at[0])

        def start(t, carry):
            zero_copy(t).start()
            return carry

        def wait(t, carry):
            zero_copy(t).wait()
            return carry

        lax.fori_loop(meta_ref[2], NT_EXP, start, 0)
        lax.fori_loop(meta_ref[2], NT_EXP, wait, 0)

    def step():
        @pl.when(j == 0)
        def _():
            @pl.when(pl.program_id(0) == 0)
            def _():
                x_copy(row0).start()

            acc_ref[...] = jnp.zeros_like(acc_ref)
            x_copy(row0).wait()

        def accumulate(rows):
            x = x_buf[rows, :]
            gate = _dot(x, wg_ref[0]) + bg_ref[0]
            up = _dot(x, wu_ref[0]) + bu_ref[0]
            gate = jnp.minimum(gate, SWIGLU_LIMIT)
            up = jnp.clip(up, -SWIGLU_LIMIT, SWIGLU_LIMIT)
            act = (up + 1.0) * gate * _sigmoid(SWIGLU_ALPHA * gate)
            acc_ref[rows, :] += _dot(act, wd_ref[0])

        def tile_pair(p, carry):
            base = pl.multiple_of(p * (2 * TM_EXP), 2 * TM_EXP)
            accumulate(pl.ds(base, TM_EXP))
            accumulate(pl.ds(base + TM_EXP, TM_EXP))
            return carry

        n_full = n_tiles - shalf_ref[s]
        lax.fori_loop(0, n_full // 2, tile_pair, 0)

        @pl.when(n_full % 2 == 1)
        def _():
            accumulate(pl.ds(pl.multiple_of((n_full - 1) * TM_EXP, TM_EXP), TM_EXP))

        @pl.when(shalf_ref[s] == 1)
        def _():
            accumulate(pl.ds(pl.multiple_of((n_tiles - 1) * TM_EXP, TM_EXP), TM_EXP // 2))

        @pl.when(j == last)
        def _():
            @pl.when(pl.program_id(0) + 1 < meta_ref[3])
            def _():
                x_copy(pl.multiple_of(srow_ref[s + 1], TM_EXP)).start()

            def finish(t, carry):
                @pl.when(t >= 2)
                def _():
                    out_copy(t - 2).wait()

                rows = pl.ds(pl.multiple_of(t * TM_EXP, TM_EXP), TM_EXP)
                o_buf[t % 2] = (acc_ref[rows, :] + bd_ref[0]).astype(BF16)
                out_copy(t).start()
                return carry

            lax.fori_loop(0, n_tiles, finish, 0)

            @pl.when(n_tiles >= 2)
            def _():
                out_copy(n_tiles - 2).wait()

            out_copy(n_tiles - 1).wait()

    step()


def _experts_chunk(chunk, xs_part, ys, tables, meta, n_steps, w_gate_up, b_gate_up, w_down, b_down):
    first = ys is None
    wmap = lambda col0: (lambda s, se, sj, srow, snt, shalf, meta: (se[s + meta[0]], 0, col0 + sj[s + meta[0]]))
    n_prefetch = len(tables) + 1
    in_specs = [
        pl.BlockSpec(memory_space=pl.ANY),
        pl.BlockSpec((1, D_MODEL, F_EXP), wmap(0)),
        pl.BlockSpec((1, D_MODEL, F_EXP), wmap(J_EXP)),
        pl.BlockSpec((1, 1, F_EXP), wmap(0)),
        pl.BlockSpec((1, 1, F_EXP), wmap(J_EXP)),
        pl.BlockSpec((1, F_EXP, D_MODEL),
                     lambda s, se, sj, srow, snt, shalf, meta: (se[s + meta[0]], sj[s + meta[0]], 0)),
        pl.BlockSpec((1, 1, D_MODEL), lambda s, se, sj, srow, snt, shalf, meta: (se[s + meta[0]], 0, 0)),
    ]
    args = [xs_part, w_gate_up, w_gate_up,
            b_gate_up.reshape(N_EXPERTS, 1, 2 * D_FF), b_gate_up.reshape(N_EXPERTS, 1, 2 * D_FF),
            w_down, b_down.reshape(N_EXPERTS, 1, D_MODEL)]
    aliases = {}
    if not first:
        in_specs.append(pl.BlockSpec(memory_space=pl.ANY))
        args.append(ys)
        aliases = {n_prefetch + len(args) - 1: 0}
    return pl.pallas_call(
        functools.partial(_expert_kernel, first_chunk=first),
        grid_spec=pltpu.PrefetchScalarGridSpec(
            num_scalar_prefetch=n_prefetch,
            grid=(n_steps,),
            in_specs=in_specs,
            out_specs=pl.BlockSpec(memory_space=pl.ANY),
            scratch_shapes=[pltpu.VMEM((SUP_EXP * TM_EXP, D_MODEL), F32),
                            pltpu.VMEM((SUP_EXP * TM_EXP, D_MODEL), BF16),
                            pltpu.VMEM((2, TM_EXP, D_MODEL), BF16),
                            pltpu.SemaphoreType.DMA(()),
                            pltpu.SemaphoreType.DMA((2,))]),
        out_shape=jax.ShapeDtypeStruct((NT_EXP * TM_EXP, D_MODEL), BF16),
        input_output_aliases=aliases,
        compiler_params=_cparams(("arbitrary",), VMEM_LIMIT),
        name=f"experts_chunk{chunk}",
    )(*tables, meta, *args)


def _combine_kernel(x1_ref, y0_ref, y1_ref, y2_ref, y3_ref, rp_ref, mod_ref, g_ref, o_ref):
    rp = rp_ref[...]
    moe = rp[:, 0:1] * y0_ref[...].astype(F32)
    for k, y_ref in enumerate((y1_ref, y2_ref, y3_ref), start=1):
        moe = moe + rp[:, k:k + 1] * y_ref[...].astype(F32)
    x2 = x1_ref[...] + mod_ref[0, 5:6, :] * moe
    o_ref[...] = x2 * lax.rsqrt(jnp.mean(x2 * x2, axis=-1, keepdims=True) + NORM_EPS) * g_ref[...]


def _combine(x1, y4, route_p, mod3, final_g, row0, n_rows):
    tm = TM_CMB
    nt = n_rows // tm
    t0 = row0 // tm
    y_specs = [pl.BlockSpec((tm, D_MODEL), functools.partial(lambda k, i: (k * nt + i, 0), k))
               for k in range(TOP_K)]
    return pl.pallas_call(
        _combine_kernel,
        grid=(nt,),
        in_specs=[pl.BlockSpec((tm, D_MODEL), lambda i: (t0 + i, 0))] + y_specs + [
            pl.BlockSpec((tm, LANES), lambda i: (t0 + i, 0)),
            pl.BlockSpec((1, 6, D_MODEL), lambda i: (_cond_row(t0 + i, tm), 0, 0)),
            pl.BlockSpec((1, D_MODEL), lambda i: (0, 0))],
        out_specs=pl.BlockSpec((tm, D_MODEL), lambda i: (i, 0)),
        out_shape=jax.ShapeDtypeStruct((n_rows, D_MODEL), F32),
        compiler_params=_cparams(("arbitrary",), 40 * 1024 * 1024),
        name="combine_final_norm",
    )(x1, y4, y4, y4, y4, route_p, mod3, final_g.reshape(1, D_MODEL))


def kernel(x_prompt, x_sample, c, c_ctx, state_ret, norm1_g, norm2_g, w_mod, b_mod, w_in, ret_decay,
           w_pool, pool_scale, w_out, w_router, b_router, w_gate_up, b_gate_up, w_down, b_down, final_g):
    assert w_mod.shape[0] == 1, "single trunk layer"
    x_ctx = x_prompt.reshape(T_CTX, D_MODEL)
    x_lat = x_sample.reshape(T_LAT, D_MODEL)
    cond = jnp.zeros((COND_ROWS, D_MODEL), F32).at[0].set(c_ctx).at[1:N_COND].set(c)

    mod = _modulation(cond, w_mod[0], b_mod[0])
    mod3 = mod.reshape(COND_ROWS, 6, D_MODEL)

    proj = _in_projection(x_ctx, x_lat, norm1_g[0], mod3, w_in[0].astype(BF16))

    ret_ctx, new_state = _retention(proj, 0, BATCH, SEQ, ret_decay[0], None, None, True, RET_HEADS)
    ret_lat, _ = _retention(proj, T_CTX, DEC_BATCH, DEC_SEQ, ret_decay[0], state_ret[:, 0],
                            _rope_tables(DEC_SEQ), False, 2)
    w_pool_bf16 = w_pool[0].astype(BF16)
    pool_ctx = _pooling(proj, 0, BATCH, SEQ, w_pool_bf16, pool_scale[0], False)
    pool_lat = _pooling(proj, T_CTX, DEC_BATCH, DEC_SEQ, w_pool_bf16, pool_scale[0], True)

    wr = jnp.zeros((D_MODEL, LANES), F32).at[:, :N_EXPERTS].set(w_router[0])
    wr_hi = wr.astype(BF16)
    wr_both = jnp.concatenate([wr_hi, (wr - wr_hi.astype(F32)).astype(BF16)], axis=1)
    br = jnp.full((1, LANES), -1e30, F32).at[0, :N_EXPERTS].set(b_router[0])
    x1, h2, route_i, route_p, route_r, counts = _out_projection(
        ret_ctx, ret_lat, pool_ctx, pool_lat, x_ctx, x_lat, mod3, norm2_g[0], w_out[0].astype(BF16),
        wr_hi, wr_both, br)

    row_off, tables, chunk_start, chunk_steps, n_row_tiles = _expert_schedule(counts[0, :N_EXPERTS])
    slot = (row_off[route_i[:, :TOP_K]] + route_r[:, :TOP_K]).T
    token = jnp.broadcast_to(jnp.arange(T_ALL, dtype=jnp.int32)[None, :], (TOP_K, T_ALL))
    token_of_slot = (jnp.arange(XS_ROWS, dtype=jnp.int32) % T_ALL).at[slot.reshape(-1)].set(
        token.reshape(-1), unique_indices=True, mode="promise_in_bounds")
    ys = None
    for c in range(CHUNKS_EXP):
        lo = CH_START_EXP[c]
        nxt = CH_START_EXP[c + 1] if c + 1 < CHUNKS_EXP else XS_ROWS
        hi = min(nxt + SUP_EXP * TM_EXP, XS_ROWS)
        xs_part = h2.at[token_of_slot[lo:hi]].get(mode="promise_in_bounds")
        meta = jnp.stack([chunk_start[c], jnp.int32(lo), n_row_tiles, chunk_steps[c]]).astype(jnp.int32)
        ys = _experts_chunk(c, xs_part, ys, tables, meta, chunk_steps[c],
                            w_gate_up[0], b_gate_up[0], w_down[0], b_down[0])
    y4_lat = ys.at[slot[:, T_CTX:].reshape(-1)].get(mode="promise_in_bounds", unique_indices=True)
    y4_ctx = ys.at[slot[:, :T_CTX].reshape(-1)].get(mode="promise_in_bounds", unique_indices=True)
    y_lat = _combine(x1, y4_lat, route_p, mod3, final_g, T_CTX, T_LAT)
    y_ctx = _combine(x1, y4_ctx, route_p, mod3, final_g, 0, T_CTX)
    return (y_ctx.reshape(BATCH, SEQ, D_MODEL), y_lat.reshape(DEC_BATCH, DEC_SEQ, D_MODEL),
            new_state.reshape(BATCH, 1, 2, RET_HEADS, RET_DK, RET_DV))
```

```python
import functools

import numpy as np
import jax
import jax.numpy as jnp
from jax import lax
from jax.experimental import pallas as pl
from jax.experimental.pallas import tpu as pltpu

F32 = jnp.float32
BF16 = jnp.bfloat16

D_MODEL = 2048
BATCH = 16
SEQ = 256
DEC_BATCH = 4
DEC_SEQ = 2048
GRID_W = 64
RET_HEADS = 4
RET_DK = 128
RET_DV = 256
RET_QK_WIDTH = RET_HEADS * RET_DK
RET_WIDTH = RET_HEADS * RET_DV
POOL_WINDOWS = (2, 4, 8, 16)
POOL_GROUPS = 4
POOL_DG = 256
POOL_WIDTH = POOL_GROUPS * POOL_DG
IN_WIDTH = 2 * RET_QK_WIDTH + 2 * RET_WIDTH + POOL_WIDTH
N_EXPERTS = 32
TOP_K = 4
D_FF = D_MODEL
SWIGLU_LIMIT = 7.0
SWIGLU_ALPHA = 1.702
ROPE_BASE = 10000.0
NORM_EPS = 1e-6
GN_EPS = 1e-6

T_CTX = BATCH * SEQ
T_LAT = DEC_BATCH * DEC_SEQ
T_ALL = T_CTX + T_LAT
N_COND = 1 + DEC_BATCH
COND_ROWS = 8
LANES = 128

RET_CHUNK = 256

TM_IN = 512
TN_IN = 2048
TM_OUT = 256
TM_CMB = 256
H2_ROWS = 16384
POOL_TILE = 256
POOL_PAD = (max(POOL_WINDOWS) // 2) * GRID_W

TM_EXP = 256
F_EXP = 512
J_EXP = D_FF // F_EXP
SUP_EXP = 8
NT_EXP = (T_ALL * TOP_K + N_EXPERTS * (TM_EXP - 1)) // TM_EXP
U_EXP = N_EXPERTS + NT_EXP // SUP_EXP + 1
XS_ROWS = (NT_EXP + SUP_EXP) * TM_EXP
CH_START_EXP = tuple(t * TM_EXP for t in (0, 24))
CHUNKS_EXP = len(CH_START_EXP)

VMEM_LIMIT = 56 * 1024 * 1024


def _cparams(sem, vmem=None):
    return pltpu.CompilerParams(dimension_semantics=sem, vmem_limit_bytes=vmem)


def _cond_row(i, tm):
    nctx = T_CTX // tm
    return jnp.where(i < nctx, 0, 1 + (i - nctx) // (DEC_SEQ // tm))


def _sigmoid(x):
    return 1.0 / (1.0 + jnp.exp(-x))


def _dot(a, b):
    return lax.dot_general(a, b, (((1,), (0,)), ((), ())), preferred_element_type=F32)


def _mod_kernel(c_ref, w_ref, b_ref, o_ref):
    c = c_ref[...]
    s = (c * _sigmoid(c)).astype(BF16)
    o_ref[...] = jnp.dot(s, w_ref[...].astype(BF16), preferred_element_type=F32) + b_ref[...]


def _modulation(cond, w_mod, b_mod):
    n = w_mod.shape[1]
    tn = 1024
    return pl.pallas_call(
        _mod_kernel,
        grid=(n // tn,),
        in_specs=[pl.BlockSpec((COND_ROWS, D_MODEL), lambda j: (0, 0)),
                  pl.BlockSpec((D_MODEL, tn), lambda j: (0, j)),
                  pl.BlockSpec((1, tn), lambda j: (0, j))],
        out_specs=pl.BlockSpec((COND_ROWS, tn), lambda j: (0, j)),
        out_shape=jax.ShapeDtypeStruct((COND_ROWS, n), F32),
        compiler_params=_cparams(("arbitrary",), 40 * 1024 * 1024),
        name="modulation",
    )(cond, w_mod, b_mod.reshape(1, n))


def _ctx_tile(i, tm):
    return jnp.minimum(i, T_CTX // tm - 1)


def _lat_tile(i, tm):
    return jnp.maximum(i - T_CTX // tm, 0)


def _inproj_kernel(xc_ref, xl_ref, g_ref, mod_ref, w_ref, o_ref, h_ref):
    def modulated_norm(x_ref):
        x = x_ref[...]
        y = x * lax.rsqrt(jnp.mean(x * x, axis=-1, keepdims=True) + NORM_EPS) * g_ref[...]
        shift = mod_ref[0, 0:1, :]
        scale = mod_ref[0, 1:2, :]
        h_ref[...] = (y * (1.0 + scale) + shift).astype(BF16)

    first = pl.program_id(1) == 0
    is_ctx = pl.program_id(0) < T_CTX // TM_IN
    pl.when(first & is_ctx)(lambda: modulated_norm(xc_ref))
    pl.when(first & jnp.logical_not(is_ctx))(lambda: modulated_norm(xl_ref))

    o_ref[...] = jnp.dot(h_ref[...], w_ref[...], preferred_element_type=F32)


def _in_projection(x_ctx, x_lat, norm_g, mod3, w_in_bf16):
    return pl.pallas_call(
        _inproj_kernel,
        grid=(T_ALL // TM_IN, IN_WIDTH // TN_IN),
        in_specs=[pl.BlockSpec((TM_IN, D_MODEL), lambda i, j: (_ctx_tile(i, TM_IN), 0)),
                  pl.BlockSpec((TM_IN, D_MODEL), lambda i, j: (_lat_tile(i, TM_IN), 0)),
                  pl.BlockSpec((1, D_MODEL), lambda i, j: (0, 0)),
                  pl.BlockSpec((1, 6, D_MODEL), lambda i, j: (_cond_row(i, TM_IN), 0, 0)),
                  pl.BlockSpec((D_MODEL, TN_IN), lambda i, j: (0, j))],
        out_specs=pl.BlockSpec((TM_IN, TN_IN), lambda i, j: (i, j)),
        out_shape=jax.ShapeDtypeStruct((T_ALL, IN_WIDTH), F32),
        scratch_shapes=[pltpu.VMEM((TM_IN, D_MODEL), BF16)],
        compiler_params=_cparams(("arbitrary", "arbitrary"), 48 * 1024 * 1024),
        name="in_projection",
    )(x_ctx, x_lat, norm_g.reshape(1, D_MODEL), mod3, w_in_bf16)


def _rope_tables(seq_len):
    t = jnp.arange(seq_len)
    row = (t // GRID_W).astype(F32)
    col = (t % GRID_W).astype(F32)
    half = RET_DK // 2
    n_freq = half // 2
    inv = ROPE_BASE ** (-jnp.arange(n_freq, dtype=F32) / n_freq)
    ang_r = row[:, None] * inv
    ang_c = col[:, None] * inv
    cos = jnp.concatenate([jnp.cos(ang_r), jnp.cos(ang_r), jnp.cos(ang_c), jnp.cos(ang_c)], axis=-1)
    sin = jnp.concatenate([-jnp.sin(ang_r), jnp.sin(ang_r), -jnp.sin(ang_c), jnp.sin(ang_c)], axis=-1)
    return cos, sin


def _retention_kernel(rd_ref, *refs, seq_len, chunk, use_rope, has_state_in, has_state_out, heads):
    refs = list(refs)
    q_ref, k_ref, v_ref, g_ref = refs[:4]
    pos = 4
    if has_state_in:
        s0_ref = refs[pos]
        pos += 1
    if use_rope:
        cos_ref, sin_ref = refs[pos], refs[pos + 1]
        pos += 2
    o_ref = refs[pos]
    pos += 1
    if has_state_out:
        st_ref = refs[pos]
        pos += 1
    scratch = refs[pos:pos + 5]

    progs = [_retention_head(rd_ref, pl.program_id(1) * heads + hh, hh,
                             q_ref, k_ref, v_ref, g_ref,
                             s0_ref if has_state_in else None,
                             (cos_ref, sin_ref) if use_rope else None,
                             o_ref, st_ref if has_state_out else None,
                             [r.at[hh] for r in scratch], seq_len, chunk)
             for hh in range(heads)]
    n_chunks = seq_len // chunk
    if n_chunks == 1:
        for fwd_chunk, _, _ in progs:
            fwd_chunk(0)
        for _, bwd_chunk, _ in progs:
            bwd_chunk(0)
    else:
        assert n_chunks % 2 == 0

        def fwd_pair(i, carry):
            for fwd_chunk, _, _ in progs:
                fwd_chunk(2 * i)
                fwd_chunk(2 * i + 1)
            return carry

        def bwd_pair(i, carry):
            for _, bwd_chunk, _ in progs:
                bwd_chunk(n_chunks - 1 - 2 * i)
                bwd_chunk(n_chunks - 2 - 2 * i)
            return carry

        lax.fori_loop(0, n_chunks // 2, fwd_pair, 0)
        lax.fori_loop(0, n_chunks // 2, bwd_pair, 0)
    for _, _, finish in progs:
        finish()


def _retention_head(rd_ref, h, hh, q_ref, k_ref, v_ref, g_ref, s0_ref, rope_refs, o_ref, st_ref, scratch,
                    seq_len, chunk):
    qs_ref, ks_ref, acc_ref, sf_ref, sb_ref = scratch
    use_rope = rope_refs is not None
    kcols = slice(hh * RET_DK, (hh + 1) * RET_DK)
    vcols = slice(hh * RET_DV, (hh + 1) * RET_DV)
    C = chunk
    n_chunks = seq_len // C

    lgf = -jnp.exp(jnp.full((C, 1), rd_ref[0, h], F32))
    lgb = -jnp.exp(jnp.full((C, 1), rd_ref[1, h], F32))
    ii = lax.broadcasted_iota(jnp.int32, (C, C), 0)
    jj = lax.broadcasted_iota(jnp.int32, (C, C), 1)
    diff = (ii - jj).astype(F32)
    decay = (jnp.where(diff >= 0, jnp.exp(lgf * jnp.maximum(diff, 0.0)), 0.0)
             + jnp.where(diff <= 0, jnp.exp(lgb * jnp.maximum(-diff, 0.0)), 0.0))
    p = lax.broadcasted_iota(jnp.int32, (C, 1), 0).astype(F32)
    xi_f = jnp.exp(lgf * (p + 1.0))
    zeta_f = jnp.exp(lgf * (C - 1.0 - p))
    xi_b = jnp.exp(lgb * (C - p))
    zeta_b = jnp.exp(lgb * p)
    cd_f = jnp.exp(lgf[0:1, :] * C)
    cd_b = jnp.exp(lgb[0:1, :] * C)

    if s0_ref is not None:
        sf_ref[...] = s0_ref[0, 0, hh]
        sb_ref[...] = s0_ref[0, 1, hh]
    else:
        sf_ref[...] = jnp.zeros_like(sf_ref)
        sb_ref[...] = jnp.zeros_like(sb_ref)

    lane = lax.broadcasted_iota(jnp.int32, (C, RET_DK), 1)
    first_half = (lane & 32) == 0

    def rope(x, cs, sn):
        swapped = jnp.where(first_half, pltpu.roll(x, RET_DK - 32, axis=1), pltpu.roll(x, 32, axis=1))
        return x * cs + swapped * sn

    def state_update(s_ref, kz, v, cd):
        upd = lax.dot_general(kz, v, (((0,), (0,)), ((), ())), preferred_element_type=F32)
        s_ref[...] = cd * s_ref[...] + upd

    def chunk_rows(c):
        start = c * C
        return pl.ds(start if isinstance(start, int) else pl.multiple_of(start, C), C)

    def fwd_chunk(c):
        sl = chunk_rows(c)
        q = q_ref[sl, kcols]
        k = k_ref[sl, kcols] * (RET_DK ** -0.5)
        if use_rope:
            cs = rope_refs[0][sl, :]
            sn = rope_refs[1][sl, :]
            q = rope(q, cs, sn)
            k = rope(k, cs, sn)
        qb = q.astype(BF16)
        kb = k.astype(BF16)
        qs_ref[sl, :] = qb
        ks_ref[sl, :] = k
        v = v_ref[sl, vcols].astype(BF16)
        scores = lax.dot_general(qb, kb, (((1,), (1,)), ((), ())), preferred_element_type=F32)
        inner = jnp.dot((scores * decay).astype(BF16), v, preferred_element_type=F32)
        cross = jnp.dot(qb, sf_ref[...].astype(BF16), preferred_element_type=F32) * xi_f
        acc_ref[sl, :] = inner + cross
        state_update(sf_ref, (k * zeta_f).astype(BF16), v, cd_f)

    def bwd_chunk(c):
        sl = chunk_rows(c)
        qb = qs_ref[sl, :]
        k = ks_ref[sl, :]
        v = v_ref[sl, vcols].astype(BF16)
        cross = jnp.dot(qb, sb_ref[...].astype(BF16), preferred_element_type=F32) * xi_b
        o = acc_ref[sl, :] + cross
        mu = jnp.mean(o, axis=-1, keepdims=True)
        oc = o - mu
        var = jnp.mean(oc * oc, axis=-1, keepdims=True)
        on = oc * lax.rsqrt(var + GN_EPS)
        g = g_ref[sl, vcols]
        o_ref[sl, vcols] = (on * (g * _sigmoid(g))).astype(BF16)
        state_update(sb_ref, (k * zeta_b).astype(BF16), v, cd_b)

    def finish():
        if st_ref is not None:
            st_ref[0, 0, hh] = sf_ref[...]
            st_ref[0, 1, hh] = sb_ref[...]

    return fwd_chunk, bwd_chunk, finish


def _retention(proj, row0, B, L, ret_decay, state_in, rope_tabs, want_state, heads):
    rb = row0 // L
    use_rope = rope_tabs is not None
    has_state_in = state_in is not None
    wk = heads * RET_DK
    wv = heads * RET_DV
    kq = RET_QK_WIDTH // wk
    kv = 2 * RET_QK_WIDTH // wv
    kg = kv + RET_WIDTH // wv
    in_specs = [pl.BlockSpec((L, wk), lambda b, h, rd: (rb + b, h)),
                pl.BlockSpec((L, wk), lambda b, h, rd: (rb + b, kq + h)),
                pl.BlockSpec((L, wv), lambda b, h, rd: (rb + b, kv + h)),
                pl.BlockSpec((L, wv), lambda b, h, rd: (rb + b, kg + h))]
    args = [proj, proj, proj, proj]
    state_spec = pl.BlockSpec((1, 2, heads, RET_DK, RET_DV), lambda b, h, rd: (b, 0, h, 0, 0))
    if has_state_in:
        in_specs.append(state_spec)
        args.append(state_in)
    if use_rope:
        in_specs += [pl.BlockSpec((L, RET_DK), lambda b, h, rd: (0, 0))] * 2
        args += list(rope_tabs)
    out_specs = [pl.BlockSpec((L, wv), lambda b, h, rd: (b, h))]
    out_shape = [jax.ShapeDtypeStruct((B * L, RET_WIDTH), BF16)]
    if want_state:
        out_specs.append(state_spec)
        out_shape.append(jax.ShapeDtypeStruct((B, 2, RET_HEADS, RET_DK, RET_DV), F32))
    kern = functools.partial(_retention_kernel, seq_len=L, chunk=RET_CHUNK, use_rope=use_rope,
                             has_state_in=has_state_in, has_state_out=want_state, heads=heads)
    res = pl.pallas_call(
        kern,
        grid_spec=pltpu.PrefetchScalarGridSpec(
            num_scalar_prefetch=1,
            grid=(B, RET_HEADS // heads),
            in_specs=in_specs,
            out_specs=out_specs,
            scratch_shapes=[pltpu.VMEM((heads, L, RET_DK), BF16),
                            pltpu.VMEM((heads, L, RET_DK), F32),
                            pltpu.VMEM((heads, L, RET_DV), F32),
                            pltpu.VMEM((heads, RET_DK, RET_DV), F32),
                            pltpu.VMEM((heads, RET_DK, RET_DV), F32)]),
        out_shape=out_shape,
        compiler_params=_cparams(("arbitrary", "arbitrary"), 40 * 1024 * 1024),
        name="retention_grid" if use_rope else "retention_seq",
    )(ret_decay, *args)
    return res if want_state else (res[0], None)


def _split3(x):
    hi = x.astype(BF16)
    r1 = x - hi.astype(F32)
    mid = r1.astype(BF16)
    lo = (r1 - mid.astype(F32)).astype(BF16)
    return jnp.concatenate([hi, mid, lo], axis=-1)


def _pool_kernel(p_ref, w_ref, sc_ref, o_ref, pad_ref, *, seq_len, grid_mode):
    PT = POOL_TILE
    n_tiles = seq_len // PT
    ii = lax.broadcasted_iota(jnp.int32, (PT, PT), 0)
    jj = lax.broadcasted_iota(jnp.int32, (PT, PT), 1)
    d = jj - ii
    ti = lax.broadcasted_iota(jnp.int32, (PT, 1), 0)

    if grid_mode:
        zeros = jnp.zeros((POOL_PAD, POOL_DG), F32)
        pad_ref[0:POOL_PAD, :] = zeros
        pad_ref[POOL_PAD + seq_len:POOL_PAD + seq_len + POOL_PAD, :] = zeros

    for gi, w in enumerate(POOL_WINDOWS):
        lo_off = -(w // 2)
        hi_off = w - w // 2 - 1
        cols = slice(gi * POOL_DG, (gi + 1) * POOL_DG)
        in_window = (d >= lo_off) & (d <= hi_off)
        if grid_mode:
            in_window = in_window & ((ii // GRID_W) == (jj // GRID_W))
        band = jnp.where(in_window, 1.0, 0.0).astype(BF16)
        wg = w_ref[gi]
        scale = sc_ref[gi]

        def window_sum(t0):
            x = p_ref[pl.ds(t0, PT), cols]
            s3 = jnp.dot(band, _split3(x), preferred_element_type=F32)
            return s3[:, 0:POOL_DG] + s3[:, POOL_DG:2 * POOL_DG] + s3[:, 2 * POOL_DG:3 * POOL_DG]

        def finish(t0, s, cnt):
            x = p_ref[pl.ds(t0, PT), cols]
            diff = (s / cnt - x).astype(BF16)
            y = jnp.dot(diff, wg, preferred_element_type=F32) * scale
            o_ref[pl.ds(t0, PT), cols] = y.astype(BF16)

        def count(pos, n):
            lo = jnp.maximum(pos + lo_off, 0)
            hi = jnp.minimum(pos + lo_off + w, n)
            return (hi - lo).astype(F32)

        if grid_mode:
            for t in range(n_tiles):
                pad_ref[POOL_PAD + t * PT:POOL_PAD + (t + 1) * PT, :] = window_sum(t * PT)
            for t in range(n_tiles):
                base = POOL_PAD + t * PT
                s = pad_ref[base + lo_off * GRID_W:base + lo_off * GRID_W + PT, :]
                for r in range(lo_off + 1, hi_off + 1):
                    s = s + pad_ref[base + r * GRID_W:base + r * GRID_W + PT, :]
                tok = ti + t * PT
                cnt = count(tok // GRID_W, seq_len // GRID_W) * count(tok % GRID_W, GRID_W)
                finish(t * PT, s, cnt)
        else:
            for t in range(n_tiles):
                finish(t * PT, window_sum(t * PT), count(ti + t * PT, seq_len))


def _pooling(proj, row0, B, L, w_pool_bf16, pool_scale, grid_mode):
    if not grid_mode:
        assert L == POOL_TILE
    rb = row0 // L
    pcol = (IN_WIDTH - POOL_WIDTH) // POOL_WIDTH
    kern = functools.partial(_pool_kernel, seq_len=L, grid_mode=grid_mode)
    return pl.pallas_call(
        kern,
        grid=(B,),
        in_specs=[pl.BlockSpec((L, POOL_WIDTH), lambda b: (rb + b, pcol)),
                  pl.BlockSpec((POOL_GROUPS, POOL_DG, POOL_DG), lambda b: (0, 0, 0)),
                  pl.BlockSpec((POOL_GROUPS, 1, POOL_DG), lambda b: (0, 0, 0))],
        out_specs=pl.BlockSpec((L, POOL_WIDTH), lambda b: (b, 0)),
        out_shape=jax.ShapeDtypeStruct((B * L, POOL_WIDTH), BF16),
        scratch_shapes=[pltpu.VMEM((L + 2 * POOL_PAD, POOL_DG), F32)],
        compiler_params=_cparams(("arbitrary",), 48 * 1024 * 1024),
        name="pool_grid" if grid_mode else "pool_seq",
    )(proj, w_pool_bf16, pool_scale.reshape(POOL_GROUPS, 1, POOL_DG))


def _outproj_kernel(*refs):
    i = pl.program_id(0)
    h2_ref = refs[14]

    @pl.when(i >= T_ALL // TM_OUT)
    def _():
        h2_ref[...] = jnp.zeros_like(h2_ref)

    pl.when(i < T_ALL // TM_OUT)(lambda: _outproj_tile(*refs))


def _outproj_tile(retc_ref, retl_ref, poolc_ref, pooll_ref, xc_ref, xl_ref, mod_ref, g_ref,
                  wt_ref, wb_ref, wrh_ref, wrb_ref, br_ref,
                  x1_ref, h2_ref, ri_ref, rp_ref, rr_ref, cnt_ref, run_ref):
    i = pl.program_id(0)

    @pl.when(i == 0)
    def _():
        run_ref[...] = jnp.zeros_like(run_ref)

    is_ctx = i < T_CTX // TM_OUT
    ret = jnp.where(is_ctx, retc_ref[...], retl_ref[...])
    pool = jnp.where(is_ctx, poolc_ref[...], pooll_ref[...])
    x = jnp.where(is_ctx, xc_ref[...], xl_ref[...])
    y = (jnp.dot(ret, wt_ref[...], preferred_element_type=F32)
         + jnp.dot(pool, wb_ref[...], preferred_element_type=F32))
    x1 = x + mod_ref[0, 2:3, :] * y
    x1_ref[...] = x1
    hn = x1 * lax.rsqrt(jnp.mean(x1 * x1, axis=-1, keepdims=True) + NORM_EPS) * g_ref[...]
    h2 = hn * (1.0 + mod_ref[0, 4:5, :]) + mod_ref[0, 3:4, :]
    hi = h2.astype(BF16)
    h2_ref[...] = hi
    lo = (h2 - hi.astype(F32)).astype(BF16)
    both = jnp.dot(hi, wrb_ref[...], preferred_element_type=F32)
    logits = (both[:, 0:LANES] + jnp.dot(lo, wrh_ref[...], preferred_element_type=F32)
              + both[:, LANES:2 * LANES]) + br_ref[...]

    tm = logits.shape[0]
    lane = lax.broadcasted_iota(jnp.int32, (tm, LANES), 1)
    work = logits
    vals, idxs, hots = [], [], []
    for _ in range(TOP_K):
        m = jnp.max(work, axis=-1, keepdims=True)
        idx = jnp.min(jnp.where(work == m, lane, LANES), axis=-1, keepdims=True)
        hot = lane == idx
        vals.append(m)
        idxs.append(idx)
        hots.append(hot)
        work = jnp.where(hot, -jnp.inf, work)
    exps = [jnp.exp(v - vals[0]) for v in vals]
    denom = exps[0] + exps[1] + exps[2] + exps[3]

    selected = jnp.zeros((tm, LANES), F32)
    for hot in hots:
        selected = selected + jnp.where(hot, 1.0, 0.0)
    r_i = lax.broadcasted_iota(jnp.int32, (tm, tm), 0)
    c_i = lax.broadcasted_iota(jnp.int32, (tm, tm), 1)
    tri = jnp.where(c_i < r_i, 1.0, 0.0).astype(BF16)
    before = jnp.dot(tri, selected.astype(BF16), preferred_element_type=F32) + run_ref[0:1, :]

    ri = jnp.zeros((tm, LANES), jnp.int32)
    rp = jnp.zeros((tm, LANES), F32)
    rr = jnp.zeros((tm, LANES), jnp.int32)
    for k in range(TOP_K):
        rank = jnp.sum(jnp.where(hots[k], before, 0.0), axis=-1, keepdims=True).astype(jnp.int32)
        ri = jnp.where(lane == k, idxs[k], ri)
        rp = jnp.where(lane == k, exps[k] / denom, rp)
        rr = jnp.where(lane == k, rank, rr)
    ri_ref[...] = ri
    rp_ref[...] = rp
    rr_ref[...] = rr

    run = run_ref[0:1, :] + jnp.sum(selected, axis=0, keepdims=True)
    run_ref[...] = jnp.broadcast_to(run, run_ref.shape)
    cnt_ref[...] = jnp.broadcast_to(run, cnt_ref.shape).astype(jnp.int32)


def _out_projection(ret_ctx, ret_lat, pool_ctx, pool_lat, x_ctx, x_lat, mod3, norm_g, w_out_bf16,
                    wr_hi, wr_both, b_router_pad):
    tm = TM_OUT
    real = lambda i: jnp.minimum(i, T_ALL // tm - 1)
    row = lambda i: (real(i), 0)
    const = lambda i: (0, 0)
    crow = lambda i: (_ctx_tile(real(i), tm), 0)
    lrow = lambda i: (_lat_tile(real(i), tm), 0)
    return pl.pallas_call(
        _outproj_kernel,
        grid=(H2_ROWS // tm,),
        in_specs=[pl.BlockSpec((tm, RET_WIDTH), crow),
                  pl.BlockSpec((tm, RET_WIDTH), lrow),
                  pl.BlockSpec((tm, POOL_WIDTH), crow),
                  pl.BlockSpec((tm, POOL_WIDTH), lrow),
                  pl.BlockSpec((tm, D_MODEL), crow),
                  pl.BlockSpec((tm, D_MODEL), lrow),
                  pl.BlockSpec((1, 6, D_MODEL), lambda i: (_cond_row(real(i), tm), 0, 0)),
                  pl.BlockSpec((1, D_MODEL), const),
                  pl.BlockSpec((RET_WIDTH, D_MODEL), const),
                  pl.BlockSpec((POOL_WIDTH, D_MODEL), lambda i: (1, 0)),
                  pl.BlockSpec((D_MODEL, LANES), const),
                  pl.BlockSpec((D_MODEL, 2 * LANES), const),
                  pl.BlockSpec((1, LANES), const)],
        out_specs=[pl.BlockSpec((tm, D_MODEL), row),
                   pl.BlockSpec((tm, D_MODEL), lambda i: (i, 0)),
                   pl.BlockSpec((tm, LANES), row),
                   pl.BlockSpec((tm, LANES), row),
                   pl.BlockSpec((tm, LANES), row),
                   pl.BlockSpec((8, LANES), const)],
        out_shape=[jax.ShapeDtypeStruct((T_ALL, D_MODEL), F32),
                   jax.ShapeDtypeStruct((H2_ROWS, D_MODEL), BF16),
                   jax.ShapeDtypeStruct((T_ALL, LANES), jnp.int32),
                   jax.ShapeDtypeStruct((T_ALL, LANES), F32),
                   jax.ShapeDtypeStruct((T_ALL, LANES), jnp.int32),
                   jax.ShapeDtypeStruct((8, LANES), jnp.int32)],
        scratch_shapes=[pltpu.VMEM((8, LANES), F32)],
        compiler_params=_cparams(("arbitrary",), 48 * 1024 * 1024),
        name="out_projection_router",
    )(ret_ctx, ret_lat, pool_ctx, pool_lat, x_ctx, x_lat, mod3, norm_g.reshape(1, D_MODEL),
      w_out_bf16, w_out_bf16, wr_hi, wr_both, b_router_pad)


def _count_le(ends, v):
    return jnp.sum((ends[None, :] <= v[:, None]).astype(jnp.int32), axis=1)


def _expert_schedule(counts):
    i32 = jnp.int32
    ntile = (counts + TM_EXP - 1) // TM_EXP
    tile_off = jnp.cumsum(ntile) - ntile
    nsup = (ntile + SUP_EXP - 1) // SUP_EXP
    sup_end = jnp.cumsum(nsup)
    u = jnp.arange(U_EXP, dtype=i32)
    sup_valid = u < sup_end[-1]
    uc = jnp.minimum(u, sup_end[-1] - 1)
    sup_e = jnp.minimum(_count_le(sup_end, uc), N_EXPERTS - 1)
    sup_idx = uc - (sup_end - nsup)[sup_e]
    per_sup = (ntile + jnp.maximum(nsup, 1) - 1) // jnp.maximum(nsup, 1)
    sup_row0 = (tile_off[sup_e] + sup_idx * per_sup[sup_e]) * TM_EXP
    sup_nt = jnp.clip(ntile[sup_e] - sup_idx * per_sup[sup_e], 1, per_sup[sup_e])
    step_e = jnp.repeat(sup_e, J_EXP)
    step_j = jnp.tile(jnp.arange(J_EXP, dtype=i32), U_EXP)
    step_row0 = jnp.repeat(sup_row0, J_EXP)
    step_nt = jnp.repeat(sup_nt, J_EXP)
    last_rows = counts - (ntile - 1) * TM_EXP
    sup_half = (sup_idx == nsup[sup_e] - 1) & (last_rows[sup_e] <= TM_EXP // 2) & (counts[sup_e] > 0)
    step_half = jnp.repeat(sup_half.astype(i32), J_EXP)
    bounds = jnp.asarray(CH_START_EXP[1:], dtype=i32)
    before = jnp.sum((sup_valid[None, :] & (sup_row0[None, :] < bounds[:, None])).astype(i32), axis=1) * J_EXP
    n_steps = jnp.sum(sup_valid.astype(i32)) * J_EXP
    chunk_start = jnp.concatenate([jnp.zeros((1,), i32), before])
    chunk_steps = jnp.concatenate([before, n_steps.reshape(1)]) - chunk_start
    tables = (step_e.astype(i32), step_j.astype(i32), step_row0.astype(i32), step_nt.astype(i32), step_half)
    return tile_off * TM_EXP, tables, chunk_start, chunk_steps, jnp.sum(ntile).astype(i32)


def _expert_kernel(se_ref, sj_ref, srow_ref, snt_ref, shalf_ref, meta_ref,
                   xs_hbm, wg_ref, wu_ref, bg_ref, bu_ref, wd_ref, bd_ref, *rest, first_chunk):
    ys_hbm, acc_ref, x_buf, o_buf, x_sem, o_sem = rest if first_chunk else rest[1:]
    s = pl.program_id(0) + meta_ref[0]
    j = sj_ref[s]
    n_tiles = snt_ref[s]
    row0 = pl.multiple_of(srow_ref[s], TM_EXP)
    last = J_EXP - 1

    def x_copy(first_row):
        rows = pl.ds(pl.multiple_of(first_row - meta_ref[1], TM_EXP), SUP_EXP * TM_EXP)
        return pltpu.make_async_copy(xs_hbm.at[rows], x_buf, x_sem)

    def out_copy(t):
        rows = pl.ds(pl.multiple_of(row0 + t * TM_EXP, TM_EXP), TM_EXP)
        return pltpu.make_async_copy(o_buf.at[t % 2], ys_hbm.at[rows], o_sem.at[t % 2])

    @pl.when((pl.program_id(0) == 0) & first_chunk)
    def _():
        o_buf[0] = jnp.zeros((TM_EXP, D_MODEL), BF16)

        def zero_copy(t):
            rows = pl.ds(pl.multiple_of(t * TM_EXP, TM_EXP), TM_EXP)
            return pltpu.make_async_copy(o_buf.at[0], ys_hbm.at[rows], o_sem.at[0])

        def start(t, carry):
            zero_copy(t).start()
            return carry

        def wait(t, carry):
            zero_copy(t).wait()
            return carry

        lax.fori_loop(meta_ref[2], NT_EXP, start, 0)
        lax.fori_loop(meta_ref[2], NT_EXP, wait, 0)

    def step():
        @pl.when(j == 0)
        def _():
            @pl.when(pl.program_id(0) == 0)
            def _():
                x_copy(row0).start()

            acc_ref[...] = jnp.zeros_like(acc_ref)
            x_copy(row0).wait()

        def accumulate(rows):
            x = x_buf[rows, :]
            gate = _dot(x, wg_ref[0]) + bg_ref[0]
            up = _dot(x, wu_ref[0]) + bu_ref[0]
            gate = jnp.minimum(gate, SWIGLU_LIMIT)
            up = jnp.clip(up, -SWIGLU_LIMIT, SWIGLU_LIMIT)
            act = (up + 1.0) * gate * _sigmoid(SWIGLU_ALPHA * gate)
            acc_ref[rows, :] += _dot(act, wd_ref[0])

        def tile_pair(p, carry):
            base = pl.multiple_of(p * (2 * TM_EXP), 2 * TM_EXP)
            accumulate(pl.ds(base, TM_EXP))
            accumulate(pl.ds(base + TM_EXP, TM_EXP))
            return carry

        n_full = n_tiles - shalf_ref[s]
        lax.fori_loop(0, n_full // 2, tile_pair, 0)

        @pl.when(n_full % 2 == 1)
        def _():
            accumulate(pl.ds(pl.multiple_of((n_full - 1) * TM_EXP, TM_EXP), TM_EXP))

        @pl.when(shalf_ref[s] == 1)
        def _():
            accumulate(pl.ds(pl.multiple_of((n_tiles - 1) * TM_EXP, TM_EXP), TM_EXP // 2))

        @pl.when(j == last)
        def _():
            @pl.when(pl.program_id(0) + 1 < meta_ref[3])
            def _():
                x_copy(pl.multiple_of(srow_ref[s + 1], TM_EXP)).start()

            def finish(t, carry):
                @pl.when(t >= 2)
                def _():
                    out_copy(t - 2).wait()

                rows = pl.ds(pl.multiple_of(t * TM_EXP, TM_EXP), TM_EXP)
                o_buf[t % 2] = (acc_ref[rows, :] + bd_ref[0]).astype(BF16)
                out_copy(t).start()
                return carry

            lax.fori_loop(0, n_tiles, finish, 0)

            @pl.when(n_tiles >= 2)
            def _():
                out_copy(n_tiles - 2).wait()

            out_copy(n_tiles - 1).wait()

    step()


def _experts_chunk(chunk, xs_part, ys, tables, meta, n_steps, w_gate_up, b_gate_up, w_down, b_down):
    first = ys is None
    wmap = lambda col0: (lambda s, se, sj, srow, snt, shalf, meta: (se[s + meta[0]], 0, col0 + sj[s + meta[0]]))
    n_prefetch = len(tables) + 1
    in_specs = [
        pl.BlockSpec(memory_space=pl.ANY),
        pl.BlockSpec((1, D_MODEL, F_EXP), wmap(0)),
        pl.BlockSpec((1, D_MODEL, F_EXP), wmap(J_EXP)),
        pl.BlockSpec((1, 1, F_EXP), wmap(0)),
        pl.BlockSpec((1, 1, F_EXP), wmap(J_EXP)),
        pl.BlockSpec((1, F_EXP, D_MODEL),
                     lambda s, se, sj, srow, snt, shalf, meta: (se[s + meta[0]], sj[s + meta[0]], 0)),
        pl.BlockSpec((1, 1, D_MODEL), lambda s, se, sj, srow, snt, shalf, meta: (se[s + meta[0]], 0, 0)),
    ]
    args = [xs_part, w_gate_up, w_gate_up,
            b_gate_up.reshape(N_EXPERTS, 1, 2 * D_FF), b_gate_up.reshape(N_EXPERTS, 1, 2 * D_FF),
            w_down, b_down.reshape(N_EXPERTS, 1, D_MODEL)]
    aliases = {}
    if not first:
        in_specs.append(pl.BlockSpec(memory_space=pl.ANY))
        args.append(ys)
        aliases = {n_prefetch + len(args) - 1: 0}
    return pl.pallas_call(
        functools.partial(_expert_kernel, first_chunk=first),
        grid_spec=pltpu.PrefetchScalarGridSpec(
            num_scalar_prefetch=n_prefetch,
            grid=(n_steps,),
            in_specs=in_specs,
            out_specs=pl.BlockSpec(memory_space=pl.ANY),
            scratch_shapes=[pltpu.VMEM((SUP_EXP * TM_EXP, D_MODEL), F32),
                            pltpu.VMEM((SUP_EXP * TM_EXP, D_MODEL), BF16),
                            pltpu.VMEM((2, TM_EXP, D_MODEL), BF16),
                            pltpu.SemaphoreType.DMA(()),
                            pltpu.SemaphoreType.DMA((2,))]),
        out_shape=jax.ShapeDtypeStruct((NT_EXP * TM_EXP, D_MODEL), BF16),
        input_output_aliases=aliases,
        compiler_params=_cparams(("arbitrary",), VMEM_LIMIT),
        name=f"experts_chunk{chunk}",
    )(*tables, meta, *args)


def _combine_kernel(x1_ref, y0_ref, y1_ref, y2_ref, y3_ref, rp_ref, mod_ref, g_ref, o_ref):
    rp = rp_ref[...]
    moe = rp[:, 0:1] * y0_ref[...].astype(F32)
    for k, y_ref in enumerate((y1_ref, y2_ref, y3_ref), start=1):
        moe = moe + rp[:, k:k + 1] * y_ref[...].astype(F32)
    x2 = x1_ref[...] + mod_ref[0, 5:6, :] * moe
    o_ref[...] = x2 * lax.rsqrt(jnp.mean(x2 * x2, axis=-1, keepdims=True) + NORM_EPS) * g_ref[...]


def _combine(x1, y4, route_p, mod3, final_g, row0, n_rows):
    tm = TM_CMB
    nt = n_rows // tm
    t0 = row0 // tm
    y_specs = [pl.BlockSpec((tm, D_MODEL), functools.partial(lambda k, i: (k * nt + i, 0), k))
               for k in range(TOP_K)]
    return pl.pallas_call(
        _combine_kernel,
        grid=(nt,),
        in_specs=[pl.BlockSpec((tm, D_MODEL), lambda i: (t0 + i, 0))] + y_specs + [
            pl.BlockSpec((tm, LANES), lambda i: (t0 + i, 0)),
            pl.BlockSpec((1, 6, D_MODEL), lambda i: (_cond_row(t0 + i, tm), 0, 0)),
            pl.BlockSpec((1, D_MODEL), lambda i: (0, 0))],
        out_specs=pl.BlockSpec((tm, D_MODEL), lambda i: (i, 0)),
        out_shape=jax.ShapeDtypeStruct((n_rows, D_MODEL), F32),
        compiler_params=_cparams(("arbitrary",), 40 * 1024 * 1024),
        name="combine_final_norm",
    )(x1, y4, y4, y4, y4, route_p, mod3, final_g.reshape(1, D_MODEL))


def kernel(x_prompt, x_sample, c, c_ctx, state_ret, norm1_g, norm2_g, w_mod, b_mod, w_in, ret_decay,
           w_pool, pool_scale, w_out, w_router, b_router, w_gate_up, b_gate_up, w_down, b_down, final_g):
    assert w_mod.shape[0] == 1, "single trunk layer"
    x_ctx = x_prompt.reshape(T_CTX, D_MODEL)
    x_lat = x_sample.reshape(T_LAT, D_MODEL)
    cond = jnp.zeros((COND_ROWS, D_MODEL), F32).at[0].set(c_ctx).at[1:N_COND].set(c)

    mod = _modulation(cond, w_mod[0], b_mod[0])
    mod3 = mod.reshape(COND_ROWS, 6, D_MODEL)

    proj = _in_projection(x_ctx, x_lat, norm1_g[0], mod3, w_in[0].astype(BF16))

    ret_ctx, new_state = _retention(proj, 0, BATCH, SEQ, ret_decay[0], None, None, True, RET_HEADS)
    ret_lat, _ = _retention(proj, T_CTX, DEC_BATCH, DEC_SEQ, ret_decay[0], state_ret[:, 0],
                            _rope_tables(DEC_SEQ), False, 2)
    w_pool_bf16 = w_pool[0].astype(BF16)
    pool_ctx = _pooling(proj, 0, BATCH, SEQ, w_pool_bf16, pool_scale[0], False)
    pool_lat = _pooling(proj, T_CTX, DEC_BATCH, DEC_SEQ, w_pool_bf16, pool_scale[0], True)

    wr = jnp.zeros((D_MODEL, LANES), F32).at[:, :N_EXPERTS].set(w_router[0])
    wr_hi = wr.astype(BF16)
    wr_both = jnp.concatenate([wr_hi, (wr - wr_hi.astype(F32)).astype(BF16)], axis=1)
    br = jnp.full((1, LANES), -1e30, F32).at[0, :N_EXPERTS].set(b_router[0])
    x1, h2, route_i, route_p, route_r, counts = _out_projection(
        ret_ctx, ret_lat, pool_ctx, pool_lat, x_ctx, x_lat, mod3, norm2_g[0], w_out[0].astype(BF16),
        wr_hi, wr_both, br)

    row_off, tables, chunk_start, chunk_steps, n_row_tiles = _expert_schedule(counts[0, :N_EXPERTS])
    slot = (row_off[route_i[:, :TOP_K]] + route_r[:, :TOP_K]).T
    token = jnp.broadcast_to(jnp.arange(T_ALL, dtype=jnp.int32)[None, :], (TOP_K, T_ALL))
    token_of_slot = (jnp.arange(XS_ROWS, dtype=jnp.int32) % T_ALL).at[slot.reshape(-1)].set(
        token.reshape(-1), unique_indices=True, mode="promise_in_bounds")
    ys = None
    for c in range(CHUNKS_EXP):
        lo = CH_START_EXP[c]
        nxt = CH_START_EXP[c + 1] if c + 1 < CHUNKS_EXP else XS_ROWS
        hi = min(nxt + SUP_EXP * TM_EXP, XS_ROWS)
        xs_part = h2.at[token_of_slot[lo:hi]].get(mode="promise_in_bounds")
        meta = jnp.stack([chunk_start[c], jnp.int32(lo), n_row_tiles, chunk_steps[c]]).astype(jnp.int32)
        ys = _experts_chunk(c, xs_part, ys, tables, meta, chunk_steps[c],
                            w_gate_up[0], b_gate_up[0], w_down[0], b_down[0])
    y4_lat = ys.at[slot[:, T_CTX:].reshape(-1)].get(mode="promise_in_bounds", unique_indices=True)
    y4_ctx = ys.at[slot[:, :T_CTX].reshape(-1)].get(mode="promise_in_bounds", unique_indices=True)
    y_lat = _combine(x1, y4_lat, route_p, mod3, final_g, T_CTX, T_LAT)
    y_ctx = _combine(x1, y4_ctx, route_p, mod3, final_g, 0, T_CTX)
    return (y_ctx.reshape(BATCH, SEQ, D_MODEL), y_lat.reshape(DEC_BATCH, DEC_SEQ, D_MODEL),
            new_state.reshape(BATCH, 1, 2, RET_HEADS, RET_DK, RET_DV))
```

```python
import functools

import numpy as np
import jax
import jax.numpy as jnp
from jax import lax
from jax.experimental import pallas as pl
from jax.experimental.pallas import tpu as pltpu

F32 = jnp.float32
BF16 = jnp.bfloat16

D_MODEL = 2048
BATCH = 16
SEQ = 256
DEC_BATCH = 4
DEC_SEQ = 2048
GRID_W = 64
RET_HEADS = 4
RET_DK = 128
RET_DV = 256
RET_QK_WIDTH = RET_HEADS * RET_DK
RET_WIDTH = RET_HEADS * RET_DV
POOL_WINDOWS = (2, 4, 8, 16)
POOL_GROUPS = 4
POOL_DG = 256
POOL_WIDTH = POOL_GROUPS * POOL_DG
IN_WIDTH = 2 * RET_QK_WIDTH + 2 * RET_WIDTH + POOL_WIDTH
N_EXPERTS = 32
TOP_K = 4
D_FF = D_MODEL
SWIGLU_LIMIT = 7.0
SWIGLU_ALPHA = 1.702
ROPE_BASE = 10000.0
NORM_EPS = 1e-6
GN_EPS = 1e-6

T_CTX = BATCH * SEQ
T_LAT = DEC_BATCH * DEC_SEQ
T_ALL = T_CTX + T_LAT
N_COND = 1 + DEC_BATCH
COND_ROWS = 8
LANES = 128

RET_CHUNK = 256

TM_IN = 512
TN_IN = 2048
TM_OUT = 256
TM_CMB = 256
H2_ROWS = 16384
POOL_TILE = 256
POOL_PAD = (max(POOL_WINDOWS) // 2) * GRID_W

TM_EXP = 256
F_EXP = 512
J_EXP = D_FF // F_EXP
SUP_EXP = 8
NT_EXP = (T_ALL * TOP_K + N_EXPERTS * (TM_EXP - 1)) // TM_EXP
U_EXP = N_EXPERTS + NT_EXP // SUP_EXP + 1
XS_ROWS = (NT_EXP + SUP_EXP) * TM_EXP
CH_START_EXP = tuple(t * TM_EXP for t in (0, 24))
CHUNKS_EXP = len(CH_START_EXP)

VMEM_LIMIT = 56 * 1024 * 1024


def _cparams(sem, vmem=None):
    return pltpu.CompilerParams(dimension_semantics=sem, vmem_limit_bytes=vmem)


def _cond_row(i, tm):
    nctx = T_CTX // tm
    return jnp.where(i < nctx, 0, 1 + (i - nctx) // (DEC_SEQ // tm))


def _sigmoid(x):
    return 1.0 / (1.0 + jnp.exp(-x))


def _dot(a, b):
    return lax.dot_general(a, b, (((1,), (0,)), ((), ())), preferred_element_type=F32)


def _mod_kernel(c_ref, w_ref, b_ref, o_ref):
    c = c_ref[...]
    s = (c * _sigmoid(c)).astype(BF16)
    o_ref[...] = jnp.dot(s, w_ref[...].astype(BF16), preferred_element_type=F32) + b_ref[...]


def _modulation(cond, w_mod, b_mod):
    n = w_mod.shape[1]
    tn = 1024
    return pl.pallas_call(
        _mod_kernel,
        grid=(n // tn,),
        in_specs=[pl.BlockSpec((COND_ROWS, D_MODEL), lambda j: (0, 0)),
                  pl.BlockSpec((D_MODEL, tn), lambda j: (0, j)),
                  pl.BlockSpec((1, tn), lambda j: (0, j))],
        out_specs=pl.BlockSpec((COND_ROWS, tn), lambda j: (0, j)),
        out_shape=jax.ShapeDtypeStruct((COND_ROWS, n), F32),
        compiler_params=_cparams(("arbitrary",), 40 * 1024 * 1024),
        name="modulation",
    )(cond, w_mod, b_mod.reshape(1, n))


def _ctx_tile(i, tm):
    return jnp.minimum(i, T_CTX // tm - 1)


def _lat_tile(i, tm):
    return jnp.maximum(i - T_CTX // tm, 0)


def _inproj_kernel(xc_ref, xl_ref, g_ref, mod_ref, w_ref, o_ref):
    is_ctx = pl.program_id(0) < T_CTX // TM_IN
    x = jnp.where(is_ctx, xc_ref[...], xl_ref[...])
    y = x * lax.rsqrt(jnp.mean(x * x, axis=-1, keepdims=True) + NORM_EPS) * g_ref[...]
    h = (y * (1.0 + mod_ref[0, 1:2, :]) + mod_ref[0, 0:1, :]).astype(BF16)
    o_ref[...] = jnp.dot(h, w_ref[...], preferred_element_type=F32)


def _in_projection(x_ctx, x_lat, norm_g, mod3, w_in_bf16):
    return pl.pallas_call(
        _inproj_kernel,
        grid=(T_ALL // TM_IN, IN_WIDTH // TN_IN),
        in_specs=[pl.BlockSpec((TM_IN, D_MODEL), lambda i, j: (_ctx_tile(i, TM_IN), 0)),
                  pl.BlockSpec((TM_IN, D_MODEL), lambda i, j: (_lat_tile(i, TM_IN), 0)),
                  pl.BlockSpec((1, D_MODEL), lambda i, j: (0, 0)),
                  pl.BlockSpec((1, 6, D_MODEL), lambda i, j: (_cond_row(i, TM_IN), 0, 0)),
                  pl.BlockSpec((D_MODEL, TN_IN), lambda i, j: (0, j))],
        out_specs=pl.BlockSpec((TM_IN, TN_IN), lambda i, j: (i, j)),
        out_shape=jax.ShapeDtypeStruct((T_ALL, IN_WIDTH), F32),
        compiler_params=_cparams(("arbitrary", "arbitrary"), 48 * 1024 * 1024),
        name="in_projection",
    )(x_ctx, x_lat, norm_g.reshape(1, D_MODEL), mod3, w_in_bf16)


def _rope_tables(seq_len):
    t = jnp.arange(seq_len)
    row = (t // GRID_W).astype(F32)
    col = (t % GRID_W).astype(F32)
    half = RET_DK // 2
    n_freq = half // 2
    inv = ROPE_BASE ** (-jnp.arange(n_freq, dtype=F32) / n_freq)
    ang_r = row[:, None] * inv
    ang_c = col[:, None] * inv
    cos = jnp.concatenate([jnp.cos(ang_r), jnp.cos(ang_r), jnp.cos(ang_c), jnp.cos(ang_c)], axis=-1)
    sin = jnp.concatenate([-jnp.sin(ang_r), jnp.sin(ang_r), -jnp.sin(ang_c), jnp.sin(ang_c)], axis=-1)
    return cos, sin


def _retention_kernel(rd_ref, *refs, seq_len, chunk, use_rope, has_state_in, has_state_out, heads):
    refs = list(refs)
    q_ref, k_ref, v_ref, g_ref = refs[:4]
    pos = 4
    if has_state_in:
        s0_ref = refs[pos]
        pos += 1
    if use_rope:
        cos_ref, sin_ref = refs[pos], refs[pos + 1]
        pos += 2
    o_ref = refs[pos]
    pos += 1
    if has_state_out:
        st_ref = refs[pos]
        pos += 1
    scratch = refs[pos:pos + 5]

    progs = [_retention_head(rd_ref, pl.program_id(1) * heads + hh, hh,
                             q_ref, k_ref, v_ref, g_ref,
                             s0_ref if has_state_in else None,
                             (cos_ref, sin_ref) if use_rope else None,
                             o_ref, st_ref if has_state_out else None,
                             [r.at[hh] for r in scratch], seq_len, chunk)
             for hh in range(heads)]
    n_chunks = seq_len // chunk
    if n_chunks == 1:
        for fwd_chunk, _, _ in progs:
            fwd_chunk(0)
        for _, bwd_chunk, _ in progs:
            bwd_chunk(0)
    else:
        assert n_chunks % 2 == 0

        def fwd_pair(i, carry):
            for fwd_chunk, _, _ in progs:
                fwd_chunk(2 * i)
                fwd_chunk(2 * i + 1)
            return carry

        def bwd_pair(i, carry):
            for _, bwd_chunk, _ in progs:
                bwd_chunk(n_chunks - 1 - 2 * i)
                bwd_chunk(n_chunks - 2 - 2 * i)
            return carry

        lax.fori_loop(0, n_chunks // 2, fwd_pair, 0)
        lax.fori_loop(0, n_chunks // 2, bwd_pair, 0)
    for _, _, finish in progs:
        finish()


def _retention_head(rd_ref, h, hh, q_ref, k_ref, v_ref, g_ref, s0_ref, rope_refs, o_ref, st_ref, scratch,
                    seq_len, chunk):
    qs_ref, ks_ref, acc_ref, sf_ref, sb_ref = scratch
    use_rope = rope_refs is not None
    kcols = slice(hh * RET_DK, (hh + 1) * RET_DK)
    vcols = slice(hh * RET_DV, (hh + 1) * RET_DV)
    C = chunk
    n_chunks = seq_len // C

    lgf = -jnp.exp(jnp.full((C, 1), rd_ref[0, h], F32))
    lgb = -jnp.exp(jnp.full((C, 1), rd_ref[1, h], F32))
    ii = lax.broadcasted_iota(jnp.int32, (C, C), 0)
    jj = lax.broadcasted_iota(jnp.int32, (C, C), 1)
    diff = (ii - jj).astype(F32)
    decay = (jnp.where(diff >= 0, jnp.exp(lgf * jnp.maximum(diff, 0.0)), 0.0)
             + jnp.where(diff <= 0, jnp.exp(lgb * jnp.maximum(-diff, 0.0)), 0.0))
    p = lax.broadcasted_iota(jnp.int32, (C, 1), 0).astype(F32)
    xi_f = jnp.exp(lgf * (p + 1.0))
    zeta_f = jnp.exp(lgf * (C - 1.0 - p))
    xi_b = jnp.exp(lgb * (C - p))
    zeta_b = jnp.exp(lgb * p)
    cd_f = jnp.exp(lgf[0:1, :] * C)
    cd_b = jnp.exp(lgb[0:1, :] * C)

    if s0_ref is not None:
        sf_ref[...] = s0_ref[0, 0, hh]
        sb_ref[...] = s0_ref[0, 1, hh]
    else:
        sf_ref[...] = jnp.zeros_like(sf_ref)
        sb_ref[...] = jnp.zeros_like(sb_ref)

    lane = lax.broadcasted_iota(jnp.int32, (C, RET_DK), 1)
    first_half = (lane & 32) == 0

    def rope(x, cs, sn):
        swapped = jnp.where(first_half, pltpu.roll(x, RET_DK - 32, axis=1), pltpu.roll(x, 32, axis=1))
        return x * cs + swapped * sn

    def state_update(s_ref, kz, v, cd):
        upd = lax.dot_general(kz, v, (((0,), (0,)), ((), ())), preferred_element_type=F32)
        s_ref[...] = cd * s_ref[...] + upd

    def chunk_rows(c):
        start = c * C
        return pl.ds(start if isinstance(start, int) else pl.multiple_of(start, C), C)

    def fwd_chunk(c):
        sl = chunk_rows(c)
        q = q_ref[sl, kcols]
        k = k_ref[sl, kcols] * (RET_DK ** -0.5)
        if use_rope:
            cs = rope_refs[0][sl, :]
            sn = rope_refs[1][sl, :]
            q = rope(q, cs, sn)
            k = rope(k, cs, sn)
        qb = q.astype(BF16)
        kb = k.astype(BF16)
        qs_ref[sl, :] = qb
        ks_ref[sl, :] = k
        v = v_ref[sl, vcols].astype(BF16)
        scores = lax.dot_general(qb, kb, (((1,), (1,)), ((), ())), preferred_element_type=F32)
        inner = jnp.dot((scores * decay).astype(BF16), v, preferred_element_type=F32)
        cross = jnp.dot(qb, sf_ref[...].astype(BF16), preferred_element_type=F32) * xi_f
        acc_ref[sl, :] = inner + cross
        state_update(sf_ref, (k * zeta_f).astype(BF16), v, cd_f)

    def bwd_chunk(c):
        sl = chunk_rows(c)
        qb = qs_ref[sl, :]
        k = ks_ref[sl, :]
        v = v_ref[sl, vcols].astype(BF16)
        cross = jnp.dot(qb, sb_ref[...].astype(BF16), preferred_element_type=F32) * xi_b
        o = acc_ref[sl, :] + cross
        mu = jnp.mean(o, axis=-1, keepdims=True)
        oc = o - mu
        var = jnp.mean(oc * oc, axis=-1, keepdims=True)
        on = oc * lax.rsqrt(var + GN_EPS)
        g = g_ref[sl, vcols]
        o_ref[sl, vcols] = (on * (g * _sigmoid(g))).astype(BF16)
        state_update(sb_ref, (k * zeta_b).astype(BF16), v, cd_b)

    def finish():
        if st_ref is not None:
            st_ref[0, 0, hh] = sf_ref[...]
            st_ref[0, 1, hh] = sb_ref[...]

    return fwd_chunk, bwd_chunk, finish


def _retention(proj, row0, B, L, ret_decay, state_in, rope_tabs, want_state, heads):
    rb = row0 // L
    use_rope = rope_tabs is not None
    has_state_in = state_in is not None
    wk = heads * RET_DK
    wv = heads * RET_DV
    kq = RET_QK_WIDTH // wk
    kv = 2 * RET_QK_WIDTH // wv
    kg = kv + RET_WIDTH // wv
    in_specs = [pl.BlockSpec((L, wk), lambda b, h, rd: (rb + b, h)),
                pl.BlockSpec((L, wk), lambda b, h, rd: (rb + b, kq + h)),
                pl.BlockSpec((L, wv), lambda b, h, rd: (rb + b, kv + h)),
                pl.BlockSpec((L, wv), lambda b, h, rd: (rb + b, kg + h))]
    args = [proj, proj, proj, proj]
    state_spec = pl.BlockSpec((1, 2, heads, RET_DK, RET_DV), lambda b, h, rd: (b, 0, h, 0, 0))
    if has_state_in:
        in_specs.append(state_spec)
        args.append(state_in)
    if use_rope:
        in_specs += [pl.BlockSpec((L, RET_DK), lambda b, h, rd: (0, 0))] * 2
        args += list(rope_tabs)
    out_specs = [pl.BlockSpec((L, wv), lambda b, h, rd: (b, h))]
    out_shape = [jax.ShapeDtypeStruct((B * L, RET_WIDTH), BF16)]
    if want_state:
        out_specs.append(state_spec)
        out_shape.append(jax.ShapeDtypeStruct((B, 2, RET_HEADS, RET_DK, RET_DV), F32))
    kern = functools.partial(_retention_kernel, seq_len=L, chunk=RET_CHUNK, use_rope=use_rope,
                             has_state_in=has_state_in, has_state_out=want_state, heads=heads)
    res = pl.pallas_call(
        kern,
        grid_spec=pltpu.PrefetchScalarGridSpec(
            num_scalar_prefetch=1,
            grid=(B, RET_HEADS // heads),
            in_specs=in_specs,
            out_specs=out_specs,
            scratch_shapes=[pltpu.VMEM((heads, L, RET_DK), BF16),
                            pltpu.VMEM((heads, L, RET_DK), F32),
                            pltpu.VMEM((heads, L, RET_DV), F32),
                            pltpu.VMEM((heads, RET_DK, RET_DV), F32),
                            pltpu.VMEM((heads, RET_DK, RET_DV), F32)]),
        out_shape=out_shape,
        compiler_params=_cparams(("arbitrary", "arbitrary"), 40 * 1024 * 1024),
        name="retention_grid" if use_rope else "retention_seq",
    )(ret_decay, *args)
    return res if want_state else (res[0], None)


def _split3(x):
    hi = x.astype(BF16)
    r1 = x - hi.astype(F32)
    mid = r1.astype(BF16)
    lo = (r1 - mid.astype(F32)).astype(BF16)
    return jnp.concatenate([hi, mid, lo], axis=-1)


def _pool_kernel(p_ref, w_ref, sc_ref, o_ref, pad_ref, *, seq_len, grid_mode):
    PT = POOL_TILE
    n_tiles = seq_len // PT
    ii = lax.broadcasted_iota(jnp.int32, (PT, PT), 0)
    jj = lax.broadcasted_iota(jnp.int32, (PT, PT), 1)
    d = jj - ii
    ti = lax.broadcasted_iota(jnp.int32, (PT, 1), 0)

    if grid_mode:
        zeros = jnp.zeros((POOL_PAD, POOL_DG), F32)
        pad_ref[0:POOL_PAD, :] = zeros
        pad_ref[POOL_PAD + seq_len:POOL_PAD + seq_len + POOL_PAD, :] = zeros

    for gi, w in enumerate(POOL_WINDOWS):
        lo_off = -(w // 2)
        hi_off = w - w // 2 - 1
        cols = slice(gi * POOL_DG, (gi + 1) * POOL_DG)
        in_window = (d >= lo_off) & (d <= hi_off)
        if grid_mode:
            in_window = in_window & ((ii // GRID_W) == (jj // GRID_W))
        band = jnp.where(in_window, 1.0, 0.0).astype(BF16)
        wg = w_ref[gi]
        scale = sc_ref[gi]

        def window_sum(t0):
            x = p_ref[pl.ds(t0, PT), cols]
            s3 = jnp.dot(band, _split3(x), preferred_element_type=F32)
            return s3[:, 0:POOL_DG] + s3[:, POOL_DG:2 * POOL_DG] + s3[:, 2 * POOL_DG:3 * POOL_DG]

        def finish(t0, s, cnt):
            x = p_ref[pl.ds(t0, PT), cols]
            diff = (s / cnt - x).astype(BF16)
            y = jnp.dot(diff, wg, preferred_element_type=F32) * scale
            o_ref[pl.ds(t0, PT), cols] = y.astype(BF16)

        def count(pos, n):
            lo = jnp.maximum(pos + lo_off, 0)
            hi = jnp.minimum(pos + lo_off + w, n)
            return (hi - lo).astype(F32)

        if grid_mode:
            for t in range(n_tiles):
                pad_ref[POOL_PAD + t * PT:POOL_PAD + (t + 1) * PT, :] = window_sum(t * PT)
            for t in range(n_tiles):
                base = POOL_PAD + t * PT
                s = pad_ref[base + lo_off * GRID_W:base + lo_off * GRID_W + PT, :]
                for r in range(lo_off + 1, hi_off + 1):
                    s = s + pad_ref[base + r * GRID_W:base + r * GRID_W + PT, :]
                tok = ti + t * PT
                cnt = count(tok // GRID_W, seq_len // GRID_W) * count(tok % GRID_W, GRID_W)
                finish(t * PT, s, cnt)
        else:
            for t in range(n_tiles):
                finish(t * PT, window_sum(t * PT), count(ti + t * PT, seq_len))


def _pooling(proj, row0, B, L, w_pool_bf16, pool_scale, grid_mode):
    if not grid_mode:
        assert L == POOL_TILE
    rb = row0 // L
    pcol = (IN_WIDTH - POOL_WIDTH) // POOL_WIDTH
    kern = functools.partial(_pool_kernel, seq_len=L, grid_mode=grid_mode)
    return pl.pallas_call(
        kern,
        grid=(B,),
        in_specs=[pl.BlockSpec((L, POOL_WIDTH), lambda b: (rb + b, pcol)),
                  pl.BlockSpec((POOL_GROUPS, POOL_DG, POOL_DG), lambda b: (0, 0, 0)),
                  pl.BlockSpec((POOL_GROUPS, 1, POOL_DG), lambda b: (0, 0, 0))],
        out_specs=pl.BlockSpec((L, POOL_WIDTH), lambda b: (b, 0)),
        out_shape=jax.ShapeDtypeStruct((B * L, POOL_WIDTH), BF16),
        scratch_shapes=[pltpu.VMEM((L + 2 * POOL_PAD, POOL_DG), F32)],
        compiler_params=_cparams(("arbitrary",), 48 * 1024 * 1024),
        name="pool_grid" if grid_mode else "pool_seq",
    )(proj, w_pool_bf16, pool_scale.reshape(POOL_GROUPS, 1, POOL_DG))


def _outproj_kernel(*refs):
    i = pl.program_id(0)
    h2_ref = refs[14]

    @pl.when(i >= T_ALL // TM_OUT)
    def _():
        h2_ref[...] = jnp.zeros_like(h2_ref)

    pl.when(i < T_ALL // TM_OUT)(lambda: _outproj_tile(*refs))


def _outproj_tile(retc_ref, retl_ref, poolc_ref, pooll_ref, xc_ref, xl_ref, mod_ref, g_ref,
                  wt_ref, wb_ref, wrh_ref, wrb_ref, br_ref,
                  x1_ref, h2_ref, ri_ref, rp_ref, rr_ref, cnt_ref, run_ref):
    i = pl.program_id(0)

    @pl.when(i == 0)
    def _():
        run_ref[...] = jnp.zeros_like(run_ref)

    is_ctx = i < T_CTX // TM_OUT
    ret = jnp.where(is_ctx, retc_ref[...], retl_ref[...])
    pool = jnp.where(is_ctx, poolc_ref[...], pooll_ref[...])
    x = jnp.where(is_ctx, xc_ref[...], xl_ref[...])
    y = (jnp.dot(ret, wt_ref[...], preferred_element_type=F32)
         + jnp.dot(pool, wb_ref[...], preferred_element_type=F32))
    x1 = x + mod_ref[0, 2:3, :] * y
    x1_ref[...] = x1
    hn = x1 * lax.rsqrt(jnp.mean(x1 * x1, axis=-1, keepdims=True) + NORM_EPS) * g_ref[...]
    h2 = hn * (1.0 + mod_ref[0, 4:5, :]) + mod_ref[0, 3:4, :]
    hi = h2.astype(BF16)
    h2_ref[...] = hi
    lo = (h2 - hi.astype(F32)).astype(BF16)
    both = jnp.dot(hi, wrb_ref[...], preferred_element_type=F32)
    logits = (both[:, 0:LANES] + jnp.dot(lo, wrh_ref[...], preferred_element_type=F32)
              + both[:, LANES:2 * LANES]) + br_ref[...]

    tm = logits.shape[0]
    lane = lax.broadcasted_iota(jnp.int32, (tm, LANES), 1)
    work = logits
    vals, idxs, hots = [], [], []
    for _ in range(TOP_K):
        m = jnp.max(work, axis=-1, keepdims=True)
        idx = jnp.min(jnp.where(work == m, lane, LANES), axis=-1, keepdims=True)
        hot = lane == idx
        vals.append(m)
        idxs.append(idx)
        hots.append(hot)
        work = jnp.where(hot, -jnp.inf, work)
    exps = [jnp.exp(v - vals[0]) for v in vals]
    denom = exps[0] + exps[1] + exps[2] + exps[3]

    selected = jnp.zeros((tm, LANES), F32)
    for hot in hots:
        selected = selected + jnp.where(hot, 1.0, 0.0)
    r_i = lax.broadcasted_iota(jnp.int32, (tm, tm), 0)
    c_i = lax.broadcasted_iota(jnp.int32, (tm, tm), 1)
    tri = jnp.where(c_i < r_i, 1.0, 0.0).astype(BF16)
    before = jnp.dot(tri, selected.astype(BF16), preferred_element_type=F32) + run_ref[0:1, :]

    ri = jnp.zeros((tm, LANES), jnp.int32)
    rp = jnp.zeros((tm, LANES), F32)
    rr = jnp.zeros((tm, LANES), jnp.int32)
    for k in range(TOP_K):
        rank = jnp.sum(jnp.where(hots[k], before, 0.0), axis=-1, keepdims=True).astype(jnp.int32)
        ri = jnp.where(lane == k, idxs[k], ri)
        rp = jnp.where(lane == k, exps[k] / denom, rp)
        rr = jnp.where(lane == k, rank, rr)
    ri_ref[...] = ri
    rp_ref[...] = rp
    rr_ref[...] = rr

    run = run_ref[0:1, :] + jnp.sum(selected, axis=0, keepdims=True)
    run_ref[...] = jnp.broadcast_to(run, run_ref.shape)
    cnt_ref[...] = jnp.broadcast_to(run, cnt_ref.shape).astype(jnp.int32)


def _out_projection(ret_ctx, ret_lat, pool_ctx, pool_lat, x_ctx, x_lat, mod3, norm_g, w_out_bf16,
                    wr_hi, wr_both, b_router_pad):
    tm = TM_OUT
    real = lambda i: jnp.minimum(i, T_ALL // tm - 1)
    row = lambda i: (real(i), 0)
    const = lambda i: (0, 0)
    crow = lambda i: (_ctx_tile(real(i), tm), 0)
    lrow = lambda i: (_lat_tile(real(i), tm), 0)
    return pl.pallas_call(
        _outproj_kernel,
        grid=(H2_ROWS // tm,),
        in_specs=[pl.BlockSpec((tm, RET_WIDTH), crow),
                  pl.BlockSpec((tm, RET_WIDTH), lrow),
                  pl.BlockSpec((tm, POOL_WIDTH), crow),
                  pl.BlockSpec((tm, POOL_WIDTH), lrow),
                  pl.BlockSpec((tm, D_MODEL), crow),
                  pl.BlockSpec((tm, D_MODEL), lrow),
                  pl.BlockSpec((1, 6, D_MODEL), lambda i: (_cond_row(real(i), tm), 0, 0)),
                  pl.BlockSpec((1, D_MODEL), const),
                  pl.BlockSpec((RET_WIDTH, D_MODEL), const),
                  pl.BlockSpec((POOL_WIDTH, D_MODEL), lambda i: (1, 0)),
                  pl.BlockSpec((D_MODEL, LANES), const),
                  pl.BlockSpec((D_MODEL, 2 * LANES), const),
                  pl.BlockSpec((1, LANES), const)],
        out_specs=[pl.BlockSpec((tm, D_MODEL), row),
                   pl.BlockSpec((tm, D_MODEL), lambda i: (i, 0)),
                   pl.BlockSpec((tm, LANES), row),
                   pl.BlockSpec((tm, LANES), row),
                   pl.BlockSpec((tm, LANES), row),
                   pl.BlockSpec((8, LANES), const)],
        out_shape=[jax.ShapeDtypeStruct((T_ALL, D_MODEL), F32),
                   jax.ShapeDtypeStruct((H2_ROWS, D_MODEL), BF16),
                   jax.ShapeDtypeStruct((T_ALL, LANES), jnp.int32),
                   jax.ShapeDtypeStruct((T_ALL, LANES), F32),
                   jax.ShapeDtypeStruct((T_ALL, LANES), jnp.int32),
                   jax.ShapeDtypeStruct((8, LANES), jnp.int32)],
        scratch_shapes=[pltpu.VMEM((8, LANES), F32)],
        compiler_params=_cparams(("arbitrary",), 48 * 1024 * 1024),
        name="out_projection_router",
    )(ret_ctx, ret_lat, pool_ctx, pool_lat, x_ctx, x_lat, mod3, norm_g.reshape(1, D_MODEL),
      w_out_bf16, w_out_bf16, wr_hi, wr_both, b_router_pad)


def _count_le(ends, v):
    return jnp.sum((ends[None, :] <= v[:, None]).astype(jnp.int32), axis=1)


def _expert_schedule(counts):
    i32 = jnp.int32
    ntile = (counts + TM_EXP - 1) // TM_EXP
    tile_off = jnp.cumsum(ntile) - ntile
    nsup = (ntile + SUP_EXP - 1) // SUP_EXP
    sup_end = jnp.cumsum(nsup)
    u = jnp.arange(U_EXP, dtype=i32)
    sup_valid = u < sup_end[-1]
    uc = jnp.minimum(u, sup_end[-1] - 1)
    sup_e = jnp.minimum(_count_le(sup_end, uc), N_EXPERTS - 1)
    sup_idx = uc - (sup_end - nsup)[sup_e]
    per_sup = (ntile + jnp.maximum(nsup, 1) - 1) // jnp.maximum(nsup, 1)
    sup_row0 = (tile_off[sup_e] + sup_idx * per_sup[sup_e]) * TM_EXP
    sup_nt = jnp.clip(ntile[sup_e] - sup_idx * per_sup[sup_e], 1, per_sup[sup_e])
    step_e = jnp.repeat(sup_e, J_EXP)
    step_j = jnp.tile(jnp.arange(J_EXP, dtype=i32), U_EXP)
    step_row0 = jnp.repeat(sup_row0, J_EXP)
    step_nt = jnp.repeat(sup_nt, J_EXP)
    last_rows = counts - (ntile - 1) * TM_EXP
    sup_half = (sup_idx == nsup[sup_e] - 1) & (last_rows[sup_e] <= TM_EXP // 2) & (counts[sup_e] > 0)
    step_half = jnp.repeat(sup_half.astype(i32), J_EXP)
    bounds = jnp.asarray(CH_START_EXP[1:], dtype=i32)
    before = jnp.sum((sup_valid[None, :] & (sup_row0[None, :] < bounds[:, None])).astype(i32), axis=1) * J_EXP
    n_steps = jnp.sum(sup_valid.astype(i32)) * J_EXP
    chunk_start = jnp.concatenate([jnp.zeros((1,), i32), before])
    chunk_steps = jnp.concatenate([before, n_steps.reshape(1)]) - chunk_start
    tables = (step_e.astype(i32), step_j.astype(i32), step_row0.astype(i32), step_nt.astype(i32), step_half)
    return tile_off * TM_EXP, tables, chunk_start, chunk_steps, jnp.sum(ntile).astype(i32)


def _expert_kernel(se_ref, sj_ref, srow_ref, snt_ref, shalf_ref, meta_ref,
                   xs_hbm, wg_ref, wu_ref, bg_ref, bu_ref, wd_ref, bd_ref, *rest, first_chunk):
    ys_hbm, acc_ref, x_buf, o_buf, x_sem, o_sem = rest if first_chunk else rest[1:]
    s = pl.program_id(0) + meta_ref[0]
    j = sj_ref[s]
    n_tiles = snt_ref[s]
    row0 = pl.multiple_of(srow_ref[s], TM_EXP)
    last = J_EXP - 1

    def x_copy(first_row):
        rows = pl.ds(pl.multiple_of(first_row - meta_ref[1], TM_EXP), SUP_EXP * TM_EXP)
        return pltpu.make_async_copy(xs_hbm.at[rows], x_buf, x_sem)

    def out_copy(t):
        rows = pl.ds(pl.multiple_of(row0 + t * TM_EXP, TM_EXP), TM_EXP)
        return pltpu.make_async_copy(o_buf.at[t % 2], ys_hbm.at[rows], o_sem.at[t % 2])

    @pl.when((pl.program_id(0) == 0) & first_chunk)
    def _():
        o_buf[0] = jnp.zeros((TM_EXP, D_MODEL), BF16)

        def zero_copy(t):
            rows = pl.ds(pl.multiple_of(t * TM_EXP, TM_EXP), TM_EXP)
            return pltpu.make_async_copy(o_buf.at[0], ys_hbm.at[rows], o_sem.at[0])

        def start(t, carry):
            zero_copy(t).start()
            return carry

        def wait(t, carry):
            zero_copy(t).wait()
            return carry

        lax.fori_loop(meta_ref[2], NT_EXP, start, 0)
        lax.fori_loop(meta_ref[2], NT_EXP, wait, 0)

    def step():
        @pl.when(j == 0)
        def _():
            @pl.when(pl.program_id(0) == 0)
            def _():
                x_copy(row0).start()

            acc_ref[...] = jnp.zeros_like(acc_ref)
            x_copy(row0).wait()

        def accumulate(rows):
            x = x_buf[rows, :]
            gate = _dot(x, wg_ref[0]) + bg_ref[0]
            up = _dot(x, wu_ref[0]) + bu_ref[0]
            gate = jnp.minimum(gate, SWIGLU_LIMIT)
            up = jnp.clip(up, -SWIGLU_LIMIT, SWIGLU_LIMIT)
            act = (up + 1.0) * gate * _sigmoid(SWIGLU_ALPHA * gate)
            acc_ref[rows, :] += _dot(act, wd_ref[0])

        def tile_pair(p, carry):
            base = pl.multiple_of(p * (2 * TM_EXP), 2 * TM_EXP)
            accumulate(pl.ds(base, TM_EXP))
            accumulate(pl.ds(base + TM_EXP, TM_EXP))
            return carry

        n_full = n_tiles - shalf_ref[s]
        lax.fori_loop(0, n_full // 2, tile_pair, 0)

        @pl.when(n_full % 2 == 1)
        def _():
            accumulate(pl.ds(pl.multiple_of((n_full - 1) * TM_EXP, TM_EXP), TM_EXP))

        @pl.when(shalf_ref[s] == 1)
        def _():
            accumulate(pl.ds(pl.multiple_of((n_tiles - 1) * TM_EXP, TM_EXP), TM_EXP // 2))

        @pl.when(j == last)
        def _():
            @pl.when(pl.program_id(0) + 1 < meta_ref[3])
            def _():
                x_copy(pl.multiple_of(srow_ref[s + 1], TM_EXP)).start()

            def finish(t, carry):
                @pl.when(t >= 2)
                def _():
                    out_copy(t - 2).wait()

                rows = pl.ds(pl.multiple_of(t * TM_EXP, TM_EXP), TM_EXP)
                o_buf[t % 2] = (acc_ref[rows, :] + bd_ref[0]).astype(BF16)
                out_copy(t).start()
                return carry

            lax.fori_loop(0, n_tiles, finish, 0)

            @pl.when(n_tiles >= 2)
            def _():
                out_copy(n_tiles - 2).wait()

            out_copy(n_tiles - 1).wait()

    step()


def _experts_chunk(chunk, xs_part, ys, tables, meta, n_steps, w_gate_up, b_gate_up, w_down, b_down):
    first = ys is None
    wmap = lambda col0: (lambda s, se, sj, srow, snt, shalf, meta: (se[s + meta[0]], 0, col0 + sj[s + meta[0]]))
    n_prefetch = len(tables) + 1
    in_specs = [
        pl.BlockSpec(memory_space=pl.ANY),
        pl.BlockSpec((1, D_MODEL, F_EXP), wmap(0)),
        pl.BlockSpec((1, D_MODEL, F_EXP), wmap(J_EXP)),
        pl.BlockSpec((1, 1, F_EXP), wmap(0)),
        pl.BlockSpec((1, 1, F_EXP), wmap(J_EXP)),
        pl.BlockSpec((1, F_EXP, D_MODEL),
                     lambda s, se, sj, srow, snt, shalf, meta: (se[s + meta[0]], sj[s + meta[0]], 0)),
        pl.BlockSpec((1, 1, D_MODEL), lambda s, se, sj, srow, snt, shalf, meta: (se[s + meta[0]], 0, 0)),
    ]
    args = [xs_part, w_gate_up, w_gate_up,
            b_gate_up.reshape(N_EXPERTS, 1, 2 * D_FF), b_gate_up.reshape(N_EXPERTS, 1, 2 * D_FF),
            w_down, b_down.reshape(N_EXPERTS, 1, D_MODEL)]
    aliases = {}
    if not first:
        in_specs.append(pl.BlockSpec(memory_space=pl.ANY))
        args.append(ys)
        aliases = {n_prefetch + len(args) - 1: 0}
    return pl.pallas_call(
        functools.partial(_expert_kernel, first_chunk=first),
        grid_spec=pltpu.PrefetchScalarGridSpec(
            num_scalar_prefetch=n_prefetch,
            grid=(n_steps,),
            in_specs=in_specs,
            out_specs=pl.BlockSpec(memory_space=pl.ANY),
            scratch_shapes=[pltpu.VMEM((SUP_EXP * TM_EXP, D_MODEL), F32),
                            pltpu.VMEM((SUP_EXP * TM_EXP, D_MODEL), BF16),
                            pltpu.VMEM((2, TM_EXP, D_MODEL), BF16),
                            pltpu.SemaphoreType.DMA(()),
                            pltpu.SemaphoreType.DMA((2,))]),
        out_shape=jax.ShapeDtypeStruct((NT_EXP * TM_EXP, D_MODEL), BF16),
        input_output_aliases=aliases,
        compiler_params=_cparams(("arbitrary",), VMEM_LIMIT),
        name=f"experts_chunk{chunk}",
    )(*tables, meta, *args)


def _combine_kernel(x1_ref, y0_ref, y1_ref, y2_ref, y3_ref, rp_ref, mod_ref, g_ref, o_ref):
    rp = rp_ref[...]
    moe = rp[:, 0:1] * y0_ref[...].astype(F32)
    for k, y_ref in enumerate((y1_ref, y2_ref, y3_ref), start=1):
        moe = moe + rp[:, k:k + 1] * y_ref[...].astype(F32)
    x2 = x1_ref[...] + mod_ref[0, 5:6, :] * moe
    o_ref[...] = x2 * lax.rsqrt(jnp.mean(x2 * x2, axis=-1, keepdims=True) + NORM_EPS) * g_ref[...]


def _combine(x1, y4, route_p, mod3, final_g, row0, n_rows):
    tm = TM_CMB
    nt = n_rows // tm
    t0 = row0 // tm
    y_specs = [pl.BlockSpec((tm, D_MODEL), functools.partial(lambda k, i: (k * nt + i, 0), k))
               for k in range(TOP_K)]
    return pl.pallas_call(
        _combine_kernel,
        grid=(nt,),
        in_specs=[pl.BlockSpec((tm, D_MODEL), lambda i: (t0 + i, 0))] + y_specs + [
            pl.BlockSpec((tm, LANES), lambda i: (t0 + i, 0)),
            pl.BlockSpec((1, 6, D_MODEL), lambda i: (_cond_row(t0 + i, tm), 0, 0)),
            pl.BlockSpec((1, D_MODEL), lambda i: (0, 0))],
        out_specs=pl.BlockSpec((tm, D_MODEL), lambda i: (i, 0)),
        out_shape=jax.ShapeDtypeStruct((n_rows, D_MODEL), F32),
        compiler_params=_cparams(("arbitrary",), 40 * 1024 * 1024),
        name="combine_final_norm",
    )(x1, y4, y4, y4, y4, route_p, mod3, final_g.reshape(1, D_MODEL))


def kernel(x_prompt, x_sample, c, c_ctx, state_ret, norm1_g, norm2_g, w_mod, b_mod, w_in, ret_decay,
           w_pool, pool_scale, w_out, w_router, b_router, w_gate_up, b_gate_up, w_down, b_down, final_g):
    assert w_mod.shape[0] == 1, "single trunk layer"
    x_ctx = x_prompt.reshape(T_CTX, D_MODEL)
    x_lat = x_sample.reshape(T_LAT, D_MODEL)
    cond = jnp.zeros((COND_ROWS, D_MODEL), F32).at[0].set(c_ctx).at[1:N_COND].set(c)

    mod = _modulation(cond, w_mod[0], b_mod[0])
    mod3 = mod.reshape(COND_ROWS, 6, D_MODEL)

    proj = _in_projection(x_ctx, x_lat, norm1_g[0], mod3, w_in[0].astype(BF16))

    ret_ctx, new_state = _retention(proj, 0, BATCH, SEQ, ret_decay[0], None, None, True, RET_HEADS)
    ret_lat, _ = _retention(proj, T_CTX, DEC_BATCH, DEC_SEQ, ret_decay[0], state_ret[:, 0],
                            _rope_tables(DEC_SEQ), False, 2)
    w_pool_bf16 = w_pool[0].astype(BF16)
    pool_ctx = _pooling(proj, 0, BATCH, SEQ, w_pool_bf16, pool_scale[0], False)
    pool_lat = _pooling(proj, T_CTX, DEC_BATCH, DEC_SEQ, w_pool_bf16, pool_scale[0], True)

    wr = jnp.zeros((D_MODEL, LANES), F32).at[:, :N_EXPERTS].set(w_router[0])
    wr_hi = wr.astype(BF16)
    wr_both = jnp.concatenate([wr_hi, (wr - wr_hi.astype(F32)).astype(BF16)], axis=1)
    br = jnp.full((1, LANES), -1e30, F32).at[0, :N_EXPERTS].set(b_router[0])
    x1, h2, route_i, route_p, route_r, counts = _out_projection(
        ret_ctx, ret_lat, pool_ctx, pool_lat, x_ctx, x_lat, mod3, norm2_g[0], w_out[0].astype(BF16),
        wr_hi, wr_both, br)

    row_off, tables, chunk_start, chunk_steps, n_row_tiles = _expert_schedule(counts[0, :N_EXPERTS])
    slot = (row_off[route_i[:, :TOP_K]] + route_r[:, :TOP_K]).T
    token = jnp.broadcast_to(jnp.arange(T_ALL, dtype=jnp.int32)[None, :], (TOP_K, T_ALL))
    token_of_slot = (jnp.arange(XS_ROWS, dtype=jnp.int32) % T_ALL).at[slot.reshape(-1)].set(
        token.reshape(-1), unique_indices=True, mode="promise_in_bounds")
    ys = None
    for c in range(CHUNKS_EXP):
        lo = CH_START_EXP[c]
        nxt = CH_START_EXP[c + 1] if c + 1 < CHUNKS_EXP else XS_ROWS
        hi = min(nxt + SUP_EXP * TM_EXP, XS_ROWS)
        xs_part = h2.at[token_of_slot[lo:hi]].get(mode="promise_in_bounds")
        meta = jnp.stack([chunk_start[c], jnp.int32(lo), n_row_tiles, chunk_steps[c]]).astype(jnp.int32)
        ys = _experts_chunk(c, xs_part, ys, tables, meta, chunk_steps[c],
                            w_gate_up[0], b_gate_up[0], w_down[0], b_down[0])
    y4_lat = ys.at[slot[:, T_CTX:].reshape(-1)].get(mode="promise_in_bounds", unique_indices=True)
    y4_ctx = ys.at[slot[:, :T_CTX].reshape(-1)].get(mode="promise_in_bounds", unique_indices=True)
    y_lat = _combine(x1, y4_lat, route_p, mod3, final_g, T_CTX, T_LAT)
    y_ctx = _combine(x1, y4_ctx, route_p, mod3, final_g, 0, T_CTX)
    return (y_ctx.reshape(BATCH, SEQ, D_MODEL), y_lat.reshape(DEC_BATCH, DEC_SEQ, D_MODEL),
            new_state.reshape(BATCH, 1, 2, RET_HEADS, RET_DK, RET_DV))
```
